```python
import math
import jax, jax.numpy as jnp
from jax import lax
import numpy as np

D_MODEL = 1024
BATCH = 2
SEQ = 16384
DEPTH = 4
DEC_BATCH = 8
DEC_SEQ = 32
PAST_LEN = 1024

CHUNK = 64
Q_BLOCK = 128
N_MIXERS = 4
N_GROUPS = DEPTH // N_MIXERS
ROPE_THETA = 10000.0
NORM_EPS = 1e-6
NEG_INF = -1e30

A_HEADS = 8
A_HD = 64
A_VD = 2 * A_HD
A_SUBLN_EPS = 1e-5
B_HEADS = 8
B_NOPE = 128
B_ROPE = 64
B_VD = 128
B_Q_LORA = 384
B_KV_LORA = 256
C_WIDTH = 3
D_WINDOWS = (2, 4, 8, 16)
D_GROUP = D_MODEL // 4
D_HIST = 15
D_FF = 2816
FFN_WIDTH = 3

kernel_name = 'hybrid_chunk_causal_encoder_step'


def rmsnorm(x, g, eps=NORM_EPS):
    xf = x.astype(jnp.float32)
    y = xf * lax.rsqrt(jnp.mean(xf * xf, axis=-1, keepdims=True) + eps)
    return (y * g.astype(jnp.float32)).astype(x.dtype)


def rope(x, pos):
    d = x.shape[-1]
    inv = jnp.power(ROPE_THETA, -jnp.arange(0, d, 2, dtype=jnp.float32) / d)
    ang = pos.astype(jnp.float32)[:, None] * inv[None, :]
    ang = ang.reshape((ang.shape[0],) + (1,) * (x.ndim - 3) + (d // 2,))
    cos, sin = jnp.cos(ang), jnp.sin(ang)
    xf = x.astype(jnp.float32)
    x1, x2 = xf[..., : d // 2], xf[..., d // 2:]
    return jnp.concatenate([x1 * cos - x2 * sin, x2 * cos + x1 * sin], axis=-1).astype(x.dtype)


def causal_dwconv(x, hist, w):
    width = w.shape[0]
    s = x.shape[1]
    xh = jnp.concatenate([hist, x], axis=1)
    y = xh[:, 0:s] * w[0]
    for k in range(1, width):
        y = y + xh[:, k:k + s] * w[k]
    return y, xh[:, -(width - 1):]


def masked_softmax(s, mask):
    s = s.astype(jnp.float32)
    if mask is not None:
        s = jnp.where(mask, s, NEG_INF)
    return jax.nn.softmax(s, axis=-1)


def chunk_causal_mask(q_start, n_q, n_k):
    qc = (q_start + jnp.arange(n_q)) // CHUNK
    kc = jnp.arange(n_k) // CHUNK
    return kc[None, :] <= qc[:, None]


def sweep_query_blocks(fn, q_arrays, seq):
    nb = seq // Q_BLOCK
    def to_blocks(a):
        return jnp.moveaxis(a.reshape((a.shape[0], nb, Q_BLOCK) + a.shape[2:]), 1, 0)
    idx = jnp.arange(nb, dtype=jnp.int32)
    out = lax.map(lambda args: fn(args[0] * Q_BLOCK, *args[1:]),
                  (idx,) + tuple(to_blocks(a) for a in q_arrays))
    out = jnp.moveaxis(out, 0, 1)
    return out.reshape((out.shape[0], seq) + out.shape[3:])


def diff_lambda(lam, layer):
    lam_init = 0.8 - 0.6 * math.exp(-0.3 * layer)
    lf = lam.astype(jnp.float32)
    val = jnp.exp(jnp.sum(lf[0] * lf[1])) - jnp.exp(jnp.sum(lf[2] * lf[3])) + lam_init
    return val, lam_init


def diff_project(h, w_qkv, pos):
    nb, s, _ = h.shape
    q, k, v = jnp.split(h @ w_qkv, 3, axis=-1)
    q = rope(q.reshape(nb, s, A_HEADS, 2, A_HD), pos)
    k = rope(k.reshape(nb, s, A_HEADS, 2, A_HD), pos)
    return q, k, v.reshape(nb, s, A_HEADS, A_VD)


def diff_attend(q, k, v, lam, mask):
    s = jnp.einsum('bqhcd,bkhcd->bchqk', q, k) / math.sqrt(A_HD)
    p = masked_softmax(s, mask)
    p = p[:, 0] - lam * p[:, 1]
    return jnp.einsum('bhqk,bkhe->bqhe', p.astype(v.dtype), v)


def diff_finish(o, subln_g, lam_init, w_o):
    o = rmsnorm(o, subln_g, A_SUBLN_EPS) * (1.0 - lam_init)
    return o.reshape(o.shape[0], o.shape[1], -1) @ w_o


def mla_project(h, w_down, q_norm_g, w_uq, kv_norm_g, pos):
    nb, s, _ = h.shape
    cq, ckv, kr = jnp.split(h @ w_down, [B_Q_LORA, B_Q_LORA + B_KV_LORA], axis=-1)
    cq = rmsnorm(cq, q_norm_g)
    ckv = rmsnorm(ckv, kv_norm_g)
    kr = rope(kr[:, :, None, :], pos)[:, :, 0]
    q = (cq @ w_uq).reshape(nb, s, B_HEADS, B_NOPE + B_ROPE)
    return q[..., :B_NOPE], rope(q[..., B_NOPE:], pos), ckv, kr


def mla_expand_kv(ckv, kr, w_uk, w_uv):
    k_nope = jnp.einsum('bsl,lhn->bshn', ckv, w_uk)
    v = jnp.einsum('bsl,lhv->bshv', ckv, w_uv)
    k_rope = jnp.broadcast_to(kr[:, :, None, :], k_nope.shape[:3] + (B_ROPE,))
    return jnp.concatenate([k_nope, k_rope], axis=-1), v


def mla_attend_expanded(q, k, v, mask):
    s = jnp.einsum('bqhd,bkhd->bhqk', q, k) / math.sqrt(B_NOPE + B_ROPE)
    p = masked_softmax(s, mask).astype(v.dtype)
    return jnp.einsum('bhqk,bkhv->bqhv', p, v)


def mla_attend_latent(q_nope, q_rope, ckv, kr, w_uk, w_uv):
    q_lat = jnp.einsum('bqhn,lhn->bqhl', q_nope, w_uk)
    s = (jnp.einsum('bqhl,bkl->bhqk', q_lat, ckv)
         + jnp.einsum('bqhr,bkr->bhqk', q_rope, kr)) / math.sqrt(B_NOPE + B_ROPE)
    p = masked_softmax(s, None).astype(ckv.dtype)
    o_lat = jnp.einsum('bhqk,bkl->bqhl', p, ckv)
    return jnp.einsum('bqhl,lhv->bqhv', o_lat, w_uv)


def short_conv_mixer(h, hist, w_in, conv_w, w_out):
    gate_b, gate_c, v = jnp.split(h @ w_in, 3, axis=-1)
    y, new_hist = causal_dwconv(gate_c * v, hist, conv_w)
    return (gate_b * y) @ w_out, new_hist


def pooling_mixer(h, hist, hist_valid, w_group, scale):
    t = h.shape[1]
    xcat = jnp.concatenate([hist, h], axis=1)
    xh = xcat.astype(jnp.float32)
    cs = jnp.concatenate([jnp.zeros_like(xh[:, :1]), jnp.cumsum(xh, axis=1)], axis=1)
    end = cs[:, D_HIST + 1:]
    tpos = jnp.arange(t)
    outs = []
    for g, w in enumerate(D_WINDOWS):
        sl = slice(g * D_GROUP, (g + 1) * D_GROUP)
        start = cs[:, D_HIST + 1 - w: D_HIST + 1 - w + t, sl]
        cnt = jnp.minimum(tpos + hist_valid + 1, w).astype(jnp.float32)[None, :, None]
        pooled = (end[..., sl] - start) / cnt
        outs.append((pooled - xh[:, D_HIST:, sl]).astype(h.dtype) @ w_group[g])
    return jnp.concatenate(outs, axis=-1) * scale, xcat[:, -D_HIST:]


def conv_ffn(h, hist, w_gate, w_up, conv_w, conv_b, w_down):
    g, new_hist = causal_dwconv(h @ w_gate, hist, conv_w)
    return (jax.nn.silu(g + conv_b) * (h @ w_up)) @ w_down, new_hist


def setup_inputs(seed: int = 0) -> dict:
    key = jax.random.key(seed)
    ks = iter(jax.random.split(key, 64))
    def nrm(shape, scale):
        return jax.random.normal(next(ks), shape, jnp.float32) * scale
    def gain(shape):
        return 1.0 + nrm(shape, 0.1)
    G = N_GROUPS
    return {
        'x_prompt': nrm((BATCH, SEQ, D_MODEL), 1.0),
        'x_sample': nrm((DEC_BATCH, DEC_SEQ, D_MODEL), 1.0),
        'cache_a_k': nrm((G, DEC_BATCH, PAST_LEN, A_HEADS, 2, A_HD), 1.0),
        'cache_a_v': nrm((G, DEC_BATCH, PAST_LEN, A_HEADS, A_VD), 1.0),
        'cache_b_latent': nrm((G, DEC_BATCH, PAST_LEN, B_KV_LORA), 1.0),
        'cache_b_krope': nrm((G, DEC_BATCH, PAST_LEN, B_ROPE), 1.0),
        'state_c_conv': nrm((G, DEC_BATCH, C_WIDTH - 1, D_MODEL), 1.0),
        'state_d_pool': nrm((G, DEC_BATCH, D_HIST, D_MODEL), 1.0),
        'state_ffn_conv': nrm((DEPTH, DEC_BATCH, FFN_WIDTH - 1, D_FF), 1.0),
        'norm_mix_g': gain((DEPTH, D_MODEL)),
        'norm_ffn_g': gain((DEPTH, D_MODEL)),
        'norm_final_g': gain((D_MODEL,)),
        'a_w_qkv': nrm((G, D_MODEL, 3 * D_MODEL), D_MODEL ** -0.5),
        'a_lam': nrm((G, 4, A_HD), 0.1),
        'a_subln_g': gain((G, A_VD)),
        'a_w_o': nrm((G, A_HEADS * A_VD, D_MODEL), (A_HEADS * A_VD) ** -0.5),
        'b_w_down': nrm((G, D_MODEL, B_Q_LORA + B_KV_LORA + B_ROPE), D_MODEL ** -0.5),
        'b_q_norm_g': gain((G, B_Q_LORA)),
        'b_w_uq': nrm((G, B_Q_LORA, B_HEADS * (B_NOPE + B_ROPE)), B_Q_LORA ** -0.5),
        'b_kv_norm_g': gain((G, B_KV_LORA)),
        'b_w_uk': nrm((G, B_KV_LORA, B_HEADS, B_NOPE), B_KV_LORA ** -0.5),
        'b_w_uv': nrm((G, B_KV_LORA, B_HEADS, B_VD), B_KV_LORA ** -0.5),
        'b_w_o': nrm((G, B_HEADS * B_VD, D_MODEL), (B_HEADS * B_VD) ** -0.5),
        'c_w_in': nrm((G, D_MODEL, 3 * D_MODEL), D_MODEL ** -0.5),
        'c_conv_w': nrm((G, C_WIDTH, D_MODEL), C_WIDTH ** -0.5),
        'c_w_out': nrm((G, D_MODEL, D_MODEL), D_MODEL ** -0.5),
        'd_w_group': nrm((G, 4, D_GROUP, D_GROUP), D_GROUP ** -0.5),
        'd_scale': gain((G, D_MODEL)),
        'ffn_w_gate': nrm((DEPTH, D_MODEL, D_FF), D_MODEL ** -0.5),
        'ffn_w_up': nrm((DEPTH, D_MODEL, D_FF), D_MODEL ** -0.5),
        'ffn_conv_w': nrm((DEPTH, FFN_WIDTH, D_FF), FFN_WIDTH ** -0.5),
        'ffn_conv_b': nrm((DEPTH, D_FF), 0.01),
        'ffn_w_down': nrm((DEPTH, D_FF, D_MODEL), D_FF ** -0.5),
    }


def reference(x_prompt, x_sample, cache_a_k, cache_a_v, cache_b_latent, cache_b_krope,
              state_c_conv, state_d_pool, state_ffn_conv, norm_mix_g, norm_ffn_g, norm_final_g,
              a_w_qkv, a_lam, a_subln_g, a_w_o, b_w_down, b_q_norm_g, b_w_uq, b_kv_norm_g,
              b_w_uk, b_w_uv, b_w_o, c_w_in, c_conv_w, c_w_out, d_w_group, d_scale,
              ffn_w_gate, ffn_w_up, ffn_conv_w, ffn_conv_b, ffn_w_down):
    xp, xs = x_prompt, x_sample
    n_p, seq = xp.shape[0], xp.shape[1]
    n_s, t_new = xs.shape[0], xs.shape[1]
    past = cache_a_k.shape[2]
    pos_p = jnp.arange(seq, dtype=jnp.int32)
    pos_s = past + jnp.arange(t_new, dtype=jnp.int32)
    ak_p, av_p, bl_p, br_p, cc_p, dp_p, fc_p = [], [], [], [], [], [], []
    ak_s, av_s, bl_s, br_s, cc_s, dp_s, fc_s = [], [], [], [], [], [], []
    for i in range(DEPTH):
        m, j = i % N_MIXERS, i // N_MIXERS
        hp = rmsnorm(xp, norm_mix_g[i])
        hs = rmsnorm(xs, norm_mix_g[i])
        if m == 0:
            lam, lam_init = diff_lambda(a_lam[j], i)
            qp, kp, vp = diff_project(hp, a_w_qkv[j], pos_p)
            op = sweep_query_blocks(
                lambda q0, qb, k=kp, v=vp, lm=lam: diff_attend(qb, k, v, lm, chunk_causal_mask(q0, Q_BLOCK, seq)),
                (qp,), seq)
            qs, ks_, vs = diff_project(hs, a_w_qkv[j], pos_s)
            osm = diff_attend(qs, jnp.concatenate([cache_a_k[j], ks_], axis=1),
                              jnp.concatenate([cache_a_v[j], vs], axis=1), lam, None)
            mp = diff_finish(op, a_subln_g[j], lam_init, a_w_o[j])
            ms = diff_finish(osm, a_subln_g[j], lam_init, a_w_o[j])
            ak_p.append(kp); av_p.append(vp); ak_s.append(ks_); av_s.append(vs)
        elif m == 1:
            qnp, qrp, lp, rp = mla_project(hp, b_w_down[j], b_q_norm_g[j], b_w_uq[j], b_kv_norm_g[j], pos_p)
            kfull, vfull = mla_expand_kv(lp, rp, b_w_uk[j], b_w_uv[j])
            qfull = jnp.concatenate([qnp, qrp], axis=-1)
            op = sweep_query_blocks(
                lambda q0, qb, k=kfull, v=vfull: mla_attend_expanded(qb, k, v, chunk_causal_mask(q0, Q_BLOCK, seq)),
                (qfull,), seq)
            qns, qrs, ls, rs = mla_project(hs, b_w_down[j], b_q_norm_g[j], b_w_uq[j], b_kv_norm_g[j], pos_s)
            osm = mla_attend_latent(qns, qrs, jnp.concatenate([cache_b_latent[j], ls], axis=1),
                                    jnp.concatenate([cache_b_krope[j], rs], axis=1), b_w_uk[j], b_w_uv[j])
            mp = op.reshape(n_p, seq, -1) @ b_w_o[j]
            ms = osm.reshape(n_s, t_new, -1) @ b_w_o[j]
            bl_p.append(lp); br_p.append(rp); bl_s.append(ls); br_s.append(rs)
        elif m == 2:
            mp, hcp = short_conv_mixer(hp, jnp.zeros((n_p, C_WIDTH - 1, D_MODEL), hp.dtype),
                                       c_w_in[j], c_conv_w[j], c_w_out[j])
            ms, hcs = short_conv_mixer(hs, state_c_conv[j], c_w_in[j], c_conv_w[j], c_w_out[j])
            cc_p.append(hcp); cc_s.append(hcs)
        else:
            mp, hdp = pooling_mixer(hp, jnp.zeros((n_p, D_HIST, D_MODEL), hp.dtype), 0,
                                    d_w_group[j], d_scale[j])
            ms, hds = pooling_mixer(hs, state_d_pool[j], D_HIST, d_w_group[j], d_scale[j])
            dp_p.append(hdp); dp_s.append(hds)
        xp = xp + mp
        xs = xs + ms
        hp = rmsnorm(xp, norm_ffn_g[i])
        hs = rmsnorm(xs, norm_ffn_g[i])
        fp, hfp = conv_ffn(hp, jnp.zeros((n_p, FFN_WIDTH - 1, D_FF), hp.dtype), ffn_w_gate[i],
                           ffn_w_up[i], ffn_conv_w[i], ffn_conv_b[i], ffn_w_down[i])
        fs, hfs = conv_ffn(hs, state_ffn_conv[i], ffn_w_gate[i], ffn_w_up[i], ffn_conv_w[i],
                           ffn_conv_b[i], ffn_w_down[i])
        fc_p.append(hfp); fc_s.append(hfs)
        xp = xp + fp
        xs = xs + fs
    y_prompt = rmsnorm(xp, norm_final_g)
    y_sample = rmsnorm(xs, norm_final_g)
    return (y_prompt, y_sample,
            jnp.stack(ak_p), jnp.stack(av_p), jnp.stack(bl_p), jnp.stack(br_p),
            jnp.stack(cc_p), jnp.stack(dp_p), jnp.stack(fc_p),
            jnp.stack(ak_s), jnp.stack(av_s), jnp.stack(bl_s), jnp.stack(br_s),
            jnp.stack(cc_s), jnp.stack(dp_s), jnp.stack(fc_s))
```

```python
import functools
import math

import jax
import jax.numpy as jnp
from jax import lax
from jax.experimental import pallas as pl
from jax.experimental.pallas import tpu as pltpu

_BF = jnp.bfloat16
_F32 = jnp.float32

_CHUNK = 64
_ROPE_THETA = 10000.0
_NORM_EPS = 1e-6
_NEG_INF = -1e30
_A_HEADS = 8
_A_HD = 64
_A_SUBLN_EPS = 1e-5
_B_HEADS = 8
_B_NOPE = 128
_B_ROPE = 64
_B_VD = 128
_B_Q_LORA = 384
_B_KV_LORA = 256
_D_WINDOWS = (2, 4, 8, 16)
_D_HIST = 15
_N_MIXERS = 4

_LANES = 128
_SUBLANES = 8
_MXU_COLS = 256
_VMEM_LIMIT = 56 * 1024 * 1024

_ROW_TILE = 512
_ATT_TILE = 512


def _params(sem):
    return pltpu.CompilerParams(dimension_semantics=sem, vmem_limit_bytes=_VMEM_LIMIT)


def _resident(shape):
    nd = len(shape)
    return pl.BlockSpec(shape, lambda *_: (0,) * nd, pipeline_mode=pl.Buffered(1))


def _rms(x, g, eps):
    ms = jnp.mean(x * x, axis=-1, keepdims=True)
    return x * lax.rsqrt(ms + eps) * g


def _rope_lanes(y, cos, sin_lo, sin_hi):
    return (y * cos + pltpu.roll(y, _LANES - 32, axis=1) * sin_lo
            + pltpu.roll(y, 32, axis=1) * sin_hi)


def _rope_tables(pos):
    d = _A_HD
    inv = jnp.power(_ROPE_THETA, -jnp.arange(0, d, 2, dtype=_F32) / d)
    ang = pos.astype(_F32)[:, None] * inv[None, :]
    cos, sin = jnp.cos(ang), jnp.sin(ang)
    zero = jnp.zeros_like(sin)
    cos_t = jnp.tile(cos, (1, 4))
    sin_lo = jnp.tile(jnp.concatenate([-sin, zero], axis=1), (1, 2))
    sin_hi = jnp.tile(jnp.concatenate([zero, sin], axis=1), (1, 2))
    return cos_t, sin_lo, sin_hi


def _shift_rows(carry, cur, k):
    ext = jnp.concatenate([carry, cur], axis=0)
    return pltpu.roll(ext, k, axis=0)[carry.shape[0]:]


def _diff_proj_body(x_ref, g_ref, w_ref, cos_ref, slo_ref, shi_ref,
                    qb_ref, kf_ref, kb_ref, vf_ref, vb_ref, *, qscale):
    d = x_ref.shape[-1]
    h = _rms(x_ref[...], g_ref[...], _NORM_EPS).astype(_BF)
    cos, slo, shi = cos_ref[...], slo_ref[...], shi_ref[...]
    cw = _MXU_COLS
    for c in range(0, d, cw):
        yq = jnp.dot(h, w_ref[:, c:c + cw], preferred_element_type=_F32)
        yk = jnp.dot(h, w_ref[:, d + c:d + c + cw], preferred_element_type=_F32)
        yv = jnp.dot(h, w_ref[:, 2 * d + c:2 * d + c + cw], preferred_element_type=_F32)
        for s in range(0, cw, _LANES):
            q = _rope_lanes(yq[:, s:s + _LANES], cos, slo, shi)
            k = _rope_lanes(yk[:, s:s + _LANES], cos, slo, shi)
            qb_ref[:, c + s:c + s + _LANES] = (q * qscale).astype(_BF)
            kf_ref[:, c + s:c + s + _LANES] = k
            kb_ref[:, c + s:c + s + _LANES] = k.astype(_BF)
        vf_ref[:, c:c + cw] = yv
        vb_ref[:, c:c + cw] = yv.astype(_BF)


def _diff_proj(x, g, w_qkv, tabs, tm):
    b, s, d = x.shape
    row = lambda dt: jax.ShapeDtypeStruct((b, s, d), dt)
    xspec = pl.BlockSpec((None, tm, d), lambda i, t: (i, t, 0))
    tspec = pl.BlockSpec((tm, _LANES), lambda i, t: (t, 0))
    return pl.pallas_call(
        functools.partial(_diff_proj_body, qscale=1.0 / math.sqrt(_A_HD)),
        grid=(b, s // tm),
        in_specs=[xspec, _resident((1, d)), _resident(w_qkv.shape), tspec, tspec, tspec],
        out_specs=[xspec] * 5,
        out_shape=[row(_BF), row(_F32), row(_BF), row(_F32), row(_BF)],
        compiler_params=_params(("parallel", "parallel")),
        name="diff_proj",
    )(x, g.reshape(1, d), w_qkv, *tabs)


def _flash_body(*refs, nmaps, causal, tq, tk, lam_init):
    if nmaps == 2:
        q_ref, k_ref, v_ref, lam_ref, sg_ref, o_ref, m_sc, l_sc, acc_sc = refs
    else:
        q_ref, k_ref, v_ref, o_ref, m_sc, l_sc, acc_sc = refs
    qi, ki = pl.program_id(2), pl.program_id(3)
    nk = pl.num_programs(3)

    @pl.when(ki == 0)
    def _():
        m_sc[...] = jnp.full(m_sc.shape, _NEG_INF, _F32)
        l_sc[...] = jnp.zeros(l_sc.shape, _F32)
        acc_sc[...] = jnp.zeros(acc_sc.shape, _F32)

    def step(masked):
        q, k, v = q_ref[...], k_ref[...], v_ref[...]
        if masked:
            rows = lax.broadcasted_iota(jnp.int32, (tq, tk), 0) // _CHUNK
            cols = lax.broadcasted_iota(jnp.int32, (tq, tk), 1) // _CHUNK
            visible = cols <= rows
        if nmaps == 2:
            lane = lax.broadcasted_iota(jnp.int32, q.shape, 1)
        for c in range(nmaps):
            qc = q if nmaps == 1 else jnp.where((lane >= _A_HD) == (c == 1), q, jnp.zeros_like(q))
            s = lax.dot_general(qc, k, (((1,), (1,)), ((), ())), preferred_element_type=_F32)
            if masked:
                s = jnp.where(visible, s, _NEG_INF)
            m_prev = m_sc[c]
            m_new = jnp.maximum(m_prev, jnp.max(s, axis=-1, keepdims=True))
            alpha = jnp.exp(m_prev - m_new)
            p = jnp.exp(s - m_new)
            l_sc[c] = alpha * l_sc[c] + jnp.sum(p, axis=-1, keepdims=True)
            acc_sc[c] = alpha * acc_sc[c] + jnp.dot(p.astype(_BF), v, preferred_element_type=_F32)
            m_sc[c] = m_new

    if causal:
        pl.when(ki < qi)(lambda: step(False))
        pl.when(ki == qi)(lambda: step(True))
    else:
        step(False)

    @pl.when(ki == nk - 1)
    def _():
        if nmaps == 2:
            lp = lam_ref[...]
            lam = (jnp.exp(jnp.sum(lp[0:1] * lp[1:2], axis=-1, keepdims=True))
                   - jnp.exp(jnp.sum(lp[2:3] * lp[3:4], axis=-1, keepdims=True)) + lam_init)
            o = acc_sc[0] / l_sc[0] - lam * (acc_sc[1] / l_sc[1])
            o = _rms(o, sg_ref[...], _A_SUBLN_EPS) * (1.0 - lam_init)
        else:
            o = acc_sc[0] / l_sc[0]
        o_ref[...] = o.astype(o_ref.dtype)


def _flash(q, k, v, heads, dq, dv, *, causal, tq, tk, nmaps=1, lam=None, subln_g=None, lam_init=0.0):
    b, sq, _ = q.shape
    sk = k.shape[1]
    nq, nk = sq // tq, sk // tk
    if causal:
        assert tq == tk and tq % _CHUNK == 0 and sq == sk
        kmap = lambda i, h, qi, ki: (i, jnp.minimum(ki, qi), h)
    else:
        kmap = lambda i, h, qi, ki: (i, ki, h)
    in_specs = [pl.BlockSpec((None, tq, dq), lambda i, h, qi, ki: (i, qi, h)),
                pl.BlockSpec((None, tk, dq), kmap),
                pl.BlockSpec((None, tk, dv), kmap)]
    args = [q, k, v]
    if nmaps == 2:
        in_specs += [_resident(lam.shape), _resident((1, dv))]
        args += [lam, subln_g.reshape(1, dv)]
    return pl.pallas_call(
        functools.partial(_flash_body, nmaps=nmaps, causal=causal, tq=tq, tk=tk, lam_init=lam_init),
        grid=(b, heads, nq, nk),
        in_specs=in_specs,
        out_specs=pl.BlockSpec((None, tq, dv), lambda i, h, qi, ki: (i, qi, h)),
        out_shape=jax.ShapeDtypeStruct((b, sq, heads * dv), _BF),
        scratch_shapes=[pltpu.VMEM((nmaps, tq, 1), _F32), pltpu.VMEM((nmaps, tq, 1), _F32),
                        pltpu.VMEM((nmaps, tq, dv), _F32)],
        compiler_params=_params(("parallel", "parallel", "parallel", "arbitrary")),
        name="flash_diff" if nmaps == 2 else "flash_mla",
    )(*args)


def _mm_body(*refs, has_res):
    if has_res:
        a_ref, w_ref, r_ref, o_ref = refs
    else:
        a_ref, w_ref, o_ref = refs
    y = jnp.dot(a_ref[...].astype(_BF), w_ref[...], preferred_element_type=_F32)
    if has_res:
        y = r_ref[...] + y
    o_ref[...] = y.astype(o_ref.dtype)


def _mm(a, w, res=None, out_dtype=_F32):
    b, s, kdim = a.shape
    n = w.shape[1]
    tm = min(_ROW_TILE, s)
    in_specs = [pl.BlockSpec((None, tm, kdim), lambda i, t: (i, t, 0)), _resident(w.shape)]
    args = [a, w]
    if res is not None:
        in_specs.append(pl.BlockSpec((None, tm, n), lambda i, t: (i, t, 0)))
        args.append(res)
    return pl.pallas_call(
        functools.partial(_mm_body, has_res=res is not None),
        grid=(b, s // tm),
        in_specs=in_specs,
        out_specs=pl.BlockSpec((None, tm, n), lambda i, t: (i, t, 0)),
        out_shape=jax.ShapeDtypeStruct((b, s, n), out_dtype),
        compiler_params=_params(("parallel", "parallel")),
        name="matmul_residual" if res is not None else "matmul",
    )(*args)


def _mla_proj_body(x_ref, g_ref, wd_ref, qg_ref, wq_ref, kvg_ref, wk_ref, wv_ref,
                   cos_ref, slo_ref, shi_ref, lat_ref, kr_ref, q_ref, k_ref, v_ref, *, qscale):
    h = _rms(x_ref[...], g_ref[...], _NORM_EPS).astype(_BF)
    cos, slo, shi = cos_ref[...], slo_ref[...], shi_ref[...]
    dn = jnp.dot(h, wd_ref[...], preferred_element_type=_F32)
    e0, e1 = _B_Q_LORA, _B_Q_LORA + _B_KV_LORA
    cq = _rms(dn[:, :e0], qg_ref[...], _NORM_EPS).astype(_BF)
    lat = _rms(dn[:, e0:e1], kvg_ref[...], _NORM_EPS)
    kr = _rope_lanes(dn[:, e1:e1 + _LANES], cos, slo, shi)
    lat_ref[...] = lat
    kr_ref[...] = kr[:, :_B_ROPE]
    hw = 2 * _LANES
    for hd in range(_B_HEADS):
        yq = jnp.dot(cq, wq_ref[:, hd * hw:(hd + 1) * hw], preferred_element_type=_F32)
        q_ref[:, hd * hw:hd * hw + _LANES] = (yq[:, :_LANES] * qscale).astype(_BF)
        q_ref[:, hd * hw + _LANES:(hd + 1) * hw] = (
            _rope_lanes(yq[:, _LANES:], cos, slo, shi) * qscale).astype(_BF)
    kin = jnp.concatenate([lat, kr], axis=1).astype(_BF)
    for c in range(0, k_ref.shape[-1], _MXU_COLS):
        k_ref[:, c:c + _MXU_COLS] = jnp.dot(
            kin, wk_ref[:, c:c + _MXU_COLS], preferred_element_type=_F32).astype(_BF)
    latb = lat.astype(_BF)
    for c in range(0, v_ref.shape[-1], _MXU_COLS):
        v_ref[:, c:c + _MXU_COLS] = jnp.dot(
            latb, wv_ref[:, c:c + _MXU_COLS], preferred_element_type=_F32).astype(_BF)


def _mla_weights(w_down, w_uq, w_uk, w_uv):
    d = w_down.shape[0]
    hw = 2 * _LANES
    wd = jnp.concatenate([w_down, jnp.zeros((d, _LANES - _B_ROPE), w_down.dtype)], axis=1)
    wq = w_uq.reshape(_B_Q_LORA, _B_HEADS, _B_NOPE + _B_ROPE)
    wq = jnp.pad(wq, ((0, 0), (0, 0), (0, hw - _B_NOPE - _B_ROPE))).reshape(_B_Q_LORA, _B_HEADS * hw)
    eye = jnp.eye(_B_ROPE, dtype=w_uk.dtype)
    wk_top = jnp.pad(w_uk, ((0, 0), (0, 0), (0, hw - _B_NOPE)))
    wk_mid = jnp.broadcast_to(jnp.pad(eye, ((0, 0), (_B_NOPE, hw - _B_NOPE - _B_ROPE)))[:, None, :],
                              (_B_ROPE, _B_HEADS, hw))
    wk_bot = jnp.zeros((_LANES - _B_ROPE, _B_HEADS, hw), w_uk.dtype)
    wk = jnp.concatenate([wk_top, wk_mid, wk_bot], axis=0).reshape(-1, _B_HEADS * hw)
    wv = w_uv.reshape(_B_KV_LORA, _B_HEADS * _B_VD)
    return wd.astype(_BF), wq.astype(_BF), wk.astype(_BF), wv.astype(_BF)


def _mla_proj(x, g, wts, q_norm_g, kv_norm_g, tabs, tm):
    b, s, d = x.shape
    wd, wq, wk, wv = wts
    xspec = pl.BlockSpec((None, tm, d), lambda i, t: (i, t, 0))
    tspec = pl.BlockSpec((tm, _LANES), lambda i, t: (t, 0))
    ospec = lambda n: pl.BlockSpec((None, tm, n), lambda i, t: (i, t, 0))
    oshape = lambda n, dt: jax.ShapeDtypeStruct((b, s, n), dt)
    return pl.pallas_call(
        functools.partial(_mla_proj_body, qscale=1.0 / math.sqrt(_B_NOPE + _B_ROPE)),
        grid=(b, s // tm),
        in_specs=[xspec, _resident((1, d)), _resident(wd.shape), _resident((1, _B_Q_LORA)),
                  _resident(wq.shape), _resident((1, _B_KV_LORA)), _resident(wk.shape),
                  _resident(wv.shape), tspec, tspec, tspec],
        out_specs=[ospec(_B_KV_LORA), ospec(_B_ROPE), ospec(wq.shape[1]), ospec(wk.shape[1]),
                   ospec(wv.shape[1])],
        out_shape=[oshape(_B_KV_LORA, _F32), oshape(_B_ROPE, _F32), oshape(wq.shape[1], _BF),
                   oshape(wk.shape[1], _BF), oshape(wv.shape[1], _BF)],
        compiler_params=_params(("parallel", "parallel")),
        name="mla_proj",
    )(x, g.reshape(1, d), wd, q_norm_g.reshape(1, -1), wq, kv_norm_g.reshape(1, -1), wk, wv, *tabs)


def _conv_mix_body(x_ref, g_ref, win_ref, cw_ref, wout_ref, hist_ref, o_ref, hout_ref,
                   carry_sc, z_sc):
    tm, d = x_ref.shape

    @pl.when(pl.program_id(1) == 0)
    def _():
        carry_sc[...] = hist_ref[...]

    x = x_ref[...]
    h = _rms(x, g_ref[...], _NORM_EPS).astype(_BF)
    cw = _MXU_COLS
    for c in range(0, d, cw):
        gate_b = jnp.dot(h, win_ref[:, c:c + cw], preferred_element_type=_F32)
        gate_c = jnp.dot(h, win_ref[:, d + c:d + c + cw], preferred_element_type=_F32)
        val = jnp.dot(h, win_ref[:, 2 * d + c:2 * d + c + cw], preferred_element_type=_F32)
        u = gate_c * val
        prev = carry_sc[:, c:c + cw]
        y = (_shift_rows(prev, u, 2) * cw_ref[0:1, c:c + cw]
             + _shift_rows(prev, u, 1) * cw_ref[1:2, c:c + cw]
             + u * cw_ref[2:3, c:c + cw])
        z_sc[:, c:c + cw] = (gate_b * y).astype(_BF)
        carry_sc[:, c:c + cw] = u[tm - _SUBLANES:, :]
    hout_ref[...] = carry_sc[...]
    o_ref[...] = x + jnp.dot(z_sc[...], wout_ref[...], preferred_element_type=_F32)


def _conv_mix(x, g, w_in, conv_w, w_out, hist8, tm):
    b, s, d = x.shape
    xspec = pl.BlockSpec((None, tm, d), lambda i, t: (i, t, 0))
    hspec = pl.BlockSpec((None, _SUBLANES, d), lambda i, t: (i, 0, 0))
    return pl.pallas_call(
        _conv_mix_body,
        grid=(b, s // tm),
        in_specs=[xspec, _resident((1, d)), _resident(w_in.shape), _resident(conv_w.shape),
                  _resident(w_out.shape), hspec],
        out_specs=[xspec, hspec],
        out_shape=[jax.ShapeDtypeStruct((b, s, d), _F32),
                   jax.ShapeDtypeStruct((b, _SUBLANES, d), _F32)],
        scratch_shapes=[pltpu.VMEM((_SUBLANES, d), _F32), pltpu.VMEM((tm, d), _BF)],
        compiler_params=_params(("arbitrary", "arbitrary")),
        name="conv_mixer",
    )(x, g.reshape(1, d), w_in, conv_w, w_out, hist8)


_POOL_HALO = 16


def _pool_mix_body(x_ref, g_ref, wg_ref, sc_ref, hist_ref, o_ref, hout_ref, carry_sc, *, hist_valid):
    tm, d = x_ref.shape
    t = pl.program_id(1)

    @pl.when(t == 0)
    def _():
        carry_sc[...] = hist_ref[...]

    x = x_ref[...]
    h = _rms(x, g_ref[...], _NORM_EPS)
    gw = d // len(_D_WINDOWS)
    tpos = t * tm + lax.broadcasted_iota(jnp.int32, (tm, gw), 0)
    outs = []
    for gi, w in enumerate(_D_WINDOWS):
        hg = h[:, gi * gw:(gi + 1) * gw]
        acc = jnp.concatenate([carry_sc[:, gi * gw:(gi + 1) * gw], hg], axis=0)
        k = 1
        while k < w:
            acc = acc + pltpu.roll(acc, k, axis=0)
            k *= 2
        cnt = jnp.minimum(tpos + (hist_valid + 1), w).astype(_F32)
        pooled = acc[_POOL_HALO:] / cnt
        outs.append(jnp.dot((pooled - hg).astype(_BF), wg_ref[gi], preferred_element_type=_F32))
    o_ref[...] = x + jnp.concatenate(outs, axis=1) * sc_ref[...]
    tail = h[tm - _POOL_HALO:, :]
    carry_sc[...] = tail
    hout_ref[...] = tail


def _pool_mix(x, g, w_group, scale, hist16, hist_valid, tm):
    b, s, d = x.shape
    assert all(w & (w - 1) == 0 and w <= _POOL_HALO for w in _D_WINDOWS)
    xspec = pl.BlockSpec((None, tm, d), lambda i, t: (i, t, 0))
    hspec = pl.BlockSpec((None, _POOL_HALO, d), lambda i, t: (i, 0, 0))
    return pl.pallas_call(
        functools.partial(_pool_mix_body, hist_valid=hist_valid),
        grid=(b, s // tm),
        in_specs=[xspec, _resident((1, d)), _resident(w_group.shape), _resident((1, d)), hspec],
        out_specs=[xspec, hspec],
        out_shape=[jax.ShapeDtypeStruct((b, s, d), _F32),
                   jax.ShapeDtypeStruct((b, _POOL_HALO, d), _F32)],
        scratch_shapes=[pltpu.VMEM((_POOL_HALO, d), _F32)],
        compiler_params=_params(("arbitrary", "arbitrary")),
        name="pool_mixer",
    )(x, g.reshape(1, d), w_group, scale.reshape(1, d), hist16)


def _ffn_body(*refs, final):
    if final:
        (x_ref, g_ref, wg_ref, wu_ref, cw_ref, cb_ref, wd_ref, hist_ref, fg_ref,
         o_ref, hout_ref, carry_sc, act_sc) = refs
    else:
        (x_ref, g_ref, wg_ref, wu_ref, cw_ref, cb_ref, wd_ref, hist_ref,
         o_ref, hout_ref, carry_sc, act_sc) = refs
    tm = x_ref.shape[0]
    f = wg_ref.shape[1]

    @pl.when(pl.program_id(1) == 0)
    def _():
        carry_sc[...] = hist_ref[...]

    x = x_ref[...]
    h = _rms(x, g_ref[...], _NORM_EPS).astype(_BF)
    cw = _MXU_COLS
    for c in range(0, f, cw):
        gate = jnp.dot(h, wg_ref[:, c:c + cw], preferred_element_type=_F32)
        up = jnp.dot(h, wu_ref[:, c:c + cw], preferred_element_type=_F32)
        prev = carry_sc[:, c:c + cw]
        y = (_shift_rows(prev, gate, 2) * cw_ref[0:1, c:c + cw]
             + _shift_rows(prev, gate, 1) * cw_ref[1:2, c:c + cw]
             + gate * cw_ref[2:3, c:c + cw] + cb_ref[:, c:c + cw])
        act_sc[:, c:c + cw] = (y / (1.0 + jnp.exp(-y)) * up).astype(_BF)
        carry_sc[:, c:c + cw] = gate[tm - _SUBLANES:, :]
    hout_ref[...] = carry_sc[...]
    out = x + jnp.dot(act_sc[...], wd_ref[...], preferred_element_type=_F32)
    if final:
        out = _rms(out, fg_ref[...], _NORM_EPS)
    o_ref[...] = out


def _ffn(x, g, w_gate, w_up, conv_w, conv_b, w_down, hist8, final_g, tm):
    b, s, d = x.shape
    f = w_gate.shape[1]
    assert f % _MXU_COLS == 0
    xspec = pl.BlockSpec((None, tm, d), lambda i, t: (i, t, 0))
    hspec = pl.BlockSpec((None, _SUBLANES, f), lambda i, t: (i, 0, 0))
    in_specs = [xspec, _resident((1, d)), _resident(w_gate.shape), _resident(w_up.shape),
                _resident(conv_w.shape), _resident((1, f)), _resident(w_down.shape), hspec]
    args = [x, g.reshape(1, d), w_gate, w_up, conv_w, conv_b.reshape(1, f), w_down, hist8]
    if final_g is not None:
        in_specs.append(_resident((1, d)))
        args.append(final_g.reshape(1, d))
    return pl.pallas_call(
        functools.partial(_ffn_body, final=final_g is not None),
        grid=(b, s // tm),
        in_specs=in_specs,
        out_specs=[xspec, hspec],
        out_shape=[jax.ShapeDtypeStruct((b, s, d), _F32),
                   jax.ShapeDtypeStruct((b, _SUBLANES, f), _F32)],
        scratch_shapes=[pltpu.VMEM((_SUBLANES, f), _F32), pltpu.VMEM((tm, f), _BF)],
        compiler_params=_params(("arbitrary", "arbitrary")),
        name="conv_ffn",
    )(*args)


def _pad_hist(hist, rows):
    return jnp.pad(hist, ((0, 0), (rows - hist.shape[1], 0), (0, 0)))


def kernel(x_prompt, x_sample, cache_a_k, cache_a_v, cache_b_latent, cache_b_krope, state_c_conv, state_d_pool, state_ffn_conv, norm_mix_g, norm_ffn_g, norm_final_g, a_w_qkv, a_lam, a_subln_g, a_w_o, b_w_down, b_q_norm_g, b_w_uq, b_kv_norm_g, b_w_uk, b_w_uv, b_w_o, c_w_in, c_conv_w, c_w_out, d_w_group, d_scale, ffn_w_gate, ffn_w_up, ffn_conv_w, ffn_conv_b, ffn_w_down):
    depth = norm_mix_g.shape[0]
    n_p, seq, d = x_prompt.shape
    n_s, t_new, _ = x_sample.shape
    past = cache_a_k.shape[2]
    f = ffn_w_gate.shape[-1]
    tm_p, tm_s = min(_ROW_TILE, seq), t_new
    ta = min(_ATT_TILE, seq)
    tabs_p = _rope_tables(jnp.arange(seq, dtype=jnp.int32))
    tabs_s = _rope_tables(past + jnp.arange(t_new, dtype=jnp.int32))
    xp, xs = x_prompt, x_sample
    outs = {k: [] for k in ("ak_p", "av_p", "bl_p", "br_p", "cc_p", "dp_p", "fc_p",
                            "ak_s", "av_s", "bl_s", "br_s", "cc_s", "dp_s", "fc_s")}
    for i in range(depth):
        m, j = i % _N_MIXERS, i // _N_MIXERS
        g_mix = norm_mix_g[i]
        if m == 0:
            lam_init = 0.8 - 0.6 * math.exp(-0.3 * i)
            w_qkv, w_o = a_w_qkv[j].astype(_BF), a_w_o[j].astype(_BF)
            attn = functools.partial(_flash, heads=_A_HEADS, dq=2 * _A_HD, dv=2 * _A_HD, nmaps=2,
                                     lam=a_lam[j], subln_g=a_subln_g[j], lam_init=lam_init)
            qb, kf, kb, vf, vb = _diff_proj(xp, g_mix, w_qkv, tabs_p, tm_p)
            op = attn(qb, kb, vb, causal=True, tq=ta, tk=ta)
            xp = _mm(op, w_o, res=xp)
            outs["ak_p"].append(kf.reshape(n_p, seq, _A_HEADS, 2, _A_HD))
            outs["av_p"].append(vf.reshape(n_p, seq, _A_HEADS, 2 * _A_HD))
            qb, kf, kb, vf, vb = _diff_proj(xs, g_mix, w_qkv, tabs_s, tm_s)
            k_all = jnp.concatenate([cache_a_k[j].reshape(n_s, past, d).astype(_BF), kb], axis=1)
            v_all = jnp.concatenate([cache_a_v[j].reshape(n_s, past, d).astype(_BF), vb], axis=1)
            osm = attn(qb, k_all, v_all, causal=False, tq=t_new, tk=past + t_new)
            xs = _mm(osm, w_o, res=xs)
            outs["ak_s"].append(kf.reshape(n_s, t_new, _A_HEADS, 2, _A_HD))
            outs["av_s"].append(vf.reshape(n_s, t_new, _A_HEADS, 2 * _A_HD))
        elif m == 1:
            wts = _mla_weights(b_w_down[j], b_w_uq[j], b_w_uk[j], b_w_uv[j])
            w_o = b_w_o[j].astype(_BF)
            hq = 2 * _LANES
            attn = functools.partial(_flash, heads=_B_HEADS, dq=hq, dv=_B_VD)
            lat, kr, q, kfull, v = _mla_proj(xp, g_mix, wts, b_q_norm_g[j], b_kv_norm_g[j], tabs_p, tm_p)
            op = attn(q, kfull, v, causal=True, tq=ta, tk=ta)
            xp = _mm(op, w_o, res=xp)
            outs["bl_p"].append(lat)
            outs["br_p"].append(kr)
            lat, kr, q, kfull, v = _mla_proj(xs, g_mix, wts, b_q_norm_g[j], b_kv_norm_g[j], tabs_s, tm_s)
            kin = jnp.concatenate([cache_b_latent[j], cache_b_krope[j],
                                   jnp.zeros((n_s, past, _LANES - _B_ROPE), _F32)], axis=-1).astype(_BF)
            k_cache = _mm(kin, wts[2], out_dtype=_BF)
            v_cache = _mm(cache_b_latent[j].astype(_BF), wts[3], out_dtype=_BF)
            k_all = jnp.concatenate([k_cache, kfull], axis=1)
            v_all = jnp.concatenate([v_cache, v], axis=1)
            osm = attn(q, k_all, v_all, causal=False, tq=t_new, tk=past + t_new)
            xs = _mm(osm, w_o, res=xs)
            outs["bl_s"].append(lat)
            outs["br_s"].append(kr)
        elif m == 2:
            w_in, w_out = c_w_in[j].astype(_BF), c_w_out[j].astype(_BF)
            xp, hc = _conv_mix(xp, g_mix, w_in, c_conv_w[j], w_out,
                               jnp.zeros((n_p, _SUBLANES, d), _F32), tm_p)
            outs["cc_p"].append(hc[:, -2:])
            xs, hc = _conv_mix(xs, g_mix, w_in, c_conv_w[j], w_out,
                               _pad_hist(state_c_conv[j], _SUBLANES), tm_s)
            outs["cc_s"].append(hc[:, -2:])
        else:
            w_grp = d_w_group[j].astype(_BF)
            xp, hd = _pool_mix(xp, g_mix, w_grp, d_scale[j],
                               jnp.zeros((n_p, _POOL_HALO, d), _F32), 0, tm_p)
            outs["dp_p"].append(hd[:, -_D_HIST:])
            xs, hd = _pool_mix(xs, g_mix, w_grp, d_scale[j],
                               _pad_hist(state_d_pool[j], _POOL_HALO), _D_HIST, tm_s)
            outs["dp_s"].append(hd[:, -_D_HIST:])
        wg, wu, wd = ffn_w_gate[i].astype(_BF), ffn_w_up[i].astype(_BF), ffn_w_down[i].astype(_BF)
        final_g = norm_final_g if i == depth - 1 else None
        xp, hf = _ffn(xp, norm_ffn_g[i], wg, wu, ffn_conv_w[i], ffn_conv_b[i], wd,
                      jnp.zeros((n_p, _SUBLANES, f), _F32), final_g, tm_p)
        outs["fc_p"].append(hf[:, -2:])
        xs, hf = _ffn(xs, norm_ffn_g[i], wg, wu, ffn_conv_w[i], ffn_conv_b[i], wd,
                      _pad_hist(state_ffn_conv[i], _SUBLANES), final_g, tm_s)
        outs["fc_s"].append(hf[:, -2:])
    st = lambda k: jnp.stack(outs[k])
    return (xp, xs, st("ak_p"), st("av_p"), st("bl_p"), st("br_p"), st("cc_p"), st("dp_p"), st("fc_p"),
            st("ak_s"), st("av_s"), st("bl_s"), st("br_s"), st("cc_s"), st("dp_s"), st("fc_s"))
```

```python
import functools
import math

import jax
import jax.numpy as jnp
from jax import lax
from jax.experimental import pallas as pl
from jax.experimental.pallas import tpu as pltpu

_BF = jnp.bfloat16
_F32 = jnp.float32

_CHUNK = 64
_ROPE_THETA = 10000.0
_NORM_EPS = 1e-6
_NEG_INF = -1e30
_A_HEADS = 8
_A_HD = 64
_A_SUBLN_EPS = 1e-5
_B_HEADS = 8
_B_NOPE = 128
_B_ROPE = 64
_B_VD = 128
_B_Q_LORA = 384
_B_KV_LORA = 256
_D_WINDOWS = (2, 4, 8, 16)
_D_HIST = 15
_N_MIXERS = 4
_LOG2E = math.log2(math.e)

_LANES = 128
_SUBLANES = 8
_MXU_COLS = 256
_VMEM_LIMIT = 56 * 1024 * 1024

_ROW_TILE = 512
_ATT_TILE = 512


def _params(sem):
    return pltpu.CompilerParams(dimension_semantics=sem, vmem_limit_bytes=_VMEM_LIMIT)


def _resident(shape):
    nd = len(shape)
    return pl.BlockSpec(shape, lambda *_: (0,) * nd, pipeline_mode=pl.Buffered(1))


def _rms(x, g, eps):
    ms = jnp.mean(x * x, axis=-1, keepdims=True)
    return x * lax.rsqrt(ms + eps) * g


def _rope_lanes(y, cos, sin_lo, sin_hi):
    return (y * cos + pltpu.roll(y, _LANES - 32, axis=1) * sin_lo
            + pltpu.roll(y, 32, axis=1) * sin_hi)


def _rope_tables(pos):
    d = _A_HD
    inv = jnp.power(_ROPE_THETA, -jnp.arange(0, d, 2, dtype=_F32) / d)
    ang = pos.astype(_F32)[:, None] * inv[None, :]
    cos, sin = jnp.cos(ang), jnp.sin(ang)
    zero = jnp.zeros_like(sin)
    cos_t = jnp.tile(cos, (1, 4))
    sin_lo = jnp.tile(jnp.concatenate([-sin, zero], axis=1), (1, 2))
    sin_hi = jnp.tile(jnp.concatenate([zero, sin], axis=1), (1, 2))
    return cos_t, sin_lo, sin_hi


def _shift_rows(carry, cur, k):
    ext = jnp.concatenate([carry, cur], axis=0)
    return pltpu.roll(ext, k, axis=0)[carry.shape[0]:]


def _mask_lanes(rows, row0, tile, base, for_keys):
    r = row0 + lax.broadcasted_iota(jnp.int32, (rows, _LANES), 0)
    chunk = (r % tile) // _CHUNK
    c = lax.broadcasted_iota(jnp.int32, (rows, _LANES), 1) - base
    n = tile // _CHUNK
    if for_keys:
        return jnp.where(c == chunk, 1.0, 0.0)
    return jnp.where(c > chunk, jnp.where(c < n, _NEG_INF, 0.0), 0.0)


def _diff_proj_body(x_ref, g_ref, w_ref, cos_ref, slo_ref, shi_ref,
                    qb_ref, kf_ref, kb_ref, vf_ref, vb_ref, *, qscale, mask_tile):
    tm, d = x_ref.shape
    h = _rms(x_ref[...], g_ref[...], _NORM_EPS).astype(_BF)
    cos, slo, shi = cos_ref[...], slo_ref[...], shi_ref[...]
    if mask_tile:
        row0 = pl.program_id(1) * tm
        q_ext = _mask_lanes(tm, row0, mask_tile, 0, False).astype(_BF)
        k_ext = _mask_lanes(tm, row0, mask_tile, 0, True).astype(_BF)
    else:
        q_ext = k_ext = jnp.zeros((tm, _LANES), _BF)
    cw = _MXU_COLS
    for c in range(0, d, cw):
        yq = jnp.dot(h, w_ref[:, c:c + cw], preferred_element_type=_F32)
        yk = jnp.dot(h, w_ref[:, d + c:d + c + cw], preferred_element_type=_F32)
        yv = jnp.dot(h, w_ref[:, 2 * d + c:2 * d + c + cw], preferred_element_type=_F32)
        for s in range(0, cw, _LANES):
            q = _rope_lanes(yq[:, s:s + _LANES], cos, slo, shi)
            k = _rope_lanes(yk[:, s:s + _LANES], cos, slo, shi)
            o = 2 * (c + s)
            qb_ref[:, o:o + _LANES] = (q * qscale).astype(_BF)
            qb_ref[:, o + _LANES:o + 2 * _LANES] = q_ext
            kf_ref[:, c + s:c + s + _LANES] = k
            kb_ref[:, o:o + _LANES] = k.astype(_BF)
            kb_ref[:, o + _LANES:o + 2 * _LANES] = k_ext
        vf_ref[:, c:c + cw] = yv
        vb_ref[:, c:c + cw] = yv.astype(_BF)


def _diff_proj(x, g, w_qkv, tabs, tm, mask_tile):
    b, s, d = x.shape
    row = lambda n, dt: jax.ShapeDtypeStruct((b, s, n), dt)
    xspec = pl.BlockSpec((None, tm, d), lambda i, t: (i, t, 0))
    wide = pl.BlockSpec((None, tm, 2 * d), lambda i, t: (i, t, 0))
    tspec = pl.BlockSpec((tm, _LANES), lambda i, t: (t, 0))
    return pl.pallas_call(
        functools.partial(_diff_proj_body, qscale=_LOG2E / math.sqrt(_A_HD), mask_tile=mask_tile),
        grid=(b, s // tm),
        in_specs=[xspec, _resident((1, d)), _resident(w_qkv.shape), tspec, tspec, tspec],
        out_specs=[wide, xspec, wide, xspec, xspec],
        out_shape=[row(2 * d, _BF), row(d, _F32), row(2 * d, _BF), row(d, _F32), row(d, _BF)],
        compiler_params=_params(("parallel", "parallel")),
        name="diff_proj",
    )(x, g.reshape(1, d), w_qkv, *tabs)


def _flash_body(*refs, nmaps, causal, tq, tk, lam_init):
    if nmaps == 2:
        q_ref, k_ref, v_ref, lam_ref, sg_ref, o_ref, m_sc, l_sc, acc_sc = refs
    else:
        q_ref, k_ref, v_ref, o_ref, m_sc, l_sc, acc_sc = refs
    qi, ki = pl.program_id(2), pl.program_id(3)
    nk = pl.num_programs(3)

    @pl.when(ki == 0)
    def _():
        m_sc[...] = jnp.full(m_sc.shape, _NEG_INF, _F32)
        l_sc[...] = jnp.zeros(l_sc.shape, _F32)
        acc_sc[...] = jnp.zeros(acc_sc.shape, _F32)

    def step(masked):
        q, k, v = q_ref[...], k_ref[...], v_ref[...]
        if masked:
            rows = lax.broadcasted_iota(jnp.int32, (tq, tk), 0) // _CHUNK
            cols = lax.broadcasted_iota(jnp.int32, (tq, tk), 1) // _CHUNK
            visible = cols <= rows
        if nmaps == 2:
            lane = lax.broadcasted_iota(jnp.int32, q.shape, 1)
        for c in range(nmaps):
            qc = q if nmaps == 1 else jnp.where((lane >= _A_HD) == (c == 1), q, jnp.zeros_like(q))
            s = lax.dot_general(qc, k, (((1,), (1,)), ((), ())), preferred_element_type=_F32)
            if masked:
                s = jnp.where(visible, s, _NEG_INF)
            m_prev = m_sc[c]
            m_new = jnp.maximum(m_prev, jnp.max(s, axis=-1, keepdims=True))
            alpha = jnp.exp2(m_prev - m_new)
            p = jnp.exp2(s - m_new)
            l_sc[c] = alpha * l_sc[c] + jnp.sum(p, axis=-1, keepdims=True)
            acc_sc[c] = alpha * acc_sc[c] + jnp.dot(p.astype(_BF), v, preferred_element_type=_F32)
            m_sc[c] = m_new

    if causal:
        pl.when(ki < qi)(lambda: step(False))
        pl.when(ki == qi)(lambda: step(True))
    else:
        step(False)

    @pl.when(ki == nk - 1)
    def _():
        if nmaps == 2:
            lp = lam_ref[...]
            lam = (jnp.exp(jnp.sum(lp[0:1] * lp[1:2], axis=-1, keepdims=True))
                   - jnp.exp(jnp.sum(lp[2:3] * lp[3:4], axis=-1, keepdims=True)) + lam_init)
            o = acc_sc[0] / l_sc[0] - lam * (acc_sc[1] / l_sc[1])
            o = _rms(o, sg_ref[...], _A_SUBLN_EPS) * (1.0 - lam_init)
        else:
            o = acc_sc[0] / l_sc[0]
        o_ref[...] = o.astype(o_ref.dtype)


def _flash(q, k, v, heads, dq, dv, *, causal, tq, tk, nmaps=1, lam=None, subln_g=None, lam_init=0.0):
    b, sq, _ = q.shape
    sk = k.shape[1]
    nq, nk = sq // tq, sk // tk
    if causal:
        assert tq == tk and tq % _CHUNK == 0 and sq == sk
        kmap = lambda i, h, qi, ki: (i, jnp.minimum(ki, qi), h)
    else:
        kmap = lambda i, h, qi, ki: (i, ki, h)
    in_specs = [pl.BlockSpec((None, tq, dq), lambda i, h, qi, ki: (i, qi, h)),
                pl.BlockSpec((None, tk, dq), kmap),
                pl.BlockSpec((None, tk, dv), kmap)]
    args = [q, k, v]
    if nmaps == 2:
        in_specs += [_resident(lam.shape), _resident((1, dv))]
        args += [lam, subln_g.reshape(1, dv)]
    return pl.pallas_call(
        functools.partial(_flash_body, nmaps=nmaps, causal=causal, tq=tq, tk=tk, lam_init=lam_init),
        grid=(b, heads, nq, nk),
        in_specs=in_specs,
        out_specs=pl.BlockSpec((None, tq, dv), lambda i, h, qi, ki: (i, qi, h)),
        out_shape=jax.ShapeDtypeStruct((b, sq, heads * dv), _BF),
        scratch_shapes=[pltpu.VMEM((nmaps, tq, 1), _F32), pltpu.VMEM((nmaps, tq, 1), _F32),
                        pltpu.VMEM((nmaps, tq, dv), _F32)],
        compiler_params=_params(("parallel", "parallel", "parallel", "arbitrary")),
        name="flash_diff" if nmaps == 2 else "flash_mla",
    )(*args)


def _sweep_body(*refs, nmaps, halves, tile, nb, ext0, lam_init):
    if nmaps == 2:
        (q_ref, k_ref, vt_ref, lam_ref, sg_ref, o_ref,
         s_buf, p_buf, a_buf, l_buf, m_sc, l_sc, acc_sc) = refs
    else:
        q_ref, k_ref, vt_ref, o_ref, s_buf, p_buf, a_buf, l_buf, m_sc, l_sc, acc_sc = refs
    hw = tile // halves
    nch = nmaps * halves
    dq = q_ref.shape[-1]
    npairs = nb * (nb + 1) // 2
    acc_sc[...] = jnp.zeros(acc_sc.shape, _F32)

    def nxt(pair):
        qi, t = pair
        last = t == qi
        return jnp.where(last, qi + 1, qi), jnp.where(last, 0, t + 1)

    def scores(pair, slot):
        qi, t = pair
        k = k_ref[pl.ds(pl.multiple_of(t * tile, tile), tile), :]
        lane = lax.broadcasted_iota(jnp.int32, (1, dq), 1)
        onehot_lane = (lane >= ext0) & (lane < ext0 + tile // _CHUNK)
        keep = jnp.where(onehot_lane, jnp.where(t == qi, 1.0, 0.0), 1.0).astype(_BF)
        k = k * keep
        for c in range(nch):
            mp, hf = divmod(c, halves)
            q = q_ref[pl.ds(pl.multiple_of(qi * tile + hf * hw, hw), hw), :]
            if nmaps == 2:
                ql = lax.broadcasted_iota(jnp.int32, q.shape, 1)
                other = (ql >= _A_HD) & (ql < 2 * _A_HD) if mp == 0 else ql < _A_HD
                q = jnp.where(other, jnp.zeros_like(q), q)
            s_buf[slot, c] = lax.dot_general(k, q, (((1,), (1,)), ((), ())),
                                             preferred_element_type=_F32)

    def probs(pair, slot):
        _, t = pair
        first = t == 0
        for c in range(nch):
            s = s_buf[slot, c]
            m_prev = jnp.where(first, _NEG_INF, m_sc[c])
            l_prev = jnp.where(first, 0.0, l_sc[c])
            m_new = jnp.maximum(m_prev, jnp.max(s, axis=0, keepdims=True))
            alpha = jnp.exp2(m_prev - m_new)
            p = jnp.exp2(s - m_new)
            l_new = alpha * l_prev + jnp.sum(p, axis=0, keepdims=True)
            m_sc[c] = m_new
            l_sc[c] = l_new
            p_buf[slot, c] = p.astype(_BF)
            a_buf[slot, c] = alpha
            l_buf[slot, c] = l_new

    def values(pair, slot):
        qi, t = pair
        vt = vt_ref[t]
        for c in range(nch):
            acc_sc[c] = a_buf[slot, c] * acc_sc[c] + jnp.dot(vt, p_buf[slot, c],
                                                             preferred_element_type=_F32)

    def finish(pair, slot):
        qi, t = pair

        @pl.when(t == qi)
        def _():
            if nmaps == 2:
                lp = lam_ref[...]
                lam = (jnp.exp(jnp.sum(lp[0:1] * lp[1:2], axis=-1, keepdims=True))
                       - jnp.exp(jnp.sum(lp[2:3] * lp[3:4], axis=-1, keepdims=True)) + lam_init)
            for hf in range(halves):
                if nmaps == 2:
                    ot = (acc_sc[hf] / l_buf[slot, hf]
                          - lam * (acc_sc[halves + hf] / l_buf[slot, halves + hf]))
                    o = _rms(ot.T, sg_ref[...], _A_SUBLN_EPS) * (1.0 - lam_init)
                else:
                    o = (acc_sc[hf] / l_buf[slot, hf]).T
                row = pl.multiple_of(qi * tile + hf * hw, hw)
                o_ref[pl.ds(row, hw), :] = o.astype(o_ref.dtype)

    zero = jnp.int32(0)
    pair0 = (zero, zero)
    if npairs == 1:
        scores(pair0, 0)
        probs(pair0, 0)
        values(pair0, 0)
        finish(pair0, 0)
        return
    pair1 = nxt(pair0)
    scores(pair0, 0)
    probs(pair0, 0)
    scores(pair1, 1)

    def tick(carry, slot):
        pc, pb, pa = carry
        values(pc, slot)
        probs(pb, 1 - slot)
        scores(pa, slot)
        finish(pc, slot)
        return pb, pa, nxt(pa)

    carry = lax.fori_loop(0, (npairs - 2) // 2, lambda _, c: tick(tick(c, 0), 1),
                          (pair0, pair1, nxt(pair1)))
    if (npairs - 2) % 2:
        tick(carry, 0)
    last = (jnp.int32(nb - 1), jnp.int32(nb - 1))
    before_last = (jnp.int32(nb - 1), jnp.int32(nb - 2))
    values(before_last, npairs % 2)
    finish(before_last, npairs % 2)
    probs(last, (npairs - 1) % 2)
    values(last, (npairs - 1) % 2)
    finish(last, (npairs - 1) % 2)


def _sweep(q, k, v, heads, dq, dv, *, tile, ext0, halves=2, nmaps=1, lam=None, subln_g=None,
           lam_init=0.0):
    b, s, _ = q.shape
    nb = s // tile
    assert tile % _CHUNK == 0 and tile % halves == 0 and ext0 + tile // _CHUNK <= dq
    vt = v.reshape(b, nb, tile, heads, dv).transpose(0, 3, 1, 4, 2)
    nch = nmaps * halves
    hw = tile // halves
    once = pl.Buffered(1)
    in_specs = [pl.BlockSpec((None, s, dq), lambda i, h: (i, 0, h), pipeline_mode=once),
                pl.BlockSpec((None, s, dq), lambda i, h: (i, 0, h), pipeline_mode=once),
                pl.BlockSpec((None, None, nb, dv, tile), lambda i, h: (i, h, 0, 0, 0),
                             pipeline_mode=once)]
    args = [q, k, vt]
    if nmaps == 2:
        in_specs += [_resident(lam.shape), _resident((1, dv))]
        args += [lam, subln_g.reshape(1, dv)]
    scratch = [pltpu.VMEM((2, nch, tile, hw), _F32), pltpu.VMEM((2, nch, tile, hw), _BF),
               pltpu.VMEM((2, nch, 1, hw), _F32), pltpu.VMEM((2, nch, 1, hw), _F32),
               pltpu.VMEM((nch, 1, hw), _F32), pltpu.VMEM((nch, 1, hw), _F32),
               pltpu.VMEM((nch, dv, hw), _F32)]
    return pl.pallas_call(
        functools.partial(_sweep_body, nmaps=nmaps, halves=halves, tile=tile, nb=nb, ext0=ext0,
                          lam_init=lam_init),
        grid=(b, heads),
        in_specs=in_specs,
        out_specs=pl.BlockSpec((None, s, dv), lambda i, h: (i, 0, h)),
        out_shape=jax.ShapeDtypeStruct((b, s, heads * dv), _BF),
        scratch_shapes=scratch,
        compiler_params=_params(("parallel", "parallel")),
        name="sweep_diff" if nmaps == 2 else "sweep_mla",
    )(*args)


def _mm_body(*refs, has_res):
    if has_res:
        a_ref, w_ref, r_ref, o_ref = refs
    else:
        a_ref, w_ref, o_ref = refs
    y = jnp.dot(a_ref[...].astype(_BF), w_ref[...], preferred_element_type=_F32)
    if has_res:
        y = r_ref[...] + y
    o_ref[...] = y.astype(o_ref.dtype)


def _mm(a, w, res=None, out_dtype=_F32):
    b, s, kdim = a.shape
    n = w.shape[1]
    tm = min(_ROW_TILE, s)
    in_specs = [pl.BlockSpec((None, tm, kdim), lambda i, t: (i, t, 0)), _resident(w.shape)]
    args = [a, w]
    if res is not None:
        in_specs.append(pl.BlockSpec((None, tm, n), lambda i, t: (i, t, 0)))
        args.append(res)
    return pl.pallas_call(
        functools.partial(_mm_body, has_res=res is not None),
        grid=(b, s // tm),
        in_specs=in_specs,
        out_specs=pl.BlockSpec((None, tm, n), lambda i, t: (i, t, 0)),
        out_shape=jax.ShapeDtypeStruct((b, s, n), out_dtype),
        compiler_params=_params(("parallel", "parallel")),
        name="matmul_residual" if res is not None else "matmul",
    )(*args)


def _mla_proj_body(x_ref, g_ref, wd_ref, qg_ref, wq_ref, kvg_ref, wk_ref, wv_ref,
                   cos_ref, slo_ref, shi_ref, lat_ref, kr_ref, q_ref, k_ref, v_ref, *, qscale, mask_tile):
    tm = x_ref.shape[0]
    h = _rms(x_ref[...], g_ref[...], _NORM_EPS).astype(_BF)
    cos, slo, shi = cos_ref[...], slo_ref[...], shi_ref[...]
    if mask_tile:
        row0 = pl.program_id(1) * tm
        q_ext = _mask_lanes(tm, row0, mask_tile, _B_ROPE, False)
        k_ext = _mask_lanes(tm, row0, mask_tile, _B_ROPE, True)
    else:
        q_ext = k_ext = jnp.zeros((tm, _LANES), _F32)
    dn = jnp.dot(h, wd_ref[...], preferred_element_type=_F32)
    e0, e1 = _B_Q_LORA, _B_Q_LORA + _B_KV_LORA
    cq = _rms(dn[:, :e0], qg_ref[...], _NORM_EPS).astype(_BF)
    lat = _rms(dn[:, e0:e1], kvg_ref[...], _NORM_EPS)
    kr = _rope_lanes(dn[:, e1:e1 + _LANES], cos, slo, shi)
    lat_ref[...] = lat
    kr_ref[...] = kr[:, :_B_ROPE]
    hw = 2 * _LANES
    for hd in range(_B_HEADS):
        yq = jnp.dot(cq, wq_ref[:, hd * hw:(hd + 1) * hw], preferred_element_type=_F32)
        q_ref[:, hd * hw:hd * hw + _LANES] = (yq[:, :_LANES] * qscale).astype(_BF)
        q_ref[:, hd * hw + _LANES:(hd + 1) * hw] = (
            _rope_lanes(yq[:, _LANES:], cos, slo, shi) * qscale + q_ext).astype(_BF)
    kin = jnp.concatenate([lat, kr], axis=1).astype(_BF)
    for hd in range(_B_HEADS):
        yk = jnp.dot(kin, wk_ref[:, hd * hw:(hd + 1) * hw], preferred_element_type=_F32)
        k_ref[:, hd * hw:hd * hw + _LANES] = yk[:, :_LANES].astype(_BF)
        k_ref[:, hd * hw + _LANES:(hd + 1) * hw] = (yk[:, _LANES:] + k_ext).astype(_BF)
    latb = lat.astype(_BF)
    for c in range(0, v_ref.shape[-1], _MXU_COLS):
        v_ref[:, c:c + _MXU_COLS] = jnp.dot(
            latb, wv_ref[:, c:c + _MXU_COLS], preferred_element_type=_F32).astype(_BF)


def _mla_weights(w_down, w_uq, w_uk, w_uv):
    d = w_down.shape[0]
    hw = 2 * _LANES
    wd = jnp.concatenate([w_down, jnp.zeros((d, _LANES - _B_ROPE), w_down.dtype)], axis=1)
    wq = w_uq.reshape(_B_Q_LORA, _B_HEADS, _B_NOPE + _B_ROPE)
    wq = jnp.pad(wq, ((0, 0), (0, 0), (0, hw - _B_NOPE - _B_ROPE))).reshape(_B_Q_LORA, _B_HEADS * hw)
    eye = jnp.eye(_B_ROPE, dtype=w_uk.dtype)
    wk_top = jnp.pad(w_uk, ((0, 0), (0, 0), (0, hw - _B_NOPE)))
    wk_mid = jnp.broadcast_to(jnp.pad(eye, ((0, 0), (_B_NOPE, hw - _B_NOPE - _B_ROPE)))[:, None, :],
                              (_B_ROPE, _B_HEADS, hw))
    wk_bot = jnp.zeros((_LANES - _B_ROPE, _B_HEADS, hw), w_uk.dtype)
    wk = jnp.concatenate([wk_top, wk_mid, wk_bot], axis=0).reshape(-1, _B_HEADS * hw)
    wv = w_uv.reshape(_B_KV_LORA, _B_HEADS * _B_VD)
    return wd.astype(_BF), wq.astype(_BF), wk.astype(_BF), wv.astype(_BF)


def _mla_proj(x, g, wts, q_norm_g, kv_norm_g, tabs, tm, mask_tile):
    b, s, d = x.shape
    wd, wq, wk, wv = wts
    xspec = pl.BlockSpec((None, tm, d), lambda i, t: (i, t, 0))
    tspec = pl.BlockSpec((tm, _LANES), lambda i, t: (t, 0))
    ospec = lambda n: pl.BlockSpec((None, tm, n), lambda i, t: (i, t, 0))
    oshape = lambda n, dt: jax.ShapeDtypeStruct((b, s, n), dt)
    return pl.pallas_call(
        functools.partial(_mla_proj_body, qscale=_LOG2E / math.sqrt(_B_NOPE + _B_ROPE),
                          mask_tile=mask_tile),
        grid=(b, s // tm),
        in_specs=[xspec, _resident((1, d)), _resident(wd.shape), _resident((1, _B_Q_LORA)),
                  _resident(wq.shape), _resident((1, _B_KV_LORA)), _resident(wk.shape),
                  _resident(wv.shape), tspec, tspec, tspec],
        out_specs=[ospec(_B_KV_LORA), ospec(_B_ROPE), ospec(wq.shape[1]), ospec(wk.shape[1]),
                   ospec(wv.shape[1])],
        out_shape=[oshape(_B_KV_LORA, _F32), oshape(_B_ROPE, _F32), oshape(wq.shape[1], _BF),
                   oshape(wk.shape[1], _BF), oshape(wv.shape[1], _BF)],
        compiler_params=_params(("parallel", "parallel")),
        name="mla_proj",
    )(x, g.reshape(1, d), wd, q_norm_g.reshape(1, -1), wq, kv_norm_g.reshape(1, -1), wk, wv, *tabs)


def _conv_mix_body(x_ref, g_ref, win_ref, cw_ref, wout_ref, hist_ref, o_ref, hout_ref,
                   carry_sc, z_sc):
    tm, d = x_ref.shape

    @pl.when(pl.program_id(1) == 0)
    def _():
        carry_sc[...] = hist_ref[...]

    x = x_ref[...]
    h = _rms(x, g_ref[...], _NORM_EPS).astype(_BF)
    cw = _MXU_COLS
    for c in range(0, d, cw):
        gate_b = jnp.dot(h, win_ref[:, c:c + cw], preferred_element_type=_F32)
        gate_c = jnp.dot(h, win_ref[:, d + c:d + c + cw], preferred_element_type=_F32)
        val = jnp.dot(h, win_ref[:, 2 * d + c:2 * d + c + cw], preferred_element_type=_F32)
        u = gate_c * val
        prev = carry_sc[:, c:c + cw]
        y = (_shift_rows(prev, u, 2) * cw_ref[0:1, c:c + cw]
             + _shift_rows(prev, u, 1) * cw_ref[1:2, c:c + cw]
             + u * cw_ref[2:3, c:c + cw])
        z_sc[:, c:c + cw] = (gate_b * y).astype(_BF)
        carry_sc[:, c:c + cw] = u[tm - _SUBLANES:, :]
    hout_ref[...] = carry_sc[...]
    o_ref[...] = x + jnp.dot(z_sc[...], wout_ref[...], preferred_element_type=_F32)


def _conv_mix(x, g, w_in, conv_w, w_out, hist8, tm):
    b, s, d = x.shape
    xspec = pl.BlockSpec((None, tm, d), lambda i, t: (i, t, 0))
    hspec = pl.BlockSpec((None, _SUBLANES, d), lambda i, t: (i, 0, 0))
    return pl.pallas_call(
        _conv_mix_body,
        grid=(b, s // tm),
        in_specs=[xspec, _resident((1, d)), _resident(w_in.shape), _resident(conv_w.shape),
                  _resident(w_out.shape), hspec],
        out_specs=[xspec, hspec],
        out_shape=[jax.ShapeDtypeStruct((b, s, d), _F32),
                   jax.ShapeDtypeStruct((b, _SUBLANES, d), _F32)],
        scratch_shapes=[pltpu.VMEM((_SUBLANES, d), _F32), pltpu.VMEM((tm, d), _BF)],
        compiler_params=_params(("arbitrary", "arbitrary")),
        name="conv_mixer",
    )(x, g.reshape(1, d), w_in, conv_w, w_out, hist8)


_POOL_HALO = 16


def _pool_mix_body(x_ref, g_ref, wg_ref, sc_ref, hist_ref, o_ref, hout_ref, carry_sc, *, hist_valid):
    tm, d = x_ref.shape
    t = pl.program_id(1)

    @pl.when(t == 0)
    def _():
        carry_sc[...] = hist_ref[...]

    x = x_ref[...]
    h = _rms(x, g_ref[...], _NORM_EPS)
    gw = d // len(_D_WINDOWS)
    tpos = t * tm + lax.broadcasted_iota(jnp.int32, (tm, gw), 0)
    outs = []
    for gi, w in enumerate(_D_WINDOWS):
        hg = h[:, gi * gw:(gi + 1) * gw]
        acc = jnp.concatenate([carry_sc[:, gi * gw:(gi + 1) * gw], hg], axis=0)
        k = 1
        while k < w:
            acc = acc + pltpu.roll(acc, k, axis=0)
            k *= 2
        cnt = jnp.minimum(tpos + (hist_valid + 1), w).astype(_F32)
        pooled = acc[_POOL_HALO:] / cnt
        outs.append(jnp.dot((pooled - hg).astype(_BF), wg_ref[gi], preferred_element_type=_F32))
    o_ref[...] = x + jnp.concatenate(outs, axis=1) * sc_ref[...]
    tail = h[tm - _POOL_HALO:, :]
    carry_sc[...] = tail
    hout_ref[...] = tail


def _pool_mix(x, g, w_group, scale, hist16, hist_valid, tm):
    b, s, d = x.shape
    assert all(w & (w - 1) == 0 and w <= _POOL_HALO for w in _D_WINDOWS)
    xspec = pl.BlockSpec((None, tm, d), lambda i, t: (i, t, 0))
    hspec = pl.BlockSpec((None, _POOL_HALO, d), lambda i, t: (i, 0, 0))
    return pl.pallas_call(
        functools.partial(_pool_mix_body, hist_valid=hist_valid),
        grid=(b, s // tm),
        in_specs=[xspec, _resident((1, d)), _resident(w_group.shape), _resident((1, d)), hspec],
        out_specs=[xspec, hspec],
        out_shape=[jax.ShapeDtypeStruct((b, s, d), _F32),
                   jax.ShapeDtypeStruct((b, _POOL_HALO, d), _F32)],
        scratch_shapes=[pltpu.VMEM((_POOL_HALO, d), _F32)],
        compiler_params=_params(("arbitrary", "arbitrary")),
        name="pool_mixer",
    )(x, g.reshape(1, d), w_group, scale.reshape(1, d), hist16)


def _ffn_body(*refs, final):
    if final:
        (x_ref, g_ref, wg_ref, wu_ref, cw_ref, cb_ref, wd_ref, hist_ref, fg_ref,
         o_ref, hout_ref, carry_sc, act_sc) = refs
    else:
        (x_ref, g_ref, wg_ref, wu_ref, cw_ref, cb_ref, wd_ref, hist_ref,
         o_ref, hout_ref, carry_sc, act_sc) = refs
    tm = x_ref.shape[0]
    f = wg_ref.shape[1]

    @pl.when(pl.program_id(1) == 0)
    def _():
        carry_sc[...] = hist_ref[...]

    x = x_ref[...]
    h = _rms(x, g_ref[...], _NORM_EPS).astype(_BF)
    cw = _MXU_COLS
    for c in range(0, f, cw):
        gate = jnp.dot(h, wg_ref[:, c:c + cw], preferred_element_type=_F32)
        up = jnp.dot(h, wu_ref[:, c:c + cw], preferred_element_type=_F32)
        prev = carry_sc[:, c:c + cw]
        y = (_shift_rows(prev, gate, 2) * cw_ref[0:1, c:c + cw]
             + _shift_rows(prev, gate, 1) * cw_ref[1:2, c:c + cw]
             + gate * cw_ref[2:3, c:c + cw] + cb_ref[:, c:c + cw])
        act_sc[:, c:c + cw] = (y / (1.0 + jnp.exp(-y)) * up).astype(_BF)
        carry_sc[:, c:c + cw] = gate[tm - _SUBLANES:, :]
    hout_ref[...] = carry_sc[...]
    out = x + jnp.dot(act_sc[...], wd_ref[...], preferred_element_type=_F32)
    if final:
        out = _rms(out, fg_ref[...], _NORM_EPS)
    o_ref[...] = out


def _ffn(x, g, w_gate, w_up, conv_w, conv_b, w_down, hist8, final_g, tm):
    b, s, d = x.shape
    f = w_gate.shape[1]
    assert f % _MXU_COLS == 0
    xspec = pl.BlockSpec((None, tm, d), lambda i, t: (i, t, 0))
    hspec = pl.BlockSpec((None, _SUBLANES, f), lambda i, t: (i, 0, 0))
    in_specs = [xspec, _resident((1, d)), _resident(w_gate.shape), _resident(w_up.shape),
                _resident(conv_w.shape), _resident((1, f)), _resident(w_down.shape), hspec]
    args = [x, g.reshape(1, d), w_gate, w_up, conv_w, conv_b.reshape(1, f), w_down, hist8]
    if final_g is not None:
        in_specs.append(_resident((1, d)))
        args.append(final_g.reshape(1, d))
    return pl.pallas_call(
        functools.partial(_ffn_body, final=final_g is not None),
        grid=(b, s // tm),
        in_specs=in_specs,
        out_specs=[xspec, hspec],
        out_shape=[jax.ShapeDtypeStruct((b, s, d), _F32),
                   jax.ShapeDtypeStruct((b, _SUBLANES, f), _F32)],
        scratch_shapes=[pltpu.VMEM((_SUBLANES, f), _F32), pltpu.VMEM((tm, f), _BF)],
        compiler_params=_params(("arbitrary", "arbitrary")),
        name="conv_ffn",
    )(*args)


def _pad_hist(hist, rows):
    return jnp.pad(hist, ((0, 0), (rows - hist.shape[1], 0), (0, 0)))


def kernel(x_prompt, x_sample, cache_a_k, cache_a_v, cache_b_latent, cache_b_krope, state_c_conv, state_d_pool, state_ffn_conv, norm_mix_g, norm_ffn_g, norm_final_g, a_w_qkv, a_lam, a_subln_g, a_w_o, b_w_down, b_q_norm_g, b_w_uq, b_kv_norm_g, b_w_uk, b_w_uv, b_w_o, c_w_in, c_conv_w, c_w_out, d_w_group, d_scale, ffn_w_gate, ffn_w_up, ffn_conv_w, ffn_conv_b, ffn_w_down):
    depth = norm_mix_g.shape[0]
    n_p, seq, d = x_prompt.shape
    n_s, t_new, _ = x_sample.shape
    past = cache_a_k.shape[2]
    f = ffn_w_gate.shape[-1]
    tm_p, tm_s = min(_ROW_TILE, seq), t_new
    ta = min(_ATT_TILE, seq)
    tabs_p = _rope_tables(jnp.arange(seq, dtype=jnp.int32))
    tabs_s = _rope_tables(past + jnp.arange(t_new, dtype=jnp.int32))
    xp, xs = x_prompt, x_sample
    outs = {k: [] for k in ("ak_p", "av_p", "bl_p", "br_p", "cc_p", "dp_p", "fc_p",
                            "ak_s", "av_s", "bl_s", "br_s", "cc_s", "dp_s", "fc_s")}
    for i in range(depth):
        m, j = i % _N_MIXERS, i // _N_MIXERS
        g_mix = norm_mix_g[i]
        if m == 0:
            lam_init = 0.8 - 0.6 * math.exp(-0.3 * i)
            w_qkv, w_o = a_w_qkv[j].astype(_BF), a_w_o[j].astype(_BF)
            hq = 2 * _LANES
            qb, kf, kb, vf, vb = _diff_proj(xp, g_mix, w_qkv, tabs_p, tm_p, ta)
            op = _sweep(qb, kb, vb, _A_HEADS, hq, 2 * _A_HD, tile=ta, ext0=2 * _A_HD, nmaps=2,
                        lam=a_lam[j], subln_g=a_subln_g[j], lam_init=lam_init)
            xp = _mm(op, w_o, res=xp)
            outs["ak_p"].append(kf.reshape(n_p, seq, _A_HEADS, 2, _A_HD))
            outs["av_p"].append(vf.reshape(n_p, seq, _A_HEADS, 2 * _A_HD))
            attn = functools.partial(_flash, heads=_A_HEADS, dq=hq, dv=2 * _A_HD, nmaps=2,
                                     lam=a_lam[j], subln_g=a_subln_g[j], lam_init=lam_init)
            qb, kf, kb, vf, vb = _diff_proj(xs, g_mix, w_qkv, tabs_s, tm_s, 0)
            k_cache = jnp.pad(cache_a_k[j].reshape(n_s, past, _A_HEADS, 2 * _A_HD).astype(_BF),
                              ((0, 0), (0, 0), (0, 0), (0, hq - 2 * _A_HD)))
            k_all = jnp.concatenate([k_cache.reshape(n_s, past, _A_HEADS * hq), kb], axis=1)
            v_all = jnp.concatenate([cache_a_v[j].reshape(n_s, past, d).astype(_BF), vb], axis=1)
            osm = attn(qb, k_all, v_all, causal=False, tq=t_new, tk=past + t_new)
            xs = _mm(osm, w_o, res=xs)
            outs["ak_s"].append(kf.reshape(n_s, t_new, _A_HEADS, 2, _A_HD))
            outs["av_s"].append(vf.reshape(n_s, t_new, _A_HEADS, 2 * _A_HD))
        elif m == 1:
            wts = _mla_weights(b_w_down[j], b_w_uq[j], b_w_uk[j], b_w_uv[j])
            w_o = b_w_o[j].astype(_BF)
            hq = 2 * _LANES
            attn = functools.partial(_flash, heads=_B_HEADS, dq=hq, dv=_B_VD)
            lat, kr, q, kfull, v = _mla_proj(xp, g_mix, wts, b_q_norm_g[j], b_kv_norm_g[j], tabs_p,
                                             tm_p, ta)
            op = _sweep(q, kfull, v, _B_HEADS, hq, _B_VD, tile=ta, ext0=_B_NOPE + _B_ROPE)
            xp = _mm(op, w_o, res=xp)
            outs["bl_p"].append(lat)
            outs["br_p"].append(kr)
            lat, kr, q, kfull, v = _mla_proj(xs, g_mix, wts, b_q_norm_g[j], b_kv_norm_g[j], tabs_s,
                                             tm_s, 0)
            kin = jnp.concatenate([cache_b_latent[j], cache_b_krope[j],
                                   jnp.zeros((n_s, past, _LANES - _B_ROPE), _F32)], axis=-1).astype(_BF)
            k_cache = _mm(kin, wts[2], out_dtype=_BF)
            v_cache = _mm(cache_b_latent[j].astype(_BF), wts[3], out_dtype=_BF)
            k_all = jnp.concatenate([k_cache, kfull], axis=1)
            v_all = jnp.concatenate([v_cache, v], axis=1)
            osm = attn(q, k_all, v_all, causal=False, tq=t_new, tk=past + t_new)
            xs = _mm(osm, w_o, res=xs)
            outs["bl_s"].append(lat)
            outs["br_s"].append(kr)
        elif m == 2:
            w_in, w_out = c_w_in[j].astype(_BF), c_w_out[j].astype(_BF)
            xp, hc = _conv_mix(xp, g_mix, w_in, c_conv_w[j], w_out,
                               jnp.zeros((n_p, _SUBLANES, d), _F32), tm_p)
            outs["cc_p"].append(hc[:, -2:])
            xs, hc = _conv_mix(xs, g_mix, w_in, c_conv_w[j], w_out,
                               _pad_hist(state_c_conv[j], _SUBLANES), tm_s)
            outs["cc_s"].append(hc[:, -2:])
        else:
            w_grp = d_w_group[j].astype(_BF)
            xp, hd = _pool_mix(xp, g_mix, w_grp, d_scale[j],
                               jnp.zeros((n_p, _POOL_HALO, d), _F32), 0, tm_p)
            outs["dp_p"].append(hd[:, -_D_HIST:])
            xs, hd = _pool_mix(xs, g_mix, w_grp, d_scale[j],
                               _pad_hist(state_d_pool[j], _POOL_HALO), _D_HIST, tm_s)
            outs["dp_s"].append(hd[:, -_D_HIST:])
        wg, wu, wd = ffn_w_gate[i].astype(_BF), ffn_w_up[i].astype(_BF), ffn_w_down[i].astype(_BF)
        final_g = norm_final_g if i == depth - 1 else None
        xp, hf = _ffn(xp, norm_ffn_g[i], wg, wu, ffn_conv_w[i], ffn_conv_b[i], wd,
                      jnp.zeros((n_p, _SUBLANES, f), _F32), final_g, tm_p)
        outs["fc_p"].append(hf[:, -2:])
        xs, hf = _ffn(xs, norm_ffn_g[i], wg, wu, ffn_conv_w[i], ffn_conv_b[i], wd,
                      _pad_hist(state_ffn_conv[i], _SUBLANES), final_g, tm_s)
        outs["fc_s"].append(hf[:, -2:])
    st = lambda k: jnp.stack(outs[k])
    return (xp, xs, st("ak_p"), st("av_p"), st("bl_p"), st("br_p"), st("cc_p"), st("dp_p"), st("fc_p"),
            st("ak_s"), st("av_s"), st("bl_s"), st("br_s"), st("cc_s"), st("dp_s"), st("fc_s"))
```

```python
import functools
import math

import jax
import jax.numpy as jnp
from jax import lax
from jax.experimental import pallas as pl
from jax.experimental.pallas import tpu as pltpu

_BF = jnp.bfloat16
_F32 = jnp.float32

_CHUNK = 64
_ROPE_THETA = 10000.0
_NORM_EPS = 1e-6
_NEG_INF = -1e30
_A_HEADS = 8
_A_HD = 64
_A_SUBLN_EPS = 1e-5
_B_HEADS = 8
_B_NOPE = 128
_B_ROPE = 64
_B_VD = 128
_B_Q_LORA = 384
_B_KV_LORA = 256
_D_WINDOWS = (2, 4, 8, 16)
_D_HIST = 15
_N_MIXERS = 4
_LOG2E = math.log2(math.e)

_LANES = 128
_SUBLANES = 8
_MXU_COLS = 256
_BF_ROWS = 16
_SWEEP_SLOTS = 4
_VMEM_LIMIT = 56 * 1024 * 1024

_ROW_TILE = 512
_ATT_TILE = 512


def _params(sem):
    return pltpu.CompilerParams(dimension_semantics=sem, vmem_limit_bytes=_VMEM_LIMIT)


def _resident(shape):
    nd = len(shape)
    return pl.BlockSpec(shape, lambda *_: (0,) * nd, pipeline_mode=pl.Buffered(1))


def _rms(x, g, eps):
    ms = jnp.mean(x * x, axis=-1, keepdims=True)
    return x * lax.rsqrt(ms + eps) * g


def _rope_lanes(y, cos, sin_lo, sin_hi):
    return (y * cos + pltpu.roll(y, _LANES - 32, axis=1) * sin_lo
            + pltpu.roll(y, 32, axis=1) * sin_hi)


def _rope_tables(pos):
    d = _A_HD
    inv = jnp.power(_ROPE_THETA, -jnp.arange(0, d, 2, dtype=_F32) / d)
    ang = pos.astype(_F32)[:, None] * inv[None, :]
    cos, sin = jnp.cos(ang), jnp.sin(ang)
    zero = jnp.zeros_like(sin)
    cos_t = jnp.tile(cos, (1, 4))
    sin_lo = jnp.tile(jnp.concatenate([-sin, zero], axis=1), (1, 2))
    sin_hi = jnp.tile(jnp.concatenate([zero, sin], axis=1), (1, 2))
    return cos_t, sin_lo, sin_hi


def _shift_rows(carry, cur, k):
    ext = jnp.concatenate([carry, cur], axis=0)
    return pltpu.roll(ext, k, axis=0)[carry.shape[0]:]


def _mask_lanes(rows, row0, tile, base, for_keys):
    r = row0 + lax.broadcasted_iota(jnp.int32, (rows, _LANES), 0)
    chunk = (r % tile) // _CHUNK
    c = lax.broadcasted_iota(jnp.int32, (rows, _LANES), 1) - base
    n = tile // _CHUNK
    if for_keys:
        return jnp.where(c == chunk, 1.0, 0.0)
    return jnp.where(c > chunk, jnp.where(c < n, _NEG_INF, 0.0), 0.0)


def _diff_proj_body(x_ref, g_ref, w_ref, cos_ref, slo_ref, shi_ref,
                    qb_ref, kf_ref, kb_ref, vf_ref, vb_ref, *, qscale, mask_tile):
    tm, d = x_ref.shape
    h = _rms(x_ref[...], g_ref[...], _NORM_EPS).astype(_BF)
    cos, slo, shi = cos_ref[...], slo_ref[...], shi_ref[...]
    if mask_tile:
        row0 = pl.program_id(1) * tm
        q_ext = _mask_lanes(tm, row0, mask_tile, 0, False).astype(_BF)
        k_ext = _mask_lanes(tm, row0, mask_tile, 0, True).astype(_BF)
    else:
        q_ext = k_ext = jnp.zeros((tm, _LANES), _BF)
    cw = _MXU_COLS
    for c in range(0, d, cw):
        yq = jnp.dot(h, w_ref[:, c:c + cw], preferred_element_type=_F32)
        yk = jnp.dot(h, w_ref[:, d + c:d + c + cw], preferred_element_type=_F32)
        yv = jnp.dot(h, w_ref[:, 2 * d + c:2 * d + c + cw], preferred_element_type=_F32)
        for s in range(0, cw, _LANES):
            q = _rope_lanes(yq[:, s:s + _LANES], cos, slo, shi)
            k = _rope_lanes(yk[:, s:s + _LANES], cos, slo, shi)
            o = 2 * (c + s)
            qb_ref[:, o:o + _LANES] = (q * qscale).astype(_BF)
            qb_ref[:, o + _LANES:o + 2 * _LANES] = q_ext
            kf_ref[:, c + s:c + s + _LANES] = k
            kb_ref[:, o:o + _LANES] = k.astype(_BF)
            kb_ref[:, o + _LANES:o + 2 * _LANES] = k_ext
        vf_ref[:, c:c + cw] = yv
        vb_ref[:, c:c + cw] = yv.astype(_BF)


def _diff_proj(x, g, w_qkv, tabs, tm, mask_tile):
    b, s, d = x.shape
    row = lambda n, dt: jax.ShapeDtypeStruct((b, s, n), dt)
    xspec = pl.BlockSpec((None, tm, d), lambda i, t: (i, t, 0))
    wide = pl.BlockSpec((None, tm, 2 * d), lambda i, t: (i, t, 0))
    tspec = pl.BlockSpec((tm, _LANES), lambda i, t: (t, 0))
    return pl.pallas_call(
        functools.partial(_diff_proj_body, qscale=_LOG2E / math.sqrt(_A_HD), mask_tile=mask_tile),
        grid=(b, s // tm),
        in_specs=[xspec, _resident((1, d)), _resident(w_qkv.shape), tspec, tspec, tspec],
        out_specs=[wide, xspec, wide, xspec, xspec],
        out_shape=[row(2 * d, _BF), row(d, _F32), row(2 * d, _BF), row(d, _F32), row(d, _BF)],
        compiler_params=_params(("parallel", "parallel")),
        name="diff_proj",
    )(x, g.reshape(1, d), w_qkv, *tabs)


def _flash_body(*refs, nmaps, causal, tq, tk, lam_init):
    if nmaps == 2:
        q_ref, k_ref, v_ref, lam_ref, sg_ref, o_ref, m_sc, l_sc, acc_sc = refs
    else:
        q_ref, k_ref, v_ref, o_ref, m_sc, l_sc, acc_sc = refs
    qi, ki = pl.program_id(2), pl.program_id(3)
    nk = pl.num_programs(3)

    @pl.when(ki == 0)
    def _():
        m_sc[...] = jnp.full(m_sc.shape, _NEG_INF, _F32)
        l_sc[...] = jnp.zeros(l_sc.shape, _F32)
        acc_sc[...] = jnp.zeros(acc_sc.shape, _F32)

    def step(masked):
        q, k, v = q_ref[...], k_ref[...], v_ref[...]
        if masked:
            rows = lax.broadcasted_iota(jnp.int32, (tq, tk), 0) // _CHUNK
            cols = lax.broadcasted_iota(jnp.int32, (tq, tk), 1) // _CHUNK
            visible = cols <= rows
        if nmaps == 2:
            lane = lax.broadcasted_iota(jnp.int32, q.shape, 1)
        for c in range(nmaps):
            qc = q if nmaps == 1 else jnp.where((lane >= _A_HD) == (c == 1), q, jnp.zeros_like(q))
            s = lax.dot_general(qc, k, (((1,), (1,)), ((), ())), preferred_element_type=_F32)
            if masked:
                s = jnp.where(visible, s, _NEG_INF)
            m_prev = m_sc[c]
            m_new = jnp.maximum(m_prev, jnp.max(s, axis=-1, keepdims=True))
            alpha = jnp.exp2(m_prev - m_new)
            p = jnp.exp2(s - m_new)
            l_sc[c] = alpha * l_sc[c] + jnp.sum(p, axis=-1, keepdims=True)
            acc_sc[c] = alpha * acc_sc[c] + jnp.dot(p.astype(_BF), v, preferred_element_type=_F32)
            m_sc[c] = m_new

    if causal:
        pl.when(ki < qi)(lambda: step(False))
        pl.when(ki == qi)(lambda: step(True))
    else:
        step(False)

    @pl.when(ki == nk - 1)
    def _():
        if nmaps == 2:
            lp = lam_ref[...]
            lam = (jnp.exp(jnp.sum(lp[0:1] * lp[1:2], axis=-1, keepdims=True))
                   - jnp.exp(jnp.sum(lp[2:3] * lp[3:4], axis=-1, keepdims=True)) + lam_init)
            o = acc_sc[0] / l_sc[0] - lam * (acc_sc[1] / l_sc[1])
            o = _rms(o, sg_ref[...], _A_SUBLN_EPS) * (1.0 - lam_init)
        else:
            o = acc_sc[0] / l_sc[0]
        o_ref[...] = o.astype(o_ref.dtype)


def _flash(q, k, v, heads, dq, dv, *, causal, tq, tk, nmaps=1, lam=None, subln_g=None, lam_init=0.0):
    b, sq, _ = q.shape
    sk = k.shape[1]
    nq, nk = sq // tq, sk // tk
    if causal:
        assert tq == tk and tq % _CHUNK == 0 and sq == sk
        kmap = lambda i, h, qi, ki: (i, jnp.minimum(ki, qi), h)
    else:
        kmap = lambda i, h, qi, ki: (i, ki, h)
    in_specs = [pl.BlockSpec((None, tq, dq), lambda i, h, qi, ki: (i, qi, h)),
                pl.BlockSpec((None, tk, dq), kmap),
                pl.BlockSpec((None, tk, dv), kmap)]
    args = [q, k, v]
    if nmaps == 2:
        in_specs += [_resident(lam.shape), _resident((1, dv))]
        args += [lam, subln_g.reshape(1, dv)]
    return pl.pallas_call(
        functools.partial(_flash_body, nmaps=nmaps, causal=causal, tq=tq, tk=tk, lam_init=lam_init),
        grid=(b, heads, nq, nk),
        in_specs=in_specs,
        out_specs=pl.BlockSpec((None, tq, dv), lambda i, h, qi, ki: (i, qi, h)),
        out_shape=jax.ShapeDtypeStruct((b, sq, heads * dv), _BF),
        scratch_shapes=[pltpu.VMEM((nmaps, tq, 1), _F32), pltpu.VMEM((nmaps, tq, 1), _F32),
                        pltpu.VMEM((nmaps, tq, dv), _F32)],
        compiler_params=_params(("parallel", "parallel", "parallel", "arbitrary")),
        name="flash_diff" if nmaps == 2 else "flash_mla",
    )(*args)


def _sweep_body(*refs, nmaps, halves, tile, nb, ext0, lam_init):
    if nmaps == 2:
        q_ref, k_ref, vt_ref, lam_ref, sg_ref, o_ref, s_buf, mb_buf, m_sc, acc_sc = refs
    else:
        q_ref, k_ref, vt_ref, o_ref, s_buf, mb_buf, m_sc, acc_sc = refs
    hw = tile // halves
    nch = nmaps * halves
    dq = q_ref.shape[-1]
    dv = o_ref.shape[-1]
    npairs = nb * (nb + 1) // 2
    acc_sc[...] = jnp.zeros(acc_sc.shape, _F32)

    def nxt(pair):
        qi, t = pair
        last = t == qi
        return jnp.where(last, qi + 1, qi), jnp.where(last, 0, t + 1)

    def scores(pair, slot):
        qi, t = pair
        k = k_ref[pl.ds(pl.multiple_of(t * tile, tile), tile), :]
        lane = lax.broadcasted_iota(jnp.int32, (1, dq), 1)
        onehot_lane = (lane >= ext0) & (lane < ext0 + tile // _CHUNK)
        keep = jnp.where(onehot_lane, jnp.where(t == qi, 1.0, 0.0), 1.0).astype(_BF)
        k = k * keep
        for c in range(nch):
            mp, hf = divmod(c, halves)
            q = q_ref[pl.ds(pl.multiple_of(qi * tile + hf * hw, hw), hw), :]
            if nmaps == 2:
                ql = lax.broadcasted_iota(jnp.int32, q.shape, 1)
                other = (ql >= _A_HD) & (ql < 2 * _A_HD) if mp == 0 else ql < _A_HD
                q = jnp.where(other, jnp.zeros_like(q), q)
            s = lax.dot_general(k, q, (((1,), (1,)), ((), ())), preferred_element_type=_F32)
            s_buf[slot, c] = s
            mb_buf[slot, c] = jnp.max(s, axis=0, keepdims=True)

    def values(pair, slot):
        _, t = pair
        first = t == 0
        vt = vt_ref[t]
        for c in range(nch):
            m_prev = jnp.where(first, _NEG_INF, m_sc[c])
            m_new = jnp.maximum(m_prev, mb_buf[slot, c])
            alpha = jnp.exp2(m_prev - m_new)
            m_sc[c] = m_new
            p = jnp.exp2(s_buf[slot, c] - m_new).astype(_BF)
            acc_sc[slot, c] = (alpha * acc_sc[(slot - 1) % _SWEEP_SLOTS, c]
                               + jnp.dot(vt, p, preferred_element_type=_F32))

    def finish(pair, slot):
        qi, t = pair

        @pl.when(t == qi)
        def _():
            if nmaps == 2:
                lp = lam_ref[...]
                lam = (jnp.exp(jnp.sum(lp[0:1] * lp[1:2], axis=-1, keepdims=True))
                       - jnp.exp(jnp.sum(lp[2:3] * lp[3:4], axis=-1, keepdims=True)) + lam_init)
            normed = lambda c: acc_sc[slot, c, :dv, :] / acc_sc[slot, c, dv:dv + 1, :]
            for hf in range(halves):
                if nmaps == 2:
                    ot = normed(hf) - lam * normed(halves + hf)
                    o = _rms(ot.T, sg_ref[...], _A_SUBLN_EPS) * (1.0 - lam_init)
                else:
                    o = normed(hf).T
                row = pl.multiple_of(qi * tile + hf * hw, hw)
                o_ref[pl.ds(row, hw), :] = o.astype(o_ref.dtype)

    zero = jnp.int32(0)
    pair0 = (zero, zero)
    scores(pair0, 0)
    if npairs == 1:
        values(pair0, 0)
        finish(pair0, 0)
        return
    pair1 = nxt(pair0)
    scores(pair1, 1)

    def tick(carry, r):
        pc, pb, pa = carry
        scores(pa, (2 + r) % _SWEEP_SLOTS)
        values(pc, r)
        return pb, pa, nxt(pa)

    def trip(_, carry):
        done = []
        for r in range(_SWEEP_SLOTS):
            done.append(carry[0])
            carry = tick(carry, r)
        for r, pair in enumerate(done):
            finish(pair, r)
        return carry

    nticks = npairs - 2
    carry = lax.fori_loop(0, nticks // _SWEEP_SLOTS, trip, (pair0, pair1, nxt(pair1)))
    for r in range(nticks % _SWEEP_SLOTS):
        pair = carry[0]
        carry = tick(carry, r)
        finish(pair, r)
    before_last, last, _ = carry
    for pair, slot in ((before_last, (npairs - 2) % _SWEEP_SLOTS), (last, (npairs - 1) % _SWEEP_SLOTS)):
        values(pair, slot)
        finish(pair, slot)


def _sweep(q, k, v, heads, dq, dv, *, tile, ext0, halves=2, nmaps=1, lam=None, subln_g=None,
           lam_init=0.0):
    b, s, _ = q.shape
    nb = s // tile
    assert tile % _CHUNK == 0 and tile % halves == 0 and ext0 + tile // _CHUNK <= dq
    vt = v.reshape(b, nb, tile, heads, dv).transpose(0, 3, 1, 4, 2)
    vt = jnp.concatenate([vt, jnp.ones((b, heads, nb, _BF_ROWS, tile), vt.dtype)], axis=3)
    dve = dv + _BF_ROWS
    nch = nmaps * halves
    hw = tile // halves
    once = pl.Buffered(1)
    in_specs = [pl.BlockSpec((None, s, dq), lambda i, h: (i, 0, h), pipeline_mode=once),
                pl.BlockSpec((None, s, dq), lambda i, h: (i, 0, h), pipeline_mode=once),
                pl.BlockSpec((None, None, nb, dve, tile), lambda i, h: (i, h, 0, 0, 0),
                             pipeline_mode=once)]
    args = [q, k, vt]
    if nmaps == 2:
        in_specs += [_resident(lam.shape), _resident((1, dv))]
        args += [lam, subln_g.reshape(1, dv)]
    scratch = [pltpu.VMEM((_SWEEP_SLOTS, nch, tile, hw), _F32),
               pltpu.VMEM((_SWEEP_SLOTS, nch, 1, hw), _F32),
               pltpu.VMEM((nch, 1, hw), _F32), pltpu.VMEM((_SWEEP_SLOTS, nch, dve, hw), _F32)]
    return pl.pallas_call(
        functools.partial(_sweep_body, nmaps=nmaps, halves=halves, tile=tile, nb=nb, ext0=ext0,
                          lam_init=lam_init),
        grid=(b, heads),
        in_specs=in_specs,
        out_specs=pl.BlockSpec((None, s, dv), lambda i, h: (i, 0, h)),
        out_shape=jax.ShapeDtypeStruct((b, s, heads * dv), _BF),
        scratch_shapes=scratch,
        compiler_params=_params(("parallel", "parallel")),
        name="sweep_diff" if nmaps == 2 else "sweep_mla",
    )(*args)


def _mm_body(*refs, has_res):
    if has_res:
        a_ref, w_ref, r_ref, o_ref = refs
    else:
        a_ref, w_ref, o_ref = refs
    y = jnp.dot(a_ref[...].astype(_BF), w_ref[...], preferred_element_type=_F32)
    if has_res:
        y = r_ref[...] + y
    o_ref[...] = y.astype(o_ref.dtype)


def _mm(a, w, res=None, out_dtype=_F32):
    b, s, kdim = a.shape
    n = w.shape[1]
    tm = min(_ROW_TILE, s)
    in_specs = [pl.BlockSpec((None, tm, kdim), lambda i, t: (i, t, 0)), _resident(w.shape)]
    args = [a, w]
    if res is not None:
        in_specs.append(pl.BlockSpec((None, tm, n), lambda i, t: (i, t, 0)))
        args.append(res)
    return pl.pallas_call(
        functools.partial(_mm_body, has_res=res is not None),
        grid=(b, s // tm),
        in_specs=in_specs,
        out_specs=pl.BlockSpec((None, tm, n), lambda i, t: (i, t, 0)),
        out_shape=jax.ShapeDtypeStruct((b, s, n), out_dtype),
        compiler_params=_params(("parallel", "parallel")),
        name="matmul_residual" if res is not None else "matmul",
    )(*args)


def _mla_proj_body(x_ref, g_ref, wd_ref, qg_ref, wq_ref, kvg_ref, wk_ref, wv_ref,
                   cos_ref, slo_ref, shi_ref, lat_ref, kr_ref, q_ref, k_ref, v_ref, *, qscale, mask_tile):
    tm = x_ref.shape[0]
    h = _rms(x_ref[...], g_ref[...], _NORM_EPS).astype(_BF)
    cos, slo, shi = cos_ref[...], slo_ref[...], shi_ref[...]
    if mask_tile:
        row0 = pl.program_id(1) * tm
        q_ext = _mask_lanes(tm, row0, mask_tile, _B_ROPE, False)
        k_ext = _mask_lanes(tm, row0, mask_tile, _B_ROPE, True)
    else:
        q_ext = k_ext = jnp.zeros((tm, _LANES), _F32)
    dn = jnp.dot(h, wd_ref[...], preferred_element_type=_F32)
    e0, e1 = _B_Q_LORA, _B_Q_LORA + _B_KV_LORA
    cq = _rms(dn[:, :e0], qg_ref[...], _NORM_EPS).astype(_BF)
    lat = _rms(dn[:, e0:e1], kvg_ref[...], _NORM_EPS)
    kr = _rope_lanes(dn[:, e1:e1 + _LANES], cos, slo, shi)
    lat_ref[...] = lat
    kr_ref[...] = kr[:, :_B_ROPE]
    hw = 2 * _LANES
    for hd in range(_B_HEADS):
        yq = jnp.dot(cq, wq_ref[:, hd * hw:(hd + 1) * hw], preferred_element_type=_F32)
        q_ref[:, hd * hw:hd * hw + _LANES] = (yq[:, :_LANES] * qscale).astype(_BF)
        q_ref[:, hd * hw + _LANES:(hd + 1) * hw] = (
            _rope_lanes(yq[:, _LANES:], cos, slo, shi) * qscale + q_ext).astype(_BF)
    kin = jnp.concatenate([lat, kr], axis=1).astype(_BF)
    for hd in range(_B_HEADS):
        yk = jnp.dot(kin, wk_ref[:, hd * hw:(hd + 1) * hw], preferred_element_type=_F32)
        k_ref[:, hd * hw:hd * hw + _LANES] = yk[:, :_LANES].astype(_BF)
        k_ref[:, hd * hw + _LANES:(hd + 1) * hw] = (yk[:, _LANES:] + k_ext).astype(_BF)
    latb = lat.astype(_BF)
    for c in range(0, v_ref.shape[-1], _MXU_COLS):
        v_ref[:, c:c + _MXU_COLS] = jnp.dot(
            latb, wv_ref[:, c:c + _MXU_COLS], preferred_element_type=_F32).astype(_BF)


def _mla_weights(w_down, w_uq, w_uk, w_uv):
    d = w_down.shape[0]
    hw = 2 * _LANES
    wd = jnp.concatenate([w_down, jnp.zeros((d, _LANES - _B_ROPE), w_down.dtype)], axis=1)
    wq = w_uq.reshape(_B_Q_LORA, _B_HEADS, _B_NOPE + _B_ROPE)
    wq = jnp.pad(wq, ((0, 0), (0, 0), (0, hw - _B_NOPE - _B_ROPE))).reshape(_B_Q_LORA, _B_HEADS * hw)
    eye = jnp.eye(_B_ROPE, dtype=w_uk.dtype)
    wk_top = jnp.pad(w_uk, ((0, 0), (0, 0), (0, hw - _B_NOPE)))
    wk_mid = jnp.broadcast_to(jnp.pad(eye, ((0, 0), (_B_NOPE, hw - _B_NOPE - _B_ROPE)))[:, None, :],
                              (_B_ROPE, _B_HEADS, hw))
    wk_bot = jnp.zeros((_LANES - _B_ROPE, _B_HEADS, hw), w_uk.dtype)
    wk = jnp.concatenate([wk_top, wk_mid, wk_bot], axis=0).reshape(-1, _B_HEADS * hw)
    wv = w_uv.reshape(_B_KV_LORA, _B_HEADS * _B_VD)
    return wd.astype(_BF), wq.astype(_BF), wk.astype(_BF), wv.astype(_BF)


def _mla_proj(x, g, wts, q_norm_g, kv_norm_g, tabs, tm, mask_tile):
    b, s, d = x.shape
    wd, wq, wk, wv = wts
    xspec = pl.BlockSpec((None, tm, d), lambda i, t: (i, t, 0))
    tspec = pl.BlockSpec((tm, _LANES), lambda i, t: (t, 0))
    ospec = lambda n: pl.BlockSpec((None, tm, n), lambda i, t: (i, t, 0))
    oshape = lambda n, dt: jax.ShapeDtypeStruct((b, s, n), dt)
    return pl.pallas_call(
        functools.partial(_mla_proj_body, qscale=_LOG2E / math.sqrt(_B_NOPE + _B_ROPE),
                          mask_tile=mask_tile),
        grid=(b, s // tm),
        in_specs=[xspec, _resident((1, d)), _resident(wd.shape), _resident((1, _B_Q_LORA)),
                  _resident(wq.shape), _resident((1, _B_KV_LORA)), _resident(wk.shape),
                  _resident(wv.shape), tspec, tspec, tspec],
        out_specs=[ospec(_B_KV_LORA), ospec(_B_ROPE), ospec(wq.shape[1]), ospec(wk.shape[1]),
                   ospec(wv.shape[1])],
        out_shape=[oshape(_B_KV_LORA, _F32), oshape(_B_ROPE, _F32), oshape(wq.shape[1], _BF),
                   oshape(wk.shape[1], _BF), oshape(wv.shape[1], _BF)],
        compiler_params=_params(("parallel", "parallel")),
        name="mla_proj",
    )(x, g.reshape(1, d), wd, q_norm_g.reshape(1, -1), wq, kv_norm_g.reshape(1, -1), wk, wv, *tabs)


def _conv_mix_body(x_ref, g_ref, win_ref, cw_ref, wout_ref, hist_ref, o_ref, hout_ref,
                   carry_sc, z_sc):
    tm, d = x_ref.shape

    @pl.when(pl.program_id(1) == 0)
    def _():
        carry_sc[...] = hist_ref[...]

    x = x_ref[...]
    h = _rms(x, g_ref[...], _NORM_EPS).astype(_BF)
    cw = _MXU_COLS
    for c in range(0, d, cw):
        gate_b = jnp.dot(h, win_ref[:, c:c + cw], preferred_element_type=_F32)
        gate_c = jnp.dot(h, win_ref[:, d + c:d + c + cw], preferred_element_type=_F32)
        val = jnp.dot(h, win_ref[:, 2 * d + c:2 * d + c + cw], preferred_element_type=_F32)
        u = gate_c * val
        prev = carry_sc[:, c:c + cw]
        y = (_shift_rows(prev, u, 2) * cw_ref[0:1, c:c + cw]
             + _shift_rows(prev, u, 1) * cw_ref[1:2, c:c + cw]
             + u * cw_ref[2:3, c:c + cw])
        z_sc[:, c:c + cw] = (gate_b * y).astype(_BF)
        carry_sc[:, c:c + cw] = u[tm - _SUBLANES:, :]
    hout_ref[...] = carry_sc[...]
    o_ref[...] = x + jnp.dot(z_sc[...], wout_ref[...], preferred_element_type=_F32)


def _conv_mix(x, g, w_in, conv_w, w_out, hist8, tm):
    b, s, d = x.shape
    xspec = pl.BlockSpec((None, tm, d), lambda i, t: (i, t, 0))
    hspec = pl.BlockSpec((None, _SUBLANES, d), lambda i, t: (i, 0, 0))
    return pl.pallas_call(
        _conv_mix_body,
        grid=(b, s // tm),
        in_specs=[xspec, _resident((1, d)), _resident(w_in.shape), _resident(conv_w.shape),
                  _resident(w_out.shape), hspec],
        out_specs=[xspec, hspec],
        out_shape=[jax.ShapeDtypeStruct((b, s, d), _F32),
                   jax.ShapeDtypeStruct((b, _SUBLANES, d), _F32)],
        scratch_shapes=[pltpu.VMEM((_SUBLANES, d), _F32), pltpu.VMEM((tm, d), _BF)],
        compiler_params=_params(("arbitrary", "arbitrary")),
        name="conv_mixer",
    )(x, g.reshape(1, d), w_in, conv_w, w_out, hist8)


_POOL_HALO = 16


def _pool_mix_body(x_ref, g_ref, wg_ref, sc_ref, hist_ref, o_ref, hout_ref, carry_sc, *, hist_valid):
    tm, d = x_ref.shape
    t = pl.program_id(1)

    @pl.when(t == 0)
    def _():
        carry_sc[...] = hist_ref[...]

    x = x_ref[...]
    h = _rms(x, g_ref[...], _NORM_EPS)
    gw = d // len(_D_WINDOWS)
    tpos = t * tm + lax.broadcasted_iota(jnp.int32, (tm, gw), 0)
    outs = []
    for gi, w in enumerate(_D_WINDOWS):
        hg = h[:, gi * gw:(gi + 1) * gw]
        acc = jnp.concatenate([carry_sc[:, gi * gw:(gi + 1) * gw], hg], axis=0)
        k = 1
        while k < w:
            acc = acc + pltpu.roll(acc, k, axis=0)
            k *= 2
        cnt = jnp.minimum(tpos + (hist_valid + 1), w).astype(_F32)
        pooled = acc[_POOL_HALO:] / cnt
        outs.append(jnp.dot((pooled - hg).astype(_BF), wg_ref[gi], preferred_element_type=_F32))
    o_ref[...] = x + jnp.concatenate(outs, axis=1) * sc_ref[...]
    tail = h[tm - _POOL_HALO:, :]
    carry_sc[...] = tail
    hout_ref[...] = tail


def _pool_mix(x, g, w_group, scale, hist16, hist_valid, tm):
    b, s, d = x.shape
    assert all(w & (w - 1) == 0 and w <= _POOL_HALO for w in _D_WINDOWS)
    xspec = pl.BlockSpec((None, tm, d), lambda i, t: (i, t, 0))
    hspec = pl.BlockSpec((None, _POOL_HALO, d), lambda i, t: (i, 0, 0))
    return pl.pallas_call(
        functools.partial(_pool_mix_body, hist_valid=hist_valid),
        grid=(b, s // tm),
        in_specs=[xspec, _resident((1, d)), _resident(w_group.shape), _resident((1, d)), hspec],
        out_specs=[xspec, hspec],
        out_shape=[jax.ShapeDtypeStruct((b, s, d), _F32),
                   jax.ShapeDtypeStruct((b, _POOL_HALO, d), _F32)],
        scratch_shapes=[pltpu.VMEM((_POOL_HALO, d), _F32)],
        compiler_params=_params(("arbitrary", "arbitrary")),
        name="pool_mixer",
    )(x, g.reshape(1, d), w_group, scale.reshape(1, d), hist16)


def _ffn_body(*refs, final):
    if final:
        (x_ref, g_ref, wg_ref, wu_ref, cw_ref, cb_ref, wd_ref, hist_ref, fg_ref,
         o_ref, hout_ref, carry_sc, act_sc) = refs
    else:
        (x_ref, g_ref, wg_ref, wu_ref, cw_ref, cb_ref, wd_ref, hist_ref,
         o_ref, hout_ref, carry_sc, act_sc) = refs
    tm = x_ref.shape[0]
    f = wg_ref.shape[1]

    @pl.when(pl.program_id(1) == 0)
    def _():
        carry_sc[...] = hist_ref[...]

    x = x_ref[...]
    h = _rms(x, g_ref[...], _NORM_EPS).astype(_BF)
    cw = _MXU_COLS
    for c in range(0, f, cw):
        gate = jnp.dot(h, wg_ref[:, c:c + cw], preferred_element_type=_F32)
        up = jnp.dot(h, wu_ref[:, c:c + cw], preferred_element_type=_F32)
        prev = carry_sc[:, c:c + cw]
        y = (_shift_rows(prev, gate, 2) * cw_ref[0:1, c:c + cw]
             + _shift_rows(prev, gate, 1) * cw_ref[1:2, c:c + cw]
             + gate * cw_ref[2:3, c:c + cw] + cb_ref[:, c:c + cw])
        act_sc[:, c:c + cw] = (y / (1.0 + jnp.exp(-y)) * up).astype(_BF)
        carry_sc[:, c:c + cw] = gate[tm - _SUBLANES:, :]
    hout_ref[...] = carry_sc[...]
    out = x + jnp.dot(act_sc[...], wd_ref[...], preferred_element_type=_F32)
    if final:
        out = _rms(out, fg_ref[...], _NORM_EPS)
    o_ref[...] = out


def _ffn(x, g, w_gate, w_up, conv_w, conv_b, w_down, hist8, final_g, tm):
    b, s, d = x.shape
    f = w_gate.shape[1]
    assert f % _MXU_COLS == 0
    xspec = pl.BlockSpec((None, tm, d), lambda i, t: (i, t, 0))
    hspec = pl.BlockSpec((None, _SUBLANES, f), lambda i, t: (i, 0, 0))
    in_specs = [xspec, _resident((1, d)), _resident(w_gate.shape), _resident(w_up.shape),
                _resident(conv_w.shape), _resident((1, f)), _resident(w_down.shape), hspec]
    args = [x, g.reshape(1, d), w_gate, w_up, conv_w, conv_b.reshape(1, f), w_down, hist8]
    if final_g is not None:
        in_specs.append(_resident((1, d)))
        args.append(final_g.reshape(1, d))
    return pl.pallas_call(
        functools.partial(_ffn_body, final=final_g is not None),
        grid=(b, s // tm),
        in_specs=in_specs,
        out_specs=[xspec, hspec],
        out_shape=[jax.ShapeDtypeStruct((b, s, d), _F32),
                   jax.ShapeDtypeStruct((b, _SUBLANES, f), _F32)],
        scratch_shapes=[pltpu.VMEM((_SUBLANES, f), _F32), pltpu.VMEM((tm, f), _BF)],
        compiler_params=_params(("arbitrary", "arbitrary")),
        name="conv_ffn",
    )(*args)


def _pad_hist(hist, rows):
    return jnp.pad(hist, ((0, 0), (rows - hist.shape[1], 0), (0, 0)))


def kernel(x_prompt, x_sample, cache_a_k, cache_a_v, cache_b_latent, cache_b_krope, state_c_conv, state_d_pool, state_ffn_conv, norm_mix_g, norm_ffn_g, norm_final_g, a_w_qkv, a_lam, a_subln_g, a_w_o, b_w_down, b_q_norm_g, b_w_uq, b_kv_norm_g, b_w_uk, b_w_uv, b_w_o, c_w_in, c_conv_w, c_w_out, d_w_group, d_scale, ffn_w_gate, ffn_w_up, ffn_conv_w, ffn_conv_b, ffn_w_down):
    depth = norm_mix_g.shape[0]
    n_p, seq, d = x_prompt.shape
    n_s, t_new, _ = x_sample.shape
    past = cache_a_k.shape[2]
    f = ffn_w_gate.shape[-1]
    tm_p, tm_s = min(_ROW_TILE, seq), t_new
    ta = min(_ATT_TILE, seq)
    tabs_p = _rope_tables(jnp.arange(seq, dtype=jnp.int32))
    tabs_s = _rope_tables(past + jnp.arange(t_new, dtype=jnp.int32))
    xp, xs = x_prompt, x_sample
    outs = {k: [] for k in ("ak_p", "av_p", "bl_p", "br_p", "cc_p", "dp_p", "fc_p",
                            "ak_s", "av_s", "bl_s", "br_s", "cc_s", "dp_s", "fc_s")}
    for i in range(depth):
        m, j = i % _N_MIXERS, i // _N_MIXERS
        g_mix = norm_mix_g[i]
        if m == 0:
            lam_init = 0.8 - 0.6 * math.exp(-0.3 * i)
            w_qkv, w_o = a_w_qkv[j].astype(_BF), a_w_o[j].astype(_BF)
            hq = 2 * _LANES
            qb, kf, kb, vf, vb = _diff_proj(xp, g_mix, w_qkv, tabs_p, tm_p, ta)
            op = _sweep(qb, kb, vb, _A_HEADS, hq, 2 * _A_HD, tile=ta, ext0=2 * _A_HD, nmaps=2,
                        lam=a_lam[j], subln_g=a_subln_g[j], lam_init=lam_init)
            xp = _mm(op, w_o, res=xp)
            outs["ak_p"].append(kf.reshape(n_p, seq, _A_HEADS, 2, _A_HD))
            outs["av_p"].append(vf.reshape(n_p, seq, _A_HEADS, 2 * _A_HD))
            attn = functools.partial(_flash, heads=_A_HEADS, dq=hq, dv=2 * _A_HD, nmaps=2,
                                     lam=a_lam[j], subln_g=a_subln_g[j], lam_init=lam_init)
            qb, kf, kb, vf, vb = _diff_proj(xs, g_mix, w_qkv, tabs_s, tm_s, 0)
            k_cache = jnp.pad(cache_a_k[j].reshape(n_s, past, _A_HEADS, 2 * _A_HD).astype(_BF),
                              ((0, 0), (0, 0), (0, 0), (0, hq - 2 * _A_HD)))
            k_all = jnp.concatenate([k_cache.reshape(n_s, past, _A_HEADS * hq), kb], axis=1)
            v_all = jnp.concatenate([cache_a_v[j].reshape(n_s, past, d).astype(_BF), vb], axis=1)
            osm = attn(qb, k_all, v_all, causal=False, tq=t_new, tk=past + t_new)
            xs = _mm(osm, w_o, res=xs)
            outs["ak_s"].append(kf.reshape(n_s, t_new, _A_HEADS, 2, _A_HD))
            outs["av_s"].append(vf.reshape(n_s, t_new, _A_HEADS, 2 * _A_HD))
        elif m == 1:
            wts = _mla_weights(b_w_down[j], b_w_uq[j], b_w_uk[j], b_w_uv[j])
            w_o = b_w_o[j].astype(_BF)
            hq = 2 * _LANES
            attn = functools.partial(_flash, heads=_B_HEADS, dq=hq, dv=_B_VD)
            lat, kr, q, kfull, v = _mla_proj(xp, g_mix, wts, b_q_norm_g[j], b_kv_norm_g[j], tabs_p,
                                             tm_p, ta)
            op = _sweep(q, kfull, v, _B_HEADS, hq, _B_VD, tile=ta, ext0=_B_NOPE + _B_ROPE)
            xp = _mm(op, w_o, res=xp)
            outs["bl_p"].append(lat)
            outs["br_p"].append(kr)
            lat, kr, q, kfull, v = _mla_proj(xs, g_mix, wts, b_q_norm_g[j], b_kv_norm_g[j], tabs_s,
                                             tm_s, 0)
            kin = jnp.concatenate([cache_b_latent[j], cache_b_krope[j],
                                   jnp.zeros((n_s, past, _LANES - _B_ROPE), _F32)], axis=-1).astype(_BF)
            k_cache = _mm(kin, wts[2], out_dtype=_BF)
            v_cache = _mm(cache_b_latent[j].astype(_BF), wts[3], out_dtype=_BF)
            k_all = jnp.concatenate([k_cache, kfull], axis=1)
            v_all = jnp.concatenate([v_cache, v], axis=1)
            osm = attn(q, k_all, v_all, causal=False, tq=t_new, tk=past + t_new)
            xs = _mm(osm, w_o, res=xs)
            outs["bl_s"].append(lat)
            outs["br_s"].append(kr)
        elif m == 2:
            w_in, w_out = c_w_in[j].astype(_BF), c_w_out[j].astype(_BF)
            xp, hc = _conv_mix(xp, g_mix, w_in, c_conv_w[j], w_out,
                               jnp.zeros((n_p, _SUBLANES, d), _F32), tm_p)
            outs["cc_p"].append(hc[:, -2:])
            xs, hc = _conv_mix(xs, g_mix, w_in, c_conv_w[j], w_out,
                               _pad_hist(state_c_conv[j], _SUBLANES), tm_s)
            outs["cc_s"].append(hc[:, -2:])
        else:
            w_grp = d_w_group[j].astype(_BF)
            xp, hd = _pool_mix(xp, g_mix, w_grp, d_scale[j],
                               jnp.zeros((n_p, _POOL_HALO, d), _F32), 0, tm_p)
            outs["dp_p"].append(hd[:, -_D_HIST:])
            xs, hd = _pool_mix(xs, g_mix, w_grp, d_scale[j],
                               _pad_hist(state_d_pool[j], _POOL_HALO), _D_HIST, tm_s)
            outs["dp_s"].append(hd[:, -_D_HIST:])
        wg, wu, wd = ffn_w_gate[i].astype(_BF), ffn_w_up[i].astype(_BF), ffn_w_down[i].astype(_BF)
        final_g = norm_final_g if i == depth - 1 else None
        xp, hf = _ffn(xp, norm_ffn_g[i], wg, wu, ffn_conv_w[i], ffn_conv_b[i], wd,
                      jnp.zeros((n_p, _SUBLANES, f), _F32), final_g, tm_p)
        outs["fc_p"].append(hf[:, -2:])
        xs, hf = _ffn(xs, norm_ffn_g[i], wg, wu, ffn_conv_w[i], ffn_conv_b[i], wd,
                      _pad_hist(state_ffn_conv[i], _SUBLANES), final_g, tm_s)
        outs["fc_s"].append(hf[:, -2:])
    st = lambda k: jnp.stack(outs[k])
    return (xp, xs, st("ak_p"), st("av_p"), st("bl_p"), st("br_p"), st("cc_p"), st("dp_p"), st("fc_p"),
            st("ak_s"), st("av_s"), st("bl_s"), st("br_s"), st("cc_s"), st("dp_s"), st("fc_s"))
```

```python
import functools
import math

import jax
import jax.numpy as jnp
from jax import lax
from jax.experimental import pallas as pl
from jax.experimental.pallas import tpu as pltpu

_BF = jnp.bfloat16
_F32 = jnp.float32

_CHUNK = 64
_ROPE_THETA = 10000.0
_NORM_EPS = 1e-6
_NEG_INF = -1e30
_A_HEADS = 8
_A_HD = 64
_A_SUBLN_EPS = 1e-5
_B_HEADS = 8
_B_NOPE = 128
_B_ROPE = 64
_B_VD = 128
_B_Q_LORA = 384
_B_KV_LORA = 256
_D_WINDOWS = (2, 4, 8, 16)
_D_HIST = 15
_N_MIXERS = 4
_LOG2E = math.log2(math.e)

_LANES = 128
_SUBLANES = 8
_MXU_COLS = 256
_BF_ROWS = 16
_SWEEP_SLOTS = 4
_VMEM_LIMIT = 56 * 1024 * 1024

_ROW_TILE = 512
_ATT_TILE = 512


def _params(sem):
    return pltpu.CompilerParams(dimension_semantics=sem, vmem_limit_bytes=_VMEM_LIMIT)


def _resident(shape):
    nd = len(shape)
    return pl.BlockSpec(shape, lambda *_: (0,) * nd, pipeline_mode=pl.Buffered(1))


def _rms(x, g, eps):
    ms = jnp.mean(x * x, axis=-1, keepdims=True)
    return x * lax.rsqrt(ms + eps) * g


def _rope_lanes(y, cos, sin_lo, sin_hi):
    return (y * cos + pltpu.roll(y, _LANES - 32, axis=1) * sin_lo
            + pltpu.roll(y, 32, axis=1) * sin_hi)


def _rope_tables(pos):
    d = _A_HD
    inv = jnp.power(_ROPE_THETA, -jnp.arange(0, d, 2, dtype=_F32) / d)
    ang = pos.astype(_F32)[:, None] * inv[None, :]
    cos, sin = jnp.cos(ang), jnp.sin(ang)
    zero = jnp.zeros_like(sin)
    cos_t = jnp.tile(cos, (1, 4))
    sin_lo = jnp.tile(jnp.concatenate([-sin, zero], axis=1), (1, 2))
    sin_hi = jnp.tile(jnp.concatenate([zero, sin], axis=1), (1, 2))
    return cos_t, sin_lo, sin_hi


def _shift_rows(carry, cur, k):
    ext = jnp.concatenate([carry, cur], axis=0)
    return pltpu.roll(ext, k, axis=0)[carry.shape[0]:]


def _mask_lanes(rows, row0, tile, base, for_keys):
    r = row0 + lax.broadcasted_iota(jnp.int32, (rows, _LANES), 0)
    chunk = (r % tile) // _CHUNK
    c = lax.broadcasted_iota(jnp.int32, (rows, _LANES), 1) - base
    n = tile // _CHUNK
    if for_keys:
        return jnp.where(c == chunk, 1.0, 0.0)
    return jnp.where(c > chunk, jnp.where(c < n, _NEG_INF, 0.0), 0.0)


def _diff_proj_body(x_ref, g_ref, w_ref, cos_ref, slo_ref, shi_ref,
                    qm_ref, kf_ref, km_ref, vf_ref, vb_ref, *, qscale, mask_tile):
    tm, d = x_ref.shape
    h = _rms(x_ref[...], g_ref[...], _NORM_EPS).astype(_BF)
    cos, slo, shi = cos_ref[...], slo_ref[...], shi_ref[...]
    if mask_tile:
        row0 = pl.program_id(1) * tm
        q_ext = _mask_lanes(tm, row0, mask_tile, _A_HD, False)
        k_ext = _mask_lanes(tm, row0, mask_tile, _A_HD, True)
    else:
        q_ext = k_ext = jnp.zeros((tm, _LANES), _F32)
    first_half = lax.broadcasted_iota(jnp.int32, (tm, _LANES), 1) < _A_HD
    zero = jnp.zeros((tm, _LANES), _F32)
    cw = _MXU_COLS
    for c in range(0, d, cw):
        yq = jnp.dot(h, w_ref[:, c:c + cw], preferred_element_type=_F32)
        yk = jnp.dot(h, w_ref[:, d + c:d + c + cw], preferred_element_type=_F32)
        yv = jnp.dot(h, w_ref[:, 2 * d + c:2 * d + c + cw], preferred_element_type=_F32)
        for s in range(0, cw, _LANES):
            cols = slice(c + s, c + s + _LANES)
            q = _rope_lanes(yq[:, s:s + _LANES], cos, slo, shi) * qscale
            k = _rope_lanes(yk[:, s:s + _LANES], cos, slo, shi)
            kf_ref[:, cols] = k
            for mp in range(2):
                qh = q if mp == 0 else pltpu.roll(q, _A_HD, axis=1)
                kh = k if mp == 0 else pltpu.roll(k, _A_HD, axis=1)
                qm_ref[mp, :, cols] = jnp.where(first_half, qh, q_ext).astype(_BF)
                km_ref[mp, 0, :, cols] = jnp.where(first_half, kh, zero).astype(_BF)
                km_ref[mp, 1, :, cols] = jnp.where(first_half, kh, k_ext).astype(_BF)
        vf_ref[:, c:c + cw] = yv
        vb_ref[:, c:c + cw] = yv.astype(_BF)


def _diff_proj(x, g, w_qkv, tabs, tm, mask_tile):
    b, s, d = x.shape
    row = lambda dt: jax.ShapeDtypeStruct((b, s, d), dt)
    xspec = pl.BlockSpec((None, tm, d), lambda i, t: (i, t, 0))
    tspec = pl.BlockSpec((tm, _LANES), lambda i, t: (t, 0))
    return pl.pallas_call(
        functools.partial(_diff_proj_body, qscale=_LOG2E / math.sqrt(_A_HD), mask_tile=mask_tile),
        grid=(b, s // tm),
        in_specs=[xspec, _resident((1, d)), _resident(w_qkv.shape), tspec, tspec, tspec],
        out_specs=[pl.BlockSpec((None, 2, tm, d), lambda i, t: (i, 0, t, 0)), xspec,
                   pl.BlockSpec((None, 2, 2, tm, d), lambda i, t: (i, 0, 0, t, 0)), xspec, xspec],
        out_shape=[jax.ShapeDtypeStruct((b, 2, s, d), _BF), row(_F32),
                   jax.ShapeDtypeStruct((b, 2, 2, s, d), _BF), row(_F32), row(_BF)],
        compiler_params=_params(("parallel", "parallel")),
        name="diff_proj",
    )(x, g.reshape(1, d), w_qkv, *tabs)


def _flash_body(*refs, nmaps, causal, tq, tk, lam_init):
    if nmaps == 2:
        q_ref, k_ref, v_ref, lam_ref, sg_ref, o_ref, m_sc, l_sc, acc_sc = refs
    else:
        q_ref, k_ref, v_ref, o_ref, m_sc, l_sc, acc_sc = refs
    qi, ki = pl.program_id(2), pl.program_id(3)
    nk = pl.num_programs(3)

    @pl.when(ki == 0)
    def _():
        m_sc[...] = jnp.full(m_sc.shape, _NEG_INF, _F32)
        l_sc[...] = jnp.zeros(l_sc.shape, _F32)
        acc_sc[...] = jnp.zeros(acc_sc.shape, _F32)

    def step(masked):
        v = v_ref[...]
        if masked:
            rows = lax.broadcasted_iota(jnp.int32, (tq, tk), 0) // _CHUNK
            cols = lax.broadcasted_iota(jnp.int32, (tq, tk), 1) // _CHUNK
            visible = cols <= rows
        for c in range(nmaps):
            s = lax.dot_general(q_ref[c], k_ref[c], (((1,), (1,)), ((), ())),
                                preferred_element_type=_F32)
            if masked:
                s = jnp.where(visible, s, _NEG_INF)
            m_prev = m_sc[c]
            m_new = jnp.maximum(m_prev, jnp.max(s, axis=-1, keepdims=True))
            alpha = jnp.exp2(m_prev - m_new)
            p = jnp.exp2(s - m_new)
            l_sc[c] = alpha * l_sc[c] + jnp.sum(p, axis=-1, keepdims=True)
            acc_sc[c] = alpha * acc_sc[c] + jnp.dot(p.astype(_BF), v, preferred_element_type=_F32)
            m_sc[c] = m_new

    if causal:
        pl.when(ki < qi)(lambda: step(False))
        pl.when(ki == qi)(lambda: step(True))
    else:
        step(False)

    @pl.when(ki == nk - 1)
    def _():
        if nmaps == 2:
            lp = lam_ref[...]
            lam = (jnp.exp(jnp.sum(lp[0:1] * lp[1:2], axis=-1, keepdims=True))
                   - jnp.exp(jnp.sum(lp[2:3] * lp[3:4], axis=-1, keepdims=True)) + lam_init)
            o = acc_sc[0] / l_sc[0] - lam * (acc_sc[1] / l_sc[1])
            o = _rms(o, sg_ref[...], _A_SUBLN_EPS) * (1.0 - lam_init)
        else:
            o = acc_sc[0] / l_sc[0]
        o_ref[...] = o.astype(o_ref.dtype)


def _flash(q, k, v, heads, dq, dv, *, causal, tq, tk, nmaps=1, lam=None, subln_g=None, lam_init=0.0):
    b, _, sq, _ = q.shape
    sk = k.shape[2]
    nq, nk = sq // tq, sk // tk
    if causal:
        assert tq == tk and tq % _CHUNK == 0 and sq == sk
        kblk = lambda qi, ki: jnp.minimum(ki, qi)
    else:
        kblk = lambda qi, ki: ki
    in_specs = [pl.BlockSpec((None, nmaps, tq, dq), lambda i, h, qi, ki: (i, 0, qi, h)),
                pl.BlockSpec((None, nmaps, tk, dq), lambda i, h, qi, ki: (i, 0, kblk(qi, ki), h)),
                pl.BlockSpec((None, tk, dv), lambda i, h, qi, ki: (i, kblk(qi, ki), h))]
    args = [q, k, v]
    if nmaps == 2:
        in_specs += [_resident(lam.shape), _resident((1, dv))]
        args += [lam, subln_g.reshape(1, dv)]
    return pl.pallas_call(
        functools.partial(_flash_body, nmaps=nmaps, causal=causal, tq=tq, tk=tk, lam_init=lam_init),
        grid=(b, heads, nq, nk),
        in_specs=in_specs,
        out_specs=pl.BlockSpec((None, tq, dv), lambda i, h, qi, ki: (i, qi, h)),
        out_shape=jax.ShapeDtypeStruct((b, sq, heads * dv), _BF),
        scratch_shapes=[pltpu.VMEM((nmaps, tq, 1), _F32), pltpu.VMEM((nmaps, tq, 1), _F32),
                        pltpu.VMEM((nmaps, tq, dv), _F32)],
        compiler_params=_params(("parallel", "parallel", "parallel", "arbitrary")),
        name="flash_diff" if nmaps == 2 else "flash_mla",
    )(*args)


def _sweep_body(*refs, nmaps, halves, tile, nb, lam_init):
    if nmaps == 2:
        q_ref, k_ref, vt_ref, lam_ref, sg_ref, o_ref, s_buf, mb_buf, m_sc, acc_sc = refs
    else:
        q_ref, k_ref, vt_ref, o_ref, s_buf, mb_buf, m_sc, acc_sc = refs
    hw = tile // halves
    nch = nmaps * halves
    dq = q_ref.shape[-1]
    dv = o_ref.shape[-1]
    npairs = nb * (nb + 1) // 2
    acc_sc[...] = jnp.zeros(acc_sc.shape, _F32)

    def nxt(pair):
        qi, t = pair
        last = t == qi
        return jnp.where(last, qi + 1, qi), jnp.where(last, 0, t + 1)

    def scores(pair, slot):
        qi, t = pair
        variant = (t == qi).astype(jnp.int32)
        for mp in range(nmaps):
            k = k_ref[mp, variant, pl.ds(pl.multiple_of(t * tile, tile), tile), :]
            for hf in range(halves):
                c = mp * halves + hf
                q = q_ref[mp, pl.ds(pl.multiple_of(qi * tile + hf * hw, hw), hw), :]
                s = lax.dot_general(k, q, (((1,), (1,)), ((), ())), preferred_element_type=_F32)
                s_buf[slot, c] = s
                mb_buf[slot, c] = jnp.max(s, axis=0, keepdims=True)

    def values(pair, slot):
        _, t = pair
        first = t == 0
        vt = vt_ref[t]
        for c in range(nch):
            m_prev = jnp.where(first, _NEG_INF, m_sc[c])
            m_new = jnp.maximum(m_prev, mb_buf[slot, c])
            alpha = jnp.exp2(m_prev - m_new)
            m_sc[c] = m_new
            p = jnp.exp2(s_buf[slot, c] - m_new).astype(_BF)
            acc_sc[slot, c] = (alpha * acc_sc[(slot - 1) % _SWEEP_SLOTS, c]
                               + jnp.dot(vt, p, preferred_element_type=_F32))

    def finish(pair, slot):
        qi, t = pair

        @pl.when(t == qi)
        def _():
            if nmaps == 2:
                lp = lam_ref[...]
                lam = (jnp.exp(jnp.sum(lp[0:1] * lp[1:2], axis=-1, keepdims=True))
                       - jnp.exp(jnp.sum(lp[2:3] * lp[3:4], axis=-1, keepdims=True)) + lam_init)
            normed = lambda c: acc_sc[slot, c, :dv, :] / acc_sc[slot, c, dv:dv + 1, :]
            for hf in range(halves):
                if nmaps == 2:
                    ot = normed(hf) - lam * normed(halves + hf)
                    o = _rms(ot.T, sg_ref[...], _A_SUBLN_EPS) * (1.0 - lam_init)
                else:
                    o = normed(hf).T
                row = pl.multiple_of(qi * tile + hf * hw, hw)
                o_ref[pl.ds(row, hw), :] = o.astype(o_ref.dtype)

    zero = jnp.int32(0)
    pair0 = (zero, zero)
    scores(pair0, 0)
    if npairs == 1:
        values(pair0, 0)
        finish(pair0, 0)
        return
    pair1 = nxt(pair0)
    scores(pair1, 1)

    def tick(carry, r):
        pc, pb, pa = carry
        scores(pa, (2 + r) % _SWEEP_SLOTS)
        values(pc, r)
        return pb, pa, nxt(pa)

    def trip(_, carry):
        done = []
        for r in range(_SWEEP_SLOTS):
            done.append(carry[0])
            carry = tick(carry, r)
        for r, pair in enumerate(done):
            finish(pair, r)
        return carry

    nticks = npairs - 2
    carry = lax.fori_loop(0, nticks // _SWEEP_SLOTS, trip, (pair0, pair1, nxt(pair1)))
    for r in range(nticks % _SWEEP_SLOTS):
        pair = carry[0]
        carry = tick(carry, r)
        finish(pair, r)
    before_last, last, _ = carry
    for pair, slot in ((before_last, (npairs - 2) % _SWEEP_SLOTS), (last, (npairs - 1) % _SWEEP_SLOTS)):
        values(pair, slot)
        finish(pair, slot)


def _sweep(q, k, v, heads, dq, dv, *, tile, halves=2, lam=None, subln_g=None, lam_init=0.0):
    b, nmaps, s, _ = q.shape
    nb = s // tile
    assert tile % _CHUNK == 0 and tile % halves == 0
    vt = v.reshape(b, nb, tile, heads, dv).transpose(0, 3, 1, 4, 2)
    vt = jnp.concatenate([vt, jnp.ones((b, heads, nb, _BF_ROWS, tile), vt.dtype)], axis=3)
    dve = dv + _BF_ROWS
    nch = nmaps * halves
    hw = tile // halves
    once = pl.Buffered(1)
    in_specs = [pl.BlockSpec((None, nmaps, s, dq), lambda i, h: (i, 0, 0, h), pipeline_mode=once),
                pl.BlockSpec((None, nmaps, 2, s, dq), lambda i, h: (i, 0, 0, 0, h),
                             pipeline_mode=once),
                pl.BlockSpec((None, None, nb, dve, tile), lambda i, h: (i, h, 0, 0, 0),
                             pipeline_mode=once)]
    args = [q, k, vt]
    if nmaps == 2:
        in_specs += [_resident(lam.shape), _resident((1, dv))]
        args += [lam, subln_g.reshape(1, dv)]
    scratch = [pltpu.VMEM((_SWEEP_SLOTS, nch, tile, hw), _F32),
               pltpu.VMEM((_SWEEP_SLOTS, nch, 1, hw), _F32),
               pltpu.VMEM((nch, 1, hw), _F32), pltpu.VMEM((_SWEEP_SLOTS, nch, dve, hw), _F32)]
    return pl.pallas_call(
        functools.partial(_sweep_body, nmaps=nmaps, halves=halves, tile=tile, nb=nb,
                          lam_init=lam_init),
        grid=(b, heads),
        in_specs=in_specs,
        out_specs=pl.BlockSpec((None, s, dv), lambda i, h: (i, 0, h)),
        out_shape=jax.ShapeDtypeStruct((b, s, heads * dv), _BF),
        scratch_shapes=scratch,
        compiler_params=_params(("parallel", "parallel")),
        name="sweep_diff" if nmaps == 2 else "sweep_mla",
    )(*args)


def _mm_body(*refs, has_res):
    if has_res:
        a_ref, w_ref, r_ref, o_ref = refs
    else:
        a_ref, w_ref, o_ref = refs
    y = jnp.dot(a_ref[...].astype(_BF), w_ref[...], preferred_element_type=_F32)
    if has_res:
        y = r_ref[...] + y
    o_ref[...] = y.astype(o_ref.dtype)


def _mm(a, w, res=None, out_dtype=_F32):
    b, s, kdim = a.shape
    n = w.shape[1]
    tm = min(_ROW_TILE, s)
    in_specs = [pl.BlockSpec((None, tm, kdim), lambda i, t: (i, t, 0)), _resident(w.shape)]
    args = [a, w]
    if res is not None:
        in_specs.append(pl.BlockSpec((None, tm, n), lambda i, t: (i, t, 0)))
        args.append(res)
    return pl.pallas_call(
        functools.partial(_mm_body, has_res=res is not None),
        grid=(b, s // tm),
        in_specs=in_specs,
        out_specs=pl.BlockSpec((None, tm, n), lambda i, t: (i, t, 0)),
        out_shape=jax.ShapeDtypeStruct((b, s, n), out_dtype),
        compiler_params=_params(("parallel", "parallel")),
        name="matmul_residual" if res is not None else "matmul",
    )(*args)


def _mla_proj_body(x_ref, g_ref, wd_ref, qg_ref, wq_ref, kvg_ref, wk_ref, wv_ref,
                   cos_ref, slo_ref, shi_ref, lat_ref, kr_ref, q_ref, k_ref, v_ref, *, qscale, mask_tile):
    tm = x_ref.shape[0]
    h = _rms(x_ref[...], g_ref[...], _NORM_EPS).astype(_BF)
    cos, slo, shi = cos_ref[...], slo_ref[...], shi_ref[...]
    if mask_tile:
        row0 = pl.program_id(1) * tm
        q_ext = _mask_lanes(tm, row0, mask_tile, _B_ROPE, False)
        k_ext = _mask_lanes(tm, row0, mask_tile, _B_ROPE, True)
    else:
        q_ext = k_ext = jnp.zeros((tm, _LANES), _F32)
    dn = jnp.dot(h, wd_ref[...], preferred_element_type=_F32)
    e0, e1 = _B_Q_LORA, _B_Q_LORA + _B_KV_LORA
    cq = _rms(dn[:, :e0], qg_ref[...], _NORM_EPS).astype(_BF)
    lat = _rms(dn[:, e0:e1], kvg_ref[...], _NORM_EPS)
    kr = _rope_lanes(dn[:, e1:e1 + _LANES], cos, slo, shi)
    lat_ref[...] = lat
    kr_ref[...] = kr[:, :_B_ROPE]
    hw = 2 * _LANES
    for hd in range(_B_HEADS):
        yq = jnp.dot(cq, wq_ref[:, hd * hw:(hd + 1) * hw], preferred_element_type=_F32)
        q_ref[:, hd * hw:hd * hw + _LANES] = (yq[:, :_LANES] * qscale).astype(_BF)
        q_ref[:, hd * hw + _LANES:(hd + 1) * hw] = (
            _rope_lanes(yq[:, _LANES:], cos, slo, shi) * qscale + q_ext).astype(_BF)
    kin = jnp.concatenate([lat, kr], axis=1).astype(_BF)
    for hd in range(_B_HEADS):
        yk = jnp.dot(kin, wk_ref[:, hd * hw:(hd + 1) * hw], preferred_element_type=_F32)
        for var in range(2):
            k_ref[var, :, hd * hw:hd * hw + _LANES] = yk[:, :_LANES].astype(_BF)
        k_ref[0, :, hd * hw + _LANES:(hd + 1) * hw] = yk[:, _LANES:].astype(_BF)
        k_ref[1, :, hd * hw + _LANES:(hd + 1) * hw] = (yk[:, _LANES:] + k_ext).astype(_BF)
    latb = lat.astype(_BF)
    for c in range(0, v_ref.shape[-1], _MXU_COLS):
        v_ref[:, c:c + _MXU_COLS] = jnp.dot(
            latb, wv_ref[:, c:c + _MXU_COLS], preferred_element_type=_F32).astype(_BF)


def _mla_weights(w_down, w_uq, w_uk, w_uv):
    d = w_down.shape[0]
    hw = 2 * _LANES
    wd = jnp.concatenate([w_down, jnp.zeros((d, _LANES - _B_ROPE), w_down.dtype)], axis=1)
    wq = w_uq.reshape(_B_Q_LORA, _B_HEADS, _B_NOPE + _B_ROPE)
    wq = jnp.pad(wq, ((0, 0), (0, 0), (0, hw - _B_NOPE - _B_ROPE))).reshape(_B_Q_LORA, _B_HEADS * hw)
    eye = jnp.eye(_B_ROPE, dtype=w_uk.dtype)
    wk_top = jnp.pad(w_uk, ((0, 0), (0, 0), (0, hw - _B_NOPE)))
    wk_mid = jnp.broadcast_to(jnp.pad(eye, ((0, 0), (_B_NOPE, hw - _B_NOPE - _B_ROPE)))[:, None, :],
                              (_B_ROPE, _B_HEADS, hw))
    wk_bot = jnp.zeros((_LANES - _B_ROPE, _B_HEADS, hw), w_uk.dtype)
    wk = jnp.concatenate([wk_top, wk_mid, wk_bot], axis=0).reshape(-1, _B_HEADS * hw)
    wv = w_uv.reshape(_B_KV_LORA, _B_HEADS * _B_VD)
    return wd.astype(_BF), wq.astype(_BF), wk.astype(_BF), wv.astype(_BF)


def _mla_proj(x, g, wts, q_norm_g, kv_norm_g, tabs, tm, mask_tile):
    b, s, d = x.shape
    wd, wq, wk, wv = wts
    xspec = pl.BlockSpec((None, tm, d), lambda i, t: (i, t, 0))
    tspec = pl.BlockSpec((tm, _LANES), lambda i, t: (t, 0))
    ospec = lambda n: pl.BlockSpec((None, tm, n), lambda i, t: (i, t, 0))
    oshape = lambda n, dt: jax.ShapeDtypeStruct((b, s, n), dt)
    return pl.pallas_call(
        functools.partial(_mla_proj_body, qscale=_LOG2E / math.sqrt(_B_NOPE + _B_ROPE),
                          mask_tile=mask_tile),
        grid=(b, s // tm),
        in_specs=[xspec, _resident((1, d)), _resident(wd.shape), _resident((1, _B_Q_LORA)),
                  _resident(wq.shape), _resident((1, _B_KV_LORA)), _resident(wk.shape),
                  _resident(wv.shape), tspec, tspec, tspec],
        out_specs=[ospec(_B_KV_LORA), ospec(_B_ROPE), ospec(wq.shape[1]),
                   pl.BlockSpec((None, 2, tm, wk.shape[1]), lambda i, t: (i, 0, t, 0)),
                   ospec(wv.shape[1])],
        out_shape=[oshape(_B_KV_LORA, _F32), oshape(_B_ROPE, _F32), oshape(wq.shape[1], _BF),
                   jax.ShapeDtypeStruct((b, 2, s, wk.shape[1]), _BF), oshape(wv.shape[1], _BF)],
        compiler_params=_params(("parallel", "parallel")),
        name="mla_proj",
    )(x, g.reshape(1, d), wd, q_norm_g.reshape(1, -1), wq, kv_norm_g.reshape(1, -1), wk, wv, *tabs)


def _conv_mix_body(x_ref, g_ref, win_ref, cw_ref, wout_ref, hist_ref, o_ref, hout_ref,
                   carry_sc, z_sc):
    tm, d = x_ref.shape

    @pl.when(pl.program_id(1) == 0)
    def _():
        carry_sc[...] = hist_ref[...]

    x = x_ref[...]
    h = _rms(x, g_ref[...], _NORM_EPS).astype(_BF)
    cw = _MXU_COLS
    for c in range(0, d, cw):
        gate_b = jnp.dot(h, win_ref[:, c:c + cw], preferred_element_type=_F32)
        gate_c = jnp.dot(h, win_ref[:, d + c:d + c + cw], preferred_element_type=_F32)
        val = jnp.dot(h, win_ref[:, 2 * d + c:2 * d + c + cw], preferred_element_type=_F32)
        u = gate_c * val
        prev = carry_sc[:, c:c + cw]
        y = (_shift_rows(prev, u, 2) * cw_ref[0:1, c:c + cw]
             + _shift_rows(prev, u, 1) * cw_ref[1:2, c:c + cw]
             + u * cw_ref[2:3, c:c + cw])
        z_sc[:, c:c + cw] = (gate_b * y).astype(_BF)
        carry_sc[:, c:c + cw] = u[tm - _SUBLANES:, :]
    hout_ref[...] = carry_sc[...]
    o_ref[...] = x + jnp.dot(z_sc[...], wout_ref[...], preferred_element_type=_F32)


def _conv_mix(x, g, w_in, conv_w, w_out, hist8, tm):
    b, s, d = x.shape
    xspec = pl.BlockSpec((None, tm, d), lambda i, t: (i, t, 0))
    hspec = pl.BlockSpec((None, _SUBLANES, d), lambda i, t: (i, 0, 0))
    return pl.pallas_call(
        _conv_mix_body,
        grid=(b, s // tm),
        in_specs=[xspec, _resident((1, d)), _resident(w_in.shape), _resident(conv_w.shape),
                  _resident(w_out.shape), hspec],
        out_specs=[xspec, hspec],
        out_shape=[jax.ShapeDtypeStruct((b, s, d), _F32),
                   jax.ShapeDtypeStruct((b, _SUBLANES, d), _F32)],
        scratch_shapes=[pltpu.VMEM((_SUBLANES, d), _F32), pltpu.VMEM((tm, d), _BF)],
        compiler_params=_params(("arbitrary", "arbitrary")),
        name="conv_mixer",
    )(x, g.reshape(1, d), w_in, conv_w, w_out, hist8)


_POOL_HALO = 16


def _pool_mix_body(x_ref, g_ref, wg_ref, sc_ref, hist_ref, o_ref, hout_ref, carry_sc, *, hist_valid):
    tm, d = x_ref.shape
    t = pl.program_id(1)

    @pl.when(t == 0)
    def _():
        carry_sc[...] = hist_ref[...]

    x = x_ref[...]
    h = _rms(x, g_ref[...], _NORM_EPS)
    gw = d // len(_D_WINDOWS)
    tpos = t * tm + lax.broadcasted_iota(jnp.int32, (tm, gw), 0)
    outs = []
    for gi, w in enumerate(_D_WINDOWS):
        hg = h[:, gi * gw:(gi + 1) * gw]
        acc = jnp.concatenate([carry_sc[:, gi * gw:(gi + 1) * gw], hg], axis=0)
        k = 1
        while k < w:
            acc = acc + pltpu.roll(acc, k, axis=0)
            k *= 2
        cnt = jnp.minimum(tpos + (hist_valid + 1), w).astype(_F32)
        pooled = acc[_POOL_HALO:] / cnt
        outs.append(jnp.dot((pooled - hg).astype(_BF), wg_ref[gi], preferred_element_type=_F32))
    o_ref[...] = x + jnp.concatenate(outs, axis=1) * sc_ref[...]
    tail = h[tm - _POOL_HALO:, :]
    carry_sc[...] = tail
    hout_ref[...] = tail


def _pool_mix(x, g, w_group, scale, hist16, hist_valid, tm):
    b, s, d = x.shape
    assert all(w & (w - 1) == 0 and w <= _POOL_HALO for w in _D_WINDOWS)
    xspec = pl.BlockSpec((None, tm, d), lambda i, t: (i, t, 0))
    hspec = pl.BlockSpec((None, _POOL_HALO, d), lambda i, t: (i, 0, 0))
    return pl.pallas_call(
        functools.partial(_pool_mix_body, hist_valid=hist_valid),
        grid=(b, s // tm),
        in_specs=[xspec, _resident((1, d)), _resident(w_group.shape), _resident((1, d)), hspec],
        out_specs=[xspec, hspec],
        out_shape=[jax.ShapeDtypeStruct((b, s, d), _F32),
                   jax.ShapeDtypeStruct((b, _POOL_HALO, d), _F32)],
        scratch_shapes=[pltpu.VMEM((_POOL_HALO, d), _F32)],
        compiler_params=_params(("arbitrary", "arbitrary")),
        name="pool_mixer",
    )(x, g.reshape(1, d), w_group, scale.reshape(1, d), hist16)


def _ffn_body(*refs, final):
    if final:
        (x_ref, g_ref, wg_ref, wu_ref, cw_ref, cb_ref, wd_ref, hist_ref, fg_ref,
         o_ref, hout_ref, carry_sc, act_sc) = refs
    else:
        (x_ref, g_ref, wg_ref, wu_ref, cw_ref, cb_ref, wd_ref, hist_ref,
         o_ref, hout_ref, carry_sc, act_sc) = refs
    tm = x_ref.shape[0]
    f = wg_ref.shape[1]

    @pl.when(pl.program_id(1) == 0)
    def _():
        carry_sc[...] = hist_ref[...]

    x = x_ref[...]
    h = _rms(x, g_ref[...], _NORM_EPS).astype(_BF)
    cw = _MXU_COLS
    for c in range(0, f, cw):
        gate = jnp.dot(h, wg_ref[:, c:c + cw], preferred_element_type=_F32)
        up = jnp.dot(h, wu_ref[:, c:c + cw], preferred_element_type=_F32)
        prev = carry_sc[:, c:c + cw]
        y = (_shift_rows(prev, gate, 2) * cw_ref[0:1, c:c + cw]
             + _shift_rows(prev, gate, 1) * cw_ref[1:2, c:c + cw]
             + gate * cw_ref[2:3, c:c + cw] + cb_ref[:, c:c + cw])
        act_sc[:, c:c + cw] = (y / (1.0 + jnp.exp(-y)) * up).astype(_BF)
        carry_sc[:, c:c + cw] = gate[tm - _SUBLANES:, :]
    hout_ref[...] = carry_sc[...]
    out = x + jnp.dot(act_sc[...], wd_ref[...], preferred_element_type=_F32)
    if final:
        out = _rms(out, fg_ref[...], _NORM_EPS)
    o_ref[...] = out


def _ffn(x, g, w_gate, w_up, conv_w, conv_b, w_down, hist8, final_g, tm):
    b, s, d = x.shape
    f = w_gate.shape[1]
    assert f % _MXU_COLS == 0
    xspec = pl.BlockSpec((None, tm, d), lambda i, t: (i, t, 0))
    hspec = pl.BlockSpec((None, _SUBLANES, f), lambda i, t: (i, 0, 0))
    in_specs = [xspec, _resident((1, d)), _resident(w_gate.shape), _resident(w_up.shape),
                _resident(conv_w.shape), _resident((1, f)), _resident(w_down.shape), hspec]
    args = [x, g.reshape(1, d), w_gate, w_up, conv_w, conv_b.reshape(1, f), w_down, hist8]
    if final_g is not None:
        in_specs.append(_resident((1, d)))
        args.append(final_g.reshape(1, d))
    return pl.pallas_call(
        functools.partial(_ffn_body, final=final_g is not None),
        grid=(b, s // tm),
        in_specs=in_specs,
        out_specs=[xspec, hspec],
        out_shape=[jax.ShapeDtypeStruct((b, s, d), _F32),
                   jax.ShapeDtypeStruct((b, _SUBLANES, f), _F32)],
        scratch_shapes=[pltpu.VMEM((_SUBLANES, f), _F32), pltpu.VMEM((tm, f), _BF)],
        compiler_params=_params(("arbitrary", "arbitrary")),
        name="conv_ffn",
    )(*args)


def _pad_hist(hist, rows):
    return jnp.pad(hist, ((0, 0), (rows - hist.shape[1], 0), (0, 0)))


def kernel(x_prompt, x_sample, cache_a_k, cache_a_v, cache_b_latent, cache_b_krope, state_c_conv, state_d_pool, state_ffn_conv, norm_mix_g, norm_ffn_g, norm_final_g, a_w_qkv, a_lam, a_subln_g, a_w_o, b_w_down, b_q_norm_g, b_w_uq, b_kv_norm_g, b_w_uk, b_w_uv, b_w_o, c_w_in, c_conv_w, c_w_out, d_w_group, d_scale, ffn_w_gate, ffn_w_up, ffn_conv_w, ffn_conv_b, ffn_w_down):
    depth = norm_mix_g.shape[0]
    n_p, seq, d = x_prompt.shape
    n_s, t_new, _ = x_sample.shape
    past = cache_a_k.shape[2]
    f = ffn_w_gate.shape[-1]
    tm_p, tm_s = min(_ROW_TILE, seq), t_new
    ta = min(_ATT_TILE, seq)
    tabs_p = _rope_tables(jnp.arange(seq, dtype=jnp.int32))
    tabs_s = _rope_tables(past + jnp.arange(t_new, dtype=jnp.int32))
    xp, xs = x_prompt, x_sample
    outs = {k: [] for k in ("ak_p", "av_p", "bl_p", "br_p", "cc_p", "dp_p", "fc_p",
                            "ak_s", "av_s", "bl_s", "br_s", "cc_s", "dp_s", "fc_s")}
    for i in range(depth):
        m, j = i % _N_MIXERS, i // _N_MIXERS
        g_mix = norm_mix_g[i]
        if m == 0:
            lam_init = 0.8 - 0.6 * math.exp(-0.3 * i)
            w_qkv, w_o = a_w_qkv[j].astype(_BF), a_w_o[j].astype(_BF)
            qm, kf, km, vf, vb = _diff_proj(xp, g_mix, w_qkv, tabs_p, tm_p, ta)
            op = _sweep(qm, km, vb, _A_HEADS, _LANES, 2 * _A_HD, tile=ta,
                        lam=a_lam[j], subln_g=a_subln_g[j], lam_init=lam_init)
            xp = _mm(op, w_o, res=xp)
            outs["ak_p"].append(kf.reshape(n_p, seq, _A_HEADS, 2, _A_HD))
            outs["av_p"].append(vf.reshape(n_p, seq, _A_HEADS, 2 * _A_HD))
            attn = functools.partial(_flash, heads=_A_HEADS, dq=_LANES, dv=2 * _A_HD, nmaps=2,
                                     lam=a_lam[j], subln_g=a_subln_g[j], lam_init=lam_init)
            qm, kf, km, vf, vb = _diff_proj(xs, g_mix, w_qkv, tabs_s, tm_s, 0)
            k_cache = jnp.pad(cache_a_k[j].astype(_BF).transpose(0, 3, 1, 2, 4),
                              ((0, 0),) * 4 + ((0, _LANES - _A_HD),))
            k_all = jnp.concatenate([k_cache.reshape(n_s, 2, past, _A_HEADS * _LANES), km[:, :, 0]],
                                    axis=2)
            v_all = jnp.concatenate([cache_a_v[j].reshape(n_s, past, d).astype(_BF), vb], axis=1)
            osm = attn(qm, k_all, v_all, causal=False, tq=t_new, tk=past + t_new)
            xs = _mm(osm, w_o, res=xs)
            outs["ak_s"].append(kf.reshape(n_s, t_new, _A_HEADS, 2, _A_HD))
            outs["av_s"].append(vf.reshape(n_s, t_new, _A_HEADS, 2 * _A_HD))
        elif m == 1:
            wts = _mla_weights(b_w_down[j], b_w_uq[j], b_w_uk[j], b_w_uv[j])
            w_o = b_w_o[j].astype(_BF)
            hq = 2 * _LANES
            attn = functools.partial(_flash, heads=_B_HEADS, dq=hq, dv=_B_VD)
            lat, kr, q, k2, v = _mla_proj(xp, g_mix, wts, b_q_norm_g[j], b_kv_norm_g[j], tabs_p,
                                          tm_p, ta)
            op = _sweep(q[:, None], k2[:, None], v, _B_HEADS, hq, _B_VD, tile=ta)
            xp = _mm(op, w_o, res=xp)
            outs["bl_p"].append(lat)
            outs["br_p"].append(kr)
            lat, kr, q, k2, v = _mla_proj(xs, g_mix, wts, b_q_norm_g[j], b_kv_norm_g[j], tabs_s,
                                          tm_s, 0)
            kin = jnp.concatenate([cache_b_latent[j], cache_b_krope[j],
                                   jnp.zeros((n_s, past, _LANES - _B_ROPE), _F32)], axis=-1).astype(_BF)
            k_cache = _mm(kin, wts[2], out_dtype=_BF)
            v_cache = _mm(cache_b_latent[j].astype(_BF), wts[3], out_dtype=_BF)
            k_all = jnp.concatenate([k_cache, k2[:, 0]], axis=1)
            v_all = jnp.concatenate([v_cache, v], axis=1)
            osm = attn(q[:, None], k_all[:, None], v_all, causal=False, tq=t_new, tk=past + t_new)
            xs = _mm(osm, w_o, res=xs)
            outs["bl_s"].append(lat)
            outs["br_s"].append(kr)
        elif m == 2:
            w_in, w_out = c_w_in[j].astype(_BF), c_w_out[j].astype(_BF)
            xp, hc = _conv_mix(xp, g_mix, w_in, c_conv_w[j], w_out,
                               jnp.zeros((n_p, _SUBLANES, d), _F32), tm_p)
            outs["cc_p"].append(hc[:, -2:])
            xs, hc = _conv_mix(xs, g_mix, w_in, c_conv_w[j], w_out,
                               _pad_hist(state_c_conv[j], _SUBLANES), tm_s)
            outs["cc_s"].append(hc[:, -2:])
        else:
            w_grp = d_w_group[j].astype(_BF)
            xp, hd = _pool_mix(xp, g_mix, w_grp, d_scale[j],
                               jnp.zeros((n_p, _POOL_HALO, d), _F32), 0, tm_p)
            outs["dp_p"].append(hd[:, -_D_HIST:])
            xs, hd = _pool_mix(xs, g_mix, w_grp, d_scale[j],
                               _pad_hist(state_d_pool[j], _POOL_HALO), _D_HIST, tm_s)
            outs["dp_s"].append(hd[:, -_D_HIST:])
        wg, wu, wd = ffn_w_gate[i].astype(_BF), ffn_w_up[i].astype(_BF), ffn_w_down[i].astype(_BF)
        final_g = norm_final_g if i == depth - 1 else None
        xp, hf = _ffn(xp, norm_ffn_g[i], wg, wu, ffn_conv_w[i], ffn_conv_b[i], wd,
                      jnp.zeros((n_p, _SUBLANES, f), _F32), final_g, tm_p)
        outs["fc_p"].append(hf[:, -2:])
        xs, hf = _ffn(xs, norm_ffn_g[i], wg, wu, ffn_conv_w[i], ffn_conv_b[i], wd,
                      _pad_hist(state_ffn_conv[i], _SUBLANES), final_g, tm_s)
        outs["fc_s"].append(hf[:, -2:])
    st = lambda k: jnp.stack(outs[k])
    return (xp, xs, st("ak_p"), st("av_p"), st("bl_p"), st("br_p"), st("cc_p"), st("dp_p"), st("fc_p"),
            st("ak_s"), st("av_s"), st("bl_s"), st("br_s"), st("cc_s"), st("dp_s"), st("fc_s"))
```

```python
import functools
import math

import jax
import jax.numpy as jnp
from jax import lax
from jax.experimental import pallas as pl
from jax.experimental.pallas import tpu as pltpu

_BF = jnp.bfloat16
_F32 = jnp.float32

_CHUNK = 64
_ROPE_THETA = 10000.0
_NORM_EPS = 1e-6
_NEG_INF = -1e30
_A_HEADS = 8
_A_HD = 64
_A_SUBLN_EPS = 1e-5
_B_HEADS = 8
_B_NOPE = 128
_B_ROPE = 64
_B_VD = 128
_B_Q_LORA = 384
_B_KV_LORA = 256
_D_WINDOWS = (2, 4, 8, 16)
_D_HIST = 15
_N_MIXERS = 4
_LOG2E = math.log2(math.e)

_LANES = 128
_SUBLANES = 8
_MXU_COLS = 256
_BF_ROWS = 16
_SWEEP_SLOTS = 4
_TRIP_TICKS = 8
_VMEM_LIMIT = 56 * 1024 * 1024

_ROW_TILE = 512
_ATT_TILE = 512


def _params(sem):
    return pltpu.CompilerParams(dimension_semantics=sem, vmem_limit_bytes=_VMEM_LIMIT)


def _resident(shape):
    nd = len(shape)
    return pl.BlockSpec(shape, lambda *_: (0,) * nd, pipeline_mode=pl.Buffered(1))


def _rms(x, g, eps):
    ms = jnp.mean(x * x, axis=-1, keepdims=True)
    return x * lax.rsqrt(ms + eps) * g


def _rope_lanes(y, cos, sin_lo, sin_hi):
    return (y * cos + pltpu.roll(y, _LANES - 32, axis=1) * sin_lo
            + pltpu.roll(y, 32, axis=1) * sin_hi)


def _rope_tables(pos):
    d = _A_HD
    inv = jnp.power(_ROPE_THETA, -jnp.arange(0, d, 2, dtype=_F32) / d)
    ang = pos.astype(_F32)[:, None] * inv[None, :]
    cos, sin = jnp.cos(ang), jnp.sin(ang)
    zero = jnp.zeros_like(sin)
    cos_t = jnp.tile(cos, (1, 4))
    sin_lo = jnp.tile(jnp.concatenate([-sin, zero], axis=1), (1, 2))
    sin_hi = jnp.tile(jnp.concatenate([zero, sin], axis=1), (1, 2))
    return cos_t, sin_lo, sin_hi


def _shift_rows(carry, cur, k):
    ext = jnp.concatenate([carry, cur], axis=0)
    return pltpu.roll(ext, k, axis=0)[carry.shape[0]:]


def _mask_lanes(rows, row0, tile, base, for_keys):
    r = row0 + lax.broadcasted_iota(jnp.int32, (rows, _LANES), 0)
    chunk = (r % tile) // _CHUNK
    c = lax.broadcasted_iota(jnp.int32, (rows, _LANES), 1) - base
    n = tile // _CHUNK
    if for_keys:
        return jnp.where(c == chunk, 1.0, 0.0)
    return jnp.where(c > chunk, jnp.where(c < n, _NEG_INF, 0.0), 0.0)


def _diff_proj_body(x_ref, g_ref, w_ref, cos_ref, slo_ref, shi_ref,
                    qm_ref, kf_ref, km_ref, vf_ref, vb_ref, *, qscale, mask_tile):
    tm, d = x_ref.shape
    h = _rms(x_ref[...], g_ref[...], _NORM_EPS).astype(_BF)
    cos, slo, shi = cos_ref[...], slo_ref[...], shi_ref[...]
    if mask_tile:
        row0 = pl.program_id(1) * tm
        q_ext = _mask_lanes(tm, row0, mask_tile, _A_HD, False)
        k_ext = _mask_lanes(tm, row0, mask_tile, _A_HD, True)
    else:
        q_ext = k_ext = jnp.zeros((tm, _LANES), _F32)
    first_half = lax.broadcasted_iota(jnp.int32, (tm, _LANES), 1) < _A_HD
    zero = jnp.zeros((tm, _LANES), _F32)
    cw = _MXU_COLS
    for c in range(0, d, cw):
        yq = jnp.dot(h, w_ref[:, c:c + cw], preferred_element_type=_F32)
        yk = jnp.dot(h, w_ref[:, d + c:d + c + cw], preferred_element_type=_F32)
        yv = jnp.dot(h, w_ref[:, 2 * d + c:2 * d + c + cw], preferred_element_type=_F32)
        for s in range(0, cw, _LANES):
            cols = slice(c + s, c + s + _LANES)
            q = _rope_lanes(yq[:, s:s + _LANES], cos, slo, shi) * qscale
            k = _rope_lanes(yk[:, s:s + _LANES], cos, slo, shi)
            kf_ref[:, cols] = k
            for mp in range(2):
                qh = q if mp == 0 else pltpu.roll(q, _A_HD, axis=1)
                kh = k if mp == 0 else pltpu.roll(k, _A_HD, axis=1)
                qm_ref[mp, :, cols] = jnp.where(first_half, qh, q_ext).astype(_BF)
                km_ref[mp, 0, :, cols] = jnp.where(first_half, kh, zero).astype(_BF)
                km_ref[mp, 1, :, cols] = jnp.where(first_half, kh, k_ext).astype(_BF)
        vf_ref[:, c:c + cw] = yv
        vb_ref[:, c:c + cw] = yv.astype(_BF)


def _diff_proj(x, g, w_qkv, tabs, tm, mask_tile):
    b, s, d = x.shape
    row = lambda dt: jax.ShapeDtypeStruct((b, s, d), dt)
    xspec = pl.BlockSpec((None, tm, d), lambda i, t: (i, t, 0))
    tspec = pl.BlockSpec((tm, _LANES), lambda i, t: (t, 0))
    return pl.pallas_call(
        functools.partial(_diff_proj_body, qscale=_LOG2E / math.sqrt(_A_HD), mask_tile=mask_tile),
        grid=(b, s // tm),
        in_specs=[xspec, _resident((1, d)), _resident(w_qkv.shape), tspec, tspec, tspec],
        out_specs=[pl.BlockSpec((None, 2, tm, d), lambda i, t: (i, 0, t, 0)), xspec,
                   pl.BlockSpec((None, 2, 2, tm, d), lambda i, t: (i, 0, 0, t, 0)), xspec, xspec],
        out_shape=[jax.ShapeDtypeStruct((b, 2, s, d), _BF), row(_F32),
                   jax.ShapeDtypeStruct((b, 2, 2, s, d), _BF), row(_F32), row(_BF)],
        compiler_params=_params(("parallel", "parallel")),
        name="diff_proj",
    )(x, g.reshape(1, d), w_qkv, *tabs)


def _flash_body(*refs, nmaps, causal, tq, tk, lam_init):
    if nmaps == 2:
        q_ref, k_ref, v_ref, lam_ref, sg_ref, o_ref, m_sc, l_sc, acc_sc = refs
    else:
        q_ref, k_ref, v_ref, o_ref, m_sc, l_sc, acc_sc = refs
    qi, ki = pl.program_id(2), pl.program_id(3)
    nk = pl.num_programs(3)

    @pl.when(ki == 0)
    def _():
        m_sc[...] = jnp.full(m_sc.shape, _NEG_INF, _F32)
        l_sc[...] = jnp.zeros(l_sc.shape, _F32)
        acc_sc[...] = jnp.zeros(acc_sc.shape, _F32)

    def step(masked):
        v = v_ref[...]
        if masked:
            rows = lax.broadcasted_iota(jnp.int32, (tq, tk), 0) // _CHUNK
            cols = lax.broadcasted_iota(jnp.int32, (tq, tk), 1) // _CHUNK
            visible = cols <= rows
        for c in range(nmaps):
            s = lax.dot_general(q_ref[c], k_ref[c], (((1,), (1,)), ((), ())),
                                preferred_element_type=_F32)
            if masked:
                s = jnp.where(visible, s, _NEG_INF)
            m_prev = m_sc[c]
            m_new = jnp.maximum(m_prev, jnp.max(s, axis=-1, keepdims=True))
            alpha = jnp.exp2(m_prev - m_new)
            p = jnp.exp2(s - m_new)
            l_sc[c] = alpha * l_sc[c] + jnp.sum(p, axis=-1, keepdims=True)
            acc_sc[c] = alpha * acc_sc[c] + jnp.dot(p.astype(_BF), v, preferred_element_type=_F32)
            m_sc[c] = m_new

    if causal:
        pl.when(ki < qi)(lambda: step(False))
        pl.when(ki == qi)(lambda: step(True))
    else:
        step(False)

    @pl.when(ki == nk - 1)
    def _():
        if nmaps == 2:
            lp = lam_ref[...]
            lam = (jnp.exp(jnp.sum(lp[0:1] * lp[1:2], axis=-1, keepdims=True))
                   - jnp.exp(jnp.sum(lp[2:3] * lp[3:4], axis=-1, keepdims=True)) + lam_init)
            o = acc_sc[0] / l_sc[0] - lam * (acc_sc[1] / l_sc[1])
            o = _rms(o, sg_ref[...], _A_SUBLN_EPS) * (1.0 - lam_init)
        else:
            o = acc_sc[0] / l_sc[0]
        o_ref[...] = o.astype(o_ref.dtype)


def _flash(q, k, v, heads, dq, dv, *, causal, tq, tk, nmaps=1, lam=None, subln_g=None, lam_init=0.0):
    b, _, sq, _ = q.shape
    sk = k.shape[2]
    nq, nk = sq // tq, sk // tk
    if causal:
        assert tq == tk and tq % _CHUNK == 0 and sq == sk
        kblk = lambda qi, ki: jnp.minimum(ki, qi)
    else:
        kblk = lambda qi, ki: ki
    in_specs = [pl.BlockSpec((None, nmaps, tq, dq), lambda i, h, qi, ki: (i, 0, qi, h)),
                pl.BlockSpec((None, nmaps, tk, dq), lambda i, h, qi, ki: (i, 0, kblk(qi, ki), h)),
                pl.BlockSpec((None, tk, dv), lambda i, h, qi, ki: (i, kblk(qi, ki), h))]
    args = [q, k, v]
    if nmaps == 2:
        in_specs += [_resident(lam.shape), _resident((1, dv))]
        args += [lam, subln_g.reshape(1, dv)]
    return pl.pallas_call(
        functools.partial(_flash_body, nmaps=nmaps, causal=causal, tq=tq, tk=tk, lam_init=lam_init),
        grid=(b, heads, nq, nk),
        in_specs=in_specs,
        out_specs=pl.BlockSpec((None, tq, dv), lambda i, h, qi, ki: (i, qi, h)),
        out_shape=jax.ShapeDtypeStruct((b, sq, heads * dv), _BF),
        scratch_shapes=[pltpu.VMEM((nmaps, tq, 1), _F32), pltpu.VMEM((nmaps, tq, 1), _F32),
                        pltpu.VMEM((nmaps, tq, dv), _F32)],
        compiler_params=_params(("parallel", "parallel", "parallel", "arbitrary")),
        name="flash_diff" if nmaps == 2 else "flash_mla",
    )(*args)


def _sweep_body(*refs, nmaps, halves, tile, nb, lam_init):
    if nmaps == 2:
        q_ref, k_ref, vt_ref, lam_ref, sg_ref, o_ref, s_buf, mb_buf, m_sc, acc_sc = refs
    else:
        q_ref, k_ref, vt_ref, o_ref, s_buf, mb_buf, m_sc, acc_sc = refs
    hw = tile // halves
    nch = nmaps * halves
    dq = q_ref.shape[-1]
    dv = o_ref.shape[-1]
    npairs = nb * (nb + 1) // 2
    acc_sc[...] = jnp.zeros(acc_sc.shape, _F32)

    def nxt(pair):
        qi, t = pair
        last = t == qi
        return jnp.where(last, qi + 1, qi), jnp.where(last, 0, t + 1)

    def scores(pair, slot):
        qi, t = pair
        variant = (t == qi).astype(jnp.int32)
        for mp in range(nmaps):
            k = k_ref[mp, variant, pl.ds(pl.multiple_of(t * tile, tile), tile), :]
            for hf in range(halves):
                c = mp * halves + hf
                q = q_ref[mp, pl.ds(pl.multiple_of(qi * tile + hf * hw, hw), hw), :]
                s = lax.dot_general(k, q, (((1,), (1,)), ((), ())), preferred_element_type=_F32)
                s_buf[slot, c] = s
                mb_buf[slot, c] = jnp.max(s, axis=0, keepdims=True)

    def values(pair, pos):
        _, t = pair
        first = t == 0
        slot = pos % _SWEEP_SLOTS
        vt = vt_ref[t]
        for c in range(nch):
            m_prev = jnp.where(first, _NEG_INF, m_sc[c])
            m_new = jnp.maximum(m_prev, mb_buf[slot, c])
            alpha = jnp.exp2(m_prev - m_new)
            m_sc[c] = m_new
            p = jnp.exp2(s_buf[slot, c] - m_new).astype(_BF)
            acc_sc[pos, c] = (alpha * acc_sc[(pos - 1) % _TRIP_TICKS, c]
                              + jnp.dot(vt, p, preferred_element_type=_F32))

    def finish(pair, pos):
        qi, t = pair

        @pl.when(t == qi)
        def _():
            if nmaps == 2:
                lp = lam_ref[...]
                lam = (jnp.exp(jnp.sum(lp[0:1] * lp[1:2], axis=-1, keepdims=True))
                       - jnp.exp(jnp.sum(lp[2:3] * lp[3:4], axis=-1, keepdims=True)) + lam_init)
            normed = lambda c: acc_sc[pos, c, :dv, :] / acc_sc[pos, c, dv:dv + 1, :]
            for hf in range(halves):
                if nmaps == 2:
                    ot = normed(hf) - lam * normed(halves + hf)
                    o = _rms(ot.T, sg_ref[...], _A_SUBLN_EPS) * (1.0 - lam_init)
                else:
                    o = normed(hf).T
                row = pl.multiple_of(qi * tile + hf * hw, hw)
                o_ref[pl.ds(row, hw), :] = o.astype(o_ref.dtype)

    zero = jnp.int32(0)
    pair0 = (zero, zero)
    scores(pair0, 0)
    if npairs == 1:
        values(pair0, 0)
        finish(pair0, 0)
        return
    pair1 = nxt(pair0)
    scores(pair1, 1)

    def tick(carry, r):
        pc, pb, pa = carry
        scores(pa, (2 + r) % _SWEEP_SLOTS)
        values(pc, r)
        return pb, pa, nxt(pa)

    def trip(_, carry):
        done = []
        for r in range(_TRIP_TICKS):
            done.append(carry[0])
            carry = tick(carry, r)
        for r, pair in enumerate(done):
            finish(pair, r)
        return carry

    nticks = npairs - 2
    carry = lax.fori_loop(0, nticks // _TRIP_TICKS, trip, (pair0, pair1, nxt(pair1)))
    for r in range(nticks % _TRIP_TICKS):
        pair = carry[0]
        carry = tick(carry, r)
        finish(pair, r)
    before_last, last, _ = carry
    for pair, pos in ((before_last, (npairs - 2) % _TRIP_TICKS), (last, (npairs - 1) % _TRIP_TICKS)):
        values(pair, pos)
        finish(pair, pos)


def _sweep(q, k, v, heads, dq, dv, *, tile, halves=2, lam=None, subln_g=None, lam_init=0.0):
    b, nmaps, s, _ = q.shape
    nb = s // tile
    assert tile % _CHUNK == 0 and tile % halves == 0
    vt = v.reshape(b, nb, tile, heads, dv).transpose(0, 3, 1, 4, 2)
    vt = jnp.concatenate([vt, jnp.ones((b, heads, nb, _BF_ROWS, tile), vt.dtype)], axis=3)
    dve = dv + _BF_ROWS
    nch = nmaps * halves
    hw = tile // halves
    once = pl.Buffered(1)
    in_specs = [pl.BlockSpec((None, nmaps, s, dq), lambda i, h: (i, 0, 0, h), pipeline_mode=once),
                pl.BlockSpec((None, nmaps, 2, s, dq), lambda i, h: (i, 0, 0, 0, h),
                             pipeline_mode=once),
                pl.BlockSpec((None, None, nb, dve, tile), lambda i, h: (i, h, 0, 0, 0),
                             pipeline_mode=once)]
    args = [q, k, vt]
    if nmaps == 2:
        in_specs += [_resident(lam.shape), _resident((1, dv))]
        args += [lam, subln_g.reshape(1, dv)]
    scratch = [pltpu.VMEM((_SWEEP_SLOTS, nch, tile, hw), _F32),
               pltpu.VMEM((_SWEEP_SLOTS, nch, 1, hw), _F32),
               pltpu.VMEM((nch, 1, hw), _F32), pltpu.VMEM((_TRIP_TICKS, nch, dve, hw), _F32)]
    return pl.pallas_call(
        functools.partial(_sweep_body, nmaps=nmaps, halves=halves, tile=tile, nb=nb,
                          lam_init=lam_init),
        grid=(b, heads),
        in_specs=in_specs,
        out_specs=pl.BlockSpec((None, s, dv), lambda i, h: (i, 0, h)),
        out_shape=jax.ShapeDtypeStruct((b, s, heads * dv), _BF),
        scratch_shapes=scratch,
        compiler_params=_params(("parallel", "parallel")),
        name="sweep_diff" if nmaps == 2 else "sweep_mla",
    )(*args)


def _mm_body(*refs, has_res):
    if has_res:
        a_ref, w_ref, r_ref, o_ref = refs
    else:
        a_ref, w_ref, o_ref = refs
    y = jnp.dot(a_ref[...].astype(_BF), w_ref[...], preferred_element_type=_F32)
    if has_res:
        y = r_ref[...] + y
    o_ref[...] = y.astype(o_ref.dtype)


def _mm(a, w, res=None, out_dtype=_F32):
    b, s, kdim = a.shape
    n = w.shape[1]
    tm = min(_ROW_TILE, s)
    in_specs = [pl.BlockSpec((None, tm, kdim), lambda i, t: (i, t, 0)), _resident(w.shape)]
    args = [a, w]
    if res is not None:
        in_specs.append(pl.BlockSpec((None, tm, n), lambda i, t: (i, t, 0)))
        args.append(res)
    return pl.pallas_call(
        functools.partial(_mm_body, has_res=res is not None),
        grid=(b, s // tm),
        in_specs=in_specs,
        out_specs=pl.BlockSpec((None, tm, n), lambda i, t: (i, t, 0)),
        out_shape=jax.ShapeDtypeStruct((b, s, n), out_dtype),
        compiler_params=_params(("parallel", "parallel")),
        name="matmul_residual" if res is not None else "matmul",
    )(*args)


def _mla_proj_body(x_ref, g_ref, wd_ref, qg_ref, wq_ref, kvg_ref, wk_ref, wv_ref,
                   cos_ref, slo_ref, shi_ref, lat_ref, kr_ref, q_ref, k_ref, v_ref, *, qscale, mask_tile):
    tm = x_ref.shape[0]
    h = _rms(x_ref[...], g_ref[...], _NORM_EPS).astype(_BF)
    cos, slo, shi = cos_ref[...], slo_ref[...], shi_ref[...]
    if mask_tile:
        row0 = pl.program_id(1) * tm
        q_ext = _mask_lanes(tm, row0, mask_tile, _B_ROPE, False)
        k_ext = _mask_lanes(tm, row0, mask_tile, _B_ROPE, True)
    else:
        q_ext = k_ext = jnp.zeros((tm, _LANES), _F32)
    dn = jnp.dot(h, wd_ref[...], preferred_element_type=_F32)
    e0, e1 = _B_Q_LORA, _B_Q_LORA + _B_KV_LORA
    cq = _rms(dn[:, :e0], qg_ref[...], _NORM_EPS).astype(_BF)
    lat = _rms(dn[:, e0:e1], kvg_ref[...], _NORM_EPS)
    kr = _rope_lanes(dn[:, e1:e1 + _LANES], cos, slo, shi)
    lat_ref[...] = lat
    kr_ref[...] = kr[:, :_B_ROPE]
    hw = 2 * _LANES
    for hd in range(_B_HEADS):
        yq = jnp.dot(cq, wq_ref[:, hd * hw:(hd + 1) * hw], preferred_element_type=_F32)
        q_ref[:, hd * hw:hd * hw + _LANES] = (yq[:, :_LANES] * qscale).astype(_BF)
        q_ref[:, hd * hw + _LANES:(hd + 1) * hw] = (
            _rope_lanes(yq[:, _LANES:], cos, slo, shi) * qscale + q_ext).astype(_BF)
    kin = jnp.concatenate([lat, kr], axis=1).astype(_BF)
    for hd in range(_B_HEADS):
        yk = jnp.dot(kin, wk_ref[:, hd * hw:(hd + 1) * hw], preferred_element_type=_F32)
        for var in range(2):
            k_ref[var, :, hd * hw:hd * hw + _LANES] = yk[:, :_LANES].astype(_BF)
        k_ref[0, :, hd * hw + _LANES:(hd + 1) * hw] = yk[:, _LANES:].astype(_BF)
        k_ref[1, :, hd * hw + _LANES:(hd + 1) * hw] = (yk[:, _LANES:] + k_ext).astype(_BF)
    latb = lat.astype(_BF)
    for c in range(0, v_ref.shape[-1], _MXU_COLS):
        v_ref[:, c:c + _MXU_COLS] = jnp.dot(
            latb, wv_ref[:, c:c + _MXU_COLS], preferred_element_type=_F32).astype(_BF)


def _mla_weights(w_down, w_uq, w_uk, w_uv):
    d = w_down.shape[0]
    hw = 2 * _LANES
    wd = jnp.concatenate([w_down, jnp.zeros((d, _LANES - _B_ROPE), w_down.dtype)], axis=1)
    wq = w_uq.reshape(_B_Q_LORA, _B_HEADS, _B_NOPE + _B_ROPE)
    wq = jnp.pad(wq, ((0, 0), (0, 0), (0, hw - _B_NOPE - _B_ROPE))).reshape(_B_Q_LORA, _B_HEADS * hw)
    eye = jnp.eye(_B_ROPE, dtype=w_uk.dtype)
    wk_top = jnp.pad(w_uk, ((0, 0), (0, 0), (0, hw - _B_NOPE)))
    wk_mid = jnp.broadcast_to(jnp.pad(eye, ((0, 0), (_B_NOPE, hw - _B_NOPE - _B_ROPE)))[:, None, :],
                              (_B_ROPE, _B_HEADS, hw))
    wk_bot = jnp.zeros((_LANES - _B_ROPE, _B_HEADS, hw), w_uk.dtype)
    wk = jnp.concatenate([wk_top, wk_mid, wk_bot], axis=0).reshape(-1, _B_HEADS * hw)
    wv = w_uv.reshape(_B_KV_LORA, _B_HEADS * _B_VD)
    return wd.astype(_BF), wq.astype(_BF), wk.astype(_BF), wv.astype(_BF)


def _mla_proj(x, g, wts, q_norm_g, kv_norm_g, tabs, tm, mask_tile):
    b, s, d = x.shape
    wd, wq, wk, wv = wts
    xspec = pl.BlockSpec((None, tm, d), lambda i, t: (i, t, 0))
    tspec = pl.BlockSpec((tm, _LANES), lambda i, t: (t, 0))
    ospec = lambda n: pl.BlockSpec((None, tm, n), lambda i, t: (i, t, 0))
    oshape = lambda n, dt: jax.ShapeDtypeStruct((b, s, n), dt)
    return pl.pallas_call(
        functools.partial(_mla_proj_body, qscale=_LOG2E / math.sqrt(_B_NOPE + _B_ROPE),
                          mask_tile=mask_tile),
        grid=(b, s // tm),
        in_specs=[xspec, _resident((1, d)), _resident(wd.shape), _resident((1, _B_Q_LORA)),
                  _resident(wq.shape), _resident((1, _B_KV_LORA)), _resident(wk.shape),
                  _resident(wv.shape), tspec, tspec, tspec],
        out_specs=[ospec(_B_KV_LORA), ospec(_B_ROPE), ospec(wq.shape[1]),
                   pl.BlockSpec((None, 2, tm, wk.shape[1]), lambda i, t: (i, 0, t, 0)),
                   ospec(wv.shape[1])],
        out_shape=[oshape(_B_KV_LORA, _F32), oshape(_B_ROPE, _F32), oshape(wq.shape[1], _BF),
                   jax.ShapeDtypeStruct((b, 2, s, wk.shape[1]), _BF), oshape(wv.shape[1], _BF)],
        compiler_params=_params(("parallel", "parallel")),
        name="mla_proj",
    )(x, g.reshape(1, d), wd, q_norm_g.reshape(1, -1), wq, kv_norm_g.reshape(1, -1), wk, wv, *tabs)


def _conv_mix_body(x_ref, g_ref, win_ref, cw_ref, wout_ref, hist_ref, o_ref, hout_ref,
                   carry_sc, z_sc):
    tm, d = x_ref.shape

    @pl.when(pl.program_id(1) == 0)
    def _():
        carry_sc[...] = hist_ref[...]

    x = x_ref[...]
    h = _rms(x, g_ref[...], _NORM_EPS).astype(_BF)
    cw = _MXU_COLS
    for c in range(0, d, cw):
        gate_b = jnp.dot(h, win_ref[:, c:c + cw], preferred_element_type=_F32)
        gate_c = jnp.dot(h, win_ref[:, d + c:d + c + cw], preferred_element_type=_F32)
        val = jnp.dot(h, win_ref[:, 2 * d + c:2 * d + c + cw], preferred_element_type=_F32)
        u = gate_c * val
        prev = carry_sc[:, c:c + cw]
        y = (_shift_rows(prev, u, 2) * cw_ref[0:1, c:c + cw]
             + _shift_rows(prev, u, 1) * cw_ref[1:2, c:c + cw]
             + u * cw_ref[2:3, c:c + cw])
        z_sc[:, c:c + cw] = (gate_b * y).astype(_BF)
        carry_sc[:, c:c + cw] = u[tm - _SUBLANES:, :]
    hout_ref[...] = carry_sc[...]
    o_ref[...] = x + jnp.dot(z_sc[...], wout_ref[...], preferred_element_type=_F32)


def _conv_mix(x, g, w_in, conv_w, w_out, hist8, tm):
    b, s, d = x.shape
    xspec = pl.BlockSpec((None, tm, d), lambda i, t: (i, t, 0))
    hspec = pl.BlockSpec((None, _SUBLANES, d), lambda i, t: (i, 0, 0))
    return pl.pallas_call(
        _conv_mix_body,
        grid=(b, s // tm),
        in_specs=[xspec, _resident((1, d)), _resident(w_in.shape), _resident(conv_w.shape),
                  _resident(w_out.shape), hspec],
        out_specs=[xspec, hspec],
        out_shape=[jax.ShapeDtypeStruct((b, s, d), _F32),
                   jax.ShapeDtypeStruct((b, _SUBLANES, d), _F32)],
        scratch_shapes=[pltpu.VMEM((_SUBLANES, d), _F32), pltpu.VMEM((tm, d), _BF)],
        compiler_params=_params(("arbitrary", "arbitrary")),
        name="conv_mixer",
    )(x, g.reshape(1, d), w_in, conv_w, w_out, hist8)


_POOL_HALO = 16


def _pool_mix_body(x_ref, g_ref, wg_ref, sc_ref, hist_ref, o_ref, hout_ref, carry_sc, *, hist_valid):
    tm, d = x_ref.shape
    t = pl.program_id(1)

    @pl.when(t == 0)
    def _():
        carry_sc[...] = hist_ref[...]

    x = x_ref[...]
    h = _rms(x, g_ref[...], _NORM_EPS)
    gw = d // len(_D_WINDOWS)
    tpos = t * tm + lax.broadcasted_iota(jnp.int32, (tm, gw), 0)
    outs = []
    for gi, w in enumerate(_D_WINDOWS):
        hg = h[:, gi * gw:(gi + 1) * gw]
        acc = jnp.concatenate([carry_sc[:, gi * gw:(gi + 1) * gw], hg], axis=0)
        k = 1
        while k < w:
            acc = acc + pltpu.roll(acc, k, axis=0)
            k *= 2
        cnt = jnp.minimum(tpos + (hist_valid + 1), w).astype(_F32)
        pooled = acc[_POOL_HALO:] / cnt
        outs.append(jnp.dot((pooled - hg).astype(_BF), wg_ref[gi], preferred_element_type=_F32))
    o_ref[...] = x + jnp.concatenate(outs, axis=1) * sc_ref[...]
    tail = h[tm - _POOL_HALO:, :]
    carry_sc[...] = tail
    hout_ref[...] = tail


def _pool_mix(x, g, w_group, scale, hist16, hist_valid, tm):
    b, s, d = x.shape
    assert all(w & (w - 1) == 0 and w <= _POOL_HALO for w in _D_WINDOWS)
    xspec = pl.BlockSpec((None, tm, d), lambda i, t: (i, t, 0))
    hspec = pl.BlockSpec((None, _POOL_HALO, d), lambda i, t: (i, 0, 0))
    return pl.pallas_call(
        functools.partial(_pool_mix_body, hist_valid=hist_valid),
        grid=(b, s // tm),
        in_specs=[xspec, _resident((1, d)), _resident(w_group.shape), _resident((1, d)), hspec],
        out_specs=[xspec, hspec],
        out_shape=[jax.ShapeDtypeStruct((b, s, d), _F32),
                   jax.ShapeDtypeStruct((b, _POOL_HALO, d), _F32)],
        scratch_shapes=[pltpu.VMEM((_POOL_HALO, d), _F32)],
        compiler_params=_params(("arbitrary", "arbitrary")),
        name="pool_mixer",
    )(x, g.reshape(1, d), w_group, scale.reshape(1, d), hist16)


def _ffn_body(*refs, final):
    if final:
        (x_ref, g_ref, wg_ref, wu_ref, cw_ref, cb_ref, wd_ref, hist_ref, fg_ref,
         o_ref, hout_ref, carry_sc, act_sc) = refs
    else:
        (x_ref, g_ref, wg_ref, wu_ref, cw_ref, cb_ref, wd_ref, hist_ref,
         o_ref, hout_ref, carry_sc, act_sc) = refs
    tm = x_ref.shape[0]
    f = wg_ref.shape[1]

    @pl.when(pl.program_id(1) == 0)
    def _():
        carry_sc[...] = hist_ref[...]

    x = x_ref[...]
    h = _rms(x, g_ref[...], _NORM_EPS).astype(_BF)
    cw = _MXU_COLS
    for c in range(0, f, cw):
        gate = jnp.dot(h, wg_ref[:, c:c + cw], preferred_element_type=_F32)
        up = jnp.dot(h, wu_ref[:, c:c + cw], preferred_element_type=_F32)
        prev = carry_sc[:, c:c + cw]
        y = (_shift_rows(prev, gate, 2) * cw_ref[0:1, c:c + cw]
             + _shift_rows(prev, gate, 1) * cw_ref[1:2, c:c + cw]
             + gate * cw_ref[2:3, c:c + cw] + cb_ref[:, c:c + cw])
        act_sc[:, c:c + cw] = (y / (1.0 + jnp.exp(-y)) * up).astype(_BF)
        carry_sc[:, c:c + cw] = gate[tm - _SUBLANES:, :]
    hout_ref[...] = carry_sc[...]
    out = x + jnp.dot(act_sc[...], wd_ref[...], preferred_element_type=_F32)
    if final:
        out = _rms(out, fg_ref[...], _NORM_EPS)
    o_ref[...] = out


def _ffn(x, g, w_gate, w_up, conv_w, conv_b, w_down, hist8, final_g, tm):
    b, s, d = x.shape
    f = w_gate.shape[1]
    assert f % _MXU_COLS == 0
    xspec = pl.BlockSpec((None, tm, d), lambda i, t: (i, t, 0))
    hspec = pl.BlockSpec((None, _SUBLANES, f), lambda i, t: (i, 0, 0))
    in_specs = [xspec, _resident((1, d)), _resident(w_gate.shape), _resident(w_up.shape),
                _resident(conv_w.shape), _resident((1, f)), _resident(w_down.shape), hspec]
    args = [x, g.reshape(1, d), w_gate, w_up, conv_w, conv_b.reshape(1, f), w_down, hist8]
    if final_g is not None:
        in_specs.append(_resident((1, d)))
        args.append(final_g.reshape(1, d))
    return pl.pallas_call(
        functools.partial(_ffn_body, final=final_g is not None),
        grid=(b, s // tm),
        in_specs=in_specs,
        out_specs=[xspec, hspec],
        out_shape=[jax.ShapeDtypeStruct((b, s, d), _F32),
                   jax.ShapeDtypeStruct((b, _SUBLANES, f), _F32)],
        scratch_shapes=[pltpu.VMEM((_SUBLANES, f), _F32), pltpu.VMEM((tm, f), _BF)],
        compiler_params=_params(("arbitrary", "arbitrary")),
        name="conv_ffn",
    )(*args)


def _pad_hist(hist, rows):
    return jnp.pad(hist, ((0, 0), (rows - hist.shape[1], 0), (0, 0)))


def kernel(x_prompt, x_sample, cache_a_k, cache_a_v, cache_b_latent, cache_b_krope, state_c_conv, state_d_pool, state_ffn_conv, norm_mix_g, norm_ffn_g, norm_final_g, a_w_qkv, a_lam, a_subln_g, a_w_o, b_w_down, b_q_norm_g, b_w_uq, b_kv_norm_g, b_w_uk, b_w_uv, b_w_o, c_w_in, c_conv_w, c_w_out, d_w_group, d_scale, ffn_w_gate, ffn_w_up, ffn_conv_w, ffn_conv_b, ffn_w_down):
    depth = norm_mix_g.shape[0]
    n_p, seq, d = x_prompt.shape
    n_s, t_new, _ = x_sample.shape
    past = cache_a_k.shape[2]
    f = ffn_w_gate.shape[-1]
    tm_p, tm_s = min(_ROW_TILE, seq), t_new
    ta = min(_ATT_TILE, seq)
    tabs_p = _rope_tables(jnp.arange(seq, dtype=jnp.int32))
    tabs_s = _rope_tables(past + jnp.arange(t_new, dtype=jnp.int32))
    xp, xs = x_prompt, x_sample
    outs = {k: [] for k in ("ak_p", "av_p", "bl_p", "br_p", "cc_p", "dp_p", "fc_p",
                            "ak_s", "av_s", "bl_s", "br_s", "cc_s", "dp_s", "fc_s")}
    for i in range(depth):
        m, j = i % _N_MIXERS, i // _N_MIXERS
        g_mix = norm_mix_g[i]
        if m == 0:
            lam_init = 0.8 - 0.6 * math.exp(-0.3 * i)
            w_qkv, w_o = a_w_qkv[j].astype(_BF), a_w_o[j].astype(_BF)
            qm, kf, km, vf, vb = _diff_proj(xp, g_mix, w_qkv, tabs_p, tm_p, ta)
            op = _sweep(qm, km, vb, _A_HEADS, _LANES, 2 * _A_HD, tile=ta,
                        lam=a_lam[j], subln_g=a_subln_g[j], lam_init=lam_init)
            xp = _mm(op, w_o, res=xp)
            outs["ak_p"].append(kf.reshape(n_p, seq, _A_HEADS, 2, _A_HD))
            outs["av_p"].append(vf.reshape(n_p, seq, _A_HEADS, 2 * _A_HD))
            attn = functools.partial(_flash, heads=_A_HEADS, dq=_LANES, dv=2 * _A_HD, nmaps=2,
                                     lam=a_lam[j], subln_g=a_subln_g[j], lam_init=lam_init)
            qm, kf, km, vf, vb = _diff_proj(xs, g_mix, w_qkv, tabs_s, tm_s, 0)
            k_cache = jnp.pad(cache_a_k[j].astype(_BF).transpose(0, 3, 1, 2, 4),
                              ((0, 0),) * 4 + ((0, _LANES - _A_HD),))
            k_all = jnp.concatenate([k_cache.reshape(n_s, 2, past, _A_HEADS * _LANES), km[:, :, 0]],
                                    axis=2)
            v_all = jnp.concatenate([cache_a_v[j].reshape(n_s, past, d).astype(_BF), vb], axis=1)
            osm = attn(qm, k_all, v_all, causal=False, tq=t_new, tk=past + t_new)
            xs = _mm(osm, w_o, res=xs)
            outs["ak_s"].append(kf.reshape(n_s, t_new, _A_HEADS, 2, _A_HD))
            outs["av_s"].append(vf.reshape(n_s, t_new, _A_HEADS, 2 * _A_HD))
        elif m == 1:
            wts = _mla_weights(b_w_down[j], b_w_uq[j], b_w_uk[j], b_w_uv[j])
            w_o = b_w_o[j].astype(_BF)
            hq = 2 * _LANES
            attn = functools.partial(_flash, heads=_B_HEADS, dq=hq, dv=_B_VD)
            lat, kr, q, k2, v = _mla_proj(xp, g_mix, wts, b_q_norm_g[j], b_kv_norm_g[j], tabs_p,
                                          tm_p, ta)
            op = _sweep(q[:, None], k2[:, None], v, _B_HEADS, hq, _B_VD, tile=ta)
            xp = _mm(op, w_o, res=xp)
            outs["bl_p"].append(lat)
            outs["br_p"].append(kr)
            lat, kr, q, k2, v = _mla_proj(xs, g_mix, wts, b_q_norm_g[j], b_kv_norm_g[j], tabs_s,
                                          tm_s, 0)
            kin = jnp.concatenate([cache_b_latent[j], cache_b_krope[j],
                                   jnp.zeros((n_s, past, _LANES - _B_ROPE), _F32)], axis=-1).astype(_BF)
            k_cache = _mm(kin, wts[2], out_dtype=_BF)
            v_cache = _mm(cache_b_latent[j].astype(_BF), wts[3], out_dtype=_BF)
            k_all = jnp.concatenate([k_cache, k2[:, 0]], axis=1)
            v_all = jnp.concatenate([v_cache, v], axis=1)
            osm = attn(q[:, None], k_all[:, None], v_all, causal=False, tq=t_new, tk=past + t_new)
            xs = _mm(osm, w_o, res=xs)
            outs["bl_s"].append(lat)
            outs["br_s"].append(kr)
        elif m == 2:
            w_in, w_out = c_w_in[j].astype(_BF), c_w_out[j].astype(_BF)
            xp, hc = _conv_mix(xp, g_mix, w_in, c_conv_w[j], w_out,
                               jnp.zeros((n_p, _SUBLANES, d), _F32), tm_p)
            outs["cc_p"].append(hc[:, -2:])
            xs, hc = _conv_mix(xs, g_mix, w_in, c_conv_w[j], w_out,
                               _pad_hist(state_c_conv[j], _SUBLANES), tm_s)
            outs["cc_s"].append(hc[:, -2:])
        else:
            w_grp = d_w_group[j].astype(_BF)
            xp, hd = _pool_mix(xp, g_mix, w_grp, d_scale[j],
                               jnp.zeros((n_p, _POOL_HALO, d), _F32), 0, tm_p)
            outs["dp_p"].append(hd[:, -_D_HIST:])
            xs, hd = _pool_mix(xs, g_mix, w_grp, d_scale[j],
                               _pad_hist(state_d_pool[j], _POOL_HALO), _D_HIST, tm_s)
            outs["dp_s"].append(hd[:, -_D_HIST:])
        wg, wu, wd = ffn_w_gate[i].astype(_BF), ffn_w_up[i].astype(_BF), ffn_w_down[i].astype(_BF)
        final_g = norm_final_g if i == depth - 1 else None
        xp, hf = _ffn(xp, norm_ffn_g[i], wg, wu, ffn_conv_w[i], ffn_conv_b[i], wd,
                      jnp.zeros((n_p, _SUBLANES, f), _F32), final_g, tm_p)
        outs["fc_p"].append(hf[:, -2:])
        xs, hf = _ffn(xs, norm_ffn_g[i], wg, wu, ffn_conv_w[i], ffn_conv_b[i], wd,
                      _pad_hist(state_ffn_conv[i], _SUBLANES), final_g, tm_s)
        outs["fc_s"].append(hf[:, -2:])
    st = lambda k: jnp.stack(outs[k])
    return (xp, xs, st("ak_p"), st("av_p"), st("bl_p"), st("br_p"), st("cc_p"), st("dp_p"), st("fc_p"),
            st("ak_s"), st("av_s"), st("bl_s"), st("br_s"), st("cc_s"), st("dp_s"), st("fc_s"))
```

```python
import functools
import math

import jax
import jax.numpy as jnp
from jax import lax
from jax.experimental import pallas as pl
from jax.experimental.pallas import tpu as pltpu

_BF = jnp.bfloat16
_F32 = jnp.float32

_CHUNK = 64
_ROPE_THETA = 10000.0
_NORM_EPS = 1e-6
_NEG_INF = -1e30
_A_HEADS = 8
_A_HD = 64
_A_SUBLN_EPS = 1e-5
_B_HEADS = 8
_B_NOPE = 128
_B_ROPE = 64
_B_VD = 128
_B_Q_LORA = 384
_B_KV_LORA = 256
_D_WINDOWS = (2, 4, 8, 16)
_D_HIST = 15
_N_MIXERS = 4
_LOG2E = math.log2(math.e)

_LANES = 128
_SUBLANES = 8
_MXU_COLS = 256
_BF_ROWS = 16
_SWEEP_SLOTS = 4
_TRIP_TICKS = 8
_SWEEP_HALVES = 2
_VMEM_LIMIT = 56 * 1024 * 1024

_ROW_TILE = 512
_ATT_TILE = 512


def _params(sem):
    return pltpu.CompilerParams(dimension_semantics=sem, vmem_limit_bytes=_VMEM_LIMIT)


def _resident(shape):
    nd = len(shape)
    return pl.BlockSpec(shape, lambda *_: (0,) * nd, pipeline_mode=pl.Buffered(1))


def _rms(x, g, eps):
    ms = jnp.mean(x * x, axis=-1, keepdims=True)
    return x * lax.rsqrt(ms + eps) * g


def _rope_lanes(y, cos, sin_lo, sin_hi):
    return (y * cos + pltpu.roll(y, _LANES - 32, axis=1) * sin_lo
            + pltpu.roll(y, 32, axis=1) * sin_hi)


def _rope_tables(pos):
    d = _A_HD
    inv = jnp.power(_ROPE_THETA, -jnp.arange(0, d, 2, dtype=_F32) / d)
    ang = pos.astype(_F32)[:, None] * inv[None, :]
    cos, sin = jnp.cos(ang), jnp.sin(ang)
    zero = jnp.zeros_like(sin)
    cos_t = jnp.tile(cos, (1, 4))
    sin_lo = jnp.tile(jnp.concatenate([-sin, zero], axis=1), (1, 2))
    sin_hi = jnp.tile(jnp.concatenate([zero, sin], axis=1), (1, 2))
    return cos_t, sin_lo, sin_hi


def _shift_rows(carry, cur, k):
    ext = jnp.concatenate([carry, cur], axis=0)
    return pltpu.roll(ext, k, axis=0)[carry.shape[0]:]


def _mask_lanes(rows, row0, tile, base, for_keys):
    r = row0 + lax.broadcasted_iota(jnp.int32, (rows, _LANES), 0)
    chunk = (r % tile) // _CHUNK
    c = lax.broadcasted_iota(jnp.int32, (rows, _LANES), 1) - base
    n = tile // _CHUNK
    if for_keys:
        return jnp.where(c == chunk, 1.0, 0.0)
    return jnp.where(c > chunk, jnp.where(c < n, _NEG_INF, 0.0), 0.0)


def _diff_proj_body(x_ref, g_ref, w_ref, cos_ref, slo_ref, shi_ref,
                    qm_ref, kf_ref, km_ref, vf_ref, vb_ref, *, qscale, mask_tile, qt_cols):
    tm, d = x_ref.shape
    h = _rms(x_ref[...], g_ref[...], _NORM_EPS).astype(_BF)
    cos, slo, shi = cos_ref[...], slo_ref[...], shi_ref[...]
    if mask_tile:
        row0 = pl.program_id(1) * tm
        q_ext = _mask_lanes(tm, row0, mask_tile, _A_HD, False)
        k_ext = _mask_lanes(tm, row0, mask_tile, _A_HD, True)
    else:
        q_ext = k_ext = jnp.zeros((tm, _LANES), _F32)
    first_half = lax.broadcasted_iota(jnp.int32, (tm, _LANES), 1) < _A_HD
    zero = jnp.zeros((tm, _LANES), _F32)
    cw = _MXU_COLS
    for c in range(0, d, cw):
        yq = jnp.dot(h, w_ref[:, c:c + cw], preferred_element_type=_F32)
        yk = jnp.dot(h, w_ref[:, d + c:d + c + cw], preferred_element_type=_F32)
        yv = jnp.dot(h, w_ref[:, 2 * d + c:2 * d + c + cw], preferred_element_type=_F32)
        for s in range(0, cw, _LANES):
            cols = slice(c + s, c + s + _LANES)
            q = _rope_lanes(yq[:, s:s + _LANES], cos, slo, shi) * qscale
            k = _rope_lanes(yk[:, s:s + _LANES], cos, slo, shi)
            kf_ref[:, cols] = k
            for mp in range(2):
                qh = q if mp == 0 else pltpu.roll(q, _A_HD, axis=1)
                kh = k if mp == 0 else pltpu.roll(k, _A_HD, axis=1)
                qv = jnp.where(first_half, qh, q_ext)
                if qt_cols:
                    qt = qv.T.astype(_BF)
                    for g in range(tm // qt_cols):
                        qm_ref[mp, (c + s) // _LANES, g] = qt[:, g * qt_cols:(g + 1) * qt_cols]
                else:
                    qm_ref[mp, :, cols] = qv.astype(_BF)
                km_ref[mp, 0, :, cols] = jnp.where(first_half, kh, zero).astype(_BF)
                km_ref[mp, 1, :, cols] = jnp.where(first_half, kh, k_ext).astype(_BF)
        vf_ref[:, c:c + cw] = yv
        vb_ref[:, c:c + cw] = yv.astype(_BF)


def _diff_proj(x, g, w_qkv, tabs, tm, mask_tile, qt_cols=0):
    b, s, d = x.shape
    row = lambda dt: jax.ShapeDtypeStruct((b, s, d), dt)
    xspec = pl.BlockSpec((None, tm, d), lambda i, t: (i, t, 0))
    tspec = pl.BlockSpec((tm, _LANES), lambda i, t: (t, 0))
    if qt_cols:
        assert tm % qt_cols == 0
        heads = d // _LANES
        q_spec = pl.BlockSpec((None, 2, heads, tm // qt_cols, _LANES, qt_cols),
                              lambda i, t: (i, 0, 0, t, 0, 0))
        q_shape = jax.ShapeDtypeStruct((b, 2, heads, s // qt_cols, _LANES, qt_cols), _BF)
    else:
        q_spec = pl.BlockSpec((None, 2, tm, d), lambda i, t: (i, 0, t, 0))
        q_shape = jax.ShapeDtypeStruct((b, 2, s, d), _BF)
    return pl.pallas_call(
        functools.partial(_diff_proj_body, qscale=_LOG2E / math.sqrt(_A_HD), mask_tile=mask_tile,
                          qt_cols=qt_cols),
        grid=(b, s // tm),
        in_specs=[xspec, _resident((1, d)), _resident(w_qkv.shape), tspec, tspec, tspec],
        out_specs=[q_spec, xspec,
                   pl.BlockSpec((None, 2, 2, tm, d), lambda i, t: (i, 0, 0, t, 0)), xspec, xspec],
        out_shape=[q_shape, row(_F32),
                   jax.ShapeDtypeStruct((b, 2, 2, s, d), _BF), row(_F32), row(_BF)],
        compiler_params=_params(("parallel", "parallel")),
        name="diff_proj",
    )(x, g.reshape(1, d), w_qkv, *tabs)


def _flash_body(*refs, nmaps, causal, tq, tk, lam_init):
    if nmaps == 2:
        q_ref, k_ref, v_ref, lam_ref, sg_ref, o_ref, m_sc, l_sc, acc_sc = refs
    else:
        q_ref, k_ref, v_ref, o_ref, m_sc, l_sc, acc_sc = refs
    qi, ki = pl.program_id(2), pl.program_id(3)
    nk = pl.num_programs(3)

    @pl.when(ki == 0)
    def _():
        m_sc[...] = jnp.full(m_sc.shape, _NEG_INF, _F32)
        l_sc[...] = jnp.zeros(l_sc.shape, _F32)
        acc_sc[...] = jnp.zeros(acc_sc.shape, _F32)

    def step(masked):
        v = v_ref[...]
        if masked:
            rows = lax.broadcasted_iota(jnp.int32, (tq, tk), 0) // _CHUNK
            cols = lax.broadcasted_iota(jnp.int32, (tq, tk), 1) // _CHUNK
            visible = cols <= rows
        for c in range(nmaps):
            s = lax.dot_general(q_ref[c], k_ref[c], (((1,), (1,)), ((), ())),
                                preferred_element_type=_F32)
            if masked:
                s = jnp.where(visible, s, _NEG_INF)
            m_prev = m_sc[c]
            m_new = jnp.maximum(m_prev, jnp.max(s, axis=-1, keepdims=True))
            alpha = jnp.exp2(m_prev - m_new)
            p = jnp.exp2(s - m_new)
            l_sc[c] = alpha * l_sc[c] + jnp.sum(p, axis=-1, keepdims=True)
            acc_sc[c] = alpha * acc_sc[c] + jnp.dot(p.astype(_BF), v, preferred_element_type=_F32)
            m_sc[c] = m_new

    if causal:
        pl.when(ki < qi)(lambda: step(False))
        pl.when(ki == qi)(lambda: step(True))
    else:
        step(False)

    @pl.when(ki == nk - 1)
    def _():
        if nmaps == 2:
            lp = lam_ref[...]
            lam = (jnp.exp(jnp.sum(lp[0:1] * lp[1:2], axis=-1, keepdims=True))
                   - jnp.exp(jnp.sum(lp[2:3] * lp[3:4], axis=-1, keepdims=True)) + lam_init)
            o = acc_sc[0] / l_sc[0] - lam * (acc_sc[1] / l_sc[1])
            o = _rms(o, sg_ref[...], _A_SUBLN_EPS) * (1.0 - lam_init)
        else:
            o = acc_sc[0] / l_sc[0]
        o_ref[...] = o.astype(o_ref.dtype)


def _flash(q, k, v, heads, dq, dv, *, causal, tq, tk, nmaps=1, lam=None, subln_g=None, lam_init=0.0):
    b, _, sq, _ = q.shape
    sk = k.shape[2]
    nq, nk = sq // tq, sk // tk
    if causal:
        assert tq == tk and tq % _CHUNK == 0 and sq == sk
        kblk = lambda qi, ki: jnp.minimum(ki, qi)
    else:
        kblk = lambda qi, ki: ki
    in_specs = [pl.BlockSpec((None, nmaps, tq, dq), lambda i, h, qi, ki: (i, 0, qi, h)),
                pl.BlockSpec((None, nmaps, tk, dq), lambda i, h, qi, ki: (i, 0, kblk(qi, ki), h)),
                pl.BlockSpec((None, tk, dv), lambda i, h, qi, ki: (i, kblk(qi, ki), h))]
    args = [q, k, v]
    if nmaps == 2:
        in_specs += [_resident(lam.shape), _resident((1, dv))]
        args += [lam, subln_g.reshape(1, dv)]
    return pl.pallas_call(
        functools.partial(_flash_body, nmaps=nmaps, causal=causal, tq=tq, tk=tk, lam_init=lam_init),
        grid=(b, heads, nq, nk),
        in_specs=in_specs,
        out_specs=pl.BlockSpec((None, tq, dv), lambda i, h, qi, ki: (i, qi, h)),
        out_shape=jax.ShapeDtypeStruct((b, sq, heads * dv), _BF),
        scratch_shapes=[pltpu.VMEM((nmaps, tq, 1), _F32), pltpu.VMEM((nmaps, tq, 1), _F32),
                        pltpu.VMEM((nmaps, tq, dv), _F32)],
        compiler_params=_params(("parallel", "parallel", "parallel", "arbitrary")),
        name="flash_diff" if nmaps == 2 else "flash_mla",
    )(*args)


def _sweep_body(*refs, nmaps, halves, tile, nb, lam_init):
    if nmaps == 2:
        q_ref, k_ref, vt_ref, lam_ref, sg_ref, o_ref, s_buf, mb_buf, m_sc, acc_sc = refs
    else:
        q_ref, k_ref, vt_ref, o_ref, s_buf, mb_buf, m_sc, acc_sc = refs
    hw = tile // halves
    nch = nmaps * halves
    dv = o_ref.shape[-1]
    npairs = nb * (nb + 1) // 2
    acc_sc[...] = jnp.zeros(acc_sc.shape, _F32)

    def nxt(pair):
        qi, t = pair
        last = t == qi
        return jnp.where(last, qi + 1, qi), jnp.where(last, 0, t + 1)

    def scores(pair, slot):
        qi, t = pair
        variant = (t == qi).astype(jnp.int32)
        for mp in range(nmaps):
            k = k_ref[mp, variant, pl.ds(pl.multiple_of(t * tile, tile), tile), :]
            for hf in range(halves):
                c = mp * halves + hf
                s = jnp.dot(k, q_ref[mp, qi * halves + hf], preferred_element_type=_F32)
                s_buf[slot, c] = s
                mb_buf[slot, c] = jnp.max(s, axis=0, keepdims=True)

    def values(pair, pos):
        _, t = pair
        first = t == 0
        slot = pos % _SWEEP_SLOTS
        vt = vt_ref[t]
        for c in range(nch):
            m_prev = jnp.where(first, _NEG_INF, m_sc[c])
            m_new = jnp.maximum(m_prev, mb_buf[slot, c])
            alpha = jnp.exp2(m_prev - m_new)
            m_sc[c] = m_new
            p = jnp.exp2(s_buf[slot, c] - m_new).astype(_BF)
            acc_sc[pos, c] = (alpha * acc_sc[(pos - 1) % _TRIP_TICKS, c]
                              + jnp.dot(vt, p, preferred_element_type=_F32))

    def finish(pair, pos):
        qi, t = pair

        @pl.when(t == qi)
        def _():
            if nmaps == 2:
                lp = lam_ref[...]
                lam = (jnp.exp(jnp.sum(lp[0:1] * lp[1:2], axis=-1, keepdims=True))
                       - jnp.exp(jnp.sum(lp[2:3] * lp[3:4], axis=-1, keepdims=True)) + lam_init)
            normed = lambda c: acc_sc[pos, c, :dv, :] / acc_sc[pos, c, dv:dv + 1, :]
            for hf in range(halves):
                if nmaps == 2:
                    ot = normed(hf) - lam * normed(halves + hf)
                    o = _rms(ot.T, sg_ref[...], _A_SUBLN_EPS) * (1.0 - lam_init)
                else:
                    o = normed(hf).T
                row = pl.multiple_of(qi * tile + hf * hw, hw)
                o_ref[pl.ds(row, hw), :] = o.astype(o_ref.dtype)

    zero = jnp.int32(0)
    pair0 = (zero, zero)
    scores(pair0, 0)
    if npairs == 1:
        values(pair0, 0)
        finish(pair0, 0)
        return
    pair1 = nxt(pair0)
    scores(pair1, 1)

    def tick(carry, r):
        pc, pb, pa = carry
        scores(pa, (2 + r) % _SWEEP_SLOTS)
        values(pc, r)
        return pb, pa, nxt(pa)

    def trip(_, carry):
        done = []
        for r in range(_TRIP_TICKS):
            done.append(carry[0])
            carry = tick(carry, r)
        for r, pair in enumerate(done):
            finish(pair, r)
        return carry

    nticks = npairs - 2
    carry = lax.fori_loop(0, nticks // _TRIP_TICKS, trip, (pair0, pair1, nxt(pair1)))
    for r in range(nticks % _TRIP_TICKS):
        pair = carry[0]
        carry = tick(carry, r)
        finish(pair, r)
    before_last, last, _ = carry
    for pair, pos in ((before_last, (npairs - 2) % _TRIP_TICKS), (last, (npairs - 1) % _TRIP_TICKS)):
        values(pair, pos)
        finish(pair, pos)


def _sweep(q, k, v, heads, dq, dv, *, tile, halves=2, lam=None, subln_g=None, lam_init=0.0):
    b, nmaps = q.shape[:2]
    s = v.shape[1]
    nb = s // tile
    assert tile % _CHUNK == 0 and tile % halves == 0 and q.shape[2:] == (heads, s * halves // tile,
                                                                        dq, tile // halves)
    vt = v.reshape(b, nb, tile, heads, dv).transpose(0, 3, 1, 4, 2)
    vt = jnp.concatenate([vt, jnp.ones((b, heads, nb, _BF_ROWS, tile), vt.dtype)], axis=3)
    dve = dv + _BF_ROWS
    nch = nmaps * halves
    hw = tile // halves
    once = pl.Buffered(1)
    in_specs = [pl.BlockSpec((None, nmaps, None, nb * halves, dq, hw),
                             lambda i, h: (i, 0, h, 0, 0, 0), pipeline_mode=once),
                pl.BlockSpec((None, nmaps, 2, s, dq), lambda i, h: (i, 0, 0, 0, h),
                             pipeline_mode=once),
                pl.BlockSpec((None, None, nb, dve, tile), lambda i, h: (i, h, 0, 0, 0),
                             pipeline_mode=once)]
    args = [q, k, vt]
    if nmaps == 2:
        in_specs += [_resident(lam.shape), _resident((1, dv))]
        args += [lam, subln_g.reshape(1, dv)]
    scratch = [pltpu.VMEM((_SWEEP_SLOTS, nch, tile, hw), _F32),
               pltpu.VMEM((_SWEEP_SLOTS, nch, 1, hw), _F32),
               pltpu.VMEM((nch, 1, hw), _F32), pltpu.VMEM((_TRIP_TICKS, nch, dve, hw), _F32)]
    return pl.pallas_call(
        functools.partial(_sweep_body, nmaps=nmaps, halves=halves, tile=tile, nb=nb,
                          lam_init=lam_init),
        grid=(b, heads),
        in_specs=in_specs,
        out_specs=pl.BlockSpec((None, s, dv), lambda i, h: (i, 0, h)),
        out_shape=jax.ShapeDtypeStruct((b, s, heads * dv), _BF),
        scratch_shapes=scratch,
        compiler_params=_params(("parallel", "parallel")),
        name="sweep_diff" if nmaps == 2 else "sweep_mla",
    )(*args)


def _mm_body(*refs, has_res):
    if has_res:
        a_ref, w_ref, r_ref, o_ref = refs
    else:
        a_ref, w_ref, o_ref = refs
    y = jnp.dot(a_ref[...].astype(_BF), w_ref[...], preferred_element_type=_F32)
    if has_res:
        y = r_ref[...] + y
    o_ref[...] = y.astype(o_ref.dtype)


def _mm(a, w, res=None, out_dtype=_F32):
    b, s, kdim = a.shape
    n = w.shape[1]
    tm = min(_ROW_TILE, s)
    in_specs = [pl.BlockSpec((None, tm, kdim), lambda i, t: (i, t, 0)), _resident(w.shape)]
    args = [a, w]
    if res is not None:
        in_specs.append(pl.BlockSpec((None, tm, n), lambda i, t: (i, t, 0)))
        args.append(res)
    return pl.pallas_call(
        functools.partial(_mm_body, has_res=res is not None),
        grid=(b, s // tm),
        in_specs=in_specs,
        out_specs=pl.BlockSpec((None, tm, n), lambda i, t: (i, t, 0)),
        out_shape=jax.ShapeDtypeStruct((b, s, n), out_dtype),
        compiler_params=_params(("parallel", "parallel")),
        name="matmul_residual" if res is not None else "matmul",
    )(*args)


def _mla_proj_body(x_ref, g_ref, wd_ref, qg_ref, wq_ref, kvg_ref, wk_ref, wv_ref,
                   cos_ref, slo_ref, shi_ref, lat_ref, kr_ref, q_ref, k_ref, v_ref, *, qscale, mask_tile,
                   qt_cols):
    tm = x_ref.shape[0]
    h = _rms(x_ref[...], g_ref[...], _NORM_EPS).astype(_BF)
    cos, slo, shi = cos_ref[...], slo_ref[...], shi_ref[...]
    if mask_tile:
        row0 = pl.program_id(1) * tm
        q_ext = _mask_lanes(tm, row0, mask_tile, _B_ROPE, False)
        k_ext = _mask_lanes(tm, row0, mask_tile, _B_ROPE, True)
    else:
        q_ext = k_ext = jnp.zeros((tm, _LANES), _F32)
    dn = jnp.dot(h, wd_ref[...], preferred_element_type=_F32)
    e0, e1 = _B_Q_LORA, _B_Q_LORA + _B_KV_LORA
    cq = _rms(dn[:, :e0], qg_ref[...], _NORM_EPS).astype(_BF)
    lat = _rms(dn[:, e0:e1], kvg_ref[...], _NORM_EPS)
    kr = _rope_lanes(dn[:, e1:e1 + _LANES], cos, slo, shi)
    lat_ref[...] = lat
    kr_ref[...] = kr[:, :_B_ROPE]
    hw = 2 * _LANES
    for hd in range(_B_HEADS):
        yq = jnp.dot(cq, wq_ref[:, hd * hw:(hd + 1) * hw], preferred_element_type=_F32)
        q_nope = yq[:, :_LANES] * qscale
        q_rope = _rope_lanes(yq[:, _LANES:], cos, slo, shi) * qscale + q_ext
        if qt_cols:
            qt = jnp.concatenate([q_nope.T, q_rope.T], axis=0).astype(_BF)
            for g in range(tm // qt_cols):
                q_ref[hd, g] = qt[:, g * qt_cols:(g + 1) * qt_cols]
        else:
            q_ref[:, hd * hw:hd * hw + _LANES] = q_nope.astype(_BF)
            q_ref[:, hd * hw + _LANES:(hd + 1) * hw] = q_rope.astype(_BF)
    kin = jnp.concatenate([lat, kr], axis=1).astype(_BF)
    for hd in range(_B_HEADS):
        yk = jnp.dot(kin, wk_ref[:, hd * hw:(hd + 1) * hw], preferred_element_type=_F32)
        for var in range(2):
            k_ref[var, :, hd * hw:hd * hw + _LANES] = yk[:, :_LANES].astype(_BF)
        k_ref[0, :, hd * hw + _LANES:(hd + 1) * hw] = yk[:, _LANES:].astype(_BF)
        k_ref[1, :, hd * hw + _LANES:(hd + 1) * hw] = (yk[:, _LANES:] + k_ext).astype(_BF)
    latb = lat.astype(_BF)
    for c in range(0, v_ref.shape[-1], _MXU_COLS):
        v_ref[:, c:c + _MXU_COLS] = jnp.dot(
            latb, wv_ref[:, c:c + _MXU_COLS], preferred_element_type=_F32).astype(_BF)


def _mla_weights(w_down, w_uq, w_uk, w_uv):
    d = w_down.shape[0]
    hw = 2 * _LANES
    wd = jnp.concatenate([w_down, jnp.zeros((d, _LANES - _B_ROPE), w_down.dtype)], axis=1)
    wq = w_uq.reshape(_B_Q_LORA, _B_HEADS, _B_NOPE + _B_ROPE)
    wq = jnp.pad(wq, ((0, 0), (0, 0), (0, hw - _B_NOPE - _B_ROPE))).reshape(_B_Q_LORA, _B_HEADS * hw)
    eye = jnp.eye(_B_ROPE, dtype=w_uk.dtype)
    wk_top = jnp.pad(w_uk, ((0, 0), (0, 0), (0, hw - _B_NOPE)))
    wk_mid = jnp.broadcast_to(jnp.pad(eye, ((0, 0), (_B_NOPE, hw - _B_NOPE - _B_ROPE)))[:, None, :],
                              (_B_ROPE, _B_HEADS, hw))
    wk_bot = jnp.zeros((_LANES - _B_ROPE, _B_HEADS, hw), w_uk.dtype)
    wk = jnp.concatenate([wk_top, wk_mid, wk_bot], axis=0).reshape(-1, _B_HEADS * hw)
    wv = w_uv.reshape(_B_KV_LORA, _B_HEADS * _B_VD)
    return wd.astype(_BF), wq.astype(_BF), wk.astype(_BF), wv.astype(_BF)


def _mla_proj(x, g, wts, q_norm_g, kv_norm_g, tabs, tm, mask_tile, qt_cols=0):
    b, s, d = x.shape
    wd, wq, wk, wv = wts
    xspec = pl.BlockSpec((None, tm, d), lambda i, t: (i, t, 0))
    tspec = pl.BlockSpec((tm, _LANES), lambda i, t: (t, 0))
    ospec = lambda n: pl.BlockSpec((None, tm, n), lambda i, t: (i, t, 0))
    oshape = lambda n, dt: jax.ShapeDtypeStruct((b, s, n), dt)
    if qt_cols:
        assert tm % qt_cols == 0
        hq = wq.shape[1] // _B_HEADS
        q_spec = pl.BlockSpec((None, _B_HEADS, tm // qt_cols, hq, qt_cols),
                              lambda i, t: (i, 0, t, 0, 0))
        q_shape = jax.ShapeDtypeStruct((b, _B_HEADS, s // qt_cols, hq, qt_cols), _BF)
    else:
        q_spec, q_shape = ospec(wq.shape[1]), oshape(wq.shape[1], _BF)
    return pl.pallas_call(
        functools.partial(_mla_proj_body, qscale=_LOG2E / math.sqrt(_B_NOPE + _B_ROPE),
                          mask_tile=mask_tile, qt_cols=qt_cols),
        grid=(b, s // tm),
        in_specs=[xspec, _resident((1, d)), _resident(wd.shape), _resident((1, _B_Q_LORA)),
                  _resident(wq.shape), _resident((1, _B_KV_LORA)), _resident(wk.shape),
                  _resident(wv.shape), tspec, tspec, tspec],
        out_specs=[ospec(_B_KV_LORA), ospec(_B_ROPE), q_spec,
                   pl.BlockSpec((None, 2, tm, wk.shape[1]), lambda i, t: (i, 0, t, 0)),
                   ospec(wv.shape[1])],
        out_shape=[oshape(_B_KV_LORA, _F32), oshape(_B_ROPE, _F32), q_shape,
                   jax.ShapeDtypeStruct((b, 2, s, wk.shape[1]), _BF), oshape(wv.shape[1], _BF)],
        compiler_params=_params(("parallel", "parallel")),
        name="mla_proj",
    )(x, g.reshape(1, d), wd, q_norm_g.reshape(1, -1), wq, kv_norm_g.reshape(1, -1), wk, wv, *tabs)


def _conv_mix_body(x_ref, g_ref, win_ref, cw_ref, wout_ref, hist_ref, o_ref, hout_ref,
                   carry_sc, z_sc):
    tm, d = x_ref.shape

    @pl.when(pl.program_id(1) == 0)
    def _():
        carry_sc[...] = hist_ref[...]

    x = x_ref[...]
    h = _rms(x, g_ref[...], _NORM_EPS).astype(_BF)
    cw = _MXU_COLS
    for c in range(0, d, cw):
        gate_b = jnp.dot(h, win_ref[:, c:c + cw], preferred_element_type=_F32)
        gate_c = jnp.dot(h, win_ref[:, d + c:d + c + cw], preferred_element_type=_F32)
        val = jnp.dot(h, win_ref[:, 2 * d + c:2 * d + c + cw], preferred_element_type=_F32)
        u = gate_c * val
        prev = carry_sc[:, c:c + cw]
        y = (_shift_rows(prev, u, 2) * cw_ref[0:1, c:c + cw]
             + _shift_rows(prev, u, 1) * cw_ref[1:2, c:c + cw]
             + u * cw_ref[2:3, c:c + cw])
        z_sc[:, c:c + cw] = (gate_b * y).astype(_BF)
        carry_sc[:, c:c + cw] = u[tm - _SUBLANES:, :]
    hout_ref[...] = carry_sc[...]
    o_ref[...] = x + jnp.dot(z_sc[...], wout_ref[...], preferred_element_type=_F32)


def _conv_mix(x, g, w_in, conv_w, w_out, hist8, tm):
    b, s, d = x.shape
    xspec = pl.BlockSpec((None, tm, d), lambda i, t: (i, t, 0))
    hspec = pl.BlockSpec((None, _SUBLANES, d), lambda i, t: (i, 0, 0))
    return pl.pallas_call(
        _conv_mix_body,
        grid=(b, s // tm),
        in_specs=[xspec, _resident((1, d)), _resident(w_in.shape), _resident(conv_w.shape),
                  _resident(w_out.shape), hspec],
        out_specs=[xspec, hspec],
        out_shape=[jax.ShapeDtypeStruct((b, s, d), _F32),
                   jax.ShapeDtypeStruct((b, _SUBLANES, d), _F32)],
        scratch_shapes=[pltpu.VMEM((_SUBLANES, d), _F32), pltpu.VMEM((tm, d), _BF)],
        compiler_params=_params(("arbitrary", "arbitrary")),
        name="conv_mixer",
    )(x, g.reshape(1, d), w_in, conv_w, w_out, hist8)


_POOL_HALO = 16


def _pool_mix_body(x_ref, g_ref, wg_ref, sc_ref, hist_ref, o_ref, hout_ref, carry_sc, *, hist_valid):
    tm, d = x_ref.shape
    t = pl.program_id(1)

    @pl.when(t == 0)
    def _():
        carry_sc[...] = hist_ref[...]

    x = x_ref[...]
    h = _rms(x, g_ref[...], _NORM_EPS)
    gw = d // len(_D_WINDOWS)
    tpos = t * tm + lax.broadcasted_iota(jnp.int32, (tm, gw), 0)
    outs = []
    for gi, w in enumerate(_D_WINDOWS):
        hg = h[:, gi * gw:(gi + 1) * gw]
        acc = jnp.concatenate([carry_sc[:, gi * gw:(gi + 1) * gw], hg], axis=0)
        k = 1
        while k < w:
            acc = acc + pltpu.roll(acc, k, axis=0)
            k *= 2
        cnt = jnp.minimum(tpos + (hist_valid + 1), w).astype(_F32)
        pooled = acc[_POOL_HALO:] / cnt
        outs.append(jnp.dot((pooled - hg).astype(_BF), wg_ref[gi], preferred_element_type=_F32))
    o_ref[...] = x + jnp.concatenate(outs, axis=1) * sc_ref[...]
    tail = h[tm - _POOL_HALO:, :]
    carry_sc[...] = tail
    hout_ref[...] = tail


def _pool_mix(x, g, w_group, scale, hist16, hist_valid, tm):
    b, s, d = x.shape
    assert all(w & (w - 1) == 0 and w <= _POOL_HALO for w in _D_WINDOWS)
    xspec = pl.BlockSpec((None, tm, d), lambda i, t: (i, t, 0))
    hspec = pl.BlockSpec((None, _POOL_HALO, d), lambda i, t: (i, 0, 0))
    return pl.pallas_call(
        functools.partial(_pool_mix_body, hist_valid=hist_valid),
        grid=(b, s // tm),
        in_specs=[xspec, _resident((1, d)), _resident(w_group.shape), _resident((1, d)), hspec],
        out_specs=[xspec, hspec],
        out_shape=[jax.ShapeDtypeStruct((b, s, d), _F32),
                   jax.ShapeDtypeStruct((b, _POOL_HALO, d), _F32)],
        scratch_shapes=[pltpu.VMEM((_POOL_HALO, d), _F32)],
        compiler_params=_params(("arbitrary", "arbitrary")),
        name="pool_mixer",
    )(x, g.reshape(1, d), w_group, scale.reshape(1, d), hist16)


def _ffn_body(*refs, final):
    if final:
        (x_ref, g_ref, wg_ref, wu_ref, cw_ref, cb_ref, wd_ref, hist_ref, fg_ref,
         o_ref, hout_ref, carry_sc, act_sc) = refs
    else:
        (x_ref, g_ref, wg_ref, wu_ref, cw_ref, cb_ref, wd_ref, hist_ref,
         o_ref, hout_ref, carry_sc, act_sc) = refs
    tm = x_ref.shape[0]
    f = wg_ref.shape[1]

    @pl.when(pl.program_id(1) == 0)
    def _():
        carry_sc[...] = hist_ref[...]

    x = x_ref[...]
    h = _rms(x, g_ref[...], _NORM_EPS).astype(_BF)
    cw = _MXU_COLS
    for c in range(0, f, cw):
        gate = jnp.dot(h, wg_ref[:, c:c + cw], preferred_element_type=_F32)
        up = jnp.dot(h, wu_ref[:, c:c + cw], preferred_element_type=_F32)
        prev = carry_sc[:, c:c + cw]
        y = (_shift_rows(prev, gate, 2) * cw_ref[0:1, c:c + cw]
             + _shift_rows(prev, gate, 1) * cw_ref[1:2, c:c + cw]
             + gate * cw_ref[2:3, c:c + cw] + cb_ref[:, c:c + cw])
        act_sc[:, c:c + cw] = (y / (1.0 + jnp.exp(-y)) * up).astype(_BF)
        carry_sc[:, c:c + cw] = gate[tm - _SUBLANES:, :]
    hout_ref[...] = carry_sc[...]
    out = x + jnp.dot(act_sc[...], wd_ref[...], preferred_element_type=_F32)
    if final:
        out = _rms(out, fg_ref[...], _NORM_EPS)
    o_ref[...] = out


def _ffn(x, g, w_gate, w_up, conv_w, conv_b, w_down, hist8, final_g, tm):
    b, s, d = x.shape
    f = w_gate.shape[1]
    assert f % _MXU_COLS == 0
    xspec = pl.BlockSpec((None, tm, d), lambda i, t: (i, t, 0))
    hspec = pl.BlockSpec((None, _SUBLANES, f), lambda i, t: (i, 0, 0))
    in_specs = [xspec, _resident((1, d)), _resident(w_gate.shape), _resident(w_up.shape),
                _resident(conv_w.shape), _resident((1, f)), _resident(w_down.shape), hspec]
    args = [x, g.reshape(1, d), w_gate, w_up, conv_w, conv_b.reshape(1, f), w_down, hist8]
    if final_g is not None:
        in_specs.append(_resident((1, d)))
        args.append(final_g.reshape(1, d))
    return pl.pallas_call(
        functools.partial(_ffn_body, final=final_g is not None),
        grid=(b, s // tm),
        in_specs=in_specs,
        out_specs=[xspec, hspec],
        out_shape=[jax.ShapeDtypeStruct((b, s, d), _F32),
                   jax.ShapeDtypeStruct((b, _SUBLANES, f), _F32)],
        scratch_shapes=[pltpu.VMEM((_SUBLANES, f), _F32), pltpu.VMEM((tm, f), _BF)],
        compiler_params=_params(("arbitrary", "arbitrary")),
        name="conv_ffn",
    )(*args)


def _pad_hist(hist, rows):
    return jnp.pad(hist, ((0, 0), (rows - hist.shape[1], 0), (0, 0)))


def kernel(x_prompt, x_sample, cache_a_k, cache_a_v, cache_b_latent, cache_b_krope, state_c_conv, state_d_pool, state_ffn_conv, norm_mix_g, norm_ffn_g, norm_final_g, a_w_qkv, a_lam, a_subln_g, a_w_o, b_w_down, b_q_norm_g, b_w_uq, b_kv_norm_g, b_w_uk, b_w_uv, b_w_o, c_w_in, c_conv_w, c_w_out, d_w_group, d_scale, ffn_w_gate, ffn_w_up, ffn_conv_w, ffn_conv_b, ffn_w_down):
    depth = norm_mix_g.shape[0]
    n_p, seq, d = x_prompt.shape
    n_s, t_new, _ = x_sample.shape
    past = cache_a_k.shape[2]
    f = ffn_w_gate.shape[-1]
    tm_p, tm_s = min(_ROW_TILE, seq), t_new
    ta = min(_ATT_TILE, seq)
    tabs_p = _rope_tables(jnp.arange(seq, dtype=jnp.int32))
    tabs_s = _rope_tables(past + jnp.arange(t_new, dtype=jnp.int32))
    xp, xs = x_prompt, x_sample
    outs = {k: [] for k in ("ak_p", "av_p", "bl_p", "br_p", "cc_p", "dp_p", "fc_p",
                            "ak_s", "av_s", "bl_s", "br_s", "cc_s", "dp_s", "fc_s")}
    for i in range(depth):
        m, j = i % _N_MIXERS, i // _N_MIXERS
        g_mix = norm_mix_g[i]
        if m == 0:
            lam_init = 0.8 - 0.6 * math.exp(-0.3 * i)
            w_qkv, w_o = a_w_qkv[j].astype(_BF), a_w_o[j].astype(_BF)
            qm, kf, km, vf, vb = _diff_proj(xp, g_mix, w_qkv, tabs_p, tm_p, ta, ta // _SWEEP_HALVES)
            op = _sweep(qm, km, vb, _A_HEADS, _LANES, 2 * _A_HD, tile=ta, halves=_SWEEP_HALVES,
                        lam=a_lam[j], subln_g=a_subln_g[j], lam_init=lam_init)
            xp = _mm(op, w_o, res=xp)
            outs["ak_p"].append(kf.reshape(n_p, seq, _A_HEADS, 2, _A_HD))
            outs["av_p"].append(vf.reshape(n_p, seq, _A_HEADS, 2 * _A_HD))
            attn = functools.partial(_flash, heads=_A_HEADS, dq=_LANES, dv=2 * _A_HD, nmaps=2,
                                     lam=a_lam[j], subln_g=a_subln_g[j], lam_init=lam_init)
            qm, kf, km, vf, vb = _diff_proj(xs, g_mix, w_qkv, tabs_s, tm_s, 0)
            k_cache = jnp.pad(cache_a_k[j].astype(_BF).transpose(0, 3, 1, 2, 4),
                              ((0, 0),) * 4 + ((0, _LANES - _A_HD),))
            k_all = jnp.concatenate([k_cache.reshape(n_s, 2, past, _A_HEADS * _LANES), km[:, :, 0]],
                                    axis=2)
            v_all = jnp.concatenate([cache_a_v[j].reshape(n_s, past, d).astype(_BF), vb], axis=1)
            osm = attn(qm, k_all, v_all, causal=False, tq=t_new, tk=past + t_new)
            xs = _mm(osm, w_o, res=xs)
            outs["ak_s"].append(kf.reshape(n_s, t_new, _A_HEADS, 2, _A_HD))
            outs["av_s"].append(vf.reshape(n_s, t_new, _A_HEADS, 2 * _A_HD))
        elif m == 1:
            wts = _mla_weights(b_w_down[j], b_w_uq[j], b_w_uk[j], b_w_uv[j])
            w_o = b_w_o[j].astype(_BF)
            hq = 2 * _LANES
            attn = functools.partial(_flash, heads=_B_HEADS, dq=hq, dv=_B_VD)
            lat, kr, q, k2, v = _mla_proj(xp, g_mix, wts, b_q_norm_g[j], b_kv_norm_g[j], tabs_p,
                                          tm_p, ta, ta // _SWEEP_HALVES)
            op = _sweep(q[:, None], k2[:, None], v, _B_HEADS, hq, _B_VD, tile=ta,
                        halves=_SWEEP_HALVES)
            xp = _mm(op, w_o, res=xp)
            outs["bl_p"].append(lat)
            outs["br_p"].append(kr)
            lat, kr, q, k2, v = _mla_proj(xs, g_mix, wts, b_q_norm_g[j], b_kv_norm_g[j], tabs_s,
                                          tm_s, 0)
            kin = jnp.concatenate([cache_b_latent[j], cache_b_krope[j],
                                   jnp.zeros((n_s, past, _LANES - _B_ROPE), _F32)], axis=-1).astype(_BF)
            k_cache = _mm(kin, wts[2], out_dtype=_BF)
            v_cache = _mm(cache_b_latent[j].astype(_BF), wts[3], out_dtype=_BF)
            k_all = jnp.concatenate([k_cache, k2[:, 0]], axis=1)
            v_all = jnp.concatenate([v_cache, v], axis=1)
            osm = attn(q[:, None], k_all[:, None], v_all, causal=False, tq=t_new, tk=past + t_new)
            xs = _mm(osm, w_o, res=xs)
            outs["bl_s"].append(lat)
            outs["br_s"].append(kr)
        elif m == 2:
            w_in, w_out = c_w_in[j].astype(_BF), c_w_out[j].astype(_BF)
            xp, hc = _conv_mix(xp, g_mix, w_in, c_conv_w[j], w_out,
                               jnp.zeros((n_p, _SUBLANES, d), _F32), tm_p)
            outs["cc_p"].append(hc[:, -2:])
            xs, hc = _conv_mix(xs, g_mix, w_in, c_conv_w[j], w_out,
                               _pad_hist(state_c_conv[j], _SUBLANES), tm_s)
            outs["cc_s"].append(hc[:, -2:])
        else:
            w_grp = d_w_group[j].astype(_BF)
            xp, hd = _pool_mix(xp, g_mix, w_grp, d_scale[j],
                               jnp.zeros((n_p, _POOL_HALO, d), _F32), 0, tm_p)
            outs["dp_p"].append(hd[:, -_D_HIST:])
            xs, hd = _pool_mix(xs, g_mix, w_grp, d_scale[j],
                               _pad_hist(state_d_pool[j], _POOL_HALO), _D_HIST, tm_s)
            outs["dp_s"].append(hd[:, -_D_HIST:])
        wg, wu, wd = ffn_w_gate[i].astype(_BF), ffn_w_up[i].astype(_BF), ffn_w_down[i].astype(_BF)
        final_g = norm_final_g if i == depth - 1 else None
        xp, hf = _ffn(xp, norm_ffn_g[i], wg, wu, ffn_conv_w[i], ffn_conv_b[i], wd,
                      jnp.zeros((n_p, _SUBLANES, f), _F32), final_g, tm_p)
        outs["fc_p"].append(hf[:, -2:])
        xs, hf = _ffn(xs, norm_ffn_g[i], wg, wu, ffn_conv_w[i], ffn_conv_b[i], wd,
                      _pad_hist(state_ffn_conv[i], _SUBLANES), final_g, tm_s)
        outs["fc_s"].append(hf[:, -2:])
    st = lambda k: jnp.stack(outs[k])
    return (xp, xs, st("ak_p"), st("av_p"), st("bl_p"), st("br_p"), st("cc_p"), st("dp_p"), st("fc_p"),
            st("ak_s"), st("av_s"), st("bl_s"), st("br_s"), st("cc_s"), st("dp_s"), st("fc_s"))
```

```python
import functools
import math

import jax
import jax.numpy as jnp
from jax import lax
from jax.experimental import pallas as pl
from jax.experimental.pallas import tpu as pltpu

_BF = jnp.bfloat16
_F32 = jnp.float32

_CHUNK = 64
_ROPE_THETA = 10000.0
_NORM_EPS = 1e-6
_NEG_INF = -1e30
_A_HEADS = 8
_A_HD = 64
_A_SUBLN_EPS = 1e-5
_B_HEADS = 8
_B_NOPE = 128
_B_ROPE = 64
_B_VD = 128
_B_Q_LORA = 384
_B_KV_LORA = 256
_D_WINDOWS = (2, 4, 8, 16)
_D_HIST = 15
_N_MIXERS = 4
_LOG2E = math.log2(math.e)

_LANES = 128
_SUBLANES = 8
_MXU_COLS = 256
_BF_ROWS = 16
_SWEEP_SLOTS = 4
_TRIP_TICKS_DIFF = 8
_TRIP_TICKS_MLA = 16
_SWEEP_HALVES = 1
_VMEM_LIMIT = 56 * 1024 * 1024

_ROW_TILE = 512
_ATT_TILE = 512


def _params(sem):
    return pltpu.CompilerParams(dimension_semantics=sem, vmem_limit_bytes=_VMEM_LIMIT)


def _resident(shape):
    nd = len(shape)
    return pl.BlockSpec(shape, lambda *_: (0,) * nd, pipeline_mode=pl.Buffered(1))


def _rms(x, g, eps):
    ms = jnp.mean(x * x, axis=-1, keepdims=True)
    return x * lax.rsqrt(ms + eps) * g


def _rope_lanes(y, cos, sin_lo, sin_hi):
    return (y * cos + pltpu.roll(y, _LANES - 32, axis=1) * sin_lo
            + pltpu.roll(y, 32, axis=1) * sin_hi)


def _rope_tables(pos):
    d = _A_HD
    inv = jnp.power(_ROPE_THETA, -jnp.arange(0, d, 2, dtype=_F32) / d)
    ang = pos.astype(_F32)[:, None] * inv[None, :]
    cos, sin = jnp.cos(ang), jnp.sin(ang)
    zero = jnp.zeros_like(sin)
    cos_t = jnp.tile(cos, (1, 4))
    sin_lo = jnp.tile(jnp.concatenate([-sin, zero], axis=1), (1, 2))
    sin_hi = jnp.tile(jnp.concatenate([zero, sin], axis=1), (1, 2))
    return cos_t, sin_lo, sin_hi


def _shift_rows(carry, cur, k):
    ext = jnp.concatenate([carry, cur], axis=0)
    return pltpu.roll(ext, k, axis=0)[carry.shape[0]:]


def _mask_lanes(rows, row0, tile, base, for_keys):
    r = row0 + lax.broadcasted_iota(jnp.int32, (rows, _LANES), 0)
    chunk = (r % tile) // _CHUNK
    c = lax.broadcasted_iota(jnp.int32, (rows, _LANES), 1) - base
    n = tile // _CHUNK
    if for_keys:
        return jnp.where(c == chunk, 1.0, 0.0)
    return jnp.where(c > chunk, jnp.where(c < n, _NEG_INF, 0.0), 0.0)


def _diff_proj_body(x_ref, g_ref, w_ref, cos_ref, slo_ref, shi_ref,
                    qm_ref, kf_ref, km_ref, vf_ref, vb_ref, *, qscale, mask_tile, qt_cols):
    tm, d = x_ref.shape
    h = _rms(x_ref[...], g_ref[...], _NORM_EPS).astype(_BF)
    cos, slo, shi = cos_ref[...], slo_ref[...], shi_ref[...]
    if mask_tile:
        row0 = pl.program_id(1) * tm
        q_ext = _mask_lanes(tm, row0, mask_tile, _A_HD, False)
        k_ext = _mask_lanes(tm, row0, mask_tile, _A_HD, True)
    else:
        q_ext = k_ext = jnp.zeros((tm, _LANES), _F32)
    first_half = lax.broadcasted_iota(jnp.int32, (tm, _LANES), 1) < _A_HD
    zero = jnp.zeros((tm, _LANES), _F32)
    cw = _MXU_COLS
    for c in range(0, d, cw):
        yq = jnp.dot(h, w_ref[:, c:c + cw], preferred_element_type=_F32)
        yk = jnp.dot(h, w_ref[:, d + c:d + c + cw], preferred_element_type=_F32)
        yv = jnp.dot(h, w_ref[:, 2 * d + c:2 * d + c + cw], preferred_element_type=_F32)
        for s in range(0, cw, _LANES):
            cols = slice(c + s, c + s + _LANES)
            q = _rope_lanes(yq[:, s:s + _LANES], cos, slo, shi) * qscale
            k = _rope_lanes(yk[:, s:s + _LANES], cos, slo, shi)
            kf_ref[:, cols] = k
            for mp in range(2):
                qh = q if mp == 0 else pltpu.roll(q, _A_HD, axis=1)
                kh = k if mp == 0 else pltpu.roll(k, _A_HD, axis=1)
                qv = jnp.where(first_half, qh, q_ext)
                if qt_cols:
                    qt = qv.T.astype(_BF)
                    for g in range(tm // qt_cols):
                        qm_ref[mp, (c + s) // _LANES, g] = qt[:, g * qt_cols:(g + 1) * qt_cols]
                else:
                    qm_ref[mp, :, cols] = qv.astype(_BF)
                km_ref[mp, 0, :, cols] = jnp.where(first_half, kh, zero).astype(_BF)
                km_ref[mp, 1, :, cols] = jnp.where(first_half, kh, k_ext).astype(_BF)
        vf_ref[:, c:c + cw] = yv
        vb_ref[:, c:c + cw] = yv.astype(_BF)


def _diff_proj(x, g, w_qkv, tabs, tm, mask_tile, qt_cols=0):
    b, s, d = x.shape
    row = lambda dt: jax.ShapeDtypeStruct((b, s, d), dt)
    xspec = pl.BlockSpec((None, tm, d), lambda i, t: (i, t, 0))
    tspec = pl.BlockSpec((tm, _LANES), lambda i, t: (t, 0))
    if qt_cols:
        assert tm % qt_cols == 0
        heads = d // _LANES
        q_spec = pl.BlockSpec((None, 2, heads, tm // qt_cols, _LANES, qt_cols),
                              lambda i, t: (i, 0, 0, t, 0, 0))
        q_shape = jax.ShapeDtypeStruct((b, 2, heads, s // qt_cols, _LANES, qt_cols), _BF)
    else:
        q_spec = pl.BlockSpec((None, 2, tm, d), lambda i, t: (i, 0, t, 0))
        q_shape = jax.ShapeDtypeStruct((b, 2, s, d), _BF)
    return pl.pallas_call(
        functools.partial(_diff_proj_body, qscale=_LOG2E / math.sqrt(_A_HD), mask_tile=mask_tile,
                          qt_cols=qt_cols),
        grid=(b, s // tm),
        in_specs=[xspec, _resident((1, d)), _resident(w_qkv.shape), tspec, tspec, tspec],
        out_specs=[q_spec, xspec,
                   pl.BlockSpec((None, 2, 2, tm, d), lambda i, t: (i, 0, 0, t, 0)), xspec, xspec],
        out_shape=[q_shape, row(_F32),
                   jax.ShapeDtypeStruct((b, 2, 2, s, d), _BF), row(_F32), row(_BF)],
        compiler_params=_params(("parallel", "parallel")),
        name="diff_proj",
    )(x, g.reshape(1, d), w_qkv, *tabs)


def _flash_body(*refs, nmaps, causal, tq, tk, lam_init):
    if nmaps == 2:
        q_ref, k_ref, v_ref, lam_ref, sg_ref, o_ref, m_sc, l_sc, acc_sc = refs
    else:
        q_ref, k_ref, v_ref, o_ref, m_sc, l_sc, acc_sc = refs
    qi, ki = pl.program_id(2), pl.program_id(3)
    nk = pl.num_programs(3)

    @pl.when(ki == 0)
    def _():
        m_sc[...] = jnp.full(m_sc.shape, _NEG_INF, _F32)
        l_sc[...] = jnp.zeros(l_sc.shape, _F32)
        acc_sc[...] = jnp.zeros(acc_sc.shape, _F32)

    def step(masked):
        v = v_ref[...]
        if masked:
            rows = lax.broadcasted_iota(jnp.int32, (tq, tk), 0) // _CHUNK
            cols = lax.broadcasted_iota(jnp.int32, (tq, tk), 1) // _CHUNK
            visible = cols <= rows
        for c in range(nmaps):
            s = lax.dot_general(q_ref[c], k_ref[c], (((1,), (1,)), ((), ())),
                                preferred_element_type=_F32)
            if masked:
                s = jnp.where(visible, s, _NEG_INF)
            m_prev = m_sc[c]
            m_new = jnp.maximum(m_prev, jnp.max(s, axis=-1, keepdims=True))
            alpha = jnp.exp2(m_prev - m_new)
            p = jnp.exp2(s - m_new)
            l_sc[c] = alpha * l_sc[c] + jnp.sum(p, axis=-1, keepdims=True)
            acc_sc[c] = alpha * acc_sc[c] + jnp.dot(p.astype(_BF), v, preferred_element_type=_F32)
            m_sc[c] = m_new

    if causal:
        pl.when(ki < qi)(lambda: step(False))
        pl.when(ki == qi)(lambda: step(True))
    else:
        step(False)

    @pl.when(ki == nk - 1)
    def _():
        if nmaps == 2:
            lp = lam_ref[...]
            lam = (jnp.exp(jnp.sum(lp[0:1] * lp[1:2], axis=-1, keepdims=True))
                   - jnp.exp(jnp.sum(lp[2:3] * lp[3:4], axis=-1, keepdims=True)) + lam_init)
            o = acc_sc[0] / l_sc[0] - lam * (acc_sc[1] / l_sc[1])
            o = _rms(o, sg_ref[...], _A_SUBLN_EPS) * (1.0 - lam_init)
        else:
            o = acc_sc[0] / l_sc[0]
        o_ref[...] = o.astype(o_ref.dtype)


def _flash(q, k, v, heads, dq, dv, *, causal, tq, tk, nmaps=1, lam=None, subln_g=None, lam_init=0.0):
    b, _, sq, _ = q.shape
    sk = k.shape[2]
    nq, nk = sq // tq, sk // tk
    if causal:
        assert tq == tk and tq % _CHUNK == 0 and sq == sk
        kblk = lambda qi, ki: jnp.minimum(ki, qi)
    else:
        kblk = lambda qi, ki: ki
    in_specs = [pl.BlockSpec((None, nmaps, tq, dq), lambda i, h, qi, ki: (i, 0, qi, h)),
                pl.BlockSpec((None, nmaps, tk, dq), lambda i, h, qi, ki: (i, 0, kblk(qi, ki), h)),
                pl.BlockSpec((None, tk, dv), lambda i, h, qi, ki: (i, kblk(qi, ki), h))]
    args = [q, k, v]
    if nmaps == 2:
        in_specs += [_resident(lam.shape), _resident((1, dv))]
        args += [lam, subln_g.reshape(1, dv)]
    return pl.pallas_call(
        functools.partial(_flash_body, nmaps=nmaps, causal=causal, tq=tq, tk=tk, lam_init=lam_init),
        grid=(b, heads, nq, nk),
        in_specs=in_specs,
        out_specs=pl.BlockSpec((None, tq, dv), lambda i, h, qi, ki: (i, qi, h)),
        out_shape=jax.ShapeDtypeStruct((b, sq, heads * dv), _BF),
        scratch_shapes=[pltpu.VMEM((nmaps, tq, 1), _F32), pltpu.VMEM((nmaps, tq, 1), _F32),
                        pltpu.VMEM((nmaps, tq, dv), _F32)],
        compiler_params=_params(("parallel", "parallel", "parallel", "arbitrary")),
        name="flash_diff" if nmaps == 2 else "flash_mla",
    )(*args)


def _sweep_body(*refs, nmaps, halves, tile, nb, lam_init):
    if nmaps == 2:
        q_ref, k_ref, vt_ref, lam_ref, sg_ref, o_ref, s_buf, mb_buf, m_sc, acc_sc = refs
    else:
        q_ref, k_ref, vt_ref, o_ref, s_buf, mb_buf, m_sc, acc_sc = refs
    hw = tile // halves
    nch = nmaps * halves
    dv = o_ref.shape[-1]
    ticks = acc_sc.shape[0]
    npairs = nb * (nb + 1) // 2
    acc_sc[...] = jnp.zeros(acc_sc.shape, _F32)

    def nxt(pair):
        qi, t = pair
        last = t == qi
        return jnp.where(last, qi + 1, qi), jnp.where(last, 0, t + 1)

    def scores(pair, slot):
        qi, t = pair
        variant = (t == qi).astype(jnp.int32)
        for mp in range(nmaps):
            k = k_ref[mp, variant, pl.ds(pl.multiple_of(t * tile, tile), tile), :]
            for hf in range(halves):
                c = mp * halves + hf
                s = jnp.dot(k, q_ref[mp, qi * halves + hf], preferred_element_type=_F32)
                s_buf[slot, c] = s
                mb_buf[slot, c] = jnp.max(s, axis=0, keepdims=True)

    def values(pair, pos):
        _, t = pair
        first = t == 0
        slot = pos % _SWEEP_SLOTS
        vt = vt_ref[t]
        for c in range(nch):
            m_prev = jnp.where(first, _NEG_INF, m_sc[c])
            m_new = jnp.maximum(m_prev, mb_buf[slot, c])
            alpha = jnp.exp2(m_prev - m_new)
            m_sc[c] = m_new
            p = jnp.exp2(s_buf[slot, c] - m_new).astype(_BF)
            acc_sc[pos, c] = (alpha * acc_sc[(pos - 1) % ticks, c]
                              + jnp.dot(vt, p, preferred_element_type=_F32))

    def finish(pair, pos):
        qi, t = pair

        @pl.when(t == qi)
        def _():
            if nmaps == 2:
                lp = lam_ref[...]
                lam = (jnp.exp(jnp.sum(lp[0:1] * lp[1:2], axis=-1, keepdims=True))
                       - jnp.exp(jnp.sum(lp[2:3] * lp[3:4], axis=-1, keepdims=True)) + lam_init)
            normed = lambda c: acc_sc[pos, c, :dv, :] / acc_sc[pos, c, dv:dv + 1, :]
            for hf in range(halves):
                if nmaps == 2:
                    ot = normed(hf) - lam * normed(halves + hf)
                    o = _rms(ot.T, sg_ref[...], _A_SUBLN_EPS) * (1.0 - lam_init)
                else:
                    o = normed(hf).T
                row = pl.multiple_of(qi * tile + hf * hw, hw)
                o_ref[pl.ds(row, hw), :] = o.astype(o_ref.dtype)

    zero = jnp.int32(0)
    pair0 = (zero, zero)
    scores(pair0, 0)
    if npairs == 1:
        values(pair0, 0)
        finish(pair0, 0)
        return
    pair1 = nxt(pair0)
    scores(pair1, 1)

    def tick(carry, r):
        pc, pb, pa = carry
        scores(pa, (2 + r) % _SWEEP_SLOTS)
        values(pc, r)
        return pb, pa, nxt(pa)

    def trip(_, carry):
        done = []
        for r in range(ticks):
            done.append(carry[0])
            carry = tick(carry, r)
        for r, pair in enumerate(done):
            finish(pair, r)
        return carry

    nticks = npairs - 2
    carry = lax.fori_loop(0, nticks // ticks, trip, (pair0, pair1, nxt(pair1)))
    for r in range(nticks % ticks):
        pair = carry[0]
        carry = tick(carry, r)
        finish(pair, r)
    before_last, last, _ = carry
    for pair, pos in ((before_last, (npairs - 2) % ticks), (last, (npairs - 1) % ticks)):
        values(pair, pos)
        finish(pair, pos)


def _sweep(q, k, v, heads, dq, dv, *, tile, halves, trip_ticks, lam=None, subln_g=None,
           lam_init=0.0):
    b, nmaps = q.shape[:2]
    s = v.shape[1]
    nb = s // tile
    assert tile % _CHUNK == 0 and tile % halves == 0 and trip_ticks % _SWEEP_SLOTS == 0
    assert q.shape[2:] == (heads, s * halves // tile, dq, tile // halves)
    vt = v.reshape(b, nb, tile, heads, dv).transpose(0, 3, 1, 4, 2)
    vt = jnp.concatenate([vt, jnp.ones((b, heads, nb, _BF_ROWS, tile), vt.dtype)], axis=3)
    dve = dv + _BF_ROWS
    nch = nmaps * halves
    hw = tile // halves
    once = pl.Buffered(1)
    in_specs = [pl.BlockSpec((None, nmaps, None, nb * halves, dq, hw),
                             lambda i, h: (i, 0, h, 0, 0, 0), pipeline_mode=once),
                pl.BlockSpec((None, nmaps, 2, s, dq), lambda i, h: (i, 0, 0, 0, h),
                             pipeline_mode=once),
                pl.BlockSpec((None, None, nb, dve, tile), lambda i, h: (i, h, 0, 0, 0),
                             pipeline_mode=once)]
    args = [q, k, vt]
    if nmaps == 2:
        in_specs += [_resident(lam.shape), _resident((1, dv))]
        args += [lam, subln_g.reshape(1, dv)]
    scratch = [pltpu.VMEM((_SWEEP_SLOTS, nch, tile, hw), _F32),
               pltpu.VMEM((_SWEEP_SLOTS, nch, 1, hw), _F32),
               pltpu.VMEM((nch, 1, hw), _F32), pltpu.VMEM((trip_ticks, nch, dve, hw), _F32)]
    return pl.pallas_call(
        functools.partial(_sweep_body, nmaps=nmaps, halves=halves, tile=tile, nb=nb,
                          lam_init=lam_init),
        grid=(b, heads),
        in_specs=in_specs,
        out_specs=pl.BlockSpec((None, s, dv), lambda i, h: (i, 0, h)),
        out_shape=jax.ShapeDtypeStruct((b, s, heads * dv), _BF),
        scratch_shapes=scratch,
        compiler_params=_params(("parallel", "parallel")),
        name="sweep_diff" if nmaps == 2 else "sweep_mla",
    )(*args)


def _mm_body(*refs, has_res):
    if has_res:
        a_ref, w_ref, r_ref, o_ref = refs
    else:
        a_ref, w_ref, o_ref = refs
    y = jnp.dot(a_ref[...].astype(_BF), w_ref[...], preferred_element_type=_F32)
    if has_res:
        y = r_ref[...] + y
    o_ref[...] = y.astype(o_ref.dtype)


def _mm(a, w, res=None, out_dtype=_F32):
    b, s, kdim = a.shape
    n = w.shape[1]
    tm = min(_ROW_TILE, s)
    in_specs = [pl.BlockSpec((None, tm, kdim), lambda i, t: (i, t, 0)), _resident(w.shape)]
    args = [a, w]
    if res is not None:
        in_specs.append(pl.BlockSpec((None, tm, n), lambda i, t: (i, t, 0)))
        args.append(res)
    return pl.pallas_call(
        functools.partial(_mm_body, has_res=res is not None),
        grid=(b, s // tm),
        in_specs=in_specs,
        out_specs=pl.BlockSpec((None, tm, n), lambda i, t: (i, t, 0)),
        out_shape=jax.ShapeDtypeStruct((b, s, n), out_dtype),
        compiler_params=_params(("parallel", "parallel")),
        name="matmul_residual" if res is not None else "matmul",
    )(*args)


def _mla_proj_body(x_ref, g_ref, wd_ref, qg_ref, wq_ref, kvg_ref, wk_ref, wv_ref,
                   cos_ref, slo_ref, shi_ref, lat_ref, kr_ref, q_ref, k_ref, v_ref, *, qscale, mask_tile,
                   qt_cols):
    tm = x_ref.shape[0]
    h = _rms(x_ref[...], g_ref[...], _NORM_EPS).astype(_BF)
    cos, slo, shi = cos_ref[...], slo_ref[...], shi_ref[...]
    if mask_tile:
        row0 = pl.program_id(1) * tm
        q_ext = _mask_lanes(tm, row0, mask_tile, _B_ROPE, False)
        k_ext = _mask_lanes(tm, row0, mask_tile, _B_ROPE, True)
    else:
        q_ext = k_ext = jnp.zeros((tm, _LANES), _F32)
    dn = jnp.dot(h, wd_ref[...], preferred_element_type=_F32)
    e0, e1 = _B_Q_LORA, _B_Q_LORA + _B_KV_LORA
    cq = _rms(dn[:, :e0], qg_ref[...], _NORM_EPS).astype(_BF)
    lat = _rms(dn[:, e0:e1], kvg_ref[...], _NORM_EPS)
    kr = _rope_lanes(dn[:, e1:e1 + _LANES], cos, slo, shi)
    lat_ref[...] = lat
    kr_ref[...] = kr[:, :_B_ROPE]
    hw = 2 * _LANES
    for hd in range(_B_HEADS):
        yq = jnp.dot(cq, wq_ref[:, hd * hw:(hd + 1) * hw], preferred_element_type=_F32)
        q_nope = yq[:, :_LANES] * qscale
        q_rope = _rope_lanes(yq[:, _LANES:], cos, slo, shi) * qscale + q_ext
        if qt_cols:
            qt = jnp.concatenate([q_nope.T, q_rope.T], axis=0).astype(_BF)
            for g in range(tm // qt_cols):
                q_ref[hd, g] = qt[:, g * qt_cols:(g + 1) * qt_cols]
        else:
            q_ref[:, hd * hw:hd * hw + _LANES] = q_nope.astype(_BF)
            q_ref[:, hd * hw + _LANES:(hd + 1) * hw] = q_rope.astype(_BF)
    kin = jnp.concatenate([lat, kr], axis=1).astype(_BF)
    for hd in range(_B_HEADS):
        yk = jnp.dot(kin, wk_ref[:, hd * hw:(hd + 1) * hw], preferred_element_type=_F32)
        for var in range(2):
            k_ref[var, :, hd * hw:hd * hw + _LANES] = yk[:, :_LANES].astype(_BF)
        k_ref[0, :, hd * hw + _LANES:(hd + 1) * hw] = yk[:, _LANES:].astype(_BF)
        k_ref[1, :, hd * hw + _LANES:(hd + 1) * hw] = (yk[:, _LANES:] + k_ext).astype(_BF)
    latb = lat.astype(_BF)
    for c in range(0, v_ref.shape[-1], _MXU_COLS):
        v_ref[:, c:c + _MXU_COLS] = jnp.dot(
            latb, wv_ref[:, c:c + _MXU_COLS], preferred_element_type=_F32).astype(_BF)


def _mla_weights(w_down, w_uq, w_uk, w_uv):
    d = w_down.shape[0]
    hw = 2 * _LANES
    wd = jnp.concatenate([w_down, jnp.zeros((d, _LANES - _B_ROPE), w_down.dtype)], axis=1)
    wq = w_uq.reshape(_B_Q_LORA, _B_HEADS, _B_NOPE + _B_ROPE)
    wq = jnp.pad(wq, ((0, 0), (0, 0), (0, hw - _B_NOPE - _B_ROPE))).reshape(_B_Q_LORA, _B_HEADS * hw)
    eye = jnp.eye(_B_ROPE, dtype=w_uk.dtype)
    wk_top = jnp.pad(w_uk, ((0, 0), (0, 0), (0, hw - _B_NOPE)))
    wk_mid = jnp.broadcast_to(jnp.pad(eye, ((0, 0), (_B_NOPE, hw - _B_NOPE - _B_ROPE)))[:, None, :],
                              (_B_ROPE, _B_HEADS, hw))
    wk_bot = jnp.zeros((_LANES - _B_ROPE, _B_HEADS, hw), w_uk.dtype)
    wk = jnp.concatenate([wk_top, wk_mid, wk_bot], axis=0).reshape(-1, _B_HEADS * hw)
    wv = w_uv.reshape(_B_KV_LORA, _B_HEADS * _B_VD)
    return wd.astype(_BF), wq.astype(_BF), wk.astype(_BF), wv.astype(_BF)


def _mla_proj(x, g, wts, q_norm_g, kv_norm_g, tabs, tm, mask_tile, qt_cols=0):
    b, s, d = x.shape
    wd, wq, wk, wv = wts
    xspec = pl.BlockSpec((None, tm, d), lambda i, t: (i, t, 0))
    tspec = pl.BlockSpec((tm, _LANES), lambda i, t: (t, 0))
    ospec = lambda n: pl.BlockSpec((None, tm, n), lambda i, t: (i, t, 0))
    oshape = lambda n, dt: jax.ShapeDtypeStruct((b, s, n), dt)
    if qt_cols:
        assert tm % qt_cols == 0
        hq = wq.shape[1] // _B_HEADS
        q_spec = pl.BlockSpec((None, _B_HEADS, tm // qt_cols, hq, qt_cols),
                              lambda i, t: (i, 0, t, 0, 0))
        q_shape = jax.ShapeDtypeStruct((b, _B_HEADS, s // qt_cols, hq, qt_cols), _BF)
    else:
        q_spec, q_shape = ospec(wq.shape[1]), oshape(wq.shape[1], _BF)
    return pl.pallas_call(
        functools.partial(_mla_proj_body, qscale=_LOG2E / math.sqrt(_B_NOPE + _B_ROPE),
                          mask_tile=mask_tile, qt_cols=qt_cols),
        grid=(b, s // tm),
        in_specs=[xspec, _resident((1, d)), _resident(wd.shape), _resident((1, _B_Q_LORA)),
                  _resident(wq.shape), _resident((1, _B_KV_LORA)), _resident(wk.shape),
                  _resident(wv.shape), tspec, tspec, tspec],
        out_specs=[ospec(_B_KV_LORA), ospec(_B_ROPE), q_spec,
                   pl.BlockSpec((None, 2, tm, wk.shape[1]), lambda i, t: (i, 0, t, 0)),
                   ospec(wv.shape[1])],
        out_shape=[oshape(_B_KV_LORA, _F32), oshape(_B_ROPE, _F32), q_shape,
                   jax.ShapeDtypeStruct((b, 2, s, wk.shape[1]), _BF), oshape(wv.shape[1], _BF)],
        compiler_params=_params(("parallel", "parallel")),
        name="mla_proj",
    )(x, g.reshape(1, d), wd, q_norm_g.reshape(1, -1), wq, kv_norm_g.reshape(1, -1), wk, wv, *tabs)


def _conv_mix_body(x_ref, g_ref, win_ref, cw_ref, wout_ref, hist_ref, o_ref, hout_ref,
                   carry_sc, z_sc):
    tm, d = x_ref.shape

    @pl.when(pl.program_id(1) == 0)
    def _():
        carry_sc[...] = hist_ref[...]

    x = x_ref[...]
    h = _rms(x, g_ref[...], _NORM_EPS).astype(_BF)
    cw = _MXU_COLS
    for c in range(0, d, cw):
        gate_b = jnp.dot(h, win_ref[:, c:c + cw], preferred_element_type=_F32)
        gate_c = jnp.dot(h, win_ref[:, d + c:d + c + cw], preferred_element_type=_F32)
        val = jnp.dot(h, win_ref[:, 2 * d + c:2 * d + c + cw], preferred_element_type=_F32)
        u = gate_c * val
        prev = carry_sc[:, c:c + cw]
        y = (_shift_rows(prev, u, 2) * cw_ref[0:1, c:c + cw]
             + _shift_rows(prev, u, 1) * cw_ref[1:2, c:c + cw]
             + u * cw_ref[2:3, c:c + cw])
        z_sc[:, c:c + cw] = (gate_b * y).astype(_BF)
        carry_sc[:, c:c + cw] = u[tm - _SUBLANES:, :]
    hout_ref[...] = carry_sc[...]
    o_ref[...] = x + jnp.dot(z_sc[...], wout_ref[...], preferred_element_type=_F32)


def _conv_mix(x, g, w_in, conv_w, w_out, hist8, tm):
    b, s, d = x.shape
    xspec = pl.BlockSpec((None, tm, d), lambda i, t: (i, t, 0))
    hspec = pl.BlockSpec((None, _SUBLANES, d), lambda i, t: (i, 0, 0))
    return pl.pallas_call(
        _conv_mix_body,
        grid=(b, s // tm),
        in_specs=[xspec, _resident((1, d)), _resident(w_in.shape), _resident(conv_w.shape),
                  _resident(w_out.shape), hspec],
        out_specs=[xspec, hspec],
        out_shape=[jax.ShapeDtypeStruct((b, s, d), _F32),
                   jax.ShapeDtypeStruct((b, _SUBLANES, d), _F32)],
        scratch_shapes=[pltpu.VMEM((_SUBLANES, d), _F32), pltpu.VMEM((tm, d), _BF)],
        compiler_params=_params(("arbitrary", "arbitrary")),
        name="conv_mixer",
    )(x, g.reshape(1, d), w_in, conv_w, w_out, hist8)


_POOL_HALO = 16


def _pool_mix_body(x_ref, g_ref, wg_ref, sc_ref, hist_ref, o_ref, hout_ref, carry_sc, *, hist_valid):
    tm, d = x_ref.shape
    t = pl.program_id(1)

    @pl.when(t == 0)
    def _():
        carry_sc[...] = hist_ref[...]

    x = x_ref[...]
    h = _rms(x, g_ref[...], _NORM_EPS)
    gw = d // len(_D_WINDOWS)
    tpos = t * tm + lax.broadcasted_iota(jnp.int32, (tm, gw), 0)
    outs = []
    for gi, w in enumerate(_D_WINDOWS):
        hg = h[:, gi * gw:(gi + 1) * gw]
        acc = jnp.concatenate([carry_sc[:, gi * gw:(gi + 1) * gw], hg], axis=0)
        k = 1
        while k < w:
            acc = acc + pltpu.roll(acc, k, axis=0)
            k *= 2
        cnt = jnp.minimum(tpos + (hist_valid + 1), w).astype(_F32)
        pooled = acc[_POOL_HALO:] / cnt
        outs.append(jnp.dot((pooled - hg).astype(_BF), wg_ref[gi], preferred_element_type=_F32))
    o_ref[...] = x + jnp.concatenate(outs, axis=1) * sc_ref[...]
    tail = h[tm - _POOL_HALO:, :]
    carry_sc[...] = tail
    hout_ref[...] = tail


def _pool_mix(x, g, w_group, scale, hist16, hist_valid, tm):
    b, s, d = x.shape
    assert all(w & (w - 1) == 0 and w <= _POOL_HALO for w in _D_WINDOWS)
    xspec = pl.BlockSpec((None, tm, d), lambda i, t: (i, t, 0))
    hspec = pl.BlockSpec((None, _POOL_HALO, d), lambda i, t: (i, 0, 0))
    return pl.pallas_call(
        functools.partial(_pool_mix_body, hist_valid=hist_valid),
        grid=(b, s // tm),
        in_specs=[xspec, _resident((1, d)), _resident(w_group.shape), _resident((1, d)), hspec],
        out_specs=[xspec, hspec],
        out_shape=[jax.ShapeDtypeStruct((b, s, d), _F32),
                   jax.ShapeDtypeStruct((b, _POOL_HALO, d), _F32)],
        scratch_shapes=[pltpu.VMEM((_POOL_HALO, d), _F32)],
        compiler_params=_params(("arbitrary", "arbitrary")),
        name="pool_mixer",
    )(x, g.reshape(1, d), w_group, scale.reshape(1, d), hist16)


def _ffn_body(*refs, final, mixed):
    refs = list(refs)
    x_ref, g_ref, wg_ref, wu_ref, cw_ref, cb_ref, wd_ref, hist_ref = refs[:8]
    del refs[:8]
    a_ref, wo_ref = (refs.pop(0), refs.pop(0)) if mixed else (None, None)
    fg_ref = refs.pop(0) if final else None
    o_ref, hout_ref, carry_sc, act_sc = refs
    tm = x_ref.shape[0]
    f = wg_ref.shape[1]

    @pl.when(pl.program_id(1) == 0)
    def _():
        carry_sc[...] = hist_ref[...]

    x = x_ref[...]
    if mixed:
        x = x + jnp.dot(a_ref[...], wo_ref[...], preferred_element_type=_F32)
    h = _rms(x, g_ref[...], _NORM_EPS).astype(_BF)
    cw = _MXU_COLS
    for c in range(0, f, cw):
        gate = jnp.dot(h, wg_ref[:, c:c + cw], preferred_element_type=_F32)
        up = jnp.dot(h, wu_ref[:, c:c + cw], preferred_element_type=_F32)
        prev = carry_sc[:, c:c + cw]
        y = (_shift_rows(prev, gate, 2) * cw_ref[0:1, c:c + cw]
             + _shift_rows(prev, gate, 1) * cw_ref[1:2, c:c + cw]
             + gate * cw_ref[2:3, c:c + cw] + cb_ref[:, c:c + cw])
        act_sc[:, c:c + cw] = (y / (1.0 + jnp.exp(-y)) * up).astype(_BF)
        carry_sc[:, c:c + cw] = gate[tm - _SUBLANES:, :]
    hout_ref[...] = carry_sc[...]
    out = x + jnp.dot(act_sc[...], wd_ref[...], preferred_element_type=_F32)
    if final:
        out = _rms(out, fg_ref[...], _NORM_EPS)
    o_ref[...] = out


def _ffn(x, g, w_gate, w_up, conv_w, conv_b, w_down, hist8, final_g, tm, mixer_out=None, w_o=None):
    b, s, d = x.shape
    f = w_gate.shape[1]
    assert f % _MXU_COLS == 0
    xspec = pl.BlockSpec((None, tm, d), lambda i, t: (i, t, 0))
    hspec = pl.BlockSpec((None, _SUBLANES, f), lambda i, t: (i, 0, 0))
    in_specs = [xspec, _resident((1, d)), _resident(w_gate.shape), _resident(w_up.shape),
                _resident(conv_w.shape), _resident((1, f)), _resident(w_down.shape), hspec]
    args = [x, g.reshape(1, d), w_gate, w_up, conv_w, conv_b.reshape(1, f), w_down, hist8]
    if mixer_out is not None:
        in_specs += [pl.BlockSpec((None, tm, mixer_out.shape[-1]), lambda i, t: (i, t, 0)),
                     _resident(w_o.shape)]
        args += [mixer_out, w_o]
    if final_g is not None:
        in_specs.append(_resident((1, d)))
        args.append(final_g.reshape(1, d))
    return pl.pallas_call(
        functools.partial(_ffn_body, final=final_g is not None, mixed=mixer_out is not None),
        grid=(b, s // tm),
        in_specs=in_specs,
        out_specs=[xspec, hspec],
        out_shape=[jax.ShapeDtypeStruct((b, s, d), _F32),
                   jax.ShapeDtypeStruct((b, _SUBLANES, f), _F32)],
        scratch_shapes=[pltpu.VMEM((_SUBLANES, f), _F32), pltpu.VMEM((tm, f), _BF)],
        compiler_params=_params(("arbitrary", "arbitrary")),
        name="conv_ffn",
    )(*args)


def _pad_hist(hist, rows):
    return jnp.pad(hist, ((0, 0), (rows - hist.shape[1], 0), (0, 0)))


def kernel(x_prompt, x_sample, cache_a_k, cache_a_v, cache_b_latent, cache_b_krope, state_c_conv, state_d_pool, state_ffn_conv, norm_mix_g, norm_ffn_g, norm_final_g, a_w_qkv, a_lam, a_subln_g, a_w_o, b_w_down, b_q_norm_g, b_w_uq, b_kv_norm_g, b_w_uk, b_w_uv, b_w_o, c_w_in, c_conv_w, c_w_out, d_w_group, d_scale, ffn_w_gate, ffn_w_up, ffn_conv_w, ffn_conv_b, ffn_w_down):
    depth = norm_mix_g.shape[0]
    n_p, seq, d = x_prompt.shape
    n_s, t_new, _ = x_sample.shape
    past = cache_a_k.shape[2]
    f = ffn_w_gate.shape[-1]
    tm_p, tm_s = min(_ROW_TILE, seq), t_new
    ta = min(_ATT_TILE, seq)
    tabs_p = _rope_tables(jnp.arange(seq, dtype=jnp.int32))
    tabs_s = _rope_tables(past + jnp.arange(t_new, dtype=jnp.int32))
    xp, xs = x_prompt, x_sample
    outs = {k: [] for k in ("ak_p", "av_p", "bl_p", "br_p", "cc_p", "dp_p", "fc_p",
                            "ak_s", "av_s", "bl_s", "br_s", "cc_s", "dp_s", "fc_s")}
    for i in range(depth):
        m, j = i % _N_MIXERS, i // _N_MIXERS
        g_mix = norm_mix_g[i]
        mix_p, mix_s = {}, {}
        if m == 0:
            lam_init = 0.8 - 0.6 * math.exp(-0.3 * i)
            w_qkv, w_o = a_w_qkv[j].astype(_BF), a_w_o[j].astype(_BF)
            qm, kf, km, vf, vb = _diff_proj(xp, g_mix, w_qkv, tabs_p, tm_p, ta, ta // _SWEEP_HALVES)
            op = _sweep(qm, km, vb, _A_HEADS, _LANES, 2 * _A_HD, tile=ta, halves=_SWEEP_HALVES,
                        trip_ticks=_TRIP_TICKS_DIFF,
                        lam=a_lam[j], subln_g=a_subln_g[j], lam_init=lam_init)
            mix_p = dict(mixer_out=op, w_o=w_o)
            outs["ak_p"].append(kf.reshape(n_p, seq, _A_HEADS, 2, _A_HD))
            outs["av_p"].append(vf.reshape(n_p, seq, _A_HEADS, 2 * _A_HD))
            attn = functools.partial(_flash, heads=_A_HEADS, dq=_LANES, dv=2 * _A_HD, nmaps=2,
                                     lam=a_lam[j], subln_g=a_subln_g[j], lam_init=lam_init)
            qm, kf, km, vf, vb = _diff_proj(xs, g_mix, w_qkv, tabs_s, tm_s, 0)
            k_cache = jnp.pad(cache_a_k[j].astype(_BF).transpose(0, 3, 1, 2, 4),
                              ((0, 0),) * 4 + ((0, _LANES - _A_HD),))
            k_all = jnp.concatenate([k_cache.reshape(n_s, 2, past, _A_HEADS * _LANES), km[:, :, 0]],
                                    axis=2)
            v_all = jnp.concatenate([cache_a_v[j].reshape(n_s, past, d).astype(_BF), vb], axis=1)
            osm = attn(qm, k_all, v_all, causal=False, tq=t_new, tk=past + t_new)
            mix_s = dict(mixer_out=osm, w_o=w_o)
            outs["ak_s"].append(kf.reshape(n_s, t_new, _A_HEADS, 2, _A_HD))
            outs["av_s"].append(vf.reshape(n_s, t_new, _A_HEADS, 2 * _A_HD))
        elif m == 1:
            wts = _mla_weights(b_w_down[j], b_w_uq[j], b_w_uk[j], b_w_uv[j])
            w_o = b_w_o[j].astype(_BF)
            hq = 2 * _LANES
            attn = functools.partial(_flash, heads=_B_HEADS, dq=hq, dv=_B_VD)
            lat, kr, q, k2, v = _mla_proj(xp, g_mix, wts, b_q_norm_g[j], b_kv_norm_g[j], tabs_p,
                                          tm_p, ta, ta // _SWEEP_HALVES)
            op = _sweep(q[:, None], k2[:, None], v, _B_HEADS, hq, _B_VD, tile=ta,
                        halves=_SWEEP_HALVES, trip_ticks=_TRIP_TICKS_MLA)
            mix_p = dict(mixer_out=op, w_o=w_o)
            outs["bl_p"].append(lat)
            outs["br_p"].append(kr)
            lat, kr, q, k2, v = _mla_proj(xs, g_mix, wts, b_q_norm_g[j], b_kv_norm_g[j], tabs_s,
                                          tm_s, 0)
            kin = jnp.concatenate([cache_b_latent[j], cache_b_krope[j],
                                   jnp.zeros((n_s, past, _LANES - _B_ROPE), _F32)], axis=-1).astype(_BF)
            k_cache = _mm(kin, wts[2], out_dtype=_BF)
            v_cache = _mm(cache_b_latent[j].astype(_BF), wts[3], out_dtype=_BF)
            k_all = jnp.concatenate([k_cache, k2[:, 0]], axis=1)
            v_all = jnp.concatenate([v_cache, v], axis=1)
            osm = attn(q[:, None], k_all[:, None], v_all, causal=False, tq=t_new, tk=past + t_new)
            mix_s = dict(mixer_out=osm, w_o=w_o)
            outs["bl_s"].append(lat)
            outs["br_s"].append(kr)
        elif m == 2:
            w_in, w_out = c_w_in[j].astype(_BF), c_w_out[j].astype(_BF)
            xp, hc = _conv_mix(xp, g_mix, w_in, c_conv_w[j], w_out,
                               jnp.zeros((n_p, _SUBLANES, d), _F32), tm_p)
            outs["cc_p"].append(hc[:, -2:])
            xs, hc = _conv_mix(xs, g_mix, w_in, c_conv_w[j], w_out,
                               _pad_hist(state_c_conv[j], _SUBLANES), tm_s)
            outs["cc_s"].append(hc[:, -2:])
        else:
            w_grp = d_w_group[j].astype(_BF)
            xp, hd = _pool_mix(xp, g_mix, w_grp, d_scale[j],
                               jnp.zeros((n_p, _POOL_HALO, d), _F32), 0, tm_p)
            outs["dp_p"].append(hd[:, -_D_HIST:])
            xs, hd = _pool_mix(xs, g_mix, w_grp, d_scale[j],
                               _pad_hist(state_d_pool[j], _POOL_HALO), _D_HIST, tm_s)
            outs["dp_s"].append(hd[:, -_D_HIST:])
        wg, wu, wd = ffn_w_gate[i].astype(_BF), ffn_w_up[i].astype(_BF), ffn_w_down[i].astype(_BF)
        final_g = norm_final_g if i == depth - 1 else None
        xp, hf = _ffn(xp, norm_ffn_g[i], wg, wu, ffn_conv_w[i], ffn_conv_b[i], wd,
                      jnp.zeros((n_p, _SUBLANES, f), _F32), final_g, tm_p, **mix_p)
        outs["fc_p"].append(hf[:, -2:])
        xs, hf = _ffn(xs, norm_ffn_g[i], wg, wu, ffn_conv_w[i], ffn_conv_b[i], wd,
                      _pad_hist(state_ffn_conv[i], _SUBLANES), final_g, tm_s, **mix_s)
        outs["fc_s"].append(hf[:, -2:])
    st = lambda k: jnp.stack(outs[k])
    return (xp, xs, st("ak_p"), st("av_p"), st("bl_p"), st("br_p"), st("cc_p"), st("dp_p"), st("fc_p"),
            st("ak_s"), st("av_s"), st("bl_s"), st("br_s"), st("cc_s"), st("dp_s"), st("fc_s"))
```

```python
import functools
import math

import jax
import jax.numpy as jnp
from jax import lax
from jax.experimental import pallas as pl
from jax.experimental.pallas import tpu as pltpu

_BF = jnp.bfloat16
_F32 = jnp.float32

_CHUNK = 64
_ROPE_THETA = 10000.0
_NORM_EPS = 1e-6
_NEG_INF = -1e30
_A_HEADS = 8
_A_HD = 64
_A_SUBLN_EPS = 1e-5
_B_HEADS = 8
_B_NOPE = 128
_B_ROPE = 64
_B_VD = 128
_B_Q_LORA = 384
_B_KV_LORA = 256
_D_WINDOWS = (2, 4, 8, 16)
_D_HIST = 15
_N_MIXERS = 4
_LOG2E = math.log2(math.e)

_LANES = 128
_SUBLANES = 8
_MXU_COLS = 256
_BF_ROWS = 16
_SWEEP_SLOTS = 4
_TRIP_TICKS_DIFF = 8
_TRIP_TICKS_MLA = 16
_SWEEP_HALVES = 1
_VMEM_LIMIT = 56 * 1024 * 1024

_ROW_TILE = 512
_ATT_TILE = 512


def _params(sem):
    return pltpu.CompilerParams(dimension_semantics=sem, vmem_limit_bytes=_VMEM_LIMIT)


def _resident(shape):
    nd = len(shape)
    return pl.BlockSpec(shape, lambda *_: (0,) * nd, pipeline_mode=pl.Buffered(1))


def _rms(x, g, eps):
    ms = jnp.mean(x * x, axis=-1, keepdims=True)
    return x * lax.rsqrt(ms + eps) * g


def _rope_lanes(y, cos, sin_lo, sin_hi):
    return (y * cos + pltpu.roll(y, _LANES - 32, axis=1) * sin_lo
            + pltpu.roll(y, 32, axis=1) * sin_hi)


def _rope_tables(pos):
    d = _A_HD
    inv = jnp.power(_ROPE_THETA, -jnp.arange(0, d, 2, dtype=_F32) / d)
    ang = pos.astype(_F32)[:, None] * inv[None, :]
    cos, sin = jnp.cos(ang), jnp.sin(ang)
    zero = jnp.zeros_like(sin)
    cos_t = jnp.tile(cos, (1, 4))
    sin_lo = jnp.tile(jnp.concatenate([-sin, zero], axis=1), (1, 2))
    sin_hi = jnp.tile(jnp.concatenate([zero, sin], axis=1), (1, 2))
    return cos_t, sin_lo, sin_hi


def _shift_rows(carry, cur, k):
    ext = jnp.concatenate([carry, cur], axis=0)
    return pltpu.roll(ext, k, axis=0)[carry.shape[0]:]


def _mask_lanes(rows, row0, tile, base, for_keys):
    r = row0 + lax.broadcasted_iota(jnp.int32, (rows, _LANES), 0)
    chunk = (r % tile) // _CHUNK
    c = lax.broadcasted_iota(jnp.int32, (rows, _LANES), 1) - base
    n = tile // _CHUNK
    if for_keys:
        return jnp.where(c == chunk, 1.0, 0.0)
    return jnp.where(c > chunk, jnp.where(c < n, _NEG_INF, 0.0), 0.0)


def _diff_proj_body(x_ref, g_ref, w_ref, cos_ref, slo_ref, shi_ref, *out_refs, qscale, qt_cols):
    tm, d = x_ref.shape
    if qt_cols:
        qm_ref, kf_ref, km_ref, vf_ref, vt_ref = out_refs
        row0 = pl.program_id(1) * tm
        q_ext = _mask_lanes(tm, row0, tm, _A_HD, False)
        k_ext = _mask_lanes(tm, row0, tm, _A_HD, True)
        ones = jnp.ones((_BF_ROWS, tm), _BF)
    else:
        qm_ref, kf_ref, vf_ref = out_refs
    h = _rms(x_ref[...], g_ref[...], _NORM_EPS).astype(_BF)
    cos, slo, shi = cos_ref[...], slo_ref[...], shi_ref[...]
    first_half = lax.broadcasted_iota(jnp.int32, (tm, _LANES), 1) < _A_HD
    zero = jnp.zeros((tm, _LANES), _F32)
    cw = _MXU_COLS
    for c in range(0, d, cw):
        yq = jnp.dot(h, w_ref[:, c:c + cw], preferred_element_type=_F32)
        yk = jnp.dot(h, w_ref[:, d + c:d + c + cw], preferred_element_type=_F32)
        yv = jnp.dot(h, w_ref[:, 2 * d + c:2 * d + c + cw], preferred_element_type=_F32)
        vf_ref[:, c:c + cw] = yv
        for s in range(0, cw, _LANES):
            cols = slice(c + s, c + s + _LANES)
            head = (c + s) // _LANES
            q = _rope_lanes(yq[:, s:s + _LANES], cos, slo, shi) * qscale
            k = _rope_lanes(yk[:, s:s + _LANES], cos, slo, shi)
            kf_ref[:, cols] = k
            if not qt_cols:
                qm_ref[0, :, cols] = jnp.where(first_half, q, zero).astype(_BF)
                qm_ref[1, :, cols] = jnp.where(first_half, zero, q).astype(_BF)
                continue
            for mp in range(2):
                qh = q if mp == 0 else pltpu.roll(q, _A_HD, axis=1)
                kh = k if mp == 0 else pltpu.roll(k, _A_HD, axis=1)
                qt = jnp.where(first_half, qh, q_ext).T.astype(_BF)
                for g in range(tm // qt_cols):
                    qm_ref[mp, head, g] = qt[:, g * qt_cols:(g + 1) * qt_cols]
                km_ref[mp, 0, :, cols] = jnp.where(first_half, kh, zero).astype(_BF)
                km_ref[mp, 1, :, cols] = jnp.where(first_half, kh, k_ext).astype(_BF)
            vt_ref[head, 0, :_LANES, :] = yv[:, s:s + _LANES].T.astype(_BF)
            vt_ref[head, 0, _LANES:, :] = ones


def _diff_proj(x, g, w_qkv, tabs, tm, qt_cols=0):
    b, s, d = x.shape
    heads = d // _LANES
    row = lambda dt: jax.ShapeDtypeStruct((b, s, d), dt)
    xspec = pl.BlockSpec((None, tm, d), lambda i, t: (i, t, 0))
    tspec = pl.BlockSpec((tm, _LANES), lambda i, t: (t, 0))
    if qt_cols:
        assert tm % qt_cols == 0
        dve = _LANES + _BF_ROWS
        out_specs = [pl.BlockSpec((None, 2, heads, tm // qt_cols, _LANES, qt_cols),
                                  lambda i, t: (i, 0, 0, t, 0, 0)),
                     xspec, pl.BlockSpec((None, 2, 2, tm, d), lambda i, t: (i, 0, 0, t, 0)), xspec,
                     pl.BlockSpec((None, heads, 1, dve, tm), lambda i, t: (i, 0, t, 0, 0))]
        out_shape = [jax.ShapeDtypeStruct((b, 2, heads, s // qt_cols, _LANES, qt_cols), _BF),
                     row(_F32), jax.ShapeDtypeStruct((b, 2, 2, s, d), _BF), row(_F32),
                     jax.ShapeDtypeStruct((b, heads, s // tm, dve, tm), _BF)]
    else:
        out_specs = [pl.BlockSpec((None, 2, tm, d), lambda i, t: (i, 0, t, 0)), xspec, xspec]
        out_shape = [jax.ShapeDtypeStruct((b, 2, s, d), _BF), row(_F32), row(_F32)]
    return pl.pallas_call(
        functools.partial(_diff_proj_body, qscale=_LOG2E / math.sqrt(_A_HD), qt_cols=qt_cols),
        grid=(b, s // tm),
        in_specs=[xspec, _resident((1, d)), _resident(w_qkv.shape), tspec, tspec, tspec],
        out_specs=out_specs,
        out_shape=out_shape,
        compiler_params=_params(("parallel", "parallel")),
        name="diff_proj",
    )(x, g.reshape(1, d), w_qkv, *tabs)


def _cached_diff_body(q_ref, kc_ref, kn_ref, vc_ref, vn_ref, lam_ref, sg_ref, o_ref, *, lam_init):
    kc, kn = kc_ref[...].astype(_BF), kn_ref[...].astype(_BF)
    vc, vn = vc_ref[...].astype(_BF), vn_ref[...].astype(_BF)
    nt = (((1,), (1,)), ((), ()))
    outs = []
    for c in range(2):
        q = q_ref[c]
        sc = lax.dot_general(q, kc, nt, preferred_element_type=_F32)
        sn = lax.dot_general(q, kn, nt, preferred_element_type=_F32)
        m = jnp.maximum(jnp.max(sc, axis=-1, keepdims=True), jnp.max(sn, axis=-1, keepdims=True))
        pc, pn = jnp.exp2(sc - m), jnp.exp2(sn - m)
        l = jnp.sum(pc, axis=-1, keepdims=True) + jnp.sum(pn, axis=-1, keepdims=True)
        acc = (jnp.dot(pc.astype(_BF), vc, preferred_element_type=_F32)
               + jnp.dot(pn.astype(_BF), vn, preferred_element_type=_F32))
        outs.append(acc / l)
    lp = lam_ref[...]
    lam = (jnp.exp(jnp.sum(lp[0:1] * lp[1:2], axis=-1, keepdims=True))
           - jnp.exp(jnp.sum(lp[2:3] * lp[3:4], axis=-1, keepdims=True)) + lam_init)
    o = _rms(outs[0] - lam * outs[1], sg_ref[...], _A_SUBLN_EPS) * (1.0 - lam_init)
    o_ref[...] = o.astype(o_ref.dtype)


def _cached_diff(qm, k_cache, k_new, v_cache, v_new, heads, lam, subln_g, lam_init):
    b, _, t, _ = qm.shape
    past = k_cache.shape[1]
    dv = _LANES
    blk = lambda rows: pl.BlockSpec((None, rows, dv), lambda i, h: (i, 0, h))
    return pl.pallas_call(
        functools.partial(_cached_diff_body, lam_init=lam_init),
        grid=(b, heads),
        in_specs=[pl.BlockSpec((None, 2, t, dv), lambda i, h: (i, 0, 0, h)),
                  blk(past), blk(t), blk(past), blk(t), _resident(lam.shape), _resident((1, dv))],
        out_specs=blk(t),
        out_shape=jax.ShapeDtypeStruct((b, t, heads * dv), _BF),
        compiler_params=_params(("parallel", "parallel")),
        name="cached_diff",
    )(qm, k_cache, k_new, v_cache, v_new, lam, subln_g.reshape(1, dv))


def _cached_mla_body(q_ref, k_ref, v_ref, o_ref):
    s = lax.dot_general(q_ref[...], k_ref[...], (((1,), (1,)), ((), ())),
                        preferred_element_type=_F32)
    p = jnp.exp2(s - jnp.max(s, axis=-1, keepdims=True))
    acc = jnp.dot(p.astype(_BF), v_ref[...], preferred_element_type=_F32)
    o_ref[...] = (acc / jnp.sum(p, axis=-1, keepdims=True)).astype(o_ref.dtype)


def _cached_mla(q, k, v, heads, dq, dv):
    b, t, _ = q.shape
    sk = k.shape[1]
    return pl.pallas_call(
        _cached_mla_body,
        grid=(b, heads),
        in_specs=[pl.BlockSpec((None, t, dq), lambda i, h: (i, 0, h)),
                  pl.BlockSpec((None, sk, dq), lambda i, h: (i, 0, h)),
                  pl.BlockSpec((None, sk, dv), lambda i, h: (i, 0, h))],
        out_specs=pl.BlockSpec((None, t, dv), lambda i, h: (i, 0, h)),
        out_shape=jax.ShapeDtypeStruct((b, t, heads * dv), _BF),
        compiler_params=_params(("parallel", "parallel")),
        name="cached_mla",
    )(q, k, v)


def _sweep_body(*refs, nmaps, halves, tile, nb, lam_init):
    if nmaps == 2:
        q_ref, k_ref, vt_ref, lam_ref, sg_ref, o_ref, s_buf, mb_buf, m_sc, acc_sc = refs
    else:
        q_ref, k_ref, vt_ref, o_ref, s_buf, mb_buf, m_sc, acc_sc = refs
    hw = tile // halves
    nch = nmaps * halves
    dv = o_ref.shape[-1]
    ticks = acc_sc.shape[0]
    npairs = nb * (nb + 1) // 2
    acc_sc[...] = jnp.zeros(acc_sc.shape, _F32)

    def nxt(pair):
        qi, t = pair
        last = t == qi
        return jnp.where(last, qi + 1, qi), jnp.where(last, 0, t + 1)

    def scores(pair, slot):
        qi, t = pair
        variant = (t == qi).astype(jnp.int32)
        for mp in range(nmaps):
            k = k_ref[mp, variant, pl.ds(pl.multiple_of(t * tile, tile), tile), :]
            for hf in range(halves):
                c = mp * halves + hf
                s = jnp.dot(k, q_ref[mp, qi * halves + hf], preferred_element_type=_F32)
                s_buf[slot, c] = s
                mb_buf[slot, c] = jnp.max(s, axis=0, keepdims=True)

    def values(pair, pos):
        _, t = pair
        first = t == 0
        slot = pos % _SWEEP_SLOTS
        vt = vt_ref[t]
        for c in range(nch):
            m_prev = jnp.where(first, _NEG_INF, m_sc[c])
            m_new = jnp.maximum(m_prev, mb_buf[slot, c])
            alpha = jnp.exp2(m_prev - m_new)
            m_sc[c] = m_new
            p = jnp.exp2(s_buf[slot, c] - m_new).astype(_BF)
            acc_sc[pos, c] = (alpha * acc_sc[(pos - 1) % ticks, c]
                              + jnp.dot(vt, p, preferred_element_type=_F32))

    def finish(pair, pos):
        qi, t = pair

        @pl.when(t == qi)
        def _():
            if nmaps == 2:
                lp = lam_ref[...]
                lam = (jnp.exp(jnp.sum(lp[0:1] * lp[1:2], axis=-1, keepdims=True))
                       - jnp.exp(jnp.sum(lp[2:3] * lp[3:4], axis=-1, keepdims=True)) + lam_init)
            normed = lambda c: acc_sc[pos, c, :dv, :] / acc_sc[pos, c, dv:dv + 1, :]
            for hf in range(halves):
                if nmaps == 2:
                    ot = normed(hf) - lam * normed(halves + hf)
                    o = _rms(ot.T, sg_ref[...], _A_SUBLN_EPS) * (1.0 - lam_init)
                else:
                    o = normed(hf).T
                row = pl.multiple_of(qi * tile + hf * hw, hw)
                o_ref[pl.ds(row, hw), :] = o.astype(o_ref.dtype)

    zero = jnp.int32(0)
    pair0 = (zero, zero)
    scores(pair0, 0)
    if npairs == 1:
        values(pair0, 0)
        finish(pair0, 0)
        return
    pair1 = nxt(pair0)
    scores(pair1, 1)

    def tick(carry, r):
        pc, pb, pa = carry
        scores(pa, (2 + r) % _SWEEP_SLOTS)
        values(pc, r)
        return pb, pa, nxt(pa)

    def trip(_, carry):
        done = []
        for r in range(ticks):
            done.append(carry[0])
            carry = tick(carry, r)
        for r, pair in enumerate(done):
            finish(pair, r)
        return carry

    nticks = npairs - 2
    carry = lax.fori_loop(0, nticks // ticks, trip, (pair0, pair1, nxt(pair1)))
    for r in range(nticks % ticks):
        pair = carry[0]
        carry = tick(carry, r)
        finish(pair, r)
    before_last, last, _ = carry
    for pair, pos in ((before_last, (npairs - 2) % ticks), (last, (npairs - 1) % ticks)):
        values(pair, pos)
        finish(pair, pos)


def _sweep(q, k, vt, heads, dq, dv, *, halves, trip_ticks, lam=None, subln_g=None, lam_init=0.0):
    b, nmaps = q.shape[:2]
    _, _, nb, dve, tile = vt.shape
    s = nb * tile
    assert tile % _CHUNK == 0 and tile % halves == 0 and trip_ticks % _SWEEP_SLOTS == 0
    assert q.shape[2:] == (heads, s * halves // tile, dq, tile // halves) and dve == dv + _BF_ROWS
    nch = nmaps * halves
    hw = tile // halves
    once = pl.Buffered(1)
    in_specs = [pl.BlockSpec((None, nmaps, None, nb * halves, dq, hw),
                             lambda i, h: (i, 0, h, 0, 0, 0), pipeline_mode=once),
                pl.BlockSpec((None, nmaps, 2, s, dq), lambda i, h: (i, 0, 0, 0, h),
                             pipeline_mode=once),
                pl.BlockSpec((None, None, nb, dve, tile), lambda i, h: (i, h, 0, 0, 0),
                             pipeline_mode=once)]
    args = [q, k, vt]
    if nmaps == 2:
        in_specs += [_resident(lam.shape), _resident((1, dv))]
        args += [lam, subln_g.reshape(1, dv)]
    scratch = [pltpu.VMEM((_SWEEP_SLOTS, nch, tile, hw), _F32),
               pltpu.VMEM((_SWEEP_SLOTS, nch, 1, hw), _F32),
               pltpu.VMEM((nch, 1, hw), _F32), pltpu.VMEM((trip_ticks, nch, dve, hw), _F32)]
    return pl.pallas_call(
        functools.partial(_sweep_body, nmaps=nmaps, halves=halves, tile=tile, nb=nb,
                          lam_init=lam_init),
        grid=(b, heads),
        in_specs=in_specs,
        out_specs=pl.BlockSpec((None, s, dv), lambda i, h: (i, 0, h)),
        out_shape=jax.ShapeDtypeStruct((b, s, heads * dv), _BF),
        scratch_shapes=scratch,
        compiler_params=_params(("parallel", "parallel")),
        name="sweep_diff" if nmaps == 2 else "sweep_mla",
    )(*args)


def _mm_body(*refs, has_res):
    if has_res:
        a_ref, w_ref, r_ref, o_ref = refs
    else:
        a_ref, w_ref, o_ref = refs
    y = jnp.dot(a_ref[...].astype(_BF), w_ref[...], preferred_element_type=_F32)
    if has_res:
        y = r_ref[...] + y
    o_ref[...] = y.astype(o_ref.dtype)


def _mm(a, w, res=None, out_dtype=_F32):
    b, s, kdim = a.shape
    n = w.shape[1]
    tm = min(_ROW_TILE, s)
    in_specs = [pl.BlockSpec((None, tm, kdim), lambda i, t: (i, t, 0)), _resident(w.shape)]
    args = [a, w]
    if res is not None:
        in_specs.append(pl.BlockSpec((None, tm, n), lambda i, t: (i, t, 0)))
        args.append(res)
    return pl.pallas_call(
        functools.partial(_mm_body, has_res=res is not None),
        grid=(b, s // tm),
        in_specs=in_specs,
        out_specs=pl.BlockSpec((None, tm, n), lambda i, t: (i, t, 0)),
        out_shape=jax.ShapeDtypeStruct((b, s, n), out_dtype),
        compiler_params=_params(("parallel", "parallel")),
        name="matmul_residual" if res is not None else "matmul",
    )(*args)


def _mla_proj_body(x_ref, g_ref, wd_ref, qg_ref, wq_ref, kvg_ref, wk_ref, wv_ref,
                   cos_ref, slo_ref, shi_ref, lat_ref, kr_ref, q_ref, k_ref, v_ref, *, qscale, mask_tile,
                   qt_cols):
    tm = x_ref.shape[0]
    h = _rms(x_ref[...], g_ref[...], _NORM_EPS).astype(_BF)
    cos, slo, shi = cos_ref[...], slo_ref[...], shi_ref[...]
    if mask_tile:
        row0 = pl.program_id(1) * tm
        q_ext = _mask_lanes(tm, row0, mask_tile, _B_ROPE, False)
        k_ext = _mask_lanes(tm, row0, mask_tile, _B_ROPE, True)
    else:
        q_ext = k_ext = jnp.zeros((tm, _LANES), _F32)
    dn = jnp.dot(h, wd_ref[...], preferred_element_type=_F32)
    e0, e1 = _B_Q_LORA, _B_Q_LORA + _B_KV_LORA
    cq = _rms(dn[:, :e0], qg_ref[...], _NORM_EPS).astype(_BF)
    lat = _rms(dn[:, e0:e1], kvg_ref[...], _NORM_EPS)
    kr = _rope_lanes(dn[:, e1:e1 + _LANES], cos, slo, shi)
    lat_ref[...] = lat
    kr_ref[...] = kr[:, :_B_ROPE]
    hw = 2 * _LANES
    for hd in range(_B_HEADS):
        yq = jnp.dot(cq, wq_ref[:, hd * hw:(hd + 1) * hw], preferred_element_type=_F32)
        q_nope = yq[:, :_LANES] * qscale
        q_rope = _rope_lanes(yq[:, _LANES:], cos, slo, shi) * qscale + q_ext
        if qt_cols:
            qt = jnp.concatenate([q_nope.T, q_rope.T], axis=0).astype(_BF)
            for g in range(tm // qt_cols):
                q_ref[hd, g] = qt[:, g * qt_cols:(g + 1) * qt_cols]
        else:
            q_ref[:, hd * hw:hd * hw + _LANES] = q_nope.astype(_BF)
            q_ref[:, hd * hw + _LANES:(hd + 1) * hw] = q_rope.astype(_BF)
    kin = jnp.concatenate([lat, kr], axis=1).astype(_BF)
    for hd in range(_B_HEADS):
        yk = jnp.dot(kin, wk_ref[:, hd * hw:(hd + 1) * hw], preferred_element_type=_F32)
        for var in range(2):
            k_ref[var, :, hd * hw:hd * hw + _LANES] = yk[:, :_LANES].astype(_BF)
        k_ref[0, :, hd * hw + _LANES:(hd + 1) * hw] = yk[:, _LANES:].astype(_BF)
        k_ref[1, :, hd * hw + _LANES:(hd + 1) * hw] = (yk[:, _LANES:] + k_ext).astype(_BF)
    latb = lat.astype(_BF)
    for c in range(0, wv_ref.shape[-1], _MXU_COLS):
        yv = jnp.dot(latb, wv_ref[:, c:c + _MXU_COLS], preferred_element_type=_F32)
        if qt_cols:
            for s in range(0, _MXU_COLS, _B_VD):
                v_ref[(c + s) // _B_VD, 0, :_B_VD, :] = yv[:, s:s + _B_VD].T.astype(_BF)
                v_ref[(c + s) // _B_VD, 0, _B_VD:, :] = jnp.ones((_BF_ROWS, tm), _BF)
        else:
            v_ref[:, c:c + _MXU_COLS] = yv.astype(_BF)


def _mla_weights(w_down, w_uq, w_uk, w_uv):
    d = w_down.shape[0]
    hw = 2 * _LANES
    wd = jnp.concatenate([w_down, jnp.zeros((d, _LANES - _B_ROPE), w_down.dtype)], axis=1)
    wq = w_uq.reshape(_B_Q_LORA, _B_HEADS, _B_NOPE + _B_ROPE)
    wq = jnp.pad(wq, ((0, 0), (0, 0), (0, hw - _B_NOPE - _B_ROPE))).reshape(_B_Q_LORA, _B_HEADS * hw)
    eye = jnp.eye(_B_ROPE, dtype=w_uk.dtype)
    wk_top = jnp.pad(w_uk, ((0, 0), (0, 0), (0, hw - _B_NOPE)))
    wk_mid = jnp.broadcast_to(jnp.pad(eye, ((0, 0), (_B_NOPE, hw - _B_NOPE - _B_ROPE)))[:, None, :],
                              (_B_ROPE, _B_HEADS, hw))
    wk_bot = jnp.zeros((_LANES - _B_ROPE, _B_HEADS, hw), w_uk.dtype)
    wk = jnp.concatenate([wk_top, wk_mid, wk_bot], axis=0).reshape(-1, _B_HEADS * hw)
    wv = w_uv.reshape(_B_KV_LORA, _B_HEADS * _B_VD)
    return wd.astype(_BF), wq.astype(_BF), wk.astype(_BF), wv.astype(_BF)


def _mla_proj(x, g, wts, q_norm_g, kv_norm_g, tabs, tm, mask_tile, qt_cols=0):
    b, s, d = x.shape
    wd, wq, wk, wv = wts
    xspec = pl.BlockSpec((None, tm, d), lambda i, t: (i, t, 0))
    tspec = pl.BlockSpec((tm, _LANES), lambda i, t: (t, 0))
    ospec = lambda n: pl.BlockSpec((None, tm, n), lambda i, t: (i, t, 0))
    oshape = lambda n, dt: jax.ShapeDtypeStruct((b, s, n), dt)
    if qt_cols:
        assert tm % qt_cols == 0
        hq = wq.shape[1] // _B_HEADS
        q_spec = pl.BlockSpec((None, _B_HEADS, tm // qt_cols, hq, qt_cols),
                              lambda i, t: (i, 0, t, 0, 0))
        q_shape = jax.ShapeDtypeStruct((b, _B_HEADS, s // qt_cols, hq, qt_cols), _BF)
        dve = _B_VD + _BF_ROWS
        v_spec = pl.BlockSpec((None, _B_HEADS, 1, dve, tm), lambda i, t: (i, 0, t, 0, 0))
        v_shape = jax.ShapeDtypeStruct((b, _B_HEADS, s // tm, dve, tm), _BF)
    else:
        q_spec, q_shape = ospec(wq.shape[1]), oshape(wq.shape[1], _BF)
        v_spec, v_shape = ospec(wv.shape[1]), oshape(wv.shape[1], _BF)
    return pl.pallas_call(
        functools.partial(_mla_proj_body, qscale=_LOG2E / math.sqrt(_B_NOPE + _B_ROPE),
                          mask_tile=mask_tile, qt_cols=qt_cols),
        grid=(b, s // tm),
        in_specs=[xspec, _resident((1, d)), _resident(wd.shape), _resident((1, _B_Q_LORA)),
                  _resident(wq.shape), _resident((1, _B_KV_LORA)), _resident(wk.shape),
                  _resident(wv.shape), tspec, tspec, tspec],
        out_specs=[ospec(_B_KV_LORA), ospec(_B_ROPE), q_spec,
                   pl.BlockSpec((None, 2, tm, wk.shape[1]), lambda i, t: (i, 0, t, 0)), v_spec],
        out_shape=[oshape(_B_KV_LORA, _F32), oshape(_B_ROPE, _F32), q_shape,
                   jax.ShapeDtypeStruct((b, 2, s, wk.shape[1]), _BF), v_shape],
        compiler_params=_params(("parallel", "parallel")),
        name="mla_proj",
    )(x, g.reshape(1, d), wd, q_norm_g.reshape(1, -1), wq, kv_norm_g.reshape(1, -1), wk, wv, *tabs)


def _conv_mix_body(x_ref, g_ref, win_ref, cw_ref, wout_ref, hist_ref, o_ref, hout_ref,
                   carry_sc, z_sc):
    tm, d = x_ref.shape

    @pl.when(pl.program_id(1) == 0)
    def _():
        carry_sc[...] = hist_ref[...]

    x = x_ref[...]
    h = _rms(x, g_ref[...], _NORM_EPS).astype(_BF)
    cw = _MXU_COLS
    for c in range(0, d, cw):
        gate_b = jnp.dot(h, win_ref[:, c:c + cw], preferred_element_type=_F32)
        gate_c = jnp.dot(h, win_ref[:, d + c:d + c + cw], preferred_element_type=_F32)
        val = jnp.dot(h, win_ref[:, 2 * d + c:2 * d + c + cw], preferred_element_type=_F32)
        u = gate_c * val
        prev = carry_sc[:, c:c + cw]
        y = (_shift_rows(prev, u, 2) * cw_ref[0:1, c:c + cw]
             + _shift_rows(prev, u, 1) * cw_ref[1:2, c:c + cw]
             + u * cw_ref[2:3, c:c + cw])
        z_sc[:, c:c + cw] = (gate_b * y).astype(_BF)
        carry_sc[:, c:c + cw] = u[tm - _SUBLANES:, :]
    hout_ref[...] = carry_sc[...]
    o_ref[...] = x + jnp.dot(z_sc[...], wout_ref[...], preferred_element_type=_F32)


def _conv_mix(x, g, w_in, conv_w, w_out, hist8, tm):
    b, s, d = x.shape
    xspec = pl.BlockSpec((None, tm, d), lambda i, t: (i, t, 0))
    hspec = pl.BlockSpec((None, _SUBLANES, d), lambda i, t: (i, 0, 0))
    return pl.pallas_call(
        _conv_mix_body,
        grid=(b, s // tm),
        in_specs=[xspec, _resident((1, d)), _resident(w_in.shape), _resident(conv_w.shape),
                  _resident(w_out.shape), hspec],
        out_specs=[xspec, hspec],
        out_shape=[jax.ShapeDtypeStruct((b, s, d), _F32),
                   jax.ShapeDtypeStruct((b, _SUBLANES, d), _F32)],
        scratch_shapes=[pltpu.VMEM((_SUBLANES, d), _F32), pltpu.VMEM((tm, d), _BF)],
        compiler_params=_params(("arbitrary", "arbitrary")),
        name="conv_mixer",
    )(x, g.reshape(1, d), w_in, conv_w, w_out, hist8)


_POOL_HALO = 16


def _pool_mix_body(x_ref, g_ref, wg_ref, sc_ref, hist_ref, o_ref, hout_ref, carry_sc, *, hist_valid):
    tm, d = x_ref.shape
    t = pl.program_id(1)

    @pl.when(t == 0)
    def _():
        carry_sc[...] = hist_ref[...]

    x = x_ref[...]
    h = _rms(x, g_ref[...], _NORM_EPS)
    gw = d // len(_D_WINDOWS)
    tpos = t * tm + lax.broadcasted_iota(jnp.int32, (tm, gw), 0)
    outs = []
    for gi, w in enumerate(_D_WINDOWS):
        hg = h[:, gi * gw:(gi + 1) * gw]
        acc = jnp.concatenate([carry_sc[:, gi * gw:(gi + 1) * gw], hg], axis=0)
        k = 1
        while k < w:
            acc = acc + pltpu.roll(acc, k, axis=0)
            k *= 2
        cnt = jnp.minimum(tpos + (hist_valid + 1), w).astype(_F32)
        pooled = acc[_POOL_HALO:] / cnt
        outs.append(jnp.dot((pooled - hg).astype(_BF), wg_ref[gi], preferred_element_type=_F32))
    o_ref[...] = x + jnp.concatenate(outs, axis=1) * sc_ref[...]
    tail = h[tm - _POOL_HALO:, :]
    carry_sc[...] = tail
    hout_ref[...] = tail


def _pool_mix(x, g, w_group, scale, hist16, hist_valid, tm):
    b, s, d = x.shape
    assert all(w & (w - 1) == 0 and w <= _POOL_HALO for w in _D_WINDOWS)
    xspec = pl.BlockSpec((None, tm, d), lambda i, t: (i, t, 0))
    hspec = pl.BlockSpec((None, _POOL_HALO, d), lambda i, t: (i, 0, 0))
    return pl.pallas_call(
        functools.partial(_pool_mix_body, hist_valid=hist_valid),
        grid=(b, s // tm),
        in_specs=[xspec, _resident((1, d)), _resident(w_group.shape), _resident((1, d)), hspec],
        out_specs=[xspec, hspec],
        out_shape=[jax.ShapeDtypeStruct((b, s, d), _F32),
                   jax.ShapeDtypeStruct((b, _POOL_HALO, d), _F32)],
        scratch_shapes=[pltpu.VMEM((_POOL_HALO, d), _F32)],
        compiler_params=_params(("arbitrary", "arbitrary")),
        name="pool_mixer",
    )(x, g.reshape(1, d), w_group, scale.reshape(1, d), hist16)


def _ffn_body(*refs, final, mixed):
    refs = list(refs)
    x_ref, g_ref, wg_ref, wu_ref, cw_ref, cb_ref, wd_ref, hist_ref = refs[:8]
    del refs[:8]
    a_ref, wo_ref = (refs.pop(0), refs.pop(0)) if mixed else (None, None)
    fg_ref = refs.pop(0) if final else None
    o_ref, hout_ref, carry_sc, act_sc = refs
    tm = x_ref.shape[0]
    f = wg_ref.shape[1]

    @pl.when(pl.program_id(1) == 0)
    def _():
        carry_sc[...] = hist_ref[...]

    x = x_ref[...]
    if mixed:
        x = x + jnp.dot(a_ref[...], wo_ref[...], preferred_element_type=_F32)
    h = _rms(x, g_ref[...], _NORM_EPS).astype(_BF)
    cw = _MXU_COLS
    for c in range(0, f, cw):
        gate = jnp.dot(h, wg_ref[:, c:c + cw], preferred_element_type=_F32)
        up = jnp.dot(h, wu_ref[:, c:c + cw], preferred_element_type=_F32)
        prev = carry_sc[:, c:c + cw]
        y = (_shift_rows(prev, gate, 2) * cw_ref[0:1, c:c + cw]
             + _shift_rows(prev, gate, 1) * cw_ref[1:2, c:c + cw]
             + gate * cw_ref[2:3, c:c + cw] + cb_ref[:, c:c + cw])
        act_sc[:, c:c + cw] = (y / (1.0 + jnp.exp(-y)) * up).astype(_BF)
        carry_sc[:, c:c + cw] = gate[tm - _SUBLANES:, :]
    hout_ref[...] = carry_sc[...]
    out = x + jnp.dot(act_sc[...], wd_ref[...], preferred_element_type=_F32)
    if final:
        out = _rms(out, fg_ref[...], _NORM_EPS)
    o_ref[...] = out


def _ffn(x, g, w_gate, w_up, conv_w, conv_b, w_down, hist8, final_g, tm, mixer_out=None, w_o=None):
    b, s, d = x.shape
    f = w_gate.shape[1]
    assert f % _MXU_COLS == 0
    xspec = pl.BlockSpec((None, tm, d), lambda i, t: (i, t, 0))
    hspec = pl.BlockSpec((None, _SUBLANES, f), lambda i, t: (i, 0, 0))
    in_specs = [xspec, _resident((1, d)), _resident(w_gate.shape), _resident(w_up.shape),
                _resident(conv_w.shape), _resident((1, f)), _resident(w_down.shape), hspec]
    args = [x, g.reshape(1, d), w_gate, w_up, conv_w, conv_b.reshape(1, f), w_down, hist8]
    if mixer_out is not None:
        in_specs += [pl.BlockSpec((None, tm, mixer_out.shape[-1]), lambda i, t: (i, t, 0)),
                     _resident(w_o.shape)]
        args += [mixer_out, w_o]
    if final_g is not None:
        in_specs.append(_resident((1, d)))
        args.append(final_g.reshape(1, d))
    return pl.pallas_call(
        functools.partial(_ffn_body, final=final_g is not None, mixed=mixer_out is not None),
        grid=(b, s // tm),
        in_specs=in_specs,
        out_specs=[xspec, hspec],
        out_shape=[jax.ShapeDtypeStruct((b, s, d), _F32),
                   jax.ShapeDtypeStruct((b, _SUBLANES, f), _F32)],
        scratch_shapes=[pltpu.VMEM((_SUBLANES, f), _F32), pltpu.VMEM((tm, f), _BF)],
        compiler_params=_params(("arbitrary", "arbitrary")),
        name="conv_ffn",
    )(*args)


def _pad_hist(hist, rows):
    return jnp.pad(hist, ((0, 0), (rows - hist.shape[1], 0), (0, 0)))


def kernel(x_prompt, x_sample, cache_a_k, cache_a_v, cache_b_latent, cache_b_krope, state_c_conv, state_d_pool, state_ffn_conv, norm_mix_g, norm_ffn_g, norm_final_g, a_w_qkv, a_lam, a_subln_g, a_w_o, b_w_down, b_q_norm_g, b_w_uq, b_kv_norm_g, b_w_uk, b_w_uv, b_w_o, c_w_in, c_conv_w, c_w_out, d_w_group, d_scale, ffn_w_gate, ffn_w_up, ffn_conv_w, ffn_conv_b, ffn_w_down):
    depth = norm_mix_g.shape[0]
    n_p, seq, d = x_prompt.shape
    n_s, t_new, _ = x_sample.shape
    past = cache_a_k.shape[2]
    f = ffn_w_gate.shape[-1]
    tm_p, tm_s = min(_ROW_TILE, seq), t_new
    ta = min(_ATT_TILE, seq)
    tabs_p = _rope_tables(jnp.arange(seq, dtype=jnp.int32))
    tabs_s = _rope_tables(past + jnp.arange(t_new, dtype=jnp.int32))
    xp, xs = x_prompt, x_sample
    outs = {k: [] for k in ("ak_p", "av_p", "bl_p", "br_p", "cc_p", "dp_p", "fc_p",
                            "ak_s", "av_s", "bl_s", "br_s", "cc_s", "dp_s", "fc_s")}
    for i in range(depth):
        m, j = i % _N_MIXERS, i // _N_MIXERS
        g_mix = norm_mix_g[i]
        mix_p, mix_s = {}, {}
        if m == 0:
            lam_init = 0.8 - 0.6 * math.exp(-0.3 * i)
            w_qkv, w_o = a_w_qkv[j].astype(_BF), a_w_o[j].astype(_BF)
            qt, kf, km, vf, vt = _diff_proj(xp, g_mix, w_qkv, tabs_p, ta, ta // _SWEEP_HALVES)
            op = _sweep(qt, km, vt, _A_HEADS, _LANES, 2 * _A_HD, halves=_SWEEP_HALVES,
                        trip_ticks=_TRIP_TICKS_DIFF,
                        lam=a_lam[j], subln_g=a_subln_g[j], lam_init=lam_init)
            mix_p = dict(mixer_out=op, w_o=w_o)
            outs["ak_p"].append(kf.reshape(n_p, seq, _A_HEADS, 2, _A_HD))
            outs["av_p"].append(vf.reshape(n_p, seq, _A_HEADS, 2 * _A_HD))
            qm, kf, vf = _diff_proj(xs, g_mix, w_qkv, tabs_s, tm_s)
            osm = _cached_diff(qm, cache_a_k[j].reshape(n_s, past, d), kf,
                               cache_a_v[j].reshape(n_s, past, d), vf, _A_HEADS,
                               a_lam[j], a_subln_g[j], lam_init)
            mix_s = dict(mixer_out=osm, w_o=w_o)
            outs["ak_s"].append(kf.reshape(n_s, t_new, _A_HEADS, 2, _A_HD))
            outs["av_s"].append(vf.reshape(n_s, t_new, _A_HEADS, 2 * _A_HD))
        elif m == 1:
            wts = _mla_weights(b_w_down[j], b_w_uq[j], b_w_uk[j], b_w_uv[j])
            w_o = b_w_o[j].astype(_BF)
            hq = 2 * _LANES
            lat, kr, qt, k2, vt = _mla_proj(xp, g_mix, wts, b_q_norm_g[j], b_kv_norm_g[j], tabs_p,
                                            ta, ta, ta // _SWEEP_HALVES)
            op = _sweep(qt[:, None], k2[:, None], vt, _B_HEADS, hq, _B_VD,
                        halves=_SWEEP_HALVES, trip_ticks=_TRIP_TICKS_MLA)
            mix_p = dict(mixer_out=op, w_o=w_o)
            outs["bl_p"].append(lat)
            outs["br_p"].append(kr)
            lat, kr, q, k2, v = _mla_proj(xs, g_mix, wts, b_q_norm_g[j], b_kv_norm_g[j], tabs_s,
                                          tm_s, 0)
            kin = jnp.concatenate([cache_b_latent[j], cache_b_krope[j],
                                   jnp.zeros((n_s, past, _LANES - _B_ROPE), _F32)], axis=-1).astype(_BF)
            k_cache = _mm(kin, wts[2], out_dtype=_BF)
            v_cache = _mm(cache_b_latent[j].astype(_BF), wts[3], out_dtype=_BF)
            k_all = jnp.concatenate([k_cache, k2[:, 0]], axis=1)
            v_all = jnp.concatenate([v_cache, v], axis=1)
            osm = _cached_mla(q, k_all, v_all, _B_HEADS, hq, _B_VD)
            mix_s = dict(mixer_out=osm, w_o=w_o)
            outs["bl_s"].append(lat)
            outs["br_s"].append(kr)
        elif m == 2:
            w_in, w_out = c_w_in[j].astype(_BF), c_w_out[j].astype(_BF)
            xp, hc = _conv_mix(xp, g_mix, w_in, c_conv_w[j], w_out,
                               jnp.zeros((n_p, _SUBLANES, d), _F32), tm_p)
            outs["cc_p"].append(hc[:, -2:])
            xs, hc = _conv_mix(xs, g_mix, w_in, c_conv_w[j], w_out,
                               _pad_hist(state_c_conv[j], _SUBLANES), tm_s)
            outs["cc_s"].append(hc[:, -2:])
        else:
            w_grp = d_w_group[j].astype(_BF)
            xp, hd = _pool_mix(xp, g_mix, w_grp, d_scale[j],
                               jnp.zeros((n_p, _POOL_HALO, d), _F32), 0, tm_p)
            outs["dp_p"].append(hd[:, -_D_HIST:])
            xs, hd = _pool_mix(xs, g_mix, w_grp, d_scale[j],
                               _pad_hist(state_d_pool[j], _POOL_HALO), _D_HIST, tm_s)
            outs["dp_s"].append(hd[:, -_D_HIST:])
        wg, wu, wd = ffn_w_gate[i].astype(_BF), ffn_w_up[i].astype(_BF), ffn_w_down[i].astype(_BF)
        final_g = norm_final_g if i == depth - 1 else None
        xp, hf = _ffn(xp, norm_ffn_g[i], wg, wu, ffn_conv_w[i], ffn_conv_b[i], wd,
                      jnp.zeros((n_p, _SUBLANES, f), _F32), final_g, tm_p, **mix_p)
        outs["fc_p"].append(hf[:, -2:])
        xs, hf = _ffn(xs, norm_ffn_g[i], wg, wu, ffn_conv_w[i], ffn_conv_b[i], wd,
                      _pad_hist(state_ffn_conv[i], _SUBLANES), final_g, tm_s, **mix_s)
        outs["fc_s"].append(hf[:, -2:])
    st = lambda k: jnp.stack(outs[k])
    return (xp, xs, st("ak_p"), st("av_p"), st("bl_p"), st("br_p"), st("cc_p"), st("dp_p"), st("fc_p"),
            st("ak_s"), st("av_s"), st("bl_s"), st("br_s"), st("cc_s"), st("dp_s"), st("fc_s"))
```

```python
import functools
import math

import jax
import jax.numpy as jnp
from jax import lax
from jax.experimental import pallas as pl
from jax.experimental.pallas import tpu as pltpu

_BF = jnp.bfloat16
_F32 = jnp.float32

_CHUNK = 64
_ROPE_THETA = 10000.0
_NORM_EPS = 1e-6
_NEG_INF = -1e30
_A_HEADS = 8
_A_HD = 64
_A_SUBLN_EPS = 1e-5
_B_HEADS = 8
_B_NOPE = 128
_B_ROPE = 64
_B_VD = 128
_B_Q_LORA = 384
_B_KV_LORA = 256
_D_WINDOWS = (2, 4, 8, 16)
_D_HIST = 15
_N_MIXERS = 4
_LOG2E = math.log2(math.e)

_LANES = 128
_SUBLANES = 8
_MXU_COLS = 256
_BF_ROWS = 16
_SWEEP_SLOTS = 4
_TRIP_TICKS_DIFF = 8
_TRIP_TICKS_MLA = 16
_SWEEP_HALVES = 1
_VMEM_LIMIT = 56 * 1024 * 1024

_ROW_TILE = 512
_ATT_TILE = 512


def _params(sem):
    return pltpu.CompilerParams(dimension_semantics=sem, vmem_limit_bytes=_VMEM_LIMIT)


def _resident(shape):
    nd = len(shape)
    return pl.BlockSpec(shape, lambda *_: (0,) * nd, pipeline_mode=pl.Buffered(1))


def _rms(x, g, eps):
    ms = jnp.mean(x * x, axis=-1, keepdims=True)
    return x * lax.rsqrt(ms + eps) * g


def _rope_lanes(y, cos, sin_lo, sin_hi):
    return (y * cos + pltpu.roll(y, _LANES - 32, axis=1) * sin_lo
            + pltpu.roll(y, 32, axis=1) * sin_hi)


def _rope_tables(pos):
    d = _A_HD
    inv = jnp.power(_ROPE_THETA, -jnp.arange(0, d, 2, dtype=_F32) / d)
    ang = pos.astype(_F32)[:, None] * inv[None, :]
    cos, sin = jnp.cos(ang), jnp.sin(ang)
    zero = jnp.zeros_like(sin)
    cos_t = jnp.tile(cos, (1, 4))
    sin_lo = jnp.tile(jnp.concatenate([-sin, zero], axis=1), (1, 2))
    sin_hi = jnp.tile(jnp.concatenate([zero, sin], axis=1), (1, 2))
    return cos_t, sin_lo, sin_hi


def _shift_rows(carry, cur, k):
    ext = jnp.concatenate([carry, cur], axis=0)
    return pltpu.roll(ext, k, axis=0)[carry.shape[0]:]


def _mask_lanes(rows, row0, tile, base, for_keys):
    r = row0 + lax.broadcasted_iota(jnp.int32, (rows, _LANES), 0)
    chunk = (r % tile) // _CHUNK
    c = lax.broadcasted_iota(jnp.int32, (rows, _LANES), 1) - base
    n = tile // _CHUNK
    if for_keys:
        return jnp.where(c == chunk, 1.0, 0.0)
    return jnp.where(c > chunk, jnp.where(c < n, _NEG_INF, 0.0), 0.0)


def _diff_proj_body(x_ref, g_ref, w_ref, cos_ref, slo_ref, shi_ref, *out_refs, qscale, qt_cols):
    tm, d = x_ref.shape
    if qt_cols:
        qm_ref, kf_ref, km_ref, vf_ref, vt_ref = out_refs
        row0 = pl.program_id(1) * tm
        q_ext = _mask_lanes(tm, row0, tm, _A_HD, False)
        k_ext = _mask_lanes(tm, row0, tm, _A_HD, True)
        ones = jnp.ones((_BF_ROWS, tm), _BF)
    else:
        qm_ref, kf_ref, vf_ref = out_refs
    h = _rms(x_ref[...], g_ref[...], _NORM_EPS).astype(_BF)
    cos, slo, shi = cos_ref[...], slo_ref[...], shi_ref[...]
    first_half = lax.broadcasted_iota(jnp.int32, (tm, _LANES), 1) < _A_HD
    zero = jnp.zeros((tm, _LANES), _F32)
    cw = _MXU_COLS
    for c in range(0, d, cw):
        yq = jnp.dot(h, w_ref[:, c:c + cw], preferred_element_type=_F32)
        yk = jnp.dot(h, w_ref[:, d + c:d + c + cw], preferred_element_type=_F32)
        yv = jnp.dot(h, w_ref[:, 2 * d + c:2 * d + c + cw], preferred_element_type=_F32)
        vf_ref[:, c:c + cw] = yv
        for s in range(0, cw, _LANES):
            cols = slice(c + s, c + s + _LANES)
            head = (c + s) // _LANES
            q = _rope_lanes(yq[:, s:s + _LANES], cos, slo, shi) * qscale
            k = _rope_lanes(yk[:, s:s + _LANES], cos, slo, shi)
            kf_ref[:, cols] = k
            if not qt_cols:
                qm_ref[0, :, cols] = jnp.where(first_half, q, zero).astype(_BF)
                qm_ref[1, :, cols] = jnp.where(first_half, zero, q).astype(_BF)
                continue
            for mp in range(2):
                qh = q if mp == 0 else pltpu.roll(q, _A_HD, axis=1)
                kh = k if mp == 0 else pltpu.roll(k, _A_HD, axis=1)
                qt = jnp.where(first_half, qh, q_ext).T.astype(_BF)
                for g in range(tm // qt_cols):
                    qm_ref[mp, head, g] = qt[:, g * qt_cols:(g + 1) * qt_cols]
                km_ref[mp, 0, :, cols] = jnp.where(first_half, kh, zero).astype(_BF)
                km_ref[mp, 1, :, cols] = jnp.where(first_half, kh, k_ext).astype(_BF)
            vt_ref[head, 0, :_LANES, :] = yv[:, s:s + _LANES].T.astype(_BF)
            vt_ref[head, 0, _LANES:, :] = ones


def _diff_proj(x, g, w_qkv, tabs, tm, qt_cols=0):
    b, s, d = x.shape
    heads = d // _LANES
    row = lambda dt: jax.ShapeDtypeStruct((b, s, d), dt)
    xspec = pl.BlockSpec((None, tm, d), lambda i, t: (i, t, 0))
    tspec = pl.BlockSpec((tm, _LANES), lambda i, t: (t, 0))
    if qt_cols:
        assert tm % qt_cols == 0
        dve = _LANES + _BF_ROWS
        out_specs = [pl.BlockSpec((None, 2, heads, tm // qt_cols, _LANES, qt_cols),
                                  lambda i, t: (i, 0, 0, t, 0, 0)),
                     xspec, pl.BlockSpec((None, 2, 2, tm, d), lambda i, t: (i, 0, 0, t, 0)), xspec,
                     pl.BlockSpec((None, heads, 1, dve, tm), lambda i, t: (i, 0, t, 0, 0))]
        out_shape = [jax.ShapeDtypeStruct((b, 2, heads, s // qt_cols, _LANES, qt_cols), _BF),
                     row(_F32), jax.ShapeDtypeStruct((b, 2, 2, s, d), _BF), row(_F32),
                     jax.ShapeDtypeStruct((b, heads, s // tm, dve, tm), _BF)]
    else:
        out_specs = [pl.BlockSpec((None, 2, tm, d), lambda i, t: (i, 0, t, 0)), xspec, xspec]
        out_shape = [jax.ShapeDtypeStruct((b, 2, s, d), _BF), row(_F32), row(_F32)]
    return pl.pallas_call(
        functools.partial(_diff_proj_body, qscale=_LOG2E / math.sqrt(_A_HD), qt_cols=qt_cols),
        grid=(b, s // tm),
        in_specs=[xspec, _resident((1, d)), _resident(w_qkv.shape), tspec, tspec, tspec],
        out_specs=out_specs,
        out_shape=out_shape,
        compiler_params=_params(("parallel", "parallel")),
        name="diff_proj",
    )(x, g.reshape(1, d), w_qkv, *tabs)


def _cached_diff_body(q_ref, kc_ref, kn_ref, vc_ref, vn_ref, lam_ref, sg_ref, o_ref, *, lam_init):
    nt = (((1,), (1,)), ((), ()))
    lp = lam_ref[...]
    lam = (jnp.exp(jnp.sum(lp[0:1] * lp[1:2], axis=-1, keepdims=True))
           - jnp.exp(jnp.sum(lp[2:3] * lp[3:4], axis=-1, keepdims=True)) + lam_init)
    for hd in range(o_ref.shape[-1] // _LANES):
        cols = slice(hd * _LANES, (hd + 1) * _LANES)
        kc, kn = kc_ref[:, cols].astype(_BF), kn_ref[:, cols].astype(_BF)
        vc, vn = vc_ref[:, cols].astype(_BF), vn_ref[:, cols].astype(_BF)
        outs = []
        for c in range(2):
            q = q_ref[c, :, cols]
            sc = lax.dot_general(q, kc, nt, preferred_element_type=_F32)
            sn = lax.dot_general(q, kn, nt, preferred_element_type=_F32)
            m = jnp.maximum(jnp.max(sc, axis=-1, keepdims=True), jnp.max(sn, axis=-1, keepdims=True))
            pc, pn = jnp.exp2(sc - m), jnp.exp2(sn - m)
            l = jnp.sum(pc, axis=-1, keepdims=True) + jnp.sum(pn, axis=-1, keepdims=True)
            acc = (jnp.dot(pc.astype(_BF), vc, preferred_element_type=_F32)
                   + jnp.dot(pn.astype(_BF), vn, preferred_element_type=_F32))
            outs.append(acc / l)
        o = _rms(outs[0] - lam * outs[1], sg_ref[...], _A_SUBLN_EPS) * (1.0 - lam_init)
        o_ref[:, cols] = o.astype(o_ref.dtype)


def _cached_diff(qm, k_cache, k_new, v_cache, v_new, lam, subln_g, lam_init):
    b, _, t, d = qm.shape
    past = k_cache.shape[1]
    blk = lambda rows: pl.BlockSpec((None, rows, d), lambda i: (i, 0, 0))
    return pl.pallas_call(
        functools.partial(_cached_diff_body, lam_init=lam_init),
        grid=(b,),
        in_specs=[pl.BlockSpec((None, 2, t, d), lambda i: (i, 0, 0, 0)),
                  blk(past), blk(t), blk(past), blk(t), _resident(lam.shape), _resident((1, _LANES))],
        out_specs=blk(t),
        out_shape=jax.ShapeDtypeStruct((b, t, d), _BF),
        compiler_params=_params(("parallel",)),
        name="cached_diff",
    )(qm, k_cache, k_new, v_cache, v_new, lam, subln_g.reshape(1, _LANES))


def _cached_mla_body(q_ref, k_ref, v_ref, o_ref):
    s = lax.dot_general(q_ref[...], k_ref[...], (((1,), (1,)), ((), ())),
                        preferred_element_type=_F32)
    p = jnp.exp2(s - jnp.max(s, axis=-1, keepdims=True))
    acc = jnp.dot(p.astype(_BF), v_ref[...], preferred_element_type=_F32)
    o_ref[...] = (acc / jnp.sum(p, axis=-1, keepdims=True)).astype(o_ref.dtype)


def _cached_mla(q, k, v, heads, dq, dv):
    b, t, _ = q.shape
    sk = k.shape[1]
    return pl.pallas_call(
        _cached_mla_body,
        grid=(b, heads),
        in_specs=[pl.BlockSpec((None, t, dq), lambda i, h: (i, 0, h)),
                  pl.BlockSpec((None, sk, dq), lambda i, h: (i, 0, h)),
                  pl.BlockSpec((None, sk, dv), lambda i, h: (i, 0, h))],
        out_specs=pl.BlockSpec((None, t, dv), lambda i, h: (i, 0, h)),
        out_shape=jax.ShapeDtypeStruct((b, t, heads * dv), _BF),
        compiler_params=_params(("parallel", "parallel")),
        name="cached_mla",
    )(q, k, v)


def _sweep_body(*refs, nmaps, halves, tile, nb, lam_init):
    if nmaps == 2:
        q_ref, k_ref, vt_ref, lam_ref, sg_ref, o_ref, s_buf, mb_buf, m_sc, acc_sc = refs
    else:
        q_ref, k_ref, vt_ref, o_ref, s_buf, mb_buf, m_sc, acc_sc = refs
    hw = tile // halves
    nch = nmaps * halves
    dv = o_ref.shape[-1]
    ticks = acc_sc.shape[0]
    npairs = nb * (nb + 1) // 2
    acc_sc[...] = jnp.zeros(acc_sc.shape, _F32)

    def nxt(pair):
        qi, t = pair
        last = t == qi
        return jnp.where(last, qi + 1, qi), jnp.where(last, 0, t + 1)

    def scores(pair, slot):
        qi, t = pair
        variant = (t == qi).astype(jnp.int32)
        for mp in range(nmaps):
            k = k_ref[mp, variant, pl.ds(pl.multiple_of(t * tile, tile), tile), :]
            for hf in range(halves):
                c = mp * halves + hf
                s = jnp.dot(k, q_ref[mp, qi * halves + hf], preferred_element_type=_F32)
                s_buf[slot, c] = s
                mb_buf[slot, c] = jnp.max(s, axis=0, keepdims=True)

    def values(pair, pos):
        _, t = pair
        first = t == 0
        slot = pos % _SWEEP_SLOTS
        vt = vt_ref[t]
        for c in range(nch):
            m_prev = jnp.where(first, _NEG_INF, m_sc[c])
            m_new = jnp.maximum(m_prev, mb_buf[slot, c])
            alpha = jnp.exp2(m_prev - m_new)
            m_sc[c] = m_new
            p = jnp.exp2(s_buf[slot, c] - m_new).astype(_BF)
            acc_sc[pos, c] = (alpha * acc_sc[(pos - 1) % ticks, c]
                              + jnp.dot(vt, p, preferred_element_type=_F32))

    def finish(pair, pos):
        qi, t = pair

        @pl.when(t == qi)
        def _():
            if nmaps == 2:
                lp = lam_ref[...]
                lam = (jnp.exp(jnp.sum(lp[0:1] * lp[1:2], axis=-1, keepdims=True))
                       - jnp.exp(jnp.sum(lp[2:3] * lp[3:4], axis=-1, keepdims=True)) + lam_init)
            normed = lambda c: acc_sc[pos, c, :dv, :] / acc_sc[pos, c, dv:dv + 1, :]
            for hf in range(halves):
                if nmaps == 2:
                    ot = normed(hf) - lam * normed(halves + hf)
                    o = _rms(ot.T, sg_ref[...], _A_SUBLN_EPS) * (1.0 - lam_init)
                else:
                    o = normed(hf).T
                row = pl.multiple_of(qi * tile + hf * hw, hw)
                o_ref[pl.ds(row, hw), :] = o.astype(o_ref.dtype)

    zero = jnp.int32(0)
    pair0 = (zero, zero)
    scores(pair0, 0)
    if npairs == 1:
        values(pair0, 0)
        finish(pair0, 0)
        return
    pair1 = nxt(pair0)
    scores(pair1, 1)

    def tick(carry, r):
        pc, pb, pa = carry
        scores(pa, (2 + r) % _SWEEP_SLOTS)
        values(pc, r)
        return pb, pa, nxt(pa)

    def run_ticks(n, carry):
        done = []
        for r in range(n):
            done.append(carry[0])
            carry = tick(carry, r)
        for r, pair in enumerate(done):
            finish(pair, r)
        return carry

    nticks = npairs - 2
    carry = lax.fori_loop(0, nticks // ticks, lambda _, c: run_ticks(ticks, c),
                          (pair0, pair1, nxt(pair1)))
    carry = run_ticks(nticks % ticks, carry)
    before_last, last, _ = carry
    for pair, pos in ((before_last, (npairs - 2) % ticks), (last, (npairs - 1) % ticks)):
        values(pair, pos)
        finish(pair, pos)


def _sweep(q, k, vt, heads, dq, dv, *, halves, trip_ticks, lam=None, subln_g=None, lam_init=0.0):
    b, nmaps = q.shape[:2]
    _, _, nb, dve, tile = vt.shape
    s = nb * tile
    assert tile % _CHUNK == 0 and tile % halves == 0 and trip_ticks % _SWEEP_SLOTS == 0
    assert q.shape[2:] == (heads, s * halves // tile, dq, tile // halves) and dve == dv + _BF_ROWS
    nch = nmaps * halves
    hw = tile // halves
    once = pl.Buffered(1)
    in_specs = [pl.BlockSpec((None, nmaps, None, nb * halves, dq, hw),
                             lambda i, h: (i, 0, h, 0, 0, 0), pipeline_mode=once),
                pl.BlockSpec((None, nmaps, 2, s, dq), lambda i, h: (i, 0, 0, 0, h),
                             pipeline_mode=once),
                pl.BlockSpec((None, None, nb, dve, tile), lambda i, h: (i, h, 0, 0, 0),
                             pipeline_mode=once)]
    args = [q, k, vt]
    if nmaps == 2:
        in_specs += [_resident(lam.shape), _resident((1, dv))]
        args += [lam, subln_g.reshape(1, dv)]
    scratch = [pltpu.VMEM((_SWEEP_SLOTS, nch, tile, hw), _F32),
               pltpu.VMEM((_SWEEP_SLOTS, nch, 1, hw), _F32),
               pltpu.VMEM((nch, 1, hw), _F32), pltpu.VMEM((trip_ticks, nch, dve, hw), _F32)]
    return pl.pallas_call(
        functools.partial(_sweep_body, nmaps=nmaps, halves=halves, tile=tile, nb=nb,
                          lam_init=lam_init),
        grid=(b, heads),
        in_specs=in_specs,
        out_specs=pl.BlockSpec((None, s, dv), lambda i, h: (i, 0, h)),
        out_shape=jax.ShapeDtypeStruct((b, s, heads * dv), _BF),
        scratch_shapes=scratch,
        compiler_params=_params(("parallel", "parallel")),
        name="sweep_diff" if nmaps == 2 else "sweep_mla",
    )(*args)


def _mm_body(*refs, has_res):
    if has_res:
        a_ref, w_ref, r_ref, o_ref = refs
    else:
        a_ref, w_ref, o_ref = refs
    y = jnp.dot(a_ref[...].astype(_BF), w_ref[...], preferred_element_type=_F32)
    if has_res:
        y = r_ref[...] + y
    o_ref[...] = y.astype(o_ref.dtype)


def _mm(a, w, res=None, out_dtype=_F32):
    b, s, kdim = a.shape
    n = w.shape[1]
    tm = min(_ROW_TILE, s)
    in_specs = [pl.BlockSpec((None, tm, kdim), lambda i, t: (i, t, 0)), _resident(w.shape)]
    args = [a, w]
    if res is not None:
        in_specs.append(pl.BlockSpec((None, tm, n), lambda i, t: (i, t, 0)))
        args.append(res)
    return pl.pallas_call(
        functools.partial(_mm_body, has_res=res is not None),
        grid=(b, s // tm),
        in_specs=in_specs,
        out_specs=pl.BlockSpec((None, tm, n), lambda i, t: (i, t, 0)),
        out_shape=jax.ShapeDtypeStruct((b, s, n), out_dtype),
        compiler_params=_params(("parallel", "parallel")),
        name="matmul_residual" if res is not None else "matmul",
    )(*args)


def _mla_proj_body(x_ref, g_ref, wd_ref, qg_ref, wq_ref, kvg_ref, wk_ref, wv_ref,
                   cos_ref, slo_ref, shi_ref, lat_ref, kr_ref, q_ref, k_ref, v_ref, *, qscale, mask_tile,
                   qt_cols):
    tm = x_ref.shape[0]
    h = _rms(x_ref[...], g_ref[...], _NORM_EPS).astype(_BF)
    cos, slo, shi = cos_ref[...], slo_ref[...], shi_ref[...]
    if mask_tile:
        row0 = pl.program_id(1) * tm
        q_ext = _mask_lanes(tm, row0, mask_tile, _B_ROPE, False)
        k_ext = _mask_lanes(tm, row0, mask_tile, _B_ROPE, True)
    else:
        q_ext = k_ext = jnp.zeros((tm, _LANES), _F32)
    dn = jnp.dot(h, wd_ref[...], preferred_element_type=_F32)
    e0, e1 = _B_Q_LORA, _B_Q_LORA + _B_KV_LORA
    cq = _rms(dn[:, :e0], qg_ref[...], _NORM_EPS).astype(_BF)
    lat = _rms(dn[:, e0:e1], kvg_ref[...], _NORM_EPS)
    kr = _rope_lanes(dn[:, e1:e1 + _LANES], cos, slo, shi)
    lat_ref[...] = lat
    kr_ref[...] = kr[:, :_B_ROPE]
    hw = 2 * _LANES
    for hd in range(_B_HEADS):
        yq = jnp.dot(cq, wq_ref[:, hd * hw:(hd + 1) * hw], preferred_element_type=_F32)
        q_nope = yq[:, :_LANES] * qscale
        q_rope = _rope_lanes(yq[:, _LANES:], cos, slo, shi) * qscale + q_ext
        if qt_cols:
            qt = jnp.concatenate([q_nope.T, q_rope.T], axis=0).astype(_BF)
            for g in range(tm // qt_cols):
                q_ref[hd, g] = qt[:, g * qt_cols:(g + 1) * qt_cols]
        else:
            q_ref[:, hd * hw:hd * hw + _LANES] = q_nope.astype(_BF)
            q_ref[:, hd * hw + _LANES:(hd + 1) * hw] = q_rope.astype(_BF)
    kin = jnp.concatenate([lat, kr], axis=1).astype(_BF)
    for hd in range(_B_HEADS):
        yk = jnp.dot(kin, wk_ref[:, hd * hw:(hd + 1) * hw], preferred_element_type=_F32)
        for var in range(2):
            k_ref[var, :, hd * hw:hd * hw + _LANES] = yk[:, :_LANES].astype(_BF)
        k_ref[0, :, hd * hw + _LANES:(hd + 1) * hw] = yk[:, _LANES:].astype(_BF)
        k_ref[1, :, hd * hw + _LANES:(hd + 1) * hw] = (yk[:, _LANES:] + k_ext).astype(_BF)
    latb = lat.astype(_BF)
    for c in range(0, wv_ref.shape[-1], _MXU_COLS):
        yv = jnp.dot(latb, wv_ref[:, c:c + _MXU_COLS], preferred_element_type=_F32)
        if qt_cols:
            for s in range(0, _MXU_COLS, _B_VD):
                v_ref[(c + s) // _B_VD, 0, :_B_VD, :] = yv[:, s:s + _B_VD].T.astype(_BF)
                v_ref[(c + s) // _B_VD, 0, _B_VD:, :] = jnp.ones((_BF_ROWS, tm), _BF)
        else:
            v_ref[:, c:c + _MXU_COLS] = yv.astype(_BF)


def _mla_weights(w_down, w_uq, w_uk, w_uv):
    d = w_down.shape[0]
    hw = 2 * _LANES
    wd = jnp.concatenate([w_down, jnp.zeros((d, _LANES - _B_ROPE), w_down.dtype)], axis=1)
    wq = w_uq.reshape(_B_Q_LORA, _B_HEADS, _B_NOPE + _B_ROPE)
    wq = jnp.pad(wq, ((0, 0), (0, 0), (0, hw - _B_NOPE - _B_ROPE))).reshape(_B_Q_LORA, _B_HEADS * hw)
    eye = jnp.eye(_B_ROPE, dtype=w_uk.dtype)
    wk_top = jnp.pad(w_uk, ((0, 0), (0, 0), (0, hw - _B_NOPE)))
    wk_mid = jnp.broadcast_to(jnp.pad(eye, ((0, 0), (_B_NOPE, hw - _B_NOPE - _B_ROPE)))[:, None, :],
                              (_B_ROPE, _B_HEADS, hw))
    wk_bot = jnp.zeros((_LANES - _B_ROPE, _B_HEADS, hw), w_uk.dtype)
    wk = jnp.concatenate([wk_top, wk_mid, wk_bot], axis=0).reshape(-1, _B_HEADS * hw)
    wv = w_uv.reshape(_B_KV_LORA, _B_HEADS * _B_VD)
    return wd.astype(_BF), wq.astype(_BF), wk.astype(_BF), wv.astype(_BF)


def _mla_proj(x, g, wts, q_norm_g, kv_norm_g, tabs, tm, mask_tile, qt_cols=0):
    b, s, d = x.shape
    wd, wq, wk, wv = wts
    xspec = pl.BlockSpec((None, tm, d), lambda i, t: (i, t, 0))
    tspec = pl.BlockSpec((tm, _LANES), lambda i, t: (t, 0))
    ospec = lambda n: pl.BlockSpec((None, tm, n), lambda i, t: (i, t, 0))
    oshape = lambda n, dt: jax.ShapeDtypeStruct((b, s, n), dt)
    if qt_cols:
        assert tm % qt_cols == 0
        hq = wq.shape[1] // _B_HEADS
        q_spec = pl.BlockSpec((None, _B_HEADS, tm // qt_cols, hq, qt_cols),
                              lambda i, t: (i, 0, t, 0, 0))
        q_shape = jax.ShapeDtypeStruct((b, _B_HEADS, s // qt_cols, hq, qt_cols), _BF)
        dve = _B_VD + _BF_ROWS
        v_spec = pl.BlockSpec((None, _B_HEADS, 1, dve, tm), lambda i, t: (i, 0, t, 0, 0))
        v_shape = jax.ShapeDtypeStruct((b, _B_HEADS, s // tm, dve, tm), _BF)
    else:
        q_spec, q_shape = ospec(wq.shape[1]), oshape(wq.shape[1], _BF)
        v_spec, v_shape = ospec(wv.shape[1]), oshape(wv.shape[1], _BF)
    return pl.pallas_call(
        functools.partial(_mla_proj_body, qscale=_LOG2E / math.sqrt(_B_NOPE + _B_ROPE),
                          mask_tile=mask_tile, qt_cols=qt_cols),
        grid=(b, s // tm),
        in_specs=[xspec, _resident((1, d)), _resident(wd.shape), _resident((1, _B_Q_LORA)),
                  _resident(wq.shape), _resident((1, _B_KV_LORA)), _resident(wk.shape),
                  _resident(wv.shape), tspec, tspec, tspec],
        out_specs=[ospec(_B_KV_LORA), ospec(_B_ROPE), q_spec,
                   pl.BlockSpec((None, 2, tm, wk.shape[1]), lambda i, t: (i, 0, t, 0)), v_spec],
        out_shape=[oshape(_B_KV_LORA, _F32), oshape(_B_ROPE, _F32), q_shape,
                   jax.ShapeDtypeStruct((b, 2, s, wk.shape[1]), _BF), v_shape],
        compiler_params=_params(("parallel", "parallel")),
        name="mla_proj",
    )(x, g.reshape(1, d), wd, q_norm_g.reshape(1, -1), wq, kv_norm_g.reshape(1, -1), wk, wv, *tabs)


def _delayed(carry_ref, cols, cur, shifts):
    nseg = carry_ref.shape[0]
    seg = cur.shape[0] // nseg
    outs = [[] for _ in shifts]
    for sg in range(nseg):
        part = cur[sg * seg:(sg + 1) * seg]
        prev = carry_ref[sg, :, cols]
        for o, k in zip(outs, shifts):
            o.append(_shift_rows(prev, part, k))
        carry_ref[sg, :, cols] = part[seg - _SUBLANES:]
    return [o[0] if nseg == 1 else jnp.concatenate(o, axis=0) for o in outs]


def _conv_mix_body(x_ref, g_ref, win_ref, cw_ref, wout_ref, hist_ref, o_ref, hout_ref,
                   carry_sc, z_sc):
    tm, d = x_ref.shape

    @pl.when(pl.program_id(1) == 0)
    def _():
        carry_sc[...] = hist_ref[...]

    x = x_ref[...]
    h = _rms(x, g_ref[...], _NORM_EPS).astype(_BF)
    cw = _MXU_COLS
    for c in range(0, d, cw):
        gate_b = jnp.dot(h, win_ref[:, c:c + cw], preferred_element_type=_F32)
        gate_c = jnp.dot(h, win_ref[:, d + c:d + c + cw], preferred_element_type=_F32)
        val = jnp.dot(h, win_ref[:, 2 * d + c:2 * d + c + cw], preferred_element_type=_F32)
        u = gate_c * val
        u2, u1 = _delayed(carry_sc, slice(c, c + cw), u, (2, 1))
        y = u2 * cw_ref[0:1, c:c + cw] + u1 * cw_ref[1:2, c:c + cw] + u * cw_ref[2:3, c:c + cw]
        z_sc[:, c:c + cw] = (gate_b * y).astype(_BF)
    hout_ref[...] = carry_sc[...]
    o_ref[...] = x + jnp.dot(z_sc[...], wout_ref[...], preferred_element_type=_F32)


def _conv_mix(x, g, w_in, conv_w, w_out, hist, tm):
    b, s, d = x.shape
    nseg = hist.shape[1]
    assert nseg == 1 or tm == s
    xspec = pl.BlockSpec((None, tm, d), lambda i, t: (i, t, 0))
    hspec = pl.BlockSpec((None, nseg, _SUBLANES, d), lambda i, t: (i, 0, 0, 0))
    return pl.pallas_call(
        _conv_mix_body,
        grid=(b, s // tm),
        in_specs=[xspec, _resident((1, d)), _resident(w_in.shape), _resident(conv_w.shape),
                  _resident(w_out.shape), hspec],
        out_specs=[xspec, hspec],
        out_shape=[jax.ShapeDtypeStruct((b, s, d), _F32),
                   jax.ShapeDtypeStruct((b, nseg, _SUBLANES, d), _F32)],
        scratch_shapes=[pltpu.VMEM((nseg, _SUBLANES, d), _F32), pltpu.VMEM((tm, d), _BF)],
        compiler_params=_params(("arbitrary", "arbitrary")),
        name="conv_mixer",
    )(x, g.reshape(1, d), w_in, conv_w, w_out, hist)


_POOL_HALO = 16


def _pool_mix_body(x_ref, g_ref, wg_ref, sc_ref, hist_ref, o_ref, hout_ref, carry_sc, *, hist_valid):
    tm, d = x_ref.shape
    t = pl.program_id(1)

    @pl.when(t == 0)
    def _():
        carry_sc[...] = hist_ref[...]

    x = x_ref[...]
    h = _rms(x, g_ref[...], _NORM_EPS)
    gw = d // len(_D_WINDOWS)
    tpos = t * tm + lax.broadcasted_iota(jnp.int32, (tm, gw), 0)
    outs = []
    for gi, w in enumerate(_D_WINDOWS):
        hg = h[:, gi * gw:(gi + 1) * gw]
        acc = jnp.concatenate([carry_sc[:, gi * gw:(gi + 1) * gw], hg], axis=0)
        k = 1
        while k < w:
            acc = acc + pltpu.roll(acc, k, axis=0)
            k *= 2
        cnt = jnp.minimum(tpos + (hist_valid + 1), w).astype(_F32)
        pooled = acc[_POOL_HALO:] / cnt
        outs.append(jnp.dot((pooled - hg).astype(_BF), wg_ref[gi], preferred_element_type=_F32))
    o_ref[...] = x + jnp.concatenate(outs, axis=1) * sc_ref[...]
    tail = h[tm - _POOL_HALO:, :]
    carry_sc[...] = tail
    hout_ref[...] = tail


def _pool_mix(x, g, w_group, scale, hist16, hist_valid, tm):
    b, s, d = x.shape
    assert all(w & (w - 1) == 0 and w <= _POOL_HALO for w in _D_WINDOWS)
    xspec = pl.BlockSpec((None, tm, d), lambda i, t: (i, t, 0))
    hspec = pl.BlockSpec((None, _POOL_HALO, d), lambda i, t: (i, 0, 0))
    return pl.pallas_call(
        functools.partial(_pool_mix_body, hist_valid=hist_valid),
        grid=(b, s // tm),
        in_specs=[xspec, _resident((1, d)), _resident(w_group.shape), _resident((1, d)), hspec],
        out_specs=[xspec, hspec],
        out_shape=[jax.ShapeDtypeStruct((b, s, d), _F32),
                   jax.ShapeDtypeStruct((b, _POOL_HALO, d), _F32)],
        scratch_shapes=[pltpu.VMEM((_POOL_HALO, d), _F32)],
        compiler_params=_params(("arbitrary", "arbitrary")),
        name="pool_mixer",
    )(x, g.reshape(1, d), w_group, scale.reshape(1, d), hist16)


def _ffn_body(*refs, final, mixed):
    refs = list(refs)
    x_ref, g_ref, wg_ref, wu_ref, cw_ref, cb_ref, wd_ref, hist_ref = refs[:8]
    del refs[:8]
    a_ref, wo_ref = (refs.pop(0), refs.pop(0)) if mixed else (None, None)
    fg_ref = refs.pop(0) if final else None
    o_ref, hout_ref, carry_sc, act_sc = refs
    tm = x_ref.shape[0]
    f = wg_ref.shape[1]

    @pl.when(pl.program_id(1) == 0)
    def _():
        carry_sc[...] = hist_ref[...]

    x = x_ref[...]
    if mixed:
        x = x + jnp.dot(a_ref[...], wo_ref[...], preferred_element_type=_F32)
    h = _rms(x, g_ref[...], _NORM_EPS).astype(_BF)
    cw = _MXU_COLS
    for c in range(0, f, cw):
        gate = jnp.dot(h, wg_ref[:, c:c + cw], preferred_element_type=_F32)
        up = jnp.dot(h, wu_ref[:, c:c + cw], preferred_element_type=_F32)
        g2, g1 = _delayed(carry_sc, slice(c, c + cw), gate, (2, 1))
        y = (g2 * cw_ref[0:1, c:c + cw] + g1 * cw_ref[1:2, c:c + cw]
             + gate * cw_ref[2:3, c:c + cw] + cb_ref[:, c:c + cw])
        act_sc[:, c:c + cw] = (y / (1.0 + jnp.exp(-y)) * up).astype(_BF)
    hout_ref[...] = carry_sc[...]
    out = x + jnp.dot(act_sc[...], wd_ref[...], preferred_element_type=_F32)
    if final:
        out = _rms(out, fg_ref[...], _NORM_EPS)
    o_ref[...] = out


def _ffn(x, g, w_gate, w_up, conv_w, conv_b, w_down, hist, final_g, tm, mixer_out=None, w_o=None):
    b, s, d = x.shape
    f = w_gate.shape[1]
    nseg = hist.shape[1]
    assert f % _MXU_COLS == 0 and (nseg == 1 or tm == s)
    xspec = pl.BlockSpec((None, tm, d), lambda i, t: (i, t, 0))
    hspec = pl.BlockSpec((None, nseg, _SUBLANES, f), lambda i, t: (i, 0, 0, 0))
    in_specs = [xspec, _resident((1, d)), _resident(w_gate.shape), _resident(w_up.shape),
                _resident(conv_w.shape), _resident((1, f)), _resident(w_down.shape), hspec]
    args = [x, g.reshape(1, d), w_gate, w_up, conv_w, conv_b.reshape(1, f), w_down, hist]
    if mixer_out is not None:
        in_specs += [pl.BlockSpec((None, tm, mixer_out.shape[-1]), lambda i, t: (i, t, 0)),
                     _resident(w_o.shape)]
        args += [mixer_out, w_o]
    if final_g is not None:
        in_specs.append(_resident((1, d)))
        args.append(final_g.reshape(1, d))
    return pl.pallas_call(
        functools.partial(_ffn_body, final=final_g is not None, mixed=mixer_out is not None),
        grid=(b, s // tm),
        in_specs=in_specs,
        out_specs=[xspec, hspec],
        out_shape=[jax.ShapeDtypeStruct((b, s, d), _F32),
                   jax.ShapeDtypeStruct((b, nseg, _SUBLANES, f), _F32)],
        scratch_shapes=[pltpu.VMEM((nseg, _SUBLANES, f), _F32), pltpu.VMEM((tm, f), _BF)],
        compiler_params=_params(("arbitrary", "arbitrary")),
        name="conv_ffn",
    )(*args)


def _pad_hist(hist, rows):
    return jnp.pad(hist, ((0, 0), (rows - hist.shape[1], 0), (0, 0)))


def kernel(x_prompt, x_sample, cache_a_k, cache_a_v, cache_b_latent, cache_b_krope, state_c_conv, state_d_pool, state_ffn_conv, norm_mix_g, norm_ffn_g, norm_final_g, a_w_qkv, a_lam, a_subln_g, a_w_o, b_w_down, b_q_norm_g, b_w_uq, b_kv_norm_g, b_w_uk, b_w_uv, b_w_o, c_w_in, c_conv_w, c_w_out, d_w_group, d_scale, ffn_w_gate, ffn_w_up, ffn_conv_w, ffn_conv_b, ffn_w_down):
    depth = norm_mix_g.shape[0]
    n_p, seq, d = x_prompt.shape
    n_s, t_new, _ = x_sample.shape
    past = cache_a_k.shape[2]
    f = ffn_w_gate.shape[-1]
    tm_p, tm_s = min(_ROW_TILE, seq), t_new
    ta = min(_ATT_TILE, seq)
    tabs_p = _rope_tables(jnp.arange(seq, dtype=jnp.int32))
    tabs_s = _rope_tables(past + jnp.arange(t_new, dtype=jnp.int32))
    xp, xs = x_prompt, x_sample
    flat = lambda a: a.reshape(1, n_s * t_new, a.shape[-1])
    outs = {k: [] for k in ("ak_p", "av_p", "bl_p", "br_p", "cc_p", "dp_p", "fc_p",
                            "ak_s", "av_s", "bl_s", "br_s", "cc_s", "dp_s", "fc_s")}
    for i in range(depth):
        m, j = i % _N_MIXERS, i // _N_MIXERS
        g_mix = norm_mix_g[i]
        mix_p, mix_s = {}, {}
        if m == 0:
            lam_init = 0.8 - 0.6 * math.exp(-0.3 * i)
            w_qkv, w_o = a_w_qkv[j].astype(_BF), a_w_o[j].astype(_BF)
            qt, kf, km, vf, vt = _diff_proj(xp, g_mix, w_qkv, tabs_p, ta, ta // _SWEEP_HALVES)
            op = _sweep(qt, km, vt, _A_HEADS, _LANES, 2 * _A_HD, halves=_SWEEP_HALVES,
                        trip_ticks=_TRIP_TICKS_DIFF,
                        lam=a_lam[j], subln_g=a_subln_g[j], lam_init=lam_init)
            mix_p = dict(mixer_out=op, w_o=w_o)
            outs["ak_p"].append(kf.reshape(n_p, seq, _A_HEADS, 2, _A_HD))
            outs["av_p"].append(vf.reshape(n_p, seq, _A_HEADS, 2 * _A_HD))
            qm, kf, vf = _diff_proj(xs, g_mix, w_qkv, tabs_s, tm_s)
            osm = _cached_diff(qm, cache_a_k[j].reshape(n_s, past, d), kf,
                               cache_a_v[j].reshape(n_s, past, d), vf,
                               a_lam[j], a_subln_g[j], lam_init)
            mix_s = dict(mixer_out=flat(osm), w_o=w_o)
            outs["ak_s"].append(kf.reshape(n_s, t_new, _A_HEADS, 2, _A_HD))
            outs["av_s"].append(vf.reshape(n_s, t_new, _A_HEADS, 2 * _A_HD))
        elif m == 1:
            wts = _mla_weights(b_w_down[j], b_w_uq[j], b_w_uk[j], b_w_uv[j])
            w_o = b_w_o[j].astype(_BF)
            hq = 2 * _LANES
            lat, kr, qt, k2, vt = _mla_proj(xp, g_mix, wts, b_q_norm_g[j], b_kv_norm_g[j], tabs_p,
                                            ta, ta, ta // _SWEEP_HALVES)
            op = _sweep(qt[:, None], k2[:, None], vt, _B_HEADS, hq, _B_VD,
                        halves=_SWEEP_HALVES, trip_ticks=_TRIP_TICKS_MLA)
            mix_p = dict(mixer_out=op, w_o=w_o)
            outs["bl_p"].append(lat)
            outs["br_p"].append(kr)
            lat, kr, q, k2, v = _mla_proj(xs, g_mix, wts, b_q_norm_g[j], b_kv_norm_g[j], tabs_s,
                                          tm_s, 0)
            kin = jnp.concatenate([cache_b_latent[j], cache_b_krope[j],
                                   jnp.zeros((n_s, past, _LANES - _B_ROPE), _F32)], axis=-1).astype(_BF)
            k_cache = _mm(kin, wts[2], out_dtype=_BF)
            v_cache = _mm(cache_b_latent[j].astype(_BF), wts[3], out_dtype=_BF)
            k_all = jnp.concatenate([k_cache, k2[:, 0]], axis=1)
            v_all = jnp.concatenate([v_cache, v], axis=1)
            osm = _cached_mla(q, k_all, v_all, _B_HEADS, hq, _B_VD)
            mix_s = dict(mixer_out=flat(osm), w_o=w_o)
            outs["bl_s"].append(lat)
            outs["br_s"].append(kr)
        elif m == 2:
            w_in, w_out = c_w_in[j].astype(_BF), c_w_out[j].astype(_BF)
            xp, hc = _conv_mix(xp, g_mix, w_in, c_conv_w[j], w_out,
                               jnp.zeros((n_p, 1, _SUBLANES, d), _F32), tm_p)
            outs["cc_p"].append(hc[:, 0, -2:])
            xs, hc = _conv_mix(flat(xs), g_mix, w_in, c_conv_w[j], w_out,
                               _pad_hist(state_c_conv[j], _SUBLANES)[None], n_s * t_new)
            xs = xs.reshape(n_s, t_new, d)
            outs["cc_s"].append(hc[0, :, -2:])
        else:
            w_grp = d_w_group[j].astype(_BF)
            xp, hd = _pool_mix(xp, g_mix, w_grp, d_scale[j],
                               jnp.zeros((n_p, _POOL_HALO, d), _F32), 0, tm_p)
            outs["dp_p"].append(hd[:, -_D_HIST:])
            xs, hd = _pool_mix(xs, g_mix, w_grp, d_scale[j],
                               _pad_hist(state_d_pool[j], _POOL_HALO), _D_HIST, tm_s)
            outs["dp_s"].append(hd[:, -_D_HIST:])
        wg, wu, wd = ffn_w_gate[i].astype(_BF), ffn_w_up[i].astype(_BF), ffn_w_down[i].astype(_BF)
        final_g = norm_final_g if i == depth - 1 else None
        xp, hf = _ffn(xp, norm_ffn_g[i], wg, wu, ffn_conv_w[i], ffn_conv_b[i], wd,
                      jnp.zeros((n_p, 1, _SUBLANES, f), _F32), final_g, tm_p, **mix_p)
        outs["fc_p"].append(hf[:, 0, -2:])
        xs, hf = _ffn(flat(xs), norm_ffn_g[i], wg, wu, ffn_conv_w[i], ffn_conv_b[i], wd,
                      _pad_hist(state_ffn_conv[i], _SUBLANES)[None], final_g, n_s * t_new, **mix_s)
        xs = xs.reshape(n_s, t_new, d)
        outs["fc_s"].append(hf[0, :, -2:])
    st = lambda k: jnp.stack(outs[k])
    return (xp, xs, st("ak_p"), st("av_p"), st("bl_p"), st("br_p"), st("cc_p"), st("dp_p"), st("fc_p"),
            st("ak_s"), st("av_s"), st("bl_s"), st("br_s"), st("cc_s"), st("dp_s"), st("fc_s"))
```

```python
import functools
import math

import jax
import jax.numpy as jnp
from jax import lax
from jax.experimental import pallas as pl
from jax.experimental.pallas import tpu as pltpu

_BF = jnp.bfloat16
_F32 = jnp.float32

_CHUNK = 64
_ROPE_THETA = 10000.0
_NORM_EPS = 1e-6
_NEG_INF = -1e30
_A_HEADS = 8
_A_HD = 64
_A_SUBLN_EPS = 1e-5
_B_HEADS = 8
_B_NOPE = 128
_B_ROPE = 64
_B_VD = 128
_B_Q_LORA = 384
_B_KV_LORA = 256
_D_WINDOWS = (2, 4, 8, 16)
_D_HIST = 15
_N_MIXERS = 4
_LOG2E = math.log2(math.e)

_LANES = 128
_SUBLANES = 8
_MXU_COLS = 256
_BF_ROWS = 16
_SWEEP_SLOTS = 4
_TRIP_TICKS_DIFF = 8
_TRIP_TICKS_MLA = 16
_SWEEP_HALVES = 1
_VMEM_LIMIT = 56 * 1024 * 1024

_ROW_TILE = 512
_ATT_TILE = 512


def _params(sem):
    return pltpu.CompilerParams(dimension_semantics=sem, vmem_limit_bytes=_VMEM_LIMIT)


def _resident(shape):
    nd = len(shape)
    return pl.BlockSpec(shape, lambda *_: (0,) * nd, pipeline_mode=pl.Buffered(1))


def _rms(x, g, eps):
    ms = jnp.mean(x * x, axis=-1, keepdims=True)
    return x * lax.rsqrt(ms + eps) * g


def _rope_lanes(y, cos, sin_lo, sin_hi):
    return (y * cos + pltpu.roll(y, _LANES - 32, axis=1) * sin_lo
            + pltpu.roll(y, 32, axis=1) * sin_hi)


def _rope_tables(pos):
    d = _A_HD
    inv = jnp.power(_ROPE_THETA, -jnp.arange(0, d, 2, dtype=_F32) / d)
    ang = pos.astype(_F32)[:, None] * inv[None, :]
    cos, sin = jnp.cos(ang), jnp.sin(ang)
    zero = jnp.zeros_like(sin)
    cos_t = jnp.tile(cos, (1, 4))
    sin_lo = jnp.tile(jnp.concatenate([-sin, zero], axis=1), (1, 2))
    sin_hi = jnp.tile(jnp.concatenate([zero, sin], axis=1), (1, 2))
    return cos_t, sin_lo, sin_hi


def _shift_rows(carry, cur, k):
    ext = jnp.concatenate([carry, cur], axis=0)
    return pltpu.roll(ext, k, axis=0)[carry.shape[0]:]


def _mask_lanes(rows, tile, base, for_keys):
    r = lax.broadcasted_iota(jnp.int32, (rows, _LANES), 0)
    chunk = (r % tile) // _CHUNK
    c = lax.broadcasted_iota(jnp.int32, (rows, _LANES), 1) - base
    n = tile // _CHUNK
    if for_keys:
        return jnp.where(c == chunk, 1.0, 0.0)
    return jnp.where(c > chunk, jnp.where(c < n, _NEG_INF, 0.0), 0.0)


def _diff_proj_body(x_ref, g_ref, w_ref, cos_ref, slo_ref, shi_ref, *out_refs, qscale, qt_cols):
    tm, d = x_ref.shape
    if qt_cols:
        qm_ref, kf_ref, km_ref, vf_ref, vt_ref = out_refs
        q_ext = _mask_lanes(tm, tm, _A_HD, False)
        ones = jnp.ones((_BF_ROWS, tm), _BF)
    else:
        qm_ref, kf_ref, vf_ref = out_refs
    h = _rms(x_ref[...], g_ref[...], _NORM_EPS).astype(_BF)
    cos, slo, shi = cos_ref[...], slo_ref[...], shi_ref[...]
    first_half = lax.broadcasted_iota(jnp.int32, (tm, _LANES), 1) < _A_HD
    zero = jnp.zeros((tm, _LANES), _F32)
    cw = _MXU_COLS
    for c in range(0, d, cw):
        yq = jnp.dot(h, w_ref[:, c:c + cw], preferred_element_type=_F32)
        yk = jnp.dot(h, w_ref[:, d + c:d + c + cw], preferred_element_type=_F32)
        yv = jnp.dot(h, w_ref[:, 2 * d + c:2 * d + c + cw], preferred_element_type=_F32)
        vf_ref[:, c:c + cw] = yv
        for s in range(0, cw, _LANES):
            cols = slice(c + s, c + s + _LANES)
            head = (c + s) // _LANES
            q = _rope_lanes(yq[:, s:s + _LANES], cos, slo, shi) * qscale
            k = _rope_lanes(yk[:, s:s + _LANES], cos, slo, shi)
            kf_ref[:, cols] = k
            if not qt_cols:
                qm_ref[0, :, cols] = jnp.where(first_half, q, zero).astype(_BF)
                qm_ref[1, :, cols] = jnp.where(first_half, zero, q).astype(_BF)
                continue
            for mp in range(2):
                qh = q if mp == 0 else pltpu.roll(q, _A_HD, axis=1)
                kh = k if mp == 0 else pltpu.roll(k, _A_HD, axis=1)
                qt = jnp.where(first_half, qh, q_ext).T.astype(_BF)
                for g in range(tm // qt_cols):
                    qm_ref[mp, head, g] = qt[:, g * qt_cols:(g + 1) * qt_cols]
                km_ref[mp, :, cols] = jnp.where(first_half, kh, zero).astype(_BF)
            vt_ref[head, 0, :_LANES, :] = yv[:, s:s + _LANES].T.astype(_BF)
            vt_ref[head, 0, _LANES:, :] = ones


def _diff_proj(x, g, w_qkv, tabs, tm, qt_cols=0):
    b, s, d = x.shape
    heads = d // _LANES
    row = lambda dt: jax.ShapeDtypeStruct((b, s, d), dt)
    xspec = pl.BlockSpec((None, tm, d), lambda i, t: (i, t, 0))
    tspec = pl.BlockSpec((tm, _LANES), lambda i, t: (t, 0))
    if qt_cols:
        assert tm % qt_cols == 0
        dve = _LANES + _BF_ROWS
        out_specs = [pl.BlockSpec((None, 2, heads, tm // qt_cols, _LANES, qt_cols),
                                  lambda i, t: (i, 0, 0, t, 0, 0)),
                     xspec, pl.BlockSpec((None, 2, tm, d), lambda i, t: (i, 0, t, 0)), xspec,
                     pl.BlockSpec((None, heads, 1, dve, tm), lambda i, t: (i, 0, t, 0, 0))]
        out_shape = [jax.ShapeDtypeStruct((b, 2, heads, s // qt_cols, _LANES, qt_cols), _BF),
                     row(_F32), jax.ShapeDtypeStruct((b, 2, s, d), _BF), row(_F32),
                     jax.ShapeDtypeStruct((b, heads, s // tm, dve, tm), _BF)]
    else:
        out_specs = [pl.BlockSpec((None, 2, tm, d), lambda i, t: (i, 0, t, 0)), xspec, xspec]
        out_shape = [jax.ShapeDtypeStruct((b, 2, s, d), _BF), row(_F32), row(_F32)]
    return pl.pallas_call(
        functools.partial(_diff_proj_body, qscale=_LOG2E / math.sqrt(_A_HD), qt_cols=qt_cols),
        grid=(b, s // tm),
        in_specs=[xspec, _resident((1, d)), _resident(w_qkv.shape), tspec, tspec, tspec],
        out_specs=out_specs,
        out_shape=out_shape,
        compiler_params=_params(("parallel", "parallel")),
        name="diff_proj",
    )(x, g.reshape(1, d), w_qkv, *tabs)


def _cached_diff_body(q_ref, kc_ref, kn_ref, vc_ref, vn_ref, lam_ref, sg_ref, o_ref, *, lam_init):
    nt = (((1,), (1,)), ((), ()))
    lp = lam_ref[...]
    lam = (jnp.exp(jnp.sum(lp[0:1] * lp[1:2], axis=-1, keepdims=True))
           - jnp.exp(jnp.sum(lp[2:3] * lp[3:4], axis=-1, keepdims=True)) + lam_init)
    for hd in range(o_ref.shape[-1] // _LANES):
        cols = slice(hd * _LANES, (hd + 1) * _LANES)
        kc, kn = kc_ref[:, cols].astype(_BF), kn_ref[:, cols].astype(_BF)
        vc, vn = vc_ref[:, cols].astype(_BF), vn_ref[:, cols].astype(_BF)
        outs = []
        for c in range(2):
            q = q_ref[c, :, cols]
            sc = lax.dot_general(q, kc, nt, preferred_element_type=_F32)
            sn = lax.dot_general(q, kn, nt, preferred_element_type=_F32)
            m = jnp.maximum(jnp.max(sc, axis=-1, keepdims=True), jnp.max(sn, axis=-1, keepdims=True))
            pc, pn = jnp.exp2(sc - m), jnp.exp2(sn - m)
            l = jnp.sum(pc, axis=-1, keepdims=True) + jnp.sum(pn, axis=-1, keepdims=True)
            acc = (jnp.dot(pc.astype(_BF), vc, preferred_element_type=_F32)
                   + jnp.dot(pn.astype(_BF), vn, preferred_element_type=_F32))
            outs.append(acc / l)
        o = _rms(outs[0] - lam * outs[1], sg_ref[...], _A_SUBLN_EPS) * (1.0 - lam_init)
        o_ref[:, cols] = o.astype(o_ref.dtype)


def _cached_diff(qm, k_cache, k_new, v_cache, v_new, lam, subln_g, lam_init):
    b, _, t, d = qm.shape
    past = k_cache.shape[1]
    blk = lambda rows: pl.BlockSpec((None, rows, d), lambda i: (i, 0, 0))
    return pl.pallas_call(
        functools.partial(_cached_diff_body, lam_init=lam_init),
        grid=(b,),
        in_specs=[pl.BlockSpec((None, 2, t, d), lambda i: (i, 0, 0, 0)),
                  blk(past), blk(t), blk(past), blk(t), _resident(lam.shape), _resident((1, _LANES))],
        out_specs=blk(t),
        out_shape=jax.ShapeDtypeStruct((b, t, d), _BF),
        compiler_params=_params(("parallel",)),
        name="cached_diff",
    )(qm, k_cache, k_new, v_cache, v_new, lam, subln_g.reshape(1, _LANES))


def _cached_mla_body(q_ref, k_ref, v_ref, o_ref):
    s = lax.dot_general(q_ref[...], k_ref[...], (((1,), (1,)), ((), ())),
                        preferred_element_type=_F32)
    p = jnp.exp2(s - jnp.max(s, axis=-1, keepdims=True))
    acc = jnp.dot(p.astype(_BF), v_ref[...], preferred_element_type=_F32)
    o_ref[...] = (acc / jnp.sum(p, axis=-1, keepdims=True)).astype(o_ref.dtype)


def _cached_mla(q, k, v, heads, dq, dv):
    b, t, _ = q.shape
    sk = k.shape[1]
    return pl.pallas_call(
        _cached_mla_body,
        grid=(b, heads),
        in_specs=[pl.BlockSpec((None, t, dq), lambda i, h: (i, 0, h)),
                  pl.BlockSpec((None, sk, dq), lambda i, h: (i, 0, h)),
                  pl.BlockSpec((None, sk, dv), lambda i, h: (i, 0, h))],
        out_specs=pl.BlockSpec((None, t, dv), lambda i, h: (i, 0, h)),
        out_shape=jax.ShapeDtypeStruct((b, t, heads * dv), _BF),
        compiler_params=_params(("parallel", "parallel")),
        name="cached_mla",
    )(q, k, v)


def _sweep_body(*refs, nmaps, halves, tile, nb, lam_init):
    if nmaps == 2:
        q_ref, k_ref, vt_ref, kmask_ref, lam_ref, sg_ref, o_ref, s_buf, mb_buf, m_sc, acc_sc = refs
    else:
        q_ref, k_ref, vt_ref, kmask_ref, o_ref, s_buf, mb_buf, m_sc, acc_sc = refs
    hw = tile // halves
    nch = nmaps * halves
    dv = o_ref.shape[-1]
    ticks = acc_sc.shape[0]
    npairs = nb * (nb + 1) // 2
    acc_sc[...] = jnp.zeros(acc_sc.shape, _F32)

    def nxt(pair):
        qi, t = pair
        last = t == qi
        return jnp.where(last, qi + 1, qi), jnp.where(last, 0, t + 1)

    def scores(pair, slot):
        qi, t = pair
        kmask = kmask_ref[(t == qi).astype(jnp.int32)]
        for mp in range(nmaps):
            k = k_ref[mp, pl.ds(pl.multiple_of(t * tile, tile), tile), :] + kmask
            for hf in range(halves):
                c = mp * halves + hf
                s = jnp.dot(k, q_ref[mp, qi * halves + hf], preferred_element_type=_F32)
                s_buf[slot, c] = s
                mb_buf[slot, c] = jnp.max(s, axis=0, keepdims=True)

    def values(pair, pos):
        _, t = pair
        first = t == 0
        slot = pos % _SWEEP_SLOTS
        vt = vt_ref[t]
        for c in range(nch):
            m_prev = jnp.where(first, _NEG_INF, m_sc[c])
            m_new = jnp.maximum(m_prev, mb_buf[slot, c])
            alpha = jnp.exp2(m_prev - m_new)
            m_sc[c] = m_new
            p = jnp.exp2(s_buf[slot, c] - m_new).astype(_BF)
            acc_sc[pos, c] = (alpha * acc_sc[(pos - 1) % ticks, c]
                              + jnp.dot(vt, p, preferred_element_type=_F32))

    def finish(pair, pos):
        qi, t = pair

        @pl.when(t == qi)
        def _():
            if nmaps == 2:
                lp = lam_ref[...]
                lam = (jnp.exp(jnp.sum(lp[0:1] * lp[1:2], axis=-1, keepdims=True))
                       - jnp.exp(jnp.sum(lp[2:3] * lp[3:4], axis=-1, keepdims=True)) + lam_init)
            normed = lambda c: acc_sc[pos, c, :dv, :] / acc_sc[pos, c, dv:dv + 1, :]
            for hf in range(halves):
                if nmaps == 2:
                    ot = normed(hf) - lam * normed(halves + hf)
                    o = _rms(ot.T, sg_ref[...], _A_SUBLN_EPS) * (1.0 - lam_init)
                else:
                    o = normed(hf).T
                row = pl.multiple_of(qi * tile + hf * hw, hw)
                o_ref[pl.ds(row, hw), :] = o.astype(o_ref.dtype)

    zero = jnp.int32(0)
    pair0 = (zero, zero)
    scores(pair0, 0)
    if npairs == 1:
        values(pair0, 0)
        finish(pair0, 0)
        return
    pair1 = nxt(pair0)
    scores(pair1, 1)

    def tick(carry, r):
        pc, pb, pa = carry
        scores(pa, (2 + r) % _SWEEP_SLOTS)
        values(pc, r)
        return pb, pa, nxt(pa)

    def run_ticks(n, carry):
        done = []
        for r in range(n):
            done.append(carry[0])
            carry = tick(carry, r)
        for r, pair in enumerate(done):
            finish(pair, r)
        return carry

    nticks = npairs - 2
    carry = lax.fori_loop(0, nticks // ticks, lambda _, c: run_ticks(ticks, c),
                          (pair0, pair1, nxt(pair1)))
    carry = run_ticks(nticks % ticks, carry)
    before_last, last, _ = carry
    for pair, pos in ((before_last, (npairs - 2) % ticks), (last, (npairs - 1) % ticks)):
        values(pair, pos)
        finish(pair, pos)


def _sweep(q, k, vt, heads, dq, dv, mask_base, *, halves, trip_ticks, q_buffers,
           lam=None, subln_g=None, lam_init=0.0):
    b, nmaps = q.shape[:2]
    _, _, nb, dve, tile = vt.shape
    s = nb * tile
    assert tile % _CHUNK == 0 and tile % halves == 0 and trip_ticks % _SWEEP_SLOTS == 0
    assert q.shape[2:] == (heads, s * halves // tile, dq, tile // halves) and dve == dv + _BF_ROWS
    nch = nmaps * halves
    hw = tile // halves
    onehot = _mask_lanes(tile, tile, mask_base % _LANES, True)
    onehot = jnp.pad(onehot, ((0, 0), (mask_base - mask_base % _LANES, 0)))
    onehot = jnp.pad(onehot, ((0, 0), (0, dq - onehot.shape[1])))
    kmask = jnp.stack([jnp.zeros_like(onehot), onehot]).astype(_BF)
    in_specs = [pl.BlockSpec((None, nmaps, None, nb * halves, dq, hw),
                             lambda i, h: (i, 0, h, 0, 0, 0), pipeline_mode=pl.Buffered(q_buffers)),
                pl.BlockSpec((None, nmaps, s, dq), lambda i, h: (i, 0, 0, h),
                             pipeline_mode=pl.Buffered(1)),
                pl.BlockSpec((None, None, nb, dve, tile), lambda i, h: (i, h, 0, 0, 0)),
                _resident(kmask.shape)]
    args = [q, k, vt, kmask]
    if nmaps == 2:
        in_specs += [_resident(lam.shape), _resident((1, dv))]
        args += [lam, subln_g.reshape(1, dv)]
    scratch = [pltpu.VMEM((_SWEEP_SLOTS, nch, tile, hw), _F32),
               pltpu.VMEM((_SWEEP_SLOTS, nch, 1, hw), _F32),
               pltpu.VMEM((nch, 1, hw), _F32), pltpu.VMEM((trip_ticks, nch, dve, hw), _F32)]
    return pl.pallas_call(
        functools.partial(_sweep_body, nmaps=nmaps, halves=halves, tile=tile, nb=nb,
                          lam_init=lam_init),
        grid=(b, heads),
        in_specs=in_specs,
        out_specs=pl.BlockSpec((None, s, dv), lambda i, h: (i, 0, h)),
        out_shape=jax.ShapeDtypeStruct((b, s, heads * dv), _BF),
        scratch_shapes=scratch,
        compiler_params=_params(("parallel", "parallel")),
        name="sweep_diff" if nmaps == 2 else "sweep_mla",
    )(*args)


def _mm_body(*refs, has_res):
    if has_res:
        a_ref, w_ref, r_ref, o_ref = refs
    else:
        a_ref, w_ref, o_ref = refs
    y = jnp.dot(a_ref[...].astype(_BF), w_ref[...], preferred_element_type=_F32)
    if has_res:
        y = r_ref[...] + y
    o_ref[...] = y.astype(o_ref.dtype)


def _mm(a, w, res=None, out_dtype=_F32):
    b, s, kdim = a.shape
    n = w.shape[1]
    tm = min(_ROW_TILE, s)
    in_specs = [pl.BlockSpec((None, tm, kdim), lambda i, t: (i, t, 0)), _resident(w.shape)]
    args = [a, w]
    if res is not None:
        in_specs.append(pl.BlockSpec((None, tm, n), lambda i, t: (i, t, 0)))
        args.append(res)
    return pl.pallas_call(
        functools.partial(_mm_body, has_res=res is not None),
        grid=(b, s // tm),
        in_specs=in_specs,
        out_specs=pl.BlockSpec((None, tm, n), lambda i, t: (i, t, 0)),
        out_shape=jax.ShapeDtypeStruct((b, s, n), out_dtype),
        compiler_params=_params(("parallel", "parallel")),
        name="matmul_residual" if res is not None else "matmul",
    )(*args)


def _mla_proj_body(x_ref, g_ref, wd_ref, qg_ref, wq_ref, kvg_ref, wk_ref, wv_ref,
                   cos_ref, slo_ref, shi_ref, lat_ref, kr_ref, q_ref, k_ref, v_ref, *, qscale, qt_cols):
    tm = x_ref.shape[0]
    h = _rms(x_ref[...], g_ref[...], _NORM_EPS).astype(_BF)
    cos, slo, shi = cos_ref[...], slo_ref[...], shi_ref[...]
    q_ext = _mask_lanes(tm, tm, _B_ROPE, False) if qt_cols else jnp.zeros((tm, _LANES), _F32)
    dn = jnp.dot(h, wd_ref[...], preferred_element_type=_F32)
    e0, e1 = _B_Q_LORA, _B_Q_LORA + _B_KV_LORA
    cq = _rms(dn[:, :e0], qg_ref[...], _NORM_EPS).astype(_BF)
    lat = _rms(dn[:, e0:e1], kvg_ref[...], _NORM_EPS)
    kr = _rope_lanes(dn[:, e1:e1 + _LANES], cos, slo, shi)
    lat_ref[...] = lat
    kr_ref[...] = kr[:, :_B_ROPE]
    hw = 2 * _LANES
    for hd in range(_B_HEADS):
        yq = jnp.dot(cq, wq_ref[:, hd * hw:(hd + 1) * hw], preferred_element_type=_F32)
        q_nope = yq[:, :_LANES] * qscale
        q_rope = _rope_lanes(yq[:, _LANES:], cos, slo, shi) * qscale + q_ext
        if qt_cols:
            qt = jnp.concatenate([q_nope.T, q_rope.T], axis=0).astype(_BF)
            for g in range(tm // qt_cols):
                q_ref[hd, g] = qt[:, g * qt_cols:(g + 1) * qt_cols]
        else:
            q_ref[:, hd * hw:hd * hw + _LANES] = q_nope.astype(_BF)
            q_ref[:, hd * hw + _LANES:(hd + 1) * hw] = q_rope.astype(_BF)
    kin = jnp.concatenate([lat, kr], axis=1).astype(_BF)
    for hd in range(_B_HEADS):
        k_ref[:, hd * hw:(hd + 1) * hw] = jnp.dot(
            kin, wk_ref[:, hd * hw:(hd + 1) * hw], preferred_element_type=_F32).astype(_BF)
    latb = lat.astype(_BF)
    for c in range(0, wv_ref.shape[-1], _MXU_COLS):
        yv = jnp.dot(latb, wv_ref[:, c:c + _MXU_COLS], preferred_element_type=_F32)
        if qt_cols:
            for s in range(0, _MXU_COLS, _B_VD):
                v_ref[(c + s) // _B_VD, 0, :_B_VD, :] = yv[:, s:s + _B_VD].T.astype(_BF)
                v_ref[(c + s) // _B_VD, 0, _B_VD:, :] = jnp.ones((_BF_ROWS, tm), _BF)
        else:
            v_ref[:, c:c + _MXU_COLS] = yv.astype(_BF)


def _mla_weights(w_down, w_uq, w_uk, w_uv):
    d = w_down.shape[0]
    hw = 2 * _LANES
    wd = jnp.concatenate([w_down, jnp.zeros((d, _LANES - _B_ROPE), w_down.dtype)], axis=1)
    wq = w_uq.reshape(_B_Q_LORA, _B_HEADS, _B_NOPE + _B_ROPE)
    wq = jnp.pad(wq, ((0, 0), (0, 0), (0, hw - _B_NOPE - _B_ROPE))).reshape(_B_Q_LORA, _B_HEADS * hw)
    eye = jnp.eye(_B_ROPE, dtype=w_uk.dtype)
    wk_top = jnp.pad(w_uk, ((0, 0), (0, 0), (0, hw - _B_NOPE)))
    wk_mid = jnp.broadcast_to(jnp.pad(eye, ((0, 0), (_B_NOPE, hw - _B_NOPE - _B_ROPE)))[:, None, :],
                              (_B_ROPE, _B_HEADS, hw))
    wk_bot = jnp.zeros((_LANES - _B_ROPE, _B_HEADS, hw), w_uk.dtype)
    wk = jnp.concatenate([wk_top, wk_mid, wk_bot], axis=0).reshape(-1, _B_HEADS * hw)
    wv = w_uv.reshape(_B_KV_LORA, _B_HEADS * _B_VD)
    return wd.astype(_BF), wq.astype(_BF), wk.astype(_BF), wv.astype(_BF)


def _mla_proj(x, g, wts, q_norm_g, kv_norm_g, tabs, tm, qt_cols=0):
    b, s, d = x.shape
    wd, wq, wk, wv = wts
    xspec = pl.BlockSpec((None, tm, d), lambda i, t: (i, t, 0))
    tspec = pl.BlockSpec((tm, _LANES), lambda i, t: (t, 0))
    ospec = lambda n: pl.BlockSpec((None, tm, n), lambda i, t: (i, t, 0))
    oshape = lambda n, dt: jax.ShapeDtypeStruct((b, s, n), dt)
    if qt_cols:
        assert tm % qt_cols == 0
        hq = wq.shape[1] // _B_HEADS
        q_spec = pl.BlockSpec((None, _B_HEADS, tm // qt_cols, hq, qt_cols),
                              lambda i, t: (i, 0, t, 0, 0))
        q_shape = jax.ShapeDtypeStruct((b, _B_HEADS, s // qt_cols, hq, qt_cols), _BF)
        dve = _B_VD + _BF_ROWS
        v_spec = pl.BlockSpec((None, _B_HEADS, 1, dve, tm), lambda i, t: (i, 0, t, 0, 0))
        v_shape = jax.ShapeDtypeStruct((b, _B_HEADS, s // tm, dve, tm), _BF)
    else:
        q_spec, q_shape = ospec(wq.shape[1]), oshape(wq.shape[1], _BF)
        v_spec, v_shape = ospec(wv.shape[1]), oshape(wv.shape[1], _BF)
    return pl.pallas_call(
        functools.partial(_mla_proj_body, qscale=_LOG2E / math.sqrt(_B_NOPE + _B_ROPE),
                          qt_cols=qt_cols),
        grid=(b, s // tm),
        in_specs=[xspec, _resident((1, d)), _resident(wd.shape), _resident((1, _B_Q_LORA)),
                  _resident(wq.shape), _resident((1, _B_KV_LORA)), _resident(wk.shape),
                  _resident(wv.shape), tspec, tspec, tspec],
        out_specs=[ospec(_B_KV_LORA), ospec(_B_ROPE), q_spec, ospec(wk.shape[1]), v_spec],
        out_shape=[oshape(_B_KV_LORA, _F32), oshape(_B_ROPE, _F32), q_shape,
                   oshape(wk.shape[1], _BF), v_shape],
        compiler_params=_params(("parallel", "parallel")),
        name="mla_proj",
    )(x, g.reshape(1, d), wd, q_norm_g.reshape(1, -1), wq, kv_norm_g.reshape(1, -1), wk, wv, *tabs)


def _delayed(carry_ref, cols, cur, shifts):
    nseg = carry_ref.shape[0]
    seg = cur.shape[0] // nseg
    outs = [[] for _ in shifts]
    for sg in range(nseg):
        part = cur[sg * seg:(sg + 1) * seg]
        prev = carry_ref[sg, :, cols]
        for o, k in zip(outs, shifts):
            o.append(_shift_rows(prev, part, k))
        carry_ref[sg, :, cols] = part[seg - _SUBLANES:]
    return [o[0] if nseg == 1 else jnp.concatenate(o, axis=0) for o in outs]


def _conv_mix_body(x_ref, g_ref, win_ref, cw_ref, wout_ref, hist_ref, o_ref, hout_ref,
                   carry_sc, z_sc):
    tm, d = x_ref.shape

    @pl.when(pl.program_id(1) == 0)
    def _():
        carry_sc[...] = hist_ref[...]

    x = x_ref[...]
    h = _rms(x, g_ref[...], _NORM_EPS).astype(_BF)
    cw = _MXU_COLS
    for c in range(0, d, cw):
        gate_b = jnp.dot(h, win_ref[:, c:c + cw], preferred_element_type=_F32)
        gate_c = jnp.dot(h, win_ref[:, d + c:d + c + cw], preferred_element_type=_F32)
        val = jnp.dot(h, win_ref[:, 2 * d + c:2 * d + c + cw], preferred_element_type=_F32)
        u = gate_c * val
        u2, u1 = _delayed(carry_sc, slice(c, c + cw), u, (2, 1))
        y = u2 * cw_ref[0:1, c:c + cw] + u1 * cw_ref[1:2, c:c + cw] + u * cw_ref[2:3, c:c + cw]
        z_sc[:, c:c + cw] = (gate_b * y).astype(_BF)
    hout_ref[...] = carry_sc[...]
    o_ref[...] = x + jnp.dot(z_sc[...], wout_ref[...], preferred_element_type=_F32)


def _conv_mix(x, g, w_in, conv_w, w_out, hist, tm):
    b, s, d = x.shape
    nseg = hist.shape[1]
    assert nseg == 1 or tm == s
    xspec = pl.BlockSpec((None, tm, d), lambda i, t: (i, t, 0))
    hspec = pl.BlockSpec((None, nseg, _SUBLANES, d), lambda i, t: (i, 0, 0, 0))
    return pl.pallas_call(
        _conv_mix_body,
        grid=(b, s // tm),
        in_specs=[xspec, _resident((1, d)), _resident(w_in.shape), _resident(conv_w.shape),
                  _resident(w_out.shape), hspec],
        out_specs=[xspec, hspec],
        out_shape=[jax.ShapeDtypeStruct((b, s, d), _F32),
                   jax.ShapeDtypeStruct((b, nseg, _SUBLANES, d), _F32)],
        scratch_shapes=[pltpu.VMEM((nseg, _SUBLANES, d), _F32), pltpu.VMEM((tm, d), _BF)],
        compiler_params=_params(("arbitrary", "arbitrary")),
        name="conv_mixer",
    )(x, g.reshape(1, d), w_in, conv_w, w_out, hist)


_POOL_HALO = 16


def _pool_mix_body(x_ref, g_ref, wg_ref, sc_ref, hist_ref, o_ref, hout_ref, carry_sc, *, hist_valid):
    tm, d = x_ref.shape
    t = pl.program_id(1)

    @pl.when(t == 0)
    def _():
        carry_sc[...] = hist_ref[...]

    x = x_ref[...]
    h = _rms(x, g_ref[...], _NORM_EPS)
    gw = d // len(_D_WINDOWS)
    tpos = t * tm + lax.broadcasted_iota(jnp.int32, (tm, gw), 0)
    outs = []
    for gi, w in enumerate(_D_WINDOWS):
        hg = h[:, gi * gw:(gi + 1) * gw]
        acc = jnp.concatenate([carry_sc[:, gi * gw:(gi + 1) * gw], hg], axis=0)
        k = 1
        while k < w:
            acc = acc + pltpu.roll(acc, k, axis=0)
            k *= 2
        cnt = jnp.minimum(tpos + (hist_valid + 1), w).astype(_F32)
        pooled = acc[_POOL_HALO:] / cnt
        outs.append(jnp.dot((pooled - hg).astype(_BF), wg_ref[gi], preferred_element_type=_F32))
    o_ref[...] = x + jnp.concatenate(outs, axis=1) * sc_ref[...]
    tail = h[tm - _POOL_HALO:, :]
    carry_sc[...] = tail
    hout_ref[...] = tail


def _pool_mix(x, g, w_group, scale, hist16, hist_valid, tm):
    b, s, d = x.shape
    assert all(w & (w - 1) == 0 and w <= _POOL_HALO for w in _D_WINDOWS)
    xspec = pl.BlockSpec((None, tm, d), lambda i, t: (i, t, 0))
    hspec = pl.BlockSpec((None, _POOL_HALO, d), lambda i, t: (i, 0, 0))
    return pl.pallas_call(
        functools.partial(_pool_mix_body, hist_valid=hist_valid),
        grid=(b, s // tm),
        in_specs=[xspec, _resident((1, d)), _resident(w_group.shape), _resident((1, d)), hspec],
        out_specs=[xspec, hspec],
        out_shape=[jax.ShapeDtypeStruct((b, s, d), _F32),
                   jax.ShapeDtypeStruct((b, _POOL_HALO, d), _F32)],
        scratch_shapes=[pltpu.VMEM((_POOL_HALO, d), _F32)],
        compiler_params=_params(("arbitrary", "arbitrary")),
        name="pool_mixer",
    )(x, g.reshape(1, d), w_group, scale.reshape(1, d), hist16)


def _ffn_body(*refs, final, mixed):
    refs = list(refs)
    x_ref, g_ref, wg_ref, wu_ref, cw_ref, cb_ref, wd_ref, hist_ref = refs[:8]
    del refs[:8]
    a_ref, wo_ref = (refs.pop(0), refs.pop(0)) if mixed else (None, None)
    fg_ref = refs.pop(0) if final else None
    o_ref, hout_ref, carry_sc, act_sc = refs
    tm = x_ref.shape[0]
    f = wg_ref.shape[1]

    @pl.when(pl.program_id(1) == 0)
    def _():
        carry_sc[...] = hist_ref[...]

    x = x_ref[...]
    if mixed:
        x = x + jnp.dot(a_ref[...], wo_ref[...], preferred_element_type=_F32)
    h = _rms(x, g_ref[...], _NORM_EPS).astype(_BF)
    cw = _MXU_COLS
    for c in range(0, f, cw):
        gate = jnp.dot(h, wg_ref[:, c:c + cw], preferred_element_type=_F32)
        up = jnp.dot(h, wu_ref[:, c:c + cw], preferred_element_type=_F32)
        g2, g1 = _delayed(carry_sc, slice(c, c + cw), gate, (2, 1))
        y = (g2 * cw_ref[0:1, c:c + cw] + g1 * cw_ref[1:2, c:c + cw]
             + gate * cw_ref[2:3, c:c + cw] + cb_ref[:, c:c + cw])
        act_sc[:, c:c + cw] = (y / (1.0 + jnp.exp(-y)) * up).astype(_BF)
    hout_ref[...] = carry_sc[...]
    out = x + jnp.dot(act_sc[...], wd_ref[...], preferred_element_type=_F32)
    if final:
        out = _rms(out, fg_ref[...], _NORM_EPS)
    o_ref[...] = out


def _ffn(x, g, w_gate, w_up, conv_w, conv_b, w_down, hist, final_g, tm, mixer_out=None, w_o=None):
    b, s, d = x.shape
    f = w_gate.shape[1]
    nseg = hist.shape[1]
    assert f % _MXU_COLS == 0 and (nseg == 1 or tm == s)
    xspec = pl.BlockSpec((None, tm, d), lambda i, t: (i, t, 0))
    hspec = pl.BlockSpec((None, nseg, _SUBLANES, f), lambda i, t: (i, 0, 0, 0))
    in_specs = [xspec, _resident((1, d)), _resident(w_gate.shape), _resident(w_up.shape),
                _resident(conv_w.shape), _resident((1, f)), _resident(w_down.shape), hspec]
    args = [x, g.reshape(1, d), w_gate, w_up, conv_w, conv_b.reshape(1, f), w_down, hist]
    if mixer_out is not None:
        in_specs += [pl.BlockSpec((None, tm, mixer_out.shape[-1]), lambda i, t: (i, t, 0)),
                     _resident(w_o.shape)]
        args += [mixer_out, w_o]
    if final_g is not None:
        in_specs.append(_resident((1, d)))
        args.append(final_g.reshape(1, d))
    return pl.pallas_call(
        functools.partial(_ffn_body, final=final_g is not None, mixed=mixer_out is not None),
        grid=(b, s // tm),
        in_specs=in_specs,
        out_specs=[xspec, hspec],
        out_shape=[jax.ShapeDtypeStruct((b, s, d), _F32),
                   jax.ShapeDtypeStruct((b, nseg, _SUBLANES, f), _F32)],
        scratch_shapes=[pltpu.VMEM((nseg, _SUBLANES, f), _F32), pltpu.VMEM((tm, f), _BF)],
        compiler_params=_params(("arbitrary", "arbitrary")),
        name="conv_ffn",
    )(*args)


def _pad_hist(hist, rows):
    return jnp.pad(hist, ((0, 0), (rows - hist.shape[1], 0), (0, 0)))


def kernel(x_prompt, x_sample, cache_a_k, cache_a_v, cache_b_latent, cache_b_krope, state_c_conv, state_d_pool, state_ffn_conv, norm_mix_g, norm_ffn_g, norm_final_g, a_w_qkv, a_lam, a_subln_g, a_w_o, b_w_down, b_q_norm_g, b_w_uq, b_kv_norm_g, b_w_uk, b_w_uv, b_w_o, c_w_in, c_conv_w, c_w_out, d_w_group, d_scale, ffn_w_gate, ffn_w_up, ffn_conv_w, ffn_conv_b, ffn_w_down):
    depth = norm_mix_g.shape[0]
    n_p, seq, d = x_prompt.shape
    n_s, t_new, _ = x_sample.shape
    past = cache_a_k.shape[2]
    f = ffn_w_gate.shape[-1]
    tm_p, tm_s = min(_ROW_TILE, seq), t_new
    ta = min(_ATT_TILE, seq)
    tabs_p = _rope_tables(jnp.arange(seq, dtype=jnp.int32))
    tabs_s = _rope_tables(past + jnp.arange(t_new, dtype=jnp.int32))
    xp, xs = x_prompt, x_sample
    flat = lambda a: a.reshape(1, n_s * t_new, a.shape[-1])
    outs = {k: [] for k in ("ak_p", "av_p", "bl_p", "br_p", "cc_p", "dp_p", "fc_p",
                            "ak_s", "av_s", "bl_s", "br_s", "cc_s", "dp_s", "fc_s")}
    for i in range(depth):
        m, j = i % _N_MIXERS, i // _N_MIXERS
        g_mix = norm_mix_g[i]
        mix_p, mix_s = {}, {}
        if m == 0:
            lam_init = 0.8 - 0.6 * math.exp(-0.3 * i)
            w_qkv, w_o = a_w_qkv[j].astype(_BF), a_w_o[j].astype(_BF)
            qt, kf, km, vf, vt = _diff_proj(xp, g_mix, w_qkv, tabs_p, ta, ta // _SWEEP_HALVES)
            op = _sweep(qt, km, vt, _A_HEADS, _LANES, 2 * _A_HD, _A_HD, halves=_SWEEP_HALVES,
                        trip_ticks=_TRIP_TICKS_DIFF, q_buffers=1,
                        lam=a_lam[j], subln_g=a_subln_g[j], lam_init=lam_init)
            mix_p = dict(mixer_out=op, w_o=w_o)
            outs["ak_p"].append(kf.reshape(n_p, seq, _A_HEADS, 2, _A_HD))
            outs["av_p"].append(vf.reshape(n_p, seq, _A_HEADS, 2 * _A_HD))
            qm, kf, vf = _diff_proj(xs, g_mix, w_qkv, tabs_s, tm_s)
            osm = _cached_diff(qm, cache_a_k[j].reshape(n_s, past, d), kf,
                               cache_a_v[j].reshape(n_s, past, d), vf,
                               a_lam[j], a_subln_g[j], lam_init)
            mix_s = dict(mixer_out=flat(osm), w_o=w_o)
            outs["ak_s"].append(kf.reshape(n_s, t_new, _A_HEADS, 2, _A_HD))
            outs["av_s"].append(vf.reshape(n_s, t_new, _A_HEADS, 2 * _A_HD))
        elif m == 1:
            wts = _mla_weights(b_w_down[j], b_w_uq[j], b_w_uk[j], b_w_uv[j])
            w_o = b_w_o[j].astype(_BF)
            hq = 2 * _LANES
            lat, kr, qt, kx, vt = _mla_proj(xp, g_mix, wts, b_q_norm_g[j], b_kv_norm_g[j], tabs_p,
                                            ta, ta // _SWEEP_HALVES)
            op = _sweep(qt[:, None], kx[:, None], vt, _B_HEADS, hq, _B_VD, _B_NOPE + _B_ROPE,
                        halves=_SWEEP_HALVES, trip_ticks=_TRIP_TICKS_MLA, q_buffers=2)
            mix_p = dict(mixer_out=op, w_o=w_o)
            outs["bl_p"].append(lat)
            outs["br_p"].append(kr)
            lat, kr, q, kx, v = _mla_proj(xs, g_mix, wts, b_q_norm_g[j], b_kv_norm_g[j], tabs_s, tm_s)
            kin = jnp.concatenate([cache_b_latent[j], cache_b_krope[j],
                                   jnp.zeros((n_s, past, _LANES - _B_ROPE), _F32)], axis=-1).astype(_BF)
            k_cache = _mm(kin, wts[2], out_dtype=_BF)
            v_cache = _mm(cache_b_latent[j].astype(_BF), wts[3], out_dtype=_BF)
            k_all = jnp.concatenate([k_cache, kx], axis=1)
            v_all = jnp.concatenate([v_cache, v], axis=1)
            osm = _cached_mla(q, k_all, v_all, _B_HEADS, hq, _B_VD)
            mix_s = dict(mixer_out=flat(osm), w_o=w_o)
            outs["bl_s"].append(lat)
            outs["br_s"].append(kr)
        elif m == 2:
            w_in, w_out = c_w_in[j].astype(_BF), c_w_out[j].astype(_BF)
            xp, hc = _conv_mix(xp, g_mix, w_in, c_conv_w[j], w_out,
                               jnp.zeros((n_p, 1, _SUBLANES, d), _F32), tm_p)
            outs["cc_p"].append(hc[:, 0, -2:])
            xs, hc = _conv_mix(flat(xs), g_mix, w_in, c_conv_w[j], w_out,
                               _pad_hist(state_c_conv[j], _SUBLANES)[None], n_s * t_new)
            xs = xs.reshape(n_s, t_new, d)
            outs["cc_s"].append(hc[0, :, -2:])
        else:
            w_grp = d_w_group[j].astype(_BF)
            xp, hd = _pool_mix(xp, g_mix, w_grp, d_scale[j],
                               jnp.zeros((n_p, _POOL_HALO, d), _F32), 0, tm_p)
            outs["dp_p"].append(hd[:, -_D_HIST:])
            xs, hd = _pool_mix(xs, g_mix, w_grp, d_scale[j],
                               _pad_hist(state_d_pool[j], _POOL_HALO), _D_HIST, tm_s)
            outs["dp_s"].append(hd[:, -_D_HIST:])
        wg, wu, wd = ffn_w_gate[i].astype(_BF), ffn_w_up[i].astype(_BF), ffn_w_down[i].astype(_BF)
        final_g = norm_final_g if i == depth - 1 else None
        xp, hf = _ffn(xp, norm_ffn_g[i], wg, wu, ffn_conv_w[i], ffn_conv_b[i], wd,
                      jnp.zeros((n_p, 1, _SUBLANES, f), _F32), final_g, tm_p, **mix_p)
        outs["fc_p"].append(hf[:, 0, -2:])
        xs, hf = _ffn(flat(xs), norm_ffn_g[i], wg, wu, ffn_conv_w[i], ffn_conv_b[i], wd,
                      _pad_hist(state_ffn_conv[i], _SUBLANES)[None], final_g, n_s * t_new, **mix_s)
        xs = xs.reshape(n_s, t_new, d)
        outs["fc_s"].append(hf[0, :, -2:])
    st = lambda k: jnp.stack(outs[k])
    return (xp, xs, st("ak_p"), st("av_p"), st("bl_p"), st("br_p"), st("cc_p"), st("dp_p"), st("fc_p"),
            st("ak_s"), st("av_s"), st("bl_s"), st("br_s"), st("cc_s"), st("dp_s"), st("fc_s"))
```

```python
import functools
import math

import jax
import jax.numpy as jnp
from jax import lax
from jax.experimental import pallas as pl
from jax.experimental.pallas import tpu as pltpu

_BF = jnp.bfloat16
_F32 = jnp.float32

_CHUNK = 64
_ROPE_THETA = 10000.0
_NORM_EPS = 1e-6
_NEG_INF = -1e30
_A_HEADS = 8
_A_HD = 64
_A_SUBLN_EPS = 1e-5
_B_HEADS = 8
_B_NOPE = 128
_B_ROPE = 64
_B_VD = 128
_B_Q_LORA = 384
_B_KV_LORA = 256
_D_WINDOWS = (2, 4, 8, 16)
_D_HIST = 15
_N_MIXERS = 4
_LOG2E = math.log2(math.e)

_LANES = 128
_SUBLANES = 8
_MXU_COLS = 256
_BF_ROWS = 16
_SWEEP_SLOTS = 4
_TRIP_TICKS_DIFF = 8
_TRIP_TICKS_MLA = 16
_SWEEP_HALVES = 1
_VMEM_LIMIT = 56 * 1024 * 1024

_ROW_TILE = 1024
_ATT_TILE = 512


def _params(sem):
    return pltpu.CompilerParams(dimension_semantics=sem, vmem_limit_bytes=_VMEM_LIMIT)


def _resident(shape):
    nd = len(shape)
    return pl.BlockSpec(shape, lambda *_: (0,) * nd, pipeline_mode=pl.Buffered(1))


def _rms(x, g, eps):
    ms = jnp.mean(x * x, axis=-1, keepdims=True)
    return x * lax.rsqrt(ms + eps) * g


def _rope_lanes(y, cos, sin_lo, sin_hi):
    return (y * cos + pltpu.roll(y, _LANES - 32, axis=1) * sin_lo
            + pltpu.roll(y, 32, axis=1) * sin_hi)


def _rope_tables(pos):
    d = _A_HD
    inv = jnp.power(_ROPE_THETA, -jnp.arange(0, d, 2, dtype=_F32) / d)
    ang = pos.astype(_F32)[:, None] * inv[None, :]
    cos, sin = jnp.cos(ang), jnp.sin(ang)
    zero = jnp.zeros_like(sin)
    cos_t = jnp.tile(cos, (1, 4))
    sin_lo = jnp.tile(jnp.concatenate([-sin, zero], axis=1), (1, 2))
    sin_hi = jnp.tile(jnp.concatenate([zero, sin], axis=1), (1, 2))
    return cos_t, sin_lo, sin_hi


def _shift_rows(carry, cur, k):
    ext = jnp.concatenate([carry, cur], axis=0)
    return pltpu.roll(ext, k, axis=0)[carry.shape[0]:]


def _mask_lanes(rows, row0, tile, base, for_keys):
    r = row0 + lax.broadcasted_iota(jnp.int32, (rows, _LANES), 0)
    chunk = (r % tile) // _CHUNK
    c = lax.broadcasted_iota(jnp.int32, (rows, _LANES), 1) - base
    n = tile // _CHUNK
    if for_keys:
        return jnp.where(c == chunk, 1.0, 0.0)
    return jnp.where(c > chunk, jnp.where(c < n, _NEG_INF, 0.0), 0.0)


def _diff_proj_body(x_ref, g_ref, w_ref, cos_ref, slo_ref, shi_ref, *out_refs, qscale, qt_cols):
    tm, d = x_ref.shape
    if qt_cols:
        qm_ref, kf_ref, km_ref, vf_ref, vt_ref = out_refs
        row0 = pl.program_id(1) * tm
        q_ext = _mask_lanes(tm, row0, tm, _A_HD, False)
        k_ext = _mask_lanes(tm, row0, tm, _A_HD, True).astype(_BF)
        ones = jnp.ones((_BF_ROWS, tm), _BF)
    else:
        qm_ref, kf_ref, vf_ref = out_refs
    h = _rms(x_ref[...], g_ref[...], _NORM_EPS).astype(_BF)
    cos, slo, shi = cos_ref[...], slo_ref[...], shi_ref[...]
    first_half = lax.broadcasted_iota(jnp.int32, (tm, _LANES), 1) < _A_HD
    zero = jnp.zeros((tm, _LANES), _F32)
    cw = _MXU_COLS
    for c in range(0, d, cw):
        yq = jnp.dot(h, w_ref[:, c:c + cw], preferred_element_type=_F32)
        yk = jnp.dot(h, w_ref[:, d + c:d + c + cw], preferred_element_type=_F32)
        yv = jnp.dot(h, w_ref[:, 2 * d + c:2 * d + c + cw], preferred_element_type=_F32)
        vf_ref[:, c:c + cw] = yv
        for s in range(0, cw, _LANES):
            cols = slice(c + s, c + s + _LANES)
            head = (c + s) // _LANES
            q = _rope_lanes(yq[:, s:s + _LANES], cos, slo, shi) * qscale
            k = _rope_lanes(yk[:, s:s + _LANES], cos, slo, shi)
            kf_ref[:, cols] = k
            if not qt_cols:
                qm_ref[0, :, cols] = jnp.where(first_half, q, zero).astype(_BF)
                qm_ref[1, :, cols] = jnp.where(first_half, zero, q).astype(_BF)
                continue
            for mp in range(2):
                qh = q if mp == 0 else pltpu.roll(q, _A_HD, axis=1)
                kh = k if mp == 0 else pltpu.roll(k, _A_HD, axis=1)
                qt = jnp.where(first_half, qh, q_ext).T.astype(_BF)
                for g in range(tm // qt_cols):
                    qm_ref[mp, head, g] = qt[:, g * qt_cols:(g + 1) * qt_cols]
                k0 = jnp.where(first_half, kh, zero).astype(_BF)
                km_ref[mp, 0, :, cols] = k0
                km_ref[mp, 1, :, cols] = k0 + k_ext
            vt_ref[head, 0, :_LANES, :] = yv[:, s:s + _LANES].T.astype(_BF)
            vt_ref[head, 0, _LANES:, :] = ones


def _diff_proj(x, g, w_qkv, tabs, tm, qt_cols=0):
    b, s, d = x.shape
    heads = d // _LANES
    row = lambda dt: jax.ShapeDtypeStruct((b, s, d), dt)
    xspec = pl.BlockSpec((None, tm, d), lambda i, t: (i, t, 0))
    tspec = pl.BlockSpec((tm, _LANES), lambda i, t: (t, 0))
    if qt_cols:
        assert tm % qt_cols == 0
        dve = _LANES + _BF_ROWS
        out_specs = [pl.BlockSpec((None, 2, heads, tm // qt_cols, _LANES, qt_cols),
                                  lambda i, t: (i, 0, 0, t, 0, 0)),
                     xspec, pl.BlockSpec((None, 2, 2, tm, d), lambda i, t: (i, 0, 0, t, 0)), xspec,
                     pl.BlockSpec((None, heads, 1, dve, tm), lambda i, t: (i, 0, t, 0, 0))]
        out_shape = [jax.ShapeDtypeStruct((b, 2, heads, s // qt_cols, _LANES, qt_cols), _BF),
                     row(_F32), jax.ShapeDtypeStruct((b, 2, 2, s, d), _BF), row(_F32),
                     jax.ShapeDtypeStruct((b, heads, s // tm, dve, tm), _BF)]
    else:
        out_specs = [pl.BlockSpec((None, 2, tm, d), lambda i, t: (i, 0, t, 0)), xspec, xspec]
        out_shape = [jax.ShapeDtypeStruct((b, 2, s, d), _BF), row(_F32), row(_F32)]
    return pl.pallas_call(
        functools.partial(_diff_proj_body, qscale=_LOG2E / math.sqrt(_A_HD), qt_cols=qt_cols),
        grid=(b, s // tm),
        in_specs=[xspec, _resident((1, d)), _resident(w_qkv.shape), tspec, tspec, tspec],
        out_specs=out_specs,
        out_shape=out_shape,
        compiler_params=_params(("parallel", "parallel")),
        name="diff_proj",
    )(x, g.reshape(1, d), w_qkv, *tabs)


def _cached_diff_body(q_ref, kc_ref, kn_ref, vc_ref, vn_ref, lam_ref, sg_ref, o_ref, *, lam_init):
    nt = (((1,), (1,)), ((), ()))
    lp = lam_ref[...]
    lam = (jnp.exp(jnp.sum(lp[0:1] * lp[1:2], axis=-1, keepdims=True))
           - jnp.exp(jnp.sum(lp[2:3] * lp[3:4], axis=-1, keepdims=True)) + lam_init)
    for hd in range(o_ref.shape[-1] // _LANES):
        cols = slice(hd * _LANES, (hd + 1) * _LANES)
        kc, kn = kc_ref[:, cols].astype(_BF), kn_ref[:, cols].astype(_BF)
        vc, vn = vc_ref[:, cols].astype(_BF), vn_ref[:, cols].astype(_BF)
        outs = []
        for c in range(2):
            q = q_ref[c, :, cols]
            sc = lax.dot_general(q, kc, nt, preferred_element_type=_F32)
            sn = lax.dot_general(q, kn, nt, preferred_element_type=_F32)
            m = jnp.maximum(jnp.max(sc, axis=-1, keepdims=True), jnp.max(sn, axis=-1, keepdims=True))
            pc, pn = jnp.exp2(sc - m), jnp.exp2(sn - m)
            l = jnp.sum(pc, axis=-1, keepdims=True) + jnp.sum(pn, axis=-1, keepdims=True)
            acc = (jnp.dot(pc.astype(_BF), vc, preferred_element_type=_F32)
                   + jnp.dot(pn.astype(_BF), vn, preferred_element_type=_F32))
            outs.append(acc / l)
        o = _rms(outs[0] - lam * outs[1], sg_ref[...], _A_SUBLN_EPS) * (1.0 - lam_init)
        o_ref[:, cols] = o.astype(o_ref.dtype)


def _cached_diff(qm, k_cache, k_new, v_cache, v_new, lam, subln_g, lam_init):
    b, _, t, d = qm.shape
    past = k_cache.shape[1]
    blk = lambda rows: pl.BlockSpec((None, rows, d), lambda i: (i, 0, 0))
    return pl.pallas_call(
        functools.partial(_cached_diff_body, lam_init=lam_init),
        grid=(b,),
        in_specs=[pl.BlockSpec((None, 2, t, d), lambda i: (i, 0, 0, 0)),
                  blk(past), blk(t), blk(past), blk(t), _resident(lam.shape), _resident((1, _LANES))],
        out_specs=blk(t),
        out_shape=jax.ShapeDtypeStruct((b, t, d), _BF),
        compiler_params=_params(("parallel",)),
        name="cached_diff",
    )(qm, k_cache, k_new, v_cache, v_new, lam, subln_g.reshape(1, _LANES))


def _cached_mla_body(q_ref, k_ref, v_ref, o_ref):
    s = lax.dot_general(q_ref[...], k_ref[...], (((1,), (1,)), ((), ())),
                        preferred_element_type=_F32)
    p = jnp.exp2(s - jnp.max(s, axis=-1, keepdims=True))
    acc = jnp.dot(p.astype(_BF), v_ref[...], preferred_element_type=_F32)
    o_ref[...] = (acc / jnp.sum(p, axis=-1, keepdims=True)).astype(o_ref.dtype)


def _cached_mla(q, k, v, heads, dq, dv):
    b, t, _ = q.shape
    sk = k.shape[1]
    return pl.pallas_call(
        _cached_mla_body,
        grid=(b, heads),
        in_specs=[pl.BlockSpec((None, t, dq), lambda i, h: (i, 0, h)),
                  pl.BlockSpec((None, sk, dq), lambda i, h: (i, 0, h)),
                  pl.BlockSpec((None, sk, dv), lambda i, h: (i, 0, h))],
        out_specs=pl.BlockSpec((None, t, dv), lambda i, h: (i, 0, h)),
        out_shape=jax.ShapeDtypeStruct((b, t, heads * dv), _BF),
        compiler_params=_params(("parallel", "parallel")),
        name="cached_mla",
    )(q, k, v)


def _sweep_body(*refs, nmaps, halves, tile, nb, lam_init):
    if nmaps == 2:
        q_ref, k_ref, vt_ref, lam_ref, sg_ref, o_ref, s_buf, mb_buf, m_sc, acc_sc = refs
    else:
        q_ref, k_ref, vt_ref, o_ref, s_buf, mb_buf, m_sc, acc_sc = refs
    hw = tile // halves
    nch = nmaps * halves
    dv = o_ref.shape[-1]
    ticks = acc_sc.shape[0]
    npairs = nb * (nb + 1) // 2
    acc_sc[...] = jnp.zeros(acc_sc.shape, _F32)

    def nxt(pair):
        qi, t = pair
        last = t == qi
        return jnp.where(last, qi + 1, qi), jnp.where(last, 0, t + 1)

    def scores(pair, slot):
        qi, t = pair
        variant = (t == qi).astype(jnp.int32)
        for mp in range(nmaps):
            k = k_ref[mp, variant, pl.ds(pl.multiple_of(t * tile, tile), tile), :]
            for hf in range(halves):
                c = mp * halves + hf
                s = jnp.dot(k, q_ref[mp, qi * halves + hf], preferred_element_type=_F32)
                s_buf[slot, c] = s
                mb_buf[slot, c] = jnp.max(s, axis=0, keepdims=True)

    def values(pair, pos):
        _, t = pair
        first = t == 0
        slot = pos % _SWEEP_SLOTS
        vt = vt_ref[t]
        for c in range(nch):
            m_prev = jnp.where(first, _NEG_INF, m_sc[c])
            m_new = jnp.maximum(m_prev, mb_buf[slot, c])
            alpha = jnp.exp2(m_prev - m_new)
            m_sc[c] = m_new
            p = jnp.exp2(s_buf[slot, c] - m_new).astype(_BF)
            acc_sc[pos, c] = (alpha * acc_sc[(pos - 1) % ticks, c]
                              + jnp.dot(vt, p, preferred_element_type=_F32))

    def finish(pair, pos):
        qi, t = pair

        @pl.when(t == qi)
        def _():
            if nmaps == 2:
                lp = lam_ref[...]
                lam = (jnp.exp(jnp.sum(lp[0:1] * lp[1:2], axis=-1, keepdims=True))
                       - jnp.exp(jnp.sum(lp[2:3] * lp[3:4], axis=-1, keepdims=True)) + lam_init)
            normed = lambda c: acc_sc[pos, c, :dv, :] / acc_sc[pos, c, dv:dv + 1, :]
            for hf in range(halves):
                if nmaps == 2:
                    ot = normed(hf) - lam * normed(halves + hf)
                    o = _rms(ot.T, sg_ref[...], _A_SUBLN_EPS) * (1.0 - lam_init)
                else:
                    o = normed(hf).T
                row = pl.multiple_of(qi * tile + hf * hw, hw)
                o_ref[pl.ds(row, hw), :] = o.astype(o_ref.dtype)

    zero = jnp.int32(0)
    pair0 = (zero, zero)
    scores(pair0, 0)
    if npairs == 1:
        values(pair0, 0)
        finish(pair0, 0)
        return
    pair1 = nxt(pair0)
    scores(pair1, 1)

    def tick(carry, r):
        pc, pb, pa = carry
        scores(pa, (2 + r) % _SWEEP_SLOTS)
        values(pc, r)
        return pb, pa, nxt(pa)

    def run_ticks(n, carry):
        done = []
        for r in range(n):
            done.append(carry[0])
            carry = tick(carry, r)
        for r, pair in enumerate(done):
            finish(pair, r)
        return carry

    nticks = npairs - 2
    carry = lax.fori_loop(0, nticks // ticks, lambda _, c: run_ticks(ticks, c),
                          (pair0, pair1, nxt(pair1)))
    carry = run_ticks(nticks % ticks, carry)
    before_last, last, _ = carry
    for pair, pos in ((before_last, (npairs - 2) % ticks), (last, (npairs - 1) % ticks)):
        values(pair, pos)
        finish(pair, pos)


def _sweep(q, k, vt, heads, dq, dv, *, halves, trip_ticks, lam=None, subln_g=None, lam_init=0.0):
    b, nmaps = q.shape[:2]
    _, _, nb, dve, tile = vt.shape
    s = nb * tile
    assert tile % _CHUNK == 0 and tile % halves == 0 and trip_ticks % _SWEEP_SLOTS == 0
    assert q.shape[2:] == (heads, s * halves // tile, dq, tile // halves) and dve == dv + _BF_ROWS
    nch = nmaps * halves
    hw = tile // halves
    once = pl.Buffered(1)
    in_specs = [pl.BlockSpec((None, nmaps, None, nb * halves, dq, hw),
                             lambda i, h: (i, 0, h, 0, 0, 0), pipeline_mode=once),
                pl.BlockSpec((None, nmaps, 2, s, dq), lambda i, h: (i, 0, 0, 0, h),
                             pipeline_mode=once),
                pl.BlockSpec((None, None, nb, dve, tile), lambda i, h: (i, h, 0, 0, 0),
                             pipeline_mode=once)]
    args = [q, k, vt]
    if nmaps == 2:
        in_specs += [_resident(lam.shape), _resident((1, dv))]
        args += [lam, subln_g.reshape(1, dv)]
    scratch = [pltpu.VMEM((_SWEEP_SLOTS, nch, tile, hw), _F32),
               pltpu.VMEM((_SWEEP_SLOTS, nch, 1, hw), _F32),
               pltpu.VMEM((nch, 1, hw), _F32), pltpu.VMEM((trip_ticks, nch, dve, hw), _F32)]
    return pl.pallas_call(
        functools.partial(_sweep_body, nmaps=nmaps, halves=halves, tile=tile, nb=nb,
                          lam_init=lam_init),
        grid=(b, heads),
        in_specs=in_specs,
        out_specs=pl.BlockSpec((None, s, dv), lambda i, h: (i, 0, h)),
        out_shape=jax.ShapeDtypeStruct((b, s, heads * dv), _BF),
        scratch_shapes=scratch,
        compiler_params=_params(("parallel", "parallel")),
        name="sweep_diff" if nmaps == 2 else "sweep_mla",
    )(*args)


def _mm_body(*refs, has_res):
    if has_res:
        a_ref, w_ref, r_ref, o_ref = refs
    else:
        a_ref, w_ref, o_ref = refs
    y = jnp.dot(a_ref[...].astype(_BF), w_ref[...], preferred_element_type=_F32)
    if has_res:
        y = r_ref[...] + y
    o_ref[...] = y.astype(o_ref.dtype)


def _mm(a, w, res=None, out_dtype=_F32):
    b, s, kdim = a.shape
    n = w.shape[1]
    tm = min(_ROW_TILE, s)
    in_specs = [pl.BlockSpec((None, tm, kdim), lambda i, t: (i, t, 0)), _resident(w.shape)]
    args = [a, w]
    if res is not None:
        in_specs.append(pl.BlockSpec((None, tm, n), lambda i, t: (i, t, 0)))
        args.append(res)
    return pl.pallas_call(
        functools.partial(_mm_body, has_res=res is not None),
        grid=(b, s // tm),
        in_specs=in_specs,
        out_specs=pl.BlockSpec((None, tm, n), lambda i, t: (i, t, 0)),
        out_shape=jax.ShapeDtypeStruct((b, s, n), out_dtype),
        compiler_params=_params(("parallel", "parallel")),
        name="matmul_residual" if res is not None else "matmul",
    )(*args)


def _mla_proj_body(x_ref, g_ref, wd_ref, qg_ref, wq_ref, kvg_ref, wk_ref, wv_ref,
                   cos_ref, slo_ref, shi_ref, lat_ref, kr_ref, q_ref, k_ref, v_ref, *, qscale, mask_tile,
                   qt_cols):
    tm = x_ref.shape[0]
    h = _rms(x_ref[...], g_ref[...], _NORM_EPS).astype(_BF)
    cos, slo, shi = cos_ref[...], slo_ref[...], shi_ref[...]
    if mask_tile:
        row0 = pl.program_id(1) * tm
        q_ext = _mask_lanes(tm, row0, mask_tile, _B_ROPE, False)
        k_ext = _mask_lanes(tm, row0, mask_tile, _B_ROPE, True)
    else:
        q_ext = k_ext = jnp.zeros((tm, _LANES), _F32)
    dn = jnp.dot(h, wd_ref[...], preferred_element_type=_F32)
    e0, e1 = _B_Q_LORA, _B_Q_LORA + _B_KV_LORA
    cq = _rms(dn[:, :e0], qg_ref[...], _NORM_EPS).astype(_BF)
    lat = _rms(dn[:, e0:e1], kvg_ref[...], _NORM_EPS)
    kr = _rope_lanes(dn[:, e1:e1 + _LANES], cos, slo, shi)
    lat_ref[...] = lat
    kr_ref[...] = kr[:, :_B_ROPE]
    hw = 2 * _LANES
    for hd in range(_B_HEADS):
        yq = jnp.dot(cq, wq_ref[:, hd * hw:(hd + 1) * hw], preferred_element_type=_F32)
        q_nope = yq[:, :_LANES] * qscale
        q_rope = _rope_lanes(yq[:, _LANES:], cos, slo, shi) * qscale + q_ext
        if qt_cols:
            qt = jnp.concatenate([q_nope.T, q_rope.T], axis=0).astype(_BF)
            for g in range(tm // qt_cols):
                q_ref[hd, g] = qt[:, g * qt_cols:(g + 1) * qt_cols]
        else:
            q_ref[:, hd * hw:hd * hw + _LANES] = q_nope.astype(_BF)
            q_ref[:, hd * hw + _LANES:(hd + 1) * hw] = q_rope.astype(_BF)
    kin = jnp.concatenate([lat, kr], axis=1).astype(_BF)
    for hd in range(_B_HEADS):
        yk = jnp.dot(kin, wk_ref[:, hd * hw:(hd + 1) * hw], preferred_element_type=_F32)
        for var in range(2):
            k_ref[var, :, hd * hw:hd * hw + _LANES] = yk[:, :_LANES].astype(_BF)
        k_ref[0, :, hd * hw + _LANES:(hd + 1) * hw] = yk[:, _LANES:].astype(_BF)
        k_ref[1, :, hd * hw + _LANES:(hd + 1) * hw] = (yk[:, _LANES:] + k_ext).astype(_BF)
    latb = lat.astype(_BF)
    for c in range(0, wv_ref.shape[-1], _MXU_COLS):
        yv = jnp.dot(latb, wv_ref[:, c:c + _MXU_COLS], preferred_element_type=_F32)
        if qt_cols:
            for s in range(0, _MXU_COLS, _B_VD):
                v_ref[(c + s) // _B_VD, 0, :_B_VD, :] = yv[:, s:s + _B_VD].T.astype(_BF)
                v_ref[(c + s) // _B_VD, 0, _B_VD:, :] = jnp.ones((_BF_ROWS, tm), _BF)
        else:
            v_ref[:, c:c + _MXU_COLS] = yv.astype(_BF)


def _mla_weights(w_down, w_uq, w_uk, w_uv):
    d = w_down.shape[0]
    hw = 2 * _LANES
    wd = jnp.concatenate([w_down, jnp.zeros((d, _LANES - _B_ROPE), w_down.dtype)], axis=1)
    wq = w_uq.reshape(_B_Q_LORA, _B_HEADS, _B_NOPE + _B_ROPE)
    wq = jnp.pad(wq, ((0, 0), (0, 0), (0, hw - _B_NOPE - _B_ROPE))).reshape(_B_Q_LORA, _B_HEADS * hw)
    eye = jnp.eye(_B_ROPE, dtype=w_uk.dtype)
    wk_top = jnp.pad(w_uk, ((0, 0), (0, 0), (0, hw - _B_NOPE)))
    wk_mid = jnp.broadcast_to(jnp.pad(eye, ((0, 0), (_B_NOPE, hw - _B_NOPE - _B_ROPE)))[:, None, :],
                              (_B_ROPE, _B_HEADS, hw))
    wk_bot = jnp.zeros((_LANES - _B_ROPE, _B_HEADS, hw), w_uk.dtype)
    wk = jnp.concatenate([wk_top, wk_mid, wk_bot], axis=0).reshape(-1, _B_HEADS * hw)
    wv = w_uv.reshape(_B_KV_LORA, _B_HEADS * _B_VD)
    return wd.astype(_BF), wq.astype(_BF), wk.astype(_BF), wv.astype(_BF)


def _mla_proj(x, g, wts, q_norm_g, kv_norm_g, tabs, tm, mask_tile, qt_cols=0):
    b, s, d = x.shape
    wd, wq, wk, wv = wts
    xspec = pl.BlockSpec((None, tm, d), lambda i, t: (i, t, 0))
    tspec = pl.BlockSpec((tm, _LANES), lambda i, t: (t, 0))
    ospec = lambda n: pl.BlockSpec((None, tm, n), lambda i, t: (i, t, 0))
    oshape = lambda n, dt: jax.ShapeDtypeStruct((b, s, n), dt)
    if qt_cols:
        assert tm % qt_cols == 0
        hq = wq.shape[1] // _B_HEADS
        q_spec = pl.BlockSpec((None, _B_HEADS, tm // qt_cols, hq, qt_cols),
                              lambda i, t: (i, 0, t, 0, 0))
        q_shape = jax.ShapeDtypeStruct((b, _B_HEADS, s // qt_cols, hq, qt_cols), _BF)
        dve = _B_VD + _BF_ROWS
        v_spec = pl.BlockSpec((None, _B_HEADS, 1, dve, tm), lambda i, t: (i, 0, t, 0, 0))
        v_shape = jax.ShapeDtypeStruct((b, _B_HEADS, s // tm, dve, tm), _BF)
    else:
        q_spec, q_shape = ospec(wq.shape[1]), oshape(wq.shape[1], _BF)
        v_spec, v_shape = ospec(wv.shape[1]), oshape(wv.shape[1], _BF)
    return pl.pallas_call(
        functools.partial(_mla_proj_body, qscale=_LOG2E / math.sqrt(_B_NOPE + _B_ROPE),
                          mask_tile=mask_tile, qt_cols=qt_cols),
        grid=(b, s // tm),
        in_specs=[xspec, _resident((1, d)), _resident(wd.shape), _resident((1, _B_Q_LORA)),
                  _resident(wq.shape), _resident((1, _B_KV_LORA)), _resident(wk.shape),
                  _resident(wv.shape), tspec, tspec, tspec],
        out_specs=[ospec(_B_KV_LORA), ospec(_B_ROPE), q_spec,
                   pl.BlockSpec((None, 2, tm, wk.shape[1]), lambda i, t: (i, 0, t, 0)), v_spec],
        out_shape=[oshape(_B_KV_LORA, _F32), oshape(_B_ROPE, _F32), q_shape,
                   jax.ShapeDtypeStruct((b, 2, s, wk.shape[1]), _BF), v_shape],
        compiler_params=_params(("parallel", "parallel")),
        name="mla_proj",
    )(x, g.reshape(1, d), wd, q_norm_g.reshape(1, -1), wq, kv_norm_g.reshape(1, -1), wk, wv, *tabs)


def _delayed(carry_ref, cols, cur, shifts):
    nseg = carry_ref.shape[0]
    seg = cur.shape[0] // nseg
    outs = [[] for _ in shifts]
    for sg in range(nseg):
        part = cur[sg * seg:(sg + 1) * seg]
        prev = carry_ref[sg, :, cols]
        for o, k in zip(outs, shifts):
            o.append(_shift_rows(prev, part, k))
        carry_ref[sg, :, cols] = part[seg - _SUBLANES:]
    return [o[0] if nseg == 1 else jnp.concatenate(o, axis=0) for o in outs]


def _conv_mix_body(x_ref, g_ref, win_ref, cw_ref, wout_ref, hist_ref, o_ref, hout_ref,
                   carry_sc, z_sc):
    tm, d = x_ref.shape

    @pl.when(pl.program_id(1) == 0)
    def _():
        carry_sc[...] = hist_ref[...]

    x = x_ref[...]
    h = _rms(x, g_ref[...], _NORM_EPS).astype(_BF)
    cw = _MXU_COLS
    for c in range(0, d, cw):
        gate_b = jnp.dot(h, win_ref[:, c:c + cw], preferred_element_type=_F32)
        gate_c = jnp.dot(h, win_ref[:, d + c:d + c + cw], preferred_element_type=_F32)
        val = jnp.dot(h, win_ref[:, 2 * d + c:2 * d + c + cw], preferred_element_type=_F32)
        u = gate_c * val
        u2, u1 = _delayed(carry_sc, slice(c, c + cw), u, (2, 1))
        y = u2 * cw_ref[0:1, c:c + cw] + u1 * cw_ref[1:2, c:c + cw] + u * cw_ref[2:3, c:c + cw]
        z_sc[:, c:c + cw] = (gate_b * y).astype(_BF)
    hout_ref[...] = carry_sc[...]
    o_ref[...] = x + jnp.dot(z_sc[...], wout_ref[...], preferred_element_type=_F32)


def _conv_mix(x, g, w_in, conv_w, w_out, hist, tm):
    b, s, d = x.shape
    nseg = hist.shape[1]
    assert nseg == 1 or tm == s
    xspec = pl.BlockSpec((None, tm, d), lambda i, t: (i, t, 0))
    hspec = pl.BlockSpec((None, nseg, _SUBLANES, d), lambda i, t: (i, 0, 0, 0))
    return pl.pallas_call(
        _conv_mix_body,
        grid=(b, s // tm),
        in_specs=[xspec, _resident((1, d)), _resident(w_in.shape), _resident(conv_w.shape),
                  _resident(w_out.shape), hspec],
        out_specs=[xspec, hspec],
        out_shape=[jax.ShapeDtypeStruct((b, s, d), _F32),
                   jax.ShapeDtypeStruct((b, nseg, _SUBLANES, d), _F32)],
        scratch_shapes=[pltpu.VMEM((nseg, _SUBLANES, d), _F32), pltpu.VMEM((tm, d), _BF)],
        compiler_params=_params(("arbitrary", "arbitrary")),
        name="conv_mixer",
    )(x, g.reshape(1, d), w_in, conv_w, w_out, hist)


_POOL_HALO = 16


def _pool_mix_body(x_ref, g_ref, wg_ref, sc_ref, hist_ref, o_ref, hout_ref, carry_sc, *, hist_valid):
    tm, d = x_ref.shape
    t = pl.program_id(1)

    @pl.when(t == 0)
    def _():
        carry_sc[...] = hist_ref[...]

    x = x_ref[...]
    h = _rms(x, g_ref[...], _NORM_EPS)
    gw = d // len(_D_WINDOWS)
    tpos = t * tm + lax.broadcasted_iota(jnp.int32, (tm, gw), 0)
    outs = []
    for gi, w in enumerate(_D_WINDOWS):
        hg = h[:, gi * gw:(gi + 1) * gw]
        acc = jnp.concatenate([carry_sc[:, gi * gw:(gi + 1) * gw], hg], axis=0)
        k = 1
        while k < w:
            acc = acc + pltpu.roll(acc, k, axis=0)
            k *= 2
        cnt = jnp.minimum(tpos + (hist_valid + 1), w).astype(_F32)
        pooled = acc[_POOL_HALO:] / cnt
        outs.append(jnp.dot((pooled - hg).astype(_BF), wg_ref[gi], preferred_element_type=_F32))
    o_ref[...] = x + jnp.concatenate(outs, axis=1) * sc_ref[...]
    tail = h[tm - _POOL_HALO:, :]
    carry_sc[...] = tail
    hout_ref[...] = tail


def _pool_mix(x, g, w_group, scale, hist16, hist_valid, tm):
    b, s, d = x.shape
    assert all(w & (w - 1) == 0 and w <= _POOL_HALO for w in _D_WINDOWS)
    xspec = pl.BlockSpec((None, tm, d), lambda i, t: (i, t, 0))
    hspec = pl.BlockSpec((None, _POOL_HALO, d), lambda i, t: (i, 0, 0))
    return pl.pallas_call(
        functools.partial(_pool_mix_body, hist_valid=hist_valid),
        grid=(b, s // tm),
        in_specs=[xspec, _resident((1, d)), _resident(w_group.shape), _resident((1, d)), hspec],
        out_specs=[xspec, hspec],
        out_shape=[jax.ShapeDtypeStruct((b, s, d), _F32),
                   jax.ShapeDtypeStruct((b, _POOL_HALO, d), _F32)],
        scratch_shapes=[pltpu.VMEM((_POOL_HALO, d), _F32)],
        compiler_params=_params(("arbitrary", "arbitrary")),
        name="pool_mixer",
    )(x, g.reshape(1, d), w_group, scale.reshape(1, d), hist16)


def _ffn_body(*refs, final, mixed):
    refs = list(refs)
    x_ref, g_ref, wg_ref, wu_ref, cw_ref, cb_ref, wd_ref, hist_ref = refs[:8]
    del refs[:8]
    a_ref, wo_ref = (refs.pop(0), refs.pop(0)) if mixed else (None, None)
    fg_ref = refs.pop(0) if final else None
    o_ref, hout_ref, carry_sc, act_sc = refs
    tm = x_ref.shape[0]
    f = wg_ref.shape[1]

    @pl.when(pl.program_id(1) == 0)
    def _():
        carry_sc[...] = hist_ref[...]

    x = x_ref[...]
    if mixed:
        x = x + jnp.dot(a_ref[...], wo_ref[...], preferred_element_type=_F32)
    h = _rms(x, g_ref[...], _NORM_EPS).astype(_BF)
    cw = _MXU_COLS
    for c in range(0, f, cw):
        gate = jnp.dot(h, wg_ref[:, c:c + cw], preferred_element_type=_F32)
        up = jnp.dot(h, wu_ref[:, c:c + cw], preferred_element_type=_F32)
        g2, g1 = _delayed(carry_sc, slice(c, c + cw), gate, (2, 1))
        y = (g2 * cw_ref[0:1, c:c + cw] + g1 * cw_ref[1:2, c:c + cw]
             + gate * cw_ref[2:3, c:c + cw] + cb_ref[:, c:c + cw])
        act_sc[:, c:c + cw] = (y / (1.0 + jnp.exp(-y)) * up).astype(_BF)
    hout_ref[...] = carry_sc[...]
    out = x + jnp.dot(act_sc[...], wd_ref[...], preferred_element_type=_F32)
    if final:
        out = _rms(out, fg_ref[...], _NORM_EPS)
    o_ref[...] = out


def _ffn(x, g, w_gate, w_up, conv_w, conv_b, w_down, hist, final_g, tm, mixer_out=None, w_o=None):
    b, s, d = x.shape
    f = w_gate.shape[1]
    nseg = hist.shape[1]
    assert f % _MXU_COLS == 0 and (nseg == 1 or tm == s)
    xspec = pl.BlockSpec((None, tm, d), lambda i, t: (i, t, 0))
    hspec = pl.BlockSpec((None, nseg, _SUBLANES, f), lambda i, t: (i, 0, 0, 0))
    in_specs = [xspec, _resident((1, d)), _resident(w_gate.shape), _resident(w_up.shape),
                _resident(conv_w.shape), _resident((1, f)), _resident(w_down.shape), hspec]
    args = [x, g.reshape(1, d), w_gate, w_up, conv_w, conv_b.reshape(1, f), w_down, hist]
    if mixer_out is not None:
        in_specs += [pl.BlockSpec((None, tm, mixer_out.shape[-1]), lambda i, t: (i, t, 0)),
                     _resident(w_o.shape)]
        args += [mixer_out, w_o]
    if final_g is not None:
        in_specs.append(_resident((1, d)))
        args.append(final_g.reshape(1, d))
    return pl.pallas_call(
        functools.partial(_ffn_body, final=final_g is not None, mixed=mixer_out is not None),
        grid=(b, s // tm),
        in_specs=in_specs,
        out_specs=[xspec, hspec],
        out_shape=[jax.ShapeDtypeStruct((b, s, d), _F32),
                   jax.ShapeDtypeStruct((b, nseg, _SUBLANES, f), _F32)],
        scratch_shapes=[pltpu.VMEM((nseg, _SUBLANES, f), _F32), pltpu.VMEM((tm, f), _BF)],
        compiler_params=_params(("arbitrary", "arbitrary")),
        name="conv_ffn",
    )(*args)


def _pad_hist(hist, rows):
    return jnp.pad(hist, ((0, 0), (rows - hist.shape[1], 0), (0, 0)))


def kernel(x_prompt, x_sample, cache_a_k, cache_a_v, cache_b_latent, cache_b_krope, state_c_conv, state_d_pool, state_ffn_conv, norm_mix_g, norm_ffn_g, norm_final_g, a_w_qkv, a_lam, a_subln_g, a_w_o, b_w_down, b_q_norm_g, b_w_uq, b_kv_norm_g, b_w_uk, b_w_uv, b_w_o, c_w_in, c_conv_w, c_w_out, d_w_group, d_scale, ffn_w_gate, ffn_w_up, ffn_conv_w, ffn_conv_b, ffn_w_down):
    depth = norm_mix_g.shape[0]
    n_p, seq, d = x_prompt.shape
    n_s, t_new, _ = x_sample.shape
    past = cache_a_k.shape[2]
    f = ffn_w_gate.shape[-1]
    tm_p, tm_s = min(_ROW_TILE, seq), t_new
    ta = min(_ATT_TILE, seq)
    tabs_p = _rope_tables(jnp.arange(seq, dtype=jnp.int32))
    tabs_s = _rope_tables(past + jnp.arange(t_new, dtype=jnp.int32))
    xp, xs = x_prompt, x_sample
    flat = lambda a: a.reshape(1, n_s * t_new, a.shape[-1])
    outs = {k: [] for k in ("ak_p", "av_p", "bl_p", "br_p", "cc_p", "dp_p", "fc_p",
                            "ak_s", "av_s", "bl_s", "br_s", "cc_s", "dp_s", "fc_s")}
    for i in range(depth):
        m, j = i % _N_MIXERS, i // _N_MIXERS
        g_mix = norm_mix_g[i]
        mix_p, mix_s = {}, {}
        if m == 0:
            lam_init = 0.8 - 0.6 * math.exp(-0.3 * i)
            w_qkv, w_o = a_w_qkv[j].astype(_BF), a_w_o[j].astype(_BF)
            qt, kf, km, vf, vt = _diff_proj(xp, g_mix, w_qkv, tabs_p, ta, ta // _SWEEP_HALVES)
            op = _sweep(qt, km, vt, _A_HEADS, _LANES, 2 * _A_HD, halves=_SWEEP_HALVES,
                        trip_ticks=_TRIP_TICKS_DIFF,
                        lam=a_lam[j], subln_g=a_subln_g[j], lam_init=lam_init)
            mix_p = dict(mixer_out=op, w_o=w_o)
            outs["ak_p"].append(kf.reshape(n_p, seq, _A_HEADS, 2, _A_HD))
            outs["av_p"].append(vf.reshape(n_p, seq, _A_HEADS, 2 * _A_HD))
            qm, kf, vf = _diff_proj(xs, g_mix, w_qkv, tabs_s, tm_s)
            osm = _cached_diff(qm, cache_a_k[j].reshape(n_s, past, d), kf,
                               cache_a_v[j].reshape(n_s, past, d), vf,
                               a_lam[j], a_subln_g[j], lam_init)
            mix_s = dict(mixer_out=flat(osm), w_o=w_o)
            outs["ak_s"].append(kf.reshape(n_s, t_new, _A_HEADS, 2, _A_HD))
            outs["av_s"].append(vf.reshape(n_s, t_new, _A_HEADS, 2 * _A_HD))
        elif m == 1:
            wts = _mla_weights(b_w_down[j], b_w_uq[j], b_w_uk[j], b_w_uv[j])
            w_o = b_w_o[j].astype(_BF)
            hq = 2 * _LANES
            lat, kr, qt, k2, vt = _mla_proj(xp, g_mix, wts, b_q_norm_g[j], b_kv_norm_g[j], tabs_p,
                                            ta, ta, ta // _SWEEP_HALVES)
            op = _sweep(qt[:, None], k2[:, None], vt, _B_HEADS, hq, _B_VD,
                        halves=_SWEEP_HALVES, trip_ticks=_TRIP_TICKS_MLA)
            mix_p = dict(mixer_out=op, w_o=w_o)
            outs["bl_p"].append(lat)
            outs["br_p"].append(kr)
            lat, kr, q, k2, v = _mla_proj(xs, g_mix, wts, b_q_norm_g[j], b_kv_norm_g[j], tabs_s,
                                          tm_s, 0)
            kin = jnp.concatenate([cache_b_latent[j], cache_b_krope[j],
                                   jnp.zeros((n_s, past, _LANES - _B_ROPE), _F32)], axis=-1).astype(_BF)
            k_cache = _mm(kin, wts[2], out_dtype=_BF)
            v_cache = _mm(cache_b_latent[j].astype(_BF), wts[3], out_dtype=_BF)
            k_all = jnp.concatenate([k_cache, k2[:, 0]], axis=1)
            v_all = jnp.concatenate([v_cache, v], axis=1)
            osm = _cached_mla(q, k_all, v_all, _B_HEADS, hq, _B_VD)
            mix_s = dict(mixer_out=flat(osm), w_o=w_o)
            outs["bl_s"].append(lat)
            outs["br_s"].append(kr)
        elif m == 2:
            w_in, w_out = c_w_in[j].astype(_BF), c_w_out[j].astype(_BF)
            xp, hc = _conv_mix(xp, g_mix, w_in, c_conv_w[j], w_out,
                               jnp.zeros((n_p, 1, _SUBLANES, d), _F32), tm_p)
            outs["cc_p"].append(hc[:, 0, -2:])
            xs, hc = _conv_mix(flat(xs), g_mix, w_in, c_conv_w[j], w_out,
                               _pad_hist(state_c_conv[j], _SUBLANES)[None], n_s * t_new)
            xs = xs.reshape(n_s, t_new, d)
            outs["cc_s"].append(hc[0, :, -2:])
        else:
            w_grp = d_w_group[j].astype(_BF)
            xp, hd = _pool_mix(xp, g_mix, w_grp, d_scale[j],
                               jnp.zeros((n_p, _POOL_HALO, d), _F32), 0, tm_p)
            outs["dp_p"].append(hd[:, -_D_HIST:])
            xs, hd = _pool_mix(xs, g_mix, w_grp, d_scale[j],
                               _pad_hist(state_d_pool[j], _POOL_HALO), _D_HIST, tm_s)
            outs["dp_s"].append(hd[:, -_D_HIST:])
        wg, wu, wd = ffn_w_gate[i].astype(_BF), ffn_w_up[i].astype(_BF), ffn_w_down[i].astype(_BF)
        final_g = norm_final_g if i == depth - 1 else None
        xp, hf = _ffn(xp, norm_ffn_g[i], wg, wu, ffn_conv_w[i], ffn_conv_b[i], wd,
                      jnp.zeros((n_p, 1, _SUBLANES, f), _F32), final_g, tm_p, **mix_p)
        outs["fc_p"].append(hf[:, 0, -2:])
        xs, hf = _ffn(flat(xs), norm_ffn_g[i], wg, wu, ffn_conv_w[i], ffn_conv_b[i], wd,
                      _pad_hist(state_ffn_conv[i], _SUBLANES)[None], final_g, n_s * t_new, **mix_s)
        xs = xs.reshape(n_s, t_new, d)
        outs["fc_s"].append(hf[0, :, -2:])
    st = lambda k: jnp.stack(outs[k])
    return (xp, xs, st("ak_p"), st("av_p"), st("bl_p"), st("br_p"), st("cc_p"), st("dp_p"), st("fc_p"),
            st("ak_s"), st("av_s"), st("bl_s"), st("br_s"), st("cc_s"), st("dp_s"), st("fc_s"))
```

```python
import functools
import math

import jax
import jax.numpy as jnp
from jax import lax
from jax.experimental import pallas as pl
from jax.experimental.pallas import tpu as pltpu

_BF = jnp.bfloat16
_F32 = jnp.float32

_CHUNK = 64
_ROPE_THETA = 10000.0
_NORM_EPS = 1e-6
_NEG_INF = -1e30
_A_HEADS = 8
_A_HD = 64
_A_SUBLN_EPS = 1e-5
_B_HEADS = 8
_B_NOPE = 128
_B_ROPE = 64
_B_VD = 128
_B_Q_LORA = 384
_B_KV_LORA = 256
_D_WINDOWS = (2, 4, 8, 16)
_D_HIST = 15
_N_MIXERS = 4
_LOG2E = math.log2(math.e)

_LANES = 128
_SUBLANES = 8
_MXU_COLS = 256
_BF_ROWS = 16
_SWEEP_SLOTS = 4
_TRIP_TICKS_DIFF = 8
_TRIP_TICKS_MLA = 16
_SWEEP_HALVES = 1
_VMEM_LIMIT = 56 * 1024 * 1024

_ROW_TILE = 1024
_ATT_TILE = 512


def _params(sem):
    return pltpu.CompilerParams(dimension_semantics=sem, vmem_limit_bytes=_VMEM_LIMIT)


def _resident(shape):
    nd = len(shape)
    return pl.BlockSpec(shape, lambda *_: (0,) * nd, pipeline_mode=pl.Buffered(1))


def _rms(x, g, eps):
    ms = jnp.mean(x * x, axis=-1, keepdims=True)
    return x * lax.rsqrt(ms + eps) * g


def _rope_lanes(y, cos, sin_lo, sin_hi):
    return (y * cos + pltpu.roll(y, _LANES - 32, axis=1) * sin_lo
            + pltpu.roll(y, 32, axis=1) * sin_hi)


def _rope_tables(pos):
    d = _A_HD
    inv = jnp.power(_ROPE_THETA, -jnp.arange(0, d, 2, dtype=_F32) / d)
    ang = pos.astype(_F32)[:, None] * inv[None, :]
    cos, sin = jnp.cos(ang), jnp.sin(ang)
    zero = jnp.zeros_like(sin)
    cos_t = jnp.tile(cos, (1, 4))
    sin_lo = jnp.tile(jnp.concatenate([-sin, zero], axis=1), (1, 2))
    sin_hi = jnp.tile(jnp.concatenate([zero, sin], axis=1), (1, 2))
    return cos_t, sin_lo, sin_hi


def _shift_rows(carry, cur, k):
    ext = jnp.concatenate([carry, cur], axis=0)
    return pltpu.roll(ext, k, axis=0)[carry.shape[0]:]


def _mask_lanes(rows, row0, tile, base, for_keys):
    r = row0 + lax.broadcasted_iota(jnp.int32, (rows, _LANES), 0)
    chunk = (r % tile) // _CHUNK
    c = lax.broadcasted_iota(jnp.int32, (rows, _LANES), 1) - base
    n = tile // _CHUNK
    if for_keys:
        return jnp.where(c == chunk, 1.0, 0.0)
    return jnp.where(c > chunk, jnp.where(c < n, _NEG_INF, 0.0), 0.0)


def _diff_proj_body(x_ref, g_ref, w_ref, cos_ref, slo_ref, shi_ref, *out_refs, qscale, qt_cols):
    tm, d = x_ref.shape
    if qt_cols:
        qm_ref, kf_ref, km_ref, vf_ref, vt_ref = out_refs
        row0 = pl.program_id(1) * tm
        q_ext = _mask_lanes(tm, row0, tm, _A_HD, False)
        k_ext = _mask_lanes(tm, row0, tm, _A_HD, True).astype(_BF)
        ones = jnp.ones((_BF_ROWS, tm), _BF)
    else:
        qm_ref, kf_ref, vf_ref = out_refs
    h = _rms(x_ref[...], g_ref[...], _NORM_EPS).astype(_BF)
    cos, slo, shi = cos_ref[...], slo_ref[...], shi_ref[...]
    first_half = lax.broadcasted_iota(jnp.int32, (tm, _LANES), 1) < _A_HD
    zero = jnp.zeros((tm, _LANES), _F32)
    cw = _MXU_COLS
    for c in range(0, d, cw):
        yq = jnp.dot(h, w_ref[:, c:c + cw], preferred_element_type=_F32)
        yk = jnp.dot(h, w_ref[:, d + c:d + c + cw], preferred_element_type=_F32)
        yv = jnp.dot(h, w_ref[:, 2 * d + c:2 * d + c + cw], preferred_element_type=_F32)
        vf_ref[:, c:c + cw] = yv
        for s in range(0, cw, _LANES):
            cols = slice(c + s, c + s + _LANES)
            head = (c + s) // _LANES
            q = _rope_lanes(yq[:, s:s + _LANES], cos, slo, shi) * qscale
            k = _rope_lanes(yk[:, s:s + _LANES], cos, slo, shi)
            kf_ref[:, cols] = k
            if not qt_cols:
                qm_ref[0, :, cols] = jnp.where(first_half, q, zero).astype(_BF)
                qm_ref[1, :, cols] = jnp.where(first_half, zero, q).astype(_BF)
                continue
            for mp in range(2):
                qh = q if mp == 0 else pltpu.roll(q, _A_HD, axis=1)
                kh = k if mp == 0 else pltpu.roll(k, _A_HD, axis=1)
                qt = jnp.where(first_half, qh, q_ext).T.astype(_BF)
                for g in range(tm // qt_cols):
                    qm_ref[mp, head, g] = qt[:, g * qt_cols:(g + 1) * qt_cols]
                k0 = jnp.where(first_half, kh, zero).astype(_BF)
                km_ref[mp, 0, :, cols] = k0
                km_ref[mp, 1, :, cols] = k0 + k_ext
            vt_ref[head, 0, :_LANES, :] = yv[:, s:s + _LANES].T.astype(_BF)
            vt_ref[head, 0, _LANES:, :] = ones


def _diff_proj(x, g, w_qkv, tabs, tm, qt_cols=0):
    b, s, d = x.shape
    heads = d // _LANES
    row = lambda dt: jax.ShapeDtypeStruct((b, s, d), dt)
    xspec = pl.BlockSpec((None, tm, d), lambda i, t: (i, t, 0))
    tspec = pl.BlockSpec((tm, _LANES), lambda i, t: (t, 0))
    if qt_cols:
        assert tm % qt_cols == 0
        dve = _LANES + _BF_ROWS
        out_specs = [pl.BlockSpec((None, 2, heads, tm // qt_cols, _LANES, qt_cols),
                                  lambda i, t: (i, 0, 0, t, 0, 0)),
                     xspec, pl.BlockSpec((None, 2, 2, tm, d), lambda i, t: (i, 0, 0, t, 0)), xspec,
                     pl.BlockSpec((None, heads, 1, dve, tm), lambda i, t: (i, 0, t, 0, 0))]
        out_shape = [jax.ShapeDtypeStruct((b, 2, heads, s // qt_cols, _LANES, qt_cols), _BF),
                     row(_F32), jax.ShapeDtypeStruct((b, 2, 2, s, d), _BF), row(_F32),
                     jax.ShapeDtypeStruct((b, heads, s // tm, dve, tm), _BF)]
    else:
        out_specs = [pl.BlockSpec((None, 2, tm, d), lambda i, t: (i, 0, t, 0)), xspec, xspec]
        out_shape = [jax.ShapeDtypeStruct((b, 2, s, d), _BF), row(_F32), row(_F32)]
    return pl.pallas_call(
        functools.partial(_diff_proj_body, qscale=_LOG2E / math.sqrt(_A_HD), qt_cols=qt_cols),
        grid=(b, s // tm),
        in_specs=[xspec, _resident((1, d)), _resident(w_qkv.shape), tspec, tspec, tspec],
        out_specs=out_specs,
        out_shape=out_shape,
        compiler_params=_params(("parallel", "parallel")),
        name="diff_proj",
    )(x, g.reshape(1, d), w_qkv, *tabs)


def _cached_diff_body(q_ref, kc_ref, kn_ref, vc_ref, vn_ref, lam_ref, sg_ref, o_ref, *, lam_init):
    nt = (((1,), (1,)), ((), ()))
    lp = lam_ref[...]
    lam = (jnp.exp(jnp.sum(lp[0:1] * lp[1:2], axis=-1, keepdims=True))
           - jnp.exp(jnp.sum(lp[2:3] * lp[3:4], axis=-1, keepdims=True)) + lam_init)
    for hd in range(o_ref.shape[-1] // _LANES):
        cols = slice(hd * _LANES, (hd + 1) * _LANES)
        kc, kn = kc_ref[:, cols].astype(_BF), kn_ref[:, cols].astype(_BF)
        vc, vn = vc_ref[:, hd, :].astype(_BF), vn_ref[:, cols].astype(_BF)
        outs = []
        for c in range(2):
            q = q_ref[c, :, cols]
            sc = lax.dot_general(q, kc, nt, preferred_element_type=_F32)
            sn = lax.dot_general(q, kn, nt, preferred_element_type=_F32)
            m = jnp.maximum(jnp.max(sc, axis=-1, keepdims=True), jnp.max(sn, axis=-1, keepdims=True))
            pc, pn = jnp.exp2(sc - m), jnp.exp2(sn - m)
            l = jnp.sum(pc, axis=-1, keepdims=True) + jnp.sum(pn, axis=-1, keepdims=True)
            acc = (jnp.dot(pc.astype(_BF), vc, preferred_element_type=_F32)
                   + jnp.dot(pn.astype(_BF), vn, preferred_element_type=_F32))
            outs.append(acc / l)
        o = _rms(outs[0] - lam * outs[1], sg_ref[...], _A_SUBLN_EPS) * (1.0 - lam_init)
        o_ref[:, cols] = o.astype(o_ref.dtype)


def _cached_diff(qm, k_cache, k_new, v_cache, v_new, lam, subln_g, lam_init):
    b, _, t, d = qm.shape
    past, heads = v_cache.shape[1:3]
    blk = lambda rows: pl.BlockSpec((None, rows, d), lambda i: (i, 0, 0))
    vblk = pl.BlockSpec((None, past, heads, _LANES), lambda i: (i, 0, 0, 0))
    return pl.pallas_call(
        functools.partial(_cached_diff_body, lam_init=lam_init),
        grid=(b,),
        in_specs=[pl.BlockSpec((None, 2, t, d), lambda i: (i, 0, 0, 0)),
                  blk(past), blk(t), vblk, blk(t), _resident(lam.shape), _resident((1, _LANES))],
        out_specs=blk(t),
        out_shape=jax.ShapeDtypeStruct((b, t, d), _BF),
        compiler_params=_params(("parallel",)),
        name="cached_diff",
    )(qm, k_cache, k_new, v_cache, v_new, lam, subln_g.reshape(1, _LANES))


def _cached_mla_body(q_ref, k_ref, v_ref, o_ref):
    s = lax.dot_general(q_ref[...], k_ref[...], (((1,), (1,)), ((), ())),
                        preferred_element_type=_F32)
    p = jnp.exp2(s - jnp.max(s, axis=-1, keepdims=True))
    acc = jnp.dot(p.astype(_BF), v_ref[...], preferred_element_type=_F32)
    o_ref[...] = (acc / jnp.sum(p, axis=-1, keepdims=True)).astype(o_ref.dtype)


def _cached_mla(q, k, v, heads, dq, dv):
    b, t, _ = q.shape
    sk = k.shape[1]
    return pl.pallas_call(
        _cached_mla_body,
        grid=(b, heads),
        in_specs=[pl.BlockSpec((None, t, dq), lambda i, h: (i, 0, h)),
                  pl.BlockSpec((None, sk, dq), lambda i, h: (i, 0, h)),
                  pl.BlockSpec((None, sk, dv), lambda i, h: (i, 0, h))],
        out_specs=pl.BlockSpec((None, t, dv), lambda i, h: (i, 0, h)),
        out_shape=jax.ShapeDtypeStruct((b, t, heads * dv), _BF),
        compiler_params=_params(("parallel", "parallel")),
        name="cached_mla",
    )(q, k, v)


def _sweep_body(*refs, nmaps, halves, tile, nb, lam_init):
    if nmaps == 2:
        q_ref, k_ref, vt_ref, lam_ref, sg_ref, o_ref, s_buf, mb_buf, m_sc, acc_sc = refs
    else:
        q_ref, k_ref, vt_ref, o_ref, s_buf, mb_buf, m_sc, acc_sc = refs
    hw = tile // halves
    nch = nmaps * halves
    dv = o_ref.shape[-1]
    ticks = acc_sc.shape[0]
    npairs = nb * (nb + 1) // 2
    acc_sc[...] = jnp.zeros(acc_sc.shape, _F32)

    def nxt(pair):
        qi, t = pair
        last = t == qi
        return jnp.where(last, qi + 1, qi), jnp.where(last, 0, t + 1)

    def scores(pair, slot):
        qi, t = pair
        variant = (t == qi).astype(jnp.int32)
        for mp in range(nmaps):
            k = k_ref[mp, variant, pl.ds(pl.multiple_of(t * tile, tile), tile), :]
            for hf in range(halves):
                c = mp * halves + hf
                s = jnp.dot(k, q_ref[mp, qi * halves + hf], preferred_element_type=_F32)
                s_buf[slot, c] = s
                mb_buf[slot, c] = jnp.max(s, axis=0, keepdims=True)

    def values(pair, pos):
        _, t = pair
        first = t == 0
        slot = pos % _SWEEP_SLOTS
        vt = vt_ref[t]
        for c in range(nch):
            m_prev = jnp.where(first, _NEG_INF, m_sc[c])
            m_new = jnp.maximum(m_prev, mb_buf[slot, c])
            alpha = jnp.exp2(m_prev - m_new)
            m_sc[c] = m_new
            p = jnp.exp2(s_buf[slot, c] - m_new).astype(_BF)
            acc_sc[pos, c] = (alpha * acc_sc[(pos - 1) % ticks, c]
                              + jnp.dot(vt, p, preferred_element_type=_F32))

    def finish(pair, pos):
        qi, t = pair

        @pl.when(t == qi)
        def _():
            if nmaps == 2:
                lp = lam_ref[...]
                lam = (jnp.exp(jnp.sum(lp[0:1] * lp[1:2], axis=-1, keepdims=True))
                       - jnp.exp(jnp.sum(lp[2:3] * lp[3:4], axis=-1, keepdims=True)) + lam_init)
            normed = lambda c: acc_sc[pos, c, :dv, :] / acc_sc[pos, c, dv:dv + 1, :]
            for hf in range(halves):
                if nmaps == 2:
                    ot = normed(hf) - lam * normed(halves + hf)
                    o = _rms(ot.T, sg_ref[...], _A_SUBLN_EPS) * (1.0 - lam_init)
                else:
                    o = normed(hf).T
                row = pl.multiple_of(qi * tile + hf * hw, hw)
                o_ref[pl.ds(row, hw), :] = o.astype(o_ref.dtype)

    zero = jnp.int32(0)
    pair0 = (zero, zero)
    scores(pair0, 0)
    if npairs == 1:
        values(pair0, 0)
        finish(pair0, 0)
        return
    pair1 = nxt(pair0)
    scores(pair1, 1)

    def tick(carry, r):
        pc, pb, pa = carry
        scores(pa, (2 + r) % _SWEEP_SLOTS)
        values(pc, r)
        return pb, pa, nxt(pa)

    def run_ticks(n, carry):
        done = []
        for r in range(n):
            done.append(carry[0])
            carry = tick(carry, r)
        for r, pair in enumerate(done):
            finish(pair, r)
        return carry

    nticks = npairs - 2
    carry = lax.fori_loop(0, nticks // ticks, lambda _, c: run_ticks(ticks, c),
                          (pair0, pair1, nxt(pair1)))
    carry = run_ticks(nticks % ticks, carry)
    before_last, last, _ = carry
    for pair, pos in ((before_last, (npairs - 2) % ticks), (last, (npairs - 1) % ticks)):
        values(pair, pos)
        finish(pair, pos)


def _sweep(q, k, vt, heads, dq, dv, *, halves, trip_ticks, lam=None, subln_g=None, lam_init=0.0):
    b, nmaps = q.shape[:2]
    _, _, nb, dve, tile = vt.shape
    s = nb * tile
    assert tile % _CHUNK == 0 and tile % halves == 0 and trip_ticks % _SWEEP_SLOTS == 0
    assert q.shape[2:] == (heads, s * halves // tile, dq, tile // halves) and dve == dv + _BF_ROWS
    nch = nmaps * halves
    hw = tile // halves
    once = pl.Buffered(1)
    in_specs = [pl.BlockSpec((None, nmaps, None, nb * halves, dq, hw),
                             lambda i, h: (i, 0, h, 0, 0, 0), pipeline_mode=once),
                pl.BlockSpec((None, nmaps, 2, s, dq), lambda i, h: (i, 0, 0, 0, h),
                             pipeline_mode=once),
                pl.BlockSpec((None, None, nb, dve, tile), lambda i, h: (i, h, 0, 0, 0),
                             pipeline_mode=once)]
    args = [q, k, vt]
    if nmaps == 2:
        in_specs += [_resident(lam.shape), _resident((1, dv))]
        args += [lam, subln_g.reshape(1, dv)]
    scratch = [pltpu.VMEM((_SWEEP_SLOTS, nch, tile, hw), _F32),
               pltpu.VMEM((_SWEEP_SLOTS, nch, 1, hw), _F32),
               pltpu.VMEM((nch, 1, hw), _F32), pltpu.VMEM((trip_ticks, nch, dve, hw), _F32)]
    return pl.pallas_call(
        functools.partial(_sweep_body, nmaps=nmaps, halves=halves, tile=tile, nb=nb,
                          lam_init=lam_init),
        grid=(b, heads),
        in_specs=in_specs,
        out_specs=pl.BlockSpec((None, s, dv), lambda i, h: (i, 0, h)),
        out_shape=jax.ShapeDtypeStruct((b, s, heads * dv), _BF),
        scratch_shapes=scratch,
        compiler_params=_params(("parallel", "parallel")),
        name="sweep_diff" if nmaps == 2 else "sweep_mla",
    )(*args)


def _mm_body(a_ref, w_ref, o_ref):
    o_ref[...] = jnp.dot(a_ref[...], w_ref[...], preferred_element_type=_F32).astype(o_ref.dtype)


def _mm(a, w, out_dtype):
    b, s, kdim = a.shape
    n = w.shape[1]
    tm = min(_ROW_TILE, s)
    return pl.pallas_call(
        _mm_body,
        grid=(b, s // tm),
        in_specs=[pl.BlockSpec((None, tm, kdim), lambda i, t: (i, t, 0)), _resident(w.shape)],
        out_specs=pl.BlockSpec((None, tm, n), lambda i, t: (i, t, 0)),
        out_shape=jax.ShapeDtypeStruct((b, s, n), out_dtype),
        compiler_params=_params(("parallel", "parallel")),
        name="matmul",
    )(a, w)


def _mla_proj_body(x_ref, g_ref, wd_ref, qg_ref, wq_ref, kvg_ref, wk_ref, wv_ref,
                   cos_ref, slo_ref, shi_ref, lat_ref, kr_ref, q_ref, k_ref, v_ref, *, qscale, mask_tile,
                   qt_cols):
    tm = x_ref.shape[0]
    h = _rms(x_ref[...], g_ref[...], _NORM_EPS).astype(_BF)
    cos, slo, shi = cos_ref[...], slo_ref[...], shi_ref[...]
    if mask_tile:
        row0 = pl.program_id(1) * tm
        q_ext = _mask_lanes(tm, row0, mask_tile, _B_ROPE, False)
        k_ext = _mask_lanes(tm, row0, mask_tile, _B_ROPE, True)
    else:
        q_ext = k_ext = jnp.zeros((tm, _LANES), _F32)
    dn = jnp.dot(h, wd_ref[...], preferred_element_type=_F32)
    e0, e1 = _B_Q_LORA, _B_Q_LORA + _B_KV_LORA
    cq = _rms(dn[:, :e0], qg_ref[...], _NORM_EPS).astype(_BF)
    lat = _rms(dn[:, e0:e1], kvg_ref[...], _NORM_EPS)
    kr = _rope_lanes(dn[:, e1:e1 + _LANES], cos, slo, shi)
    lat_ref[...] = lat
    kr_ref[...] = kr[:, :_B_ROPE]
    hw = 2 * _LANES
    for hd in range(_B_HEADS):
        yq = jnp.dot(cq, wq_ref[:, hd * hw:(hd + 1) * hw], preferred_element_type=_F32)
        q_nope = yq[:, :_LANES] * qscale
        q_rope = _rope_lanes(yq[:, _LANES:], cos, slo, shi) * qscale + q_ext
        if qt_cols:
            qt = jnp.concatenate([q_nope.T, q_rope.T], axis=0).astype(_BF)
            for g in range(tm // qt_cols):
                q_ref[hd, g] = qt[:, g * qt_cols:(g + 1) * qt_cols]
        else:
            q_ref[:, hd * hw:hd * hw + _LANES] = q_nope.astype(_BF)
            q_ref[:, hd * hw + _LANES:(hd + 1) * hw] = q_rope.astype(_BF)
    kin = jnp.concatenate([lat, kr], axis=1).astype(_BF)
    for hd in range(_B_HEADS):
        yk = jnp.dot(kin, wk_ref[:, hd * hw:(hd + 1) * hw], preferred_element_type=_F32)
        for var in range(2):
            k_ref[var, :, hd * hw:hd * hw + _LANES] = yk[:, :_LANES].astype(_BF)
        k_ref[0, :, hd * hw + _LANES:(hd + 1) * hw] = yk[:, _LANES:].astype(_BF)
        k_ref[1, :, hd * hw + _LANES:(hd + 1) * hw] = (yk[:, _LANES:] + k_ext).astype(_BF)
    latb = lat.astype(_BF)
    for c in range(0, wv_ref.shape[-1], _MXU_COLS):
        yv = jnp.dot(latb, wv_ref[:, c:c + _MXU_COLS], preferred_element_type=_F32)
        if qt_cols:
            for s in range(0, _MXU_COLS, _B_VD):
                v_ref[(c + s) // _B_VD, 0, :_B_VD, :] = yv[:, s:s + _B_VD].T.astype(_BF)
                v_ref[(c + s) // _B_VD, 0, _B_VD:, :] = jnp.ones((_BF_ROWS, tm), _BF)
        else:
            v_ref[:, c:c + _MXU_COLS] = yv.astype(_BF)


def _mla_weights(w_down, w_uq, w_uk, w_uv):
    d = w_down.shape[0]
    hw = 2 * _LANES
    wd = jnp.concatenate([w_down, jnp.zeros((d, _LANES - _B_ROPE), w_down.dtype)], axis=1)
    wq = w_uq.reshape(_B_Q_LORA, _B_HEADS, _B_NOPE + _B_ROPE)
    wq = jnp.pad(wq, ((0, 0), (0, 0), (0, hw - _B_NOPE - _B_ROPE))).reshape(_B_Q_LORA, _B_HEADS * hw)
    eye = jnp.eye(_B_ROPE, dtype=w_uk.dtype)
    wk_top = jnp.pad(w_uk, ((0, 0), (0, 0), (0, hw - _B_NOPE)))
    wk_mid = jnp.broadcast_to(jnp.pad(eye, ((0, 0), (_B_NOPE, hw - _B_NOPE - _B_ROPE)))[:, None, :],
                              (_B_ROPE, _B_HEADS, hw))
    wk_bot = jnp.zeros((_LANES - _B_ROPE, _B_HEADS, hw), w_uk.dtype)
    wk = jnp.concatenate([wk_top, wk_mid, wk_bot], axis=0).reshape(-1, _B_HEADS * hw)
    wv = w_uv.reshape(_B_KV_LORA, _B_HEADS * _B_VD)
    return wd.astype(_BF), wq.astype(_BF), wk.astype(_BF), wv.astype(_BF)


def _mla_proj(x, g, wts, q_norm_g, kv_norm_g, tabs, tm, mask_tile, qt_cols=0):
    b, s, d = x.shape
    wd, wq, wk, wv = wts
    xspec = pl.BlockSpec((None, tm, d), lambda i, t: (i, t, 0))
    tspec = pl.BlockSpec((tm, _LANES), lambda i, t: (t, 0))
    ospec = lambda n: pl.BlockSpec((None, tm, n), lambda i, t: (i, t, 0))
    oshape = lambda n, dt: jax.ShapeDtypeStruct((b, s, n), dt)
    if qt_cols:
        assert tm % qt_cols == 0
        hq = wq.shape[1] // _B_HEADS
        q_spec = pl.BlockSpec((None, _B_HEADS, tm // qt_cols, hq, qt_cols),
                              lambda i, t: (i, 0, t, 0, 0))
        q_shape = jax.ShapeDtypeStruct((b, _B_HEADS, s // qt_cols, hq, qt_cols), _BF)
        dve = _B_VD + _BF_ROWS
        v_spec = pl.BlockSpec((None, _B_HEADS, 1, dve, tm), lambda i, t: (i, 0, t, 0, 0))
        v_shape = jax.ShapeDtypeStruct((b, _B_HEADS, s // tm, dve, tm), _BF)
    else:
        q_spec, q_shape = ospec(wq.shape[1]), oshape(wq.shape[1], _BF)
        v_spec, v_shape = ospec(wv.shape[1]), oshape(wv.shape[1], _BF)
    return pl.pallas_call(
        functools.partial(_mla_proj_body, qscale=_LOG2E / math.sqrt(_B_NOPE + _B_ROPE),
                          mask_tile=mask_tile, qt_cols=qt_cols),
        grid=(b, s // tm),
        in_specs=[xspec, _resident((1, d)), _resident(wd.shape), _resident((1, _B_Q_LORA)),
                  _resident(wq.shape), _resident((1, _B_KV_LORA)), _resident(wk.shape),
                  _resident(wv.shape), tspec, tspec, tspec],
        out_specs=[ospec(_B_KV_LORA), ospec(_B_ROPE), q_spec,
                   pl.BlockSpec((None, 2, tm, wk.shape[1]), lambda i, t: (i, 0, t, 0)), v_spec],
        out_shape=[oshape(_B_KV_LORA, _F32), oshape(_B_ROPE, _F32), q_shape,
                   jax.ShapeDtypeStruct((b, 2, s, wk.shape[1]), _BF), v_shape],
        compiler_params=_params(("parallel", "parallel")),
        name="mla_proj",
    )(x, g.reshape(1, d), wd, q_norm_g.reshape(1, -1), wq, kv_norm_g.reshape(1, -1), wk, wv, *tabs)


def _delayed(carry_ref, cols, cur, shifts):
    nseg = carry_ref.shape[0]
    seg = cur.shape[0] // nseg
    outs = [[] for _ in shifts]
    for sg in range(nseg):
        part = cur[sg * seg:(sg + 1) * seg]
        prev = carry_ref[sg, :, cols]
        for o, k in zip(outs, shifts):
            o.append(_shift_rows(prev, part, k))
        carry_ref[sg, :, cols] = part[seg - _SUBLANES:]
    return [o[0] if nseg == 1 else jnp.concatenate(o, axis=0) for o in outs]


def _conv_mix_body(x_ref, g_ref, win_ref, cw_ref, wout_ref, hist_ref, o_ref, hout_ref,
                   carry_sc, z_sc):
    tm, d = x_ref.shape

    @pl.when(pl.program_id(1) == 0)
    def _():
        carry_sc[...] = hist_ref[...]

    x = x_ref[...]
    h = _rms(x, g_ref[...], _NORM_EPS).astype(_BF)
    cw = _MXU_COLS
    for c in range(0, d, cw):
        gate_b = jnp.dot(h, win_ref[:, c:c + cw], preferred_element_type=_F32)
        gate_c = jnp.dot(h, win_ref[:, d + c:d + c + cw], preferred_element_type=_F32)
        val = jnp.dot(h, win_ref[:, 2 * d + c:2 * d + c + cw], preferred_element_type=_F32)
        u = gate_c * val
        u2, u1 = _delayed(carry_sc, slice(c, c + cw), u, (2, 1))
        y = u2 * cw_ref[0:1, c:c + cw] + u1 * cw_ref[1:2, c:c + cw] + u * cw_ref[2:3, c:c + cw]
        z_sc[:, c:c + cw] = (gate_b * y).astype(_BF)
    hout_ref[...] = carry_sc[...]
    o_ref[...] = x + jnp.dot(z_sc[...], wout_ref[...], preferred_element_type=_F32)


def _conv_mix(x, g, w_in, conv_w, w_out, hist, tm):
    b, s, d = x.shape
    nseg = hist.shape[1]
    assert nseg == 1 or tm == s
    xspec = pl.BlockSpec((None, tm, d), lambda i, t: (i, t, 0))
    hspec = pl.BlockSpec((None, nseg, _SUBLANES, d), lambda i, t: (i, 0, 0, 0))
    return pl.pallas_call(
        _conv_mix_body,
        grid=(b, s // tm),
        in_specs=[xspec, _resident((1, d)), _resident(w_in.shape), _resident(conv_w.shape),
                  _resident(w_out.shape), hspec],
        out_specs=[xspec, hspec],
        out_shape=[jax.ShapeDtypeStruct((b, s, d), _F32),
                   jax.ShapeDtypeStruct((b, nseg, _SUBLANES, d), _F32)],
        scratch_shapes=[pltpu.VMEM((nseg, _SUBLANES, d), _F32), pltpu.VMEM((tm, d), _BF)],
        compiler_params=_params(("arbitrary", "arbitrary")),
        name="conv_mixer",
    )(x, g.reshape(1, d), w_in, conv_w, w_out, hist)


_POOL_HALO = 16


def _pool_mix_body(x_ref, g_ref, wg_ref, sc_ref, hist_ref, o_ref, hout_ref, carry_sc, *, hist_valid):
    tm, d = x_ref.shape
    t = pl.program_id(1)

    @pl.when(t == 0)
    def _():
        carry_sc[...] = hist_ref[...]

    x = x_ref[...]
    h = _rms(x, g_ref[...], _NORM_EPS)
    gw = d // len(_D_WINDOWS)
    tpos = t * tm + lax.broadcasted_iota(jnp.int32, (tm, gw), 0)
    outs = []
    for gi, w in enumerate(_D_WINDOWS):
        hg = h[:, gi * gw:(gi + 1) * gw]
        acc = jnp.concatenate([carry_sc[:, gi * gw:(gi + 1) * gw], hg], axis=0)
        k = 1
        while k < w:
            acc = acc + pltpu.roll(acc, k, axis=0)
            k *= 2
        cnt = jnp.minimum(tpos + (hist_valid + 1), w).astype(_F32)
        pooled = acc[_POOL_HALO:] / cnt
        outs.append(jnp.dot((pooled - hg).astype(_BF), wg_ref[gi], preferred_element_type=_F32))
    o_ref[...] = x + jnp.concatenate(outs, axis=1) * sc_ref[...]
    tail = h[tm - _POOL_HALO:, :]
    carry_sc[...] = tail
    hout_ref[...] = tail


def _pool_mix(x, g, w_group, scale, hist16, hist_valid, tm):
    b, s, d = x.shape
    assert all(w & (w - 1) == 0 and w <= _POOL_HALO for w in _D_WINDOWS)
    xspec = pl.BlockSpec((None, tm, d), lambda i, t: (i, t, 0))
    hspec = pl.BlockSpec((None, _POOL_HALO, d), lambda i, t: (i, 0, 0))
    return pl.pallas_call(
        functools.partial(_pool_mix_body, hist_valid=hist_valid),
        grid=(b, s // tm),
        in_specs=[xspec, _resident((1, d)), _resident(w_group.shape), _resident((1, d)), hspec],
        out_specs=[xspec, hspec],
        out_shape=[jax.ShapeDtypeStruct((b, s, d), _F32),
                   jax.ShapeDtypeStruct((b, _POOL_HALO, d), _F32)],
        scratch_shapes=[pltpu.VMEM((_POOL_HALO, d), _F32)],
        compiler_params=_params(("arbitrary", "arbitrary")),
        name="pool_mixer",
    )(x, g.reshape(1, d), w_group, scale.reshape(1, d), hist16)


def _ffn_body(*refs, final, mixed):
    refs = list(refs)
    x_ref, g_ref, wg_ref, wu_ref, cw_ref, cb_ref, wd_ref, hist_ref = refs[:8]
    del refs[:8]
    a_ref, wo_ref = (refs.pop(0), refs.pop(0)) if mixed else (None, None)
    fg_ref = refs.pop(0) if final else None
    o_ref, hout_ref, carry_sc, act_sc = refs
    tm = x_ref.shape[0]
    f = wg_ref.shape[1]

    @pl.when(pl.program_id(1) == 0)
    def _():
        carry_sc[...] = hist_ref[...]

    x = x_ref[...]
    if mixed:
        x = x + jnp.dot(a_ref[...], wo_ref[...], preferred_element_type=_F32)
    h = _rms(x, g_ref[...], _NORM_EPS).astype(_BF)
    cw = _MXU_COLS
    for c in range(0, f, cw):
        gate = jnp.dot(h, wg_ref[:, c:c + cw], preferred_element_type=_F32)
        up = jnp.dot(h, wu_ref[:, c:c + cw], preferred_element_type=_F32)
        g2, g1 = _delayed(carry_sc, slice(c, c + cw), gate, (2, 1))
        y = (g2 * cw_ref[0:1, c:c + cw] + g1 * cw_ref[1:2, c:c + cw]
             + gate * cw_ref[2:3, c:c + cw] + cb_ref[:, c:c + cw])
        act_sc[:, c:c + cw] = (y / (1.0 + jnp.exp(-y)) * up).astype(_BF)
    hout_ref[...] = carry_sc[...]
    out = x + jnp.dot(act_sc[...], wd_ref[...], preferred_element_type=_F32)
    if final:
        out = _rms(out, fg_ref[...], _NORM_EPS)
    o_ref[...] = out


def _ffn(x, g, w_gate, w_up, conv_w, conv_b, w_down, hist, final_g, tm, mixer_out=None, w_o=None):
    b, s, d = x.shape
    f = w_gate.shape[1]
    nseg = hist.shape[1]
    assert f % _MXU_COLS == 0 and (nseg == 1 or tm == s)
    xspec = pl.BlockSpec((None, tm, d), lambda i, t: (i, t, 0))
    hspec = pl.BlockSpec((None, nseg, _SUBLANES, f), lambda i, t: (i, 0, 0, 0))
    in_specs = [xspec, _resident((1, d)), _resident(w_gate.shape), _resident(w_up.shape),
                _resident(conv_w.shape), _resident((1, f)), _resident(w_down.shape), hspec]
    args = [x, g.reshape(1, d), w_gate, w_up, conv_w, conv_b.reshape(1, f), w_down, hist]
    if mixer_out is not None:
        in_specs += [pl.BlockSpec((None, tm, mixer_out.shape[-1]), lambda i, t: (i, t, 0)),
                     _resident(w_o.shape)]
        args += [mixer_out, w_o]
    if final_g is not None:
        in_specs.append(_resident((1, d)))
        args.append(final_g.reshape(1, d))
    return pl.pallas_call(
        functools.partial(_ffn_body, final=final_g is not None, mixed=mixer_out is not None),
        grid=(b, s // tm),
        in_specs=in_specs,
        out_specs=[xspec, hspec],
        out_shape=[jax.ShapeDtypeStruct((b, s, d), _F32),
                   jax.ShapeDtypeStruct((b, nseg, _SUBLANES, f), _F32)],
        scratch_shapes=[pltpu.VMEM((nseg, _SUBLANES, f), _F32), pltpu.VMEM((tm, f), _BF)],
        compiler_params=_params(("arbitrary", "arbitrary")),
        name="conv_ffn",
    )(*args)


def _pad_hist(hist, rows):
    return jnp.pad(hist, ((0, 0), (rows - hist.shape[1], 0), (0, 0)))


def kernel(x_prompt, x_sample, cache_a_k, cache_a_v, cache_b_latent, cache_b_krope, state_c_conv, state_d_pool, state_ffn_conv, norm_mix_g, norm_ffn_g, norm_final_g, a_w_qkv, a_lam, a_subln_g, a_w_o, b_w_down, b_q_norm_g, b_w_uq, b_kv_norm_g, b_w_uk, b_w_uv, b_w_o, c_w_in, c_conv_w, c_w_out, d_w_group, d_scale, ffn_w_gate, ffn_w_up, ffn_conv_w, ffn_conv_b, ffn_w_down):
    depth = norm_mix_g.shape[0]
    n_p, seq, d = x_prompt.shape
    n_s, t_new, _ = x_sample.shape
    past = cache_a_k.shape[2]
    f = ffn_w_gate.shape[-1]
    tm_p, tm_s = min(_ROW_TILE, seq), t_new
    ta = min(_ATT_TILE, seq)
    tabs_p = _rope_tables(jnp.arange(seq, dtype=jnp.int32))
    tabs_s = _rope_tables(past + jnp.arange(t_new, dtype=jnp.int32))
    xp, xs = x_prompt, x_sample
    flat = lambda a: a.reshape(1, n_s * t_new, a.shape[-1])
    outs = {k: [] for k in ("ak_p", "av_p", "bl_p", "br_p", "cc_p", "dp_p", "fc_p",
                            "ak_s", "av_s", "bl_s", "br_s", "cc_s", "dp_s", "fc_s")}
    for i in range(depth):
        m, j = i % _N_MIXERS, i // _N_MIXERS
        g_mix = norm_mix_g[i]
        mix_p, mix_s = {}, {}
        if m == 0:
            lam_init = 0.8 - 0.6 * math.exp(-0.3 * i)
            w_qkv, w_o = a_w_qkv[j].astype(_BF), a_w_o[j].astype(_BF)
            qt, kf, km, vf, vt = _diff_proj(xp, g_mix, w_qkv, tabs_p, ta, ta // _SWEEP_HALVES)
            op = _sweep(qt, km, vt, _A_HEADS, _LANES, 2 * _A_HD, halves=_SWEEP_HALVES,
                        trip_ticks=_TRIP_TICKS_DIFF,
                        lam=a_lam[j], subln_g=a_subln_g[j], lam_init=lam_init)
            mix_p = dict(mixer_out=op, w_o=w_o)
            outs["ak_p"].append(kf.reshape(n_p, seq, _A_HEADS, 2, _A_HD))
            outs["av_p"].append(vf.reshape(n_p, seq, _A_HEADS, 2 * _A_HD))
            qm, kf, vf = _diff_proj(xs, g_mix, w_qkv, tabs_s, tm_s)
            osm = _cached_diff(qm, cache_a_k[j].reshape(n_s, past, d), kf, cache_a_v[j], vf,
                               a_lam[j], a_subln_g[j], lam_init)
            mix_s = dict(mixer_out=flat(osm), w_o=w_o)
            outs["ak_s"].append(kf.reshape(n_s, t_new, _A_HEADS, 2, _A_HD))
            outs["av_s"].append(vf.reshape(n_s, t_new, _A_HEADS, 2 * _A_HD))
        elif m == 1:
            wts = _mla_weights(b_w_down[j], b_w_uq[j], b_w_uk[j], b_w_uv[j])
            w_o = b_w_o[j].astype(_BF)
            hq = 2 * _LANES
            lat, kr, qt, k2, vt = _mla_proj(xp, g_mix, wts, b_q_norm_g[j], b_kv_norm_g[j], tabs_p,
                                            ta, ta, ta // _SWEEP_HALVES)
            op = _sweep(qt[:, None], k2[:, None], vt, _B_HEADS, hq, _B_VD,
                        halves=_SWEEP_HALVES, trip_ticks=_TRIP_TICKS_MLA)
            mix_p = dict(mixer_out=op, w_o=w_o)
            outs["bl_p"].append(lat)
            outs["br_p"].append(kr)
            lat, kr, q, k2, v = _mla_proj(xs, g_mix, wts, b_q_norm_g[j], b_kv_norm_g[j], tabs_s,
                                          tm_s, 0)
            kin = jnp.concatenate([cache_b_latent[j], cache_b_krope[j],
                                   jnp.zeros((n_s, past, _LANES - _B_ROPE), _F32)], axis=-1).astype(_BF)
            k_cache = _mm(kin, wts[2], out_dtype=_BF)
            v_cache = _mm(cache_b_latent[j].astype(_BF), wts[3], out_dtype=_BF)
            k_all = jnp.concatenate([k_cache, k2[:, 0]], axis=1)
            v_all = jnp.concatenate([v_cache, v], axis=1)
            osm = _cached_mla(q, k_all, v_all, _B_HEADS, hq, _B_VD)
            mix_s = dict(mixer_out=flat(osm), w_o=w_o)
            outs["bl_s"].append(lat)
            outs["br_s"].append(kr)
        elif m == 2:
            w_in, w_out = c_w_in[j].astype(_BF), c_w_out[j].astype(_BF)
            xp, hc = _conv_mix(xp, g_mix, w_in, c_conv_w[j], w_out,
                               jnp.zeros((n_p, 1, _SUBLANES, d), _F32), tm_p)
            outs["cc_p"].append(hc[:, 0, -2:])
            xs, hc = _conv_mix(flat(xs), g_mix, w_in, c_conv_w[j], w_out,
                               _pad_hist(state_c_conv[j], _SUBLANES)[None], n_s * t_new)
            xs = xs.reshape(n_s, t_new, d)
            outs["cc_s"].append(hc[0, :, -2:])
        else:
            w_grp = d_w_group[j].astype(_BF)
            xp, hd = _pool_mix(xp, g_mix, w_grp, d_scale[j],
                               jnp.zeros((n_p, _POOL_HALO, d), _F32), 0, tm_p)
            outs["dp_p"].append(hd[:, -_D_HIST:])
            xs, hd = _pool_mix(xs, g_mix, w_grp, d_scale[j],
                               _pad_hist(state_d_pool[j], _POOL_HALO), _D_HIST, tm_s)
            outs["dp_s"].append(hd[:, -_D_HIST:])
        wg, wu, wd = ffn_w_gate[i].astype(_BF), ffn_w_up[i].astype(_BF), ffn_w_down[i].astype(_BF)
        final_g = norm_final_g if i == depth - 1 else None
        xp, hf = _ffn(xp, norm_ffn_g[i], wg, wu, ffn_conv_w[i], ffn_conv_b[i], wd,
                      jnp.zeros((n_p, 1, _SUBLANES, f), _F32), final_g, tm_p, **mix_p)
        outs["fc_p"].append(hf[:, 0, -2:])
        xs, hf = _ffn(flat(xs), norm_ffn_g[i], wg, wu, ffn_conv_w[i], ffn_conv_b[i], wd,
                      _pad_hist(state_ffn_conv[i], _SUBLANES)[None], final_g, n_s * t_new, **mix_s)
        xs = xs.reshape(n_s, t_new, d)
        outs["fc_s"].append(hf[0, :, -2:])
    st = lambda k: jnp.stack(outs[k])
    return (xp, xs, st("ak_p"), st("av_p"), st("bl_p"), st("br_p"), st("cc_p"), st("dp_p"), st("fc_p"),
            st("ak_s"), st("av_s"), st("bl_s"), st("br_s"), st("cc_s"), st("dp_s"), st("fc_s"))
```

```python
import functools
import math

import jax
import jax.numpy as jnp
from jax import lax
from jax.experimental import pallas as pl
from jax.experimental.pallas import tpu as pltpu

_BF = jnp.bfloat16
_F32 = jnp.float32

_CHUNK = 64
_ROPE_THETA = 10000.0
_NORM_EPS = 1e-6
_NEG_INF = -1e30
_A_HEADS = 8
_A_HD = 64
_A_SUBLN_EPS = 1e-5
_B_HEADS = 8
_B_NOPE = 128
_B_ROPE = 64
_B_VD = 128
_B_Q_LORA = 384
_B_KV_LORA = 256
_D_WINDOWS = (2, 4, 8, 16)
_D_HIST = 15
_N_MIXERS = 4
_LOG2E = math.log2(math.e)

_LANES = 128
_SUBLANES = 8
_MXU_COLS = 256
_BF_ROWS = 16
_SWEEP_SLOTS = 4
_TRIP_TICKS_DIFF = 8
_TRIP_TICKS_MLA = 16
_SWEEP_HALVES = 1
_VMEM_LIMIT = 56 * 1024 * 1024

_ROW_TILE = 1024
_ATT_TILE = 512


def _params(sem):
    return pltpu.CompilerParams(dimension_semantics=sem, vmem_limit_bytes=_VMEM_LIMIT)


def _resident(shape):
    nd = len(shape)
    return pl.BlockSpec(shape, lambda *_: (0,) * nd, pipeline_mode=pl.Buffered(1))


def _rms(x, g, eps):
    ms = jnp.mean(x * x, axis=-1, keepdims=True)
    return x * lax.rsqrt(ms + eps) * g


def _rope_lanes(y, cos, sin_lo, sin_hi):
    return (y * cos + pltpu.roll(y, _LANES - 32, axis=1) * sin_lo
            + pltpu.roll(y, 32, axis=1) * sin_hi)


def _rope_tables(pos):
    d = _A_HD
    inv = jnp.power(_ROPE_THETA, -jnp.arange(0, d, 2, dtype=_F32) / d)
    ang = pos.astype(_F32)[:, None] * inv[None, :]
    cos, sin = jnp.cos(ang), jnp.sin(ang)
    zero = jnp.zeros_like(sin)
    cos_t = jnp.tile(cos, (1, 4))
    sin_lo = jnp.tile(jnp.concatenate([-sin, zero], axis=1), (1, 2))
    sin_hi = jnp.tile(jnp.concatenate([zero, sin], axis=1), (1, 2))
    return cos_t, sin_lo, sin_hi


def _shift_rows(carry, cur, k):
    ext = jnp.concatenate([carry, cur], axis=0)
    return pltpu.roll(ext, k, axis=0)[carry.shape[0]:]


def _mask_lanes(rows, row0, tile, base, for_keys):
    r = row0 + lax.broadcasted_iota(jnp.int32, (rows, _LANES), 0)
    chunk = (r % tile) // _CHUNK
    c = lax.broadcasted_iota(jnp.int32, (rows, _LANES), 1) - base
    n = tile // _CHUNK
    if for_keys:
        return jnp.where(c == chunk, 1.0, 0.0)
    return jnp.where(c > chunk, jnp.where(c < n, _NEG_INF, 0.0), 0.0)


def _diff_proj_body(x_ref, g_ref, w_ref, cos_ref, slo_ref, shi_ref, *out_refs, qscale, qt_cols):
    tm, d = x_ref.shape
    if qt_cols:
        qm_ref, kf_ref, km_ref, vf_ref, vt_ref = out_refs
        row0 = pl.program_id(1) * tm
        q_ext = _mask_lanes(tm, row0, tm, _A_HD, False)
        k_ext = _mask_lanes(tm, row0, tm, _A_HD, True).astype(_BF)
        ones = jnp.ones((_BF_ROWS, tm), _BF)
    else:
        qm_ref, kf_ref, vf_ref = out_refs
    h = _rms(x_ref[...], g_ref[...], _NORM_EPS).astype(_BF)
    cos, slo, shi = cos_ref[...], slo_ref[...], shi_ref[...]
    first_half = lax.broadcasted_iota(jnp.int32, (tm, _LANES), 1) < _A_HD
    zero = jnp.zeros((tm, _LANES), _F32)
    cw = _MXU_COLS
    for c in range(0, d, cw):
        yq = jnp.dot(h, w_ref[:, c:c + cw], preferred_element_type=_F32)
        yk = jnp.dot(h, w_ref[:, d + c:d + c + cw], preferred_element_type=_F32)
        yv = jnp.dot(h, w_ref[:, 2 * d + c:2 * d + c + cw], preferred_element_type=_F32)
        vf_ref[:, c:c + cw] = yv
        for s in range(0, cw, _LANES):
            cols = slice(c + s, c + s + _LANES)
            head = (c + s) // _LANES
            q = _rope_lanes(yq[:, s:s + _LANES], cos, slo, shi) * qscale
            k = _rope_lanes(yk[:, s:s + _LANES], cos, slo, shi)
            kf_ref[:, cols] = k
            if not qt_cols:
                qm_ref[0, :, cols] = jnp.where(first_half, q, zero).astype(_BF)
                qm_ref[1, :, cols] = jnp.where(first_half, zero, q).astype(_BF)
                continue
            for mp in range(2):
                qh = q if mp == 0 else pltpu.roll(q, _A_HD, axis=1)
                kh = k if mp == 0 else pltpu.roll(k, _A_HD, axis=1)
                qt = jnp.where(first_half, qh, q_ext).T.astype(_BF)
                for g in range(tm // qt_cols):
                    qm_ref[mp, head, g] = qt[:, g * qt_cols:(g + 1) * qt_cols]
                k0 = jnp.where(first_half, kh, zero).astype(_BF)
                km_ref[mp, 0, :, cols] = k0
                km_ref[mp, 1, :, cols] = k0 + k_ext
            vt_ref[head, 0, :_LANES, :] = yv[:, s:s + _LANES].T.astype(_BF)
            vt_ref[head, 0, _LANES:, :] = ones


def _diff_proj(x, g, w_qkv, tabs, tm, qt_cols=0):
    b, s, d = x.shape
    heads = d // _LANES
    row = lambda dt: jax.ShapeDtypeStruct((b, s, d), dt)
    xspec = pl.BlockSpec((None, tm, d), lambda i, t: (i, t, 0))
    tspec = pl.BlockSpec((tm, _LANES), lambda i, t: (t, 0))
    if qt_cols:
        assert tm % qt_cols == 0
        dve = _LANES + _BF_ROWS
        out_specs = [pl.BlockSpec((None, 2, heads, tm // qt_cols, _LANES, qt_cols),
                                  lambda i, t: (i, 0, 0, t, 0, 0)),
                     xspec, pl.BlockSpec((None, 2, 2, tm, d), lambda i, t: (i, 0, 0, t, 0)), xspec,
                     pl.BlockSpec((None, heads, 1, dve, tm), lambda i, t: (i, 0, t, 0, 0))]
        out_shape = [jax.ShapeDtypeStruct((b, 2, heads, s // qt_cols, _LANES, qt_cols), _BF),
                     row(_F32), jax.ShapeDtypeStruct((b, 2, 2, s, d), _BF), row(_F32),
                     jax.ShapeDtypeStruct((b, heads, s // tm, dve, tm), _BF)]
    else:
        out_specs = [pl.BlockSpec((None, 2, tm, d), lambda i, t: (i, 0, t, 0)), xspec, xspec]
        out_shape = [jax.ShapeDtypeStruct((b, 2, s, d), _BF), row(_F32), row(_F32)]
    return pl.pallas_call(
        functools.partial(_diff_proj_body, qscale=_LOG2E / math.sqrt(_A_HD), qt_cols=qt_cols),
        grid=(b, s // tm),
        in_specs=[xspec, _resident((1, d)), _resident(w_qkv.shape), tspec, tspec, tspec],
        out_specs=out_specs,
        out_shape=out_shape,
        compiler_params=_params(("parallel", "parallel")),
        name="diff_proj",
    )(x, g.reshape(1, d), w_qkv, *tabs)


def _cached_diff_body(q_ref, kc_ref, kn_ref, vc_ref, vn_ref, lam_ref, sg_ref, o_ref, *, lam_init):
    nt = (((1,), (1,)), ((), ()))
    lp = lam_ref[...]
    lam = (jnp.exp(jnp.sum(lp[0:1] * lp[1:2], axis=-1, keepdims=True))
           - jnp.exp(jnp.sum(lp[2:3] * lp[3:4], axis=-1, keepdims=True)) + lam_init)
    for hd in range(o_ref.shape[-1] // _LANES):
        cols = slice(hd * _LANES, (hd + 1) * _LANES)
        kc, kn = kc_ref[:, cols].astype(_BF), kn_ref[:, cols].astype(_BF)
        vc, vn = vc_ref[:, hd, :].astype(_BF), vn_ref[:, cols].astype(_BF)
        outs = []
        for c in range(2):
            q = q_ref[c, :, cols]
            sc = lax.dot_general(q, kc, nt, preferred_element_type=_F32)
            sn = lax.dot_general(q, kn, nt, preferred_element_type=_F32)
            m = jnp.maximum(jnp.max(sc, axis=-1, keepdims=True), jnp.max(sn, axis=-1, keepdims=True))
            pc, pn = jnp.exp2(sc - m), jnp.exp2(sn - m)
            l = jnp.sum(pc, axis=-1, keepdims=True) + jnp.sum(pn, axis=-1, keepdims=True)
            acc = (jnp.dot(pc.astype(_BF), vc, preferred_element_type=_F32)
                   + jnp.dot(pn.astype(_BF), vn, preferred_element_type=_F32))
            outs.append(acc / l)
        o = _rms(outs[0] - lam * outs[1], sg_ref[...], _A_SUBLN_EPS) * (1.0 - lam_init)
        o_ref[:, cols] = o.astype(o_ref.dtype)


def _cached_diff(qm, k_cache, k_new, v_cache, v_new, lam, subln_g, lam_init):
    b, _, t, d = qm.shape
    past, heads = v_cache.shape[1:3]
    blk = lambda rows: pl.BlockSpec((None, rows, d), lambda i: (i, 0, 0))
    vblk = pl.BlockSpec((None, past, heads, _LANES), lambda i: (i, 0, 0, 0))
    return pl.pallas_call(
        functools.partial(_cached_diff_body, lam_init=lam_init),
        grid=(b,),
        in_specs=[pl.BlockSpec((None, 2, t, d), lambda i: (i, 0, 0, 0)),
                  blk(past), blk(t), vblk, blk(t), _resident(lam.shape), _resident((1, _LANES))],
        out_specs=blk(t),
        out_shape=jax.ShapeDtypeStruct((b, t, d), _BF),
        compiler_params=_params(("parallel",)),
        name="cached_diff",
    )(qm, k_cache, k_new, v_cache, v_new, lam, subln_g.reshape(1, _LANES))


def _cached_mla_body(q_ref, k_ref, v_ref, o_ref, *, heads):
    dq, dv = q_ref.shape[-1] // heads, v_ref.shape[-1] // heads
    for hd in range(heads):
        s = lax.dot_general(q_ref[:, hd * dq:(hd + 1) * dq], k_ref[:, hd * dq:(hd + 1) * dq],
                            (((1,), (1,)), ((), ())), preferred_element_type=_F32)
        p = jnp.exp2(s - jnp.max(s, axis=-1, keepdims=True))
        acc = jnp.dot(p.astype(_BF), v_ref[:, hd * dv:(hd + 1) * dv], preferred_element_type=_F32)
        o_ref[:, hd * dv:(hd + 1) * dv] = (acc / jnp.sum(p, axis=-1, keepdims=True)).astype(o_ref.dtype)


def _cached_mla(q, k, v, heads):
    b, t, _ = q.shape
    whole = lambda a: pl.BlockSpec((None,) + a.shape[1:], lambda i: (i, 0, 0))
    return pl.pallas_call(
        functools.partial(_cached_mla_body, heads=heads),
        grid=(b,),
        in_specs=[whole(q), whole(k), whole(v)],
        out_specs=pl.BlockSpec((None, t, v.shape[-1]), lambda i: (i, 0, 0)),
        out_shape=jax.ShapeDtypeStruct((b, t, v.shape[-1]), _BF),
        compiler_params=_params(("parallel",)),
        name="cached_mla",
    )(q, k, v)


def _sweep_body(*refs, nmaps, halves, tile, nb, lam_init):
    if nmaps == 2:
        q_ref, k_ref, vt_ref, lam_ref, sg_ref, o_ref, s_buf, mb_buf, m_sc, acc_sc = refs
    else:
        q_ref, k_ref, vt_ref, o_ref, s_buf, mb_buf, m_sc, acc_sc = refs
    hw = tile // halves
    nch = nmaps * halves
    dv = o_ref.shape[-1]
    ticks = acc_sc.shape[0]
    npairs = nb * (nb + 1) // 2
    acc_sc[...] = jnp.zeros(acc_sc.shape, _F32)

    def nxt(pair):
        qi, t = pair
        last = t == qi
        return jnp.where(last, qi + 1, qi), jnp.where(last, 0, t + 1)

    def scores(pair, slot):
        qi, t = pair
        variant = (t == qi).astype(jnp.int32)
        for mp in range(nmaps):
            k = k_ref[mp, variant, pl.ds(pl.multiple_of(t * tile, tile), tile), :]
            for hf in range(halves):
                c = mp * halves + hf
                s = jnp.dot(k, q_ref[mp, qi * halves + hf], preferred_element_type=_F32)
                s_buf[slot, c] = s
                mb_buf[slot, c] = jnp.max(s, axis=0, keepdims=True)

    def values(pair, pos):
        _, t = pair
        first = t == 0
        slot = pos % _SWEEP_SLOTS
        vt = vt_ref[t]
        for c in range(nch):
            m_prev = jnp.where(first, _NEG_INF, m_sc[c])
            m_new = jnp.maximum(m_prev, mb_buf[slot, c])
            alpha = jnp.exp2(m_prev - m_new)
            m_sc[c] = m_new
            p = jnp.exp2(s_buf[slot, c] - m_new).astype(_BF)
            acc_sc[pos, c] = (alpha * acc_sc[(pos - 1) % ticks, c]
                              + jnp.dot(vt, p, preferred_element_type=_F32))

    def finish(pair, pos):
        qi, t = pair

        @pl.when(t == qi)
        def _():
            if nmaps == 2:
                lp = lam_ref[...]
                lam = (jnp.exp(jnp.sum(lp[0:1] * lp[1:2], axis=-1, keepdims=True))
                       - jnp.exp(jnp.sum(lp[2:3] * lp[3:4], axis=-1, keepdims=True)) + lam_init)
            normed = lambda c: acc_sc[pos, c, :dv, :] / acc_sc[pos, c, dv:dv + 1, :]
            for hf in range(halves):
                if nmaps == 2:
                    ot = normed(hf) - lam * normed(halves + hf)
                    o = _rms(ot.T, sg_ref[...], _A_SUBLN_EPS) * (1.0 - lam_init)
                else:
                    o = normed(hf).T
                row = pl.multiple_of(qi * tile + hf * hw, hw)
                o_ref[pl.ds(row, hw), :] = o.astype(o_ref.dtype)

    zero = jnp.int32(0)
    pair0 = (zero, zero)
    scores(pair0, 0)
    if npairs == 1:
        values(pair0, 0)
        finish(pair0, 0)
        return
    pair1 = nxt(pair0)
    scores(pair1, 1)

    def tick(carry, r):
        pc, pb, pa = carry
        scores(pa, (2 + r) % _SWEEP_SLOTS)
        values(pc, r)
        return pb, pa, nxt(pa)

    def run_ticks(n, carry):
        done = []
        for r in range(n):
            done.append(carry[0])
            carry = tick(carry, r)
        for r, pair in enumerate(done):
            finish(pair, r)
        return carry

    nticks = npairs - 2
    carry = lax.fori_loop(0, nticks // ticks, lambda _, c: run_ticks(ticks, c),
                          (pair0, pair1, nxt(pair1)))
    carry = run_ticks(nticks % ticks, carry)
    before_last, last, _ = carry
    for pair, pos in ((before_last, (npairs - 2) % ticks), (last, (npairs - 1) % ticks)):
        values(pair, pos)
        finish(pair, pos)


def _sweep(q, k, vt, heads, dq, dv, *, halves, trip_ticks, lam=None, subln_g=None, lam_init=0.0):
    b, nmaps = q.shape[:2]
    _, _, nb, dve, tile = vt.shape
    s = nb * tile
    assert tile % _CHUNK == 0 and tile % halves == 0 and trip_ticks % _SWEEP_SLOTS == 0
    assert q.shape[2:] == (heads, s * halves // tile, dq, tile // halves) and dve == dv + _BF_ROWS
    nch = nmaps * halves
    hw = tile // halves
    once = pl.Buffered(1)
    in_specs = [pl.BlockSpec((None, nmaps, None, nb * halves, dq, hw),
                             lambda i, h: (i, 0, h, 0, 0, 0), pipeline_mode=once),
                pl.BlockSpec((None, nmaps, 2, s, dq), lambda i, h: (i, 0, 0, 0, h),
                             pipeline_mode=once),
                pl.BlockSpec((None, None, nb, dve, tile), lambda i, h: (i, h, 0, 0, 0),
                             pipeline_mode=once)]
    args = [q, k, vt]
    if nmaps == 2:
        in_specs += [_resident(lam.shape), _resident((1, dv))]
        args += [lam, subln_g.reshape(1, dv)]
    scratch = [pltpu.VMEM((_SWEEP_SLOTS, nch, tile, hw), _F32),
               pltpu.VMEM((_SWEEP_SLOTS, nch, 1, hw), _F32),
               pltpu.VMEM((nch, 1, hw), _F32), pltpu.VMEM((trip_ticks, nch, dve, hw), _F32)]
    return pl.pallas_call(
        functools.partial(_sweep_body, nmaps=nmaps, halves=halves, tile=tile, nb=nb,
                          lam_init=lam_init),
        grid=(b, heads),
        in_specs=in_specs,
        out_specs=pl.BlockSpec((None, s, dv), lambda i, h: (i, 0, h)),
        out_shape=jax.ShapeDtypeStruct((b, s, heads * dv), _BF),
        scratch_shapes=scratch,
        compiler_params=_params(("parallel", "parallel")),
        name="sweep_diff" if nmaps == 2 else "sweep_mla",
    )(*args)


def _mm_body(a_ref, w_ref, o_ref):
    o_ref[...] = jnp.dot(a_ref[...], w_ref[...], preferred_element_type=_F32).astype(o_ref.dtype)


def _mm(a, w, out_dtype):
    b, s, kdim = a.shape
    n = w.shape[1]
    tm = min(_ROW_TILE, s)
    return pl.pallas_call(
        _mm_body,
        grid=(b, s // tm),
        in_specs=[pl.BlockSpec((None, tm, kdim), lambda i, t: (i, t, 0)), _resident(w.shape)],
        out_specs=pl.BlockSpec((None, tm, n), lambda i, t: (i, t, 0)),
        out_shape=jax.ShapeDtypeStruct((b, s, n), out_dtype),
        compiler_params=_params(("parallel", "parallel")),
        name="matmul",
    )(a, w)


def _mla_proj_body(x_ref, g_ref, wd_ref, qg_ref, wq_ref, kvg_ref, wk_ref, wv_ref,
                   cos_ref, slo_ref, shi_ref, lat_ref, kr_ref, q_ref, k_ref, v_ref, *, qscale, mask_tile,
                   qt_cols):
    tm = x_ref.shape[0]
    h = _rms(x_ref[...], g_ref[...], _NORM_EPS).astype(_BF)
    cos, slo, shi = cos_ref[...], slo_ref[...], shi_ref[...]
    if mask_tile:
        row0 = pl.program_id(1) * tm
        q_ext = _mask_lanes(tm, row0, mask_tile, _B_ROPE, False)
        k_ext = _mask_lanes(tm, row0, mask_tile, _B_ROPE, True)
    else:
        q_ext = k_ext = jnp.zeros((tm, _LANES), _F32)
    dn = jnp.dot(h, wd_ref[...], preferred_element_type=_F32)
    e0, e1 = _B_Q_LORA, _B_Q_LORA + _B_KV_LORA
    cq = _rms(dn[:, :e0], qg_ref[...], _NORM_EPS).astype(_BF)
    lat = _rms(dn[:, e0:e1], kvg_ref[...], _NORM_EPS)
    kr = _rope_lanes(dn[:, e1:e1 + _LANES], cos, slo, shi)
    lat_ref[...] = lat
    kr_ref[...] = kr[:, :_B_ROPE]
    hw = 2 * _LANES
    for hd in range(_B_HEADS):
        yq = jnp.dot(cq, wq_ref[:, hd * hw:(hd + 1) * hw], preferred_element_type=_F32)
        q_nope = yq[:, :_LANES] * qscale
        q_rope = _rope_lanes(yq[:, _LANES:], cos, slo, shi) * qscale + q_ext
        if qt_cols:
            qt = jnp.concatenate([q_nope.T, q_rope.T], axis=0).astype(_BF)
            for g in range(tm // qt_cols):
                q_ref[hd, g] = qt[:, g * qt_cols:(g + 1) * qt_cols]
        else:
            q_ref[:, hd * hw:hd * hw + _LANES] = q_nope.astype(_BF)
            q_ref[:, hd * hw + _LANES:(hd + 1) * hw] = q_rope.astype(_BF)
    kin = jnp.concatenate([lat, kr], axis=1).astype(_BF)
    for hd in range(_B_HEADS):
        yk = jnp.dot(kin, wk_ref[:, hd * hw:(hd + 1) * hw], preferred_element_type=_F32)
        for var in range(2):
            k_ref[var, :, hd * hw:hd * hw + _LANES] = yk[:, :_LANES].astype(_BF)
        k_ref[0, :, hd * hw + _LANES:(hd + 1) * hw] = yk[:, _LANES:].astype(_BF)
        k_ref[1, :, hd * hw + _LANES:(hd + 1) * hw] = (yk[:, _LANES:] + k_ext).astype(_BF)
    latb = lat.astype(_BF)
    for c in range(0, wv_ref.shape[-1], _MXU_COLS):
        yv = jnp.dot(latb, wv_ref[:, c:c + _MXU_COLS], preferred_element_type=_F32)
        if qt_cols:
            for s in range(0, _MXU_COLS, _B_VD):
                v_ref[(c + s) // _B_VD, 0, :_B_VD, :] = yv[:, s:s + _B_VD].T.astype(_BF)
                v_ref[(c + s) // _B_VD, 0, _B_VD:, :] = jnp.ones((_BF_ROWS, tm), _BF)
        else:
            v_ref[:, c:c + _MXU_COLS] = yv.astype(_BF)


def _mla_weights(w_down, w_uq, w_uk, w_uv):
    d = w_down.shape[0]
    hw = 2 * _LANES
    wd = jnp.concatenate([w_down, jnp.zeros((d, _LANES - _B_ROPE), w_down.dtype)], axis=1)
    wq = w_uq.reshape(_B_Q_LORA, _B_HEADS, _B_NOPE + _B_ROPE)
    wq = jnp.pad(wq, ((0, 0), (0, 0), (0, hw - _B_NOPE - _B_ROPE))).reshape(_B_Q_LORA, _B_HEADS * hw)
    eye = jnp.eye(_B_ROPE, dtype=w_uk.dtype)
    wk_top = jnp.pad(w_uk, ((0, 0), (0, 0), (0, hw - _B_NOPE)))
    wk_mid = jnp.broadcast_to(jnp.pad(eye, ((0, 0), (_B_NOPE, hw - _B_NOPE - _B_ROPE)))[:, None, :],
                              (_B_ROPE, _B_HEADS, hw))
    wk_bot = jnp.zeros((_LANES - _B_ROPE, _B_HEADS, hw), w_uk.dtype)
    wk = jnp.concatenate([wk_top, wk_mid, wk_bot], axis=0).reshape(-1, _B_HEADS * hw)
    wv = w_uv.reshape(_B_KV_LORA, _B_HEADS * _B_VD)
    return wd.astype(_BF), wq.astype(_BF), wk.astype(_BF), wv.astype(_BF)


def _mla_proj(x, g, wts, q_norm_g, kv_norm_g, tabs, tm, mask_tile, qt_cols=0):
    b, s, d = x.shape
    wd, wq, wk, wv = wts
    xspec = pl.BlockSpec((None, tm, d), lambda i, t: (i, t, 0))
    tspec = pl.BlockSpec((tm, _LANES), lambda i, t: (t, 0))
    ospec = lambda n: pl.BlockSpec((None, tm, n), lambda i, t: (i, t, 0))
    oshape = lambda n, dt: jax.ShapeDtypeStruct((b, s, n), dt)
    if qt_cols:
        assert tm % qt_cols == 0
        hq = wq.shape[1] // _B_HEADS
        q_spec = pl.BlockSpec((None, _B_HEADS, tm // qt_cols, hq, qt_cols),
                              lambda i, t: (i, 0, t, 0, 0))
        q_shape = jax.ShapeDtypeStruct((b, _B_HEADS, s // qt_cols, hq, qt_cols), _BF)
        dve = _B_VD + _BF_ROWS
        v_spec = pl.BlockSpec((None, _B_HEADS, 1, dve, tm), lambda i, t: (i, 0, t, 0, 0))
        v_shape = jax.ShapeDtypeStruct((b, _B_HEADS, s // tm, dve, tm), _BF)
    else:
        q_spec, q_shape = ospec(wq.shape[1]), oshape(wq.shape[1], _BF)
        v_spec, v_shape = ospec(wv.shape[1]), oshape(wv.shape[1], _BF)
    return pl.pallas_call(
        functools.partial(_mla_proj_body, qscale=_LOG2E / math.sqrt(_B_NOPE + _B_ROPE),
                          mask_tile=mask_tile, qt_cols=qt_cols),
        grid=(b, s // tm),
        in_specs=[xspec, _resident((1, d)), _resident(wd.shape), _resident((1, _B_Q_LORA)),
                  _resident(wq.shape), _resident((1, _B_KV_LORA)), _resident(wk.shape),
                  _resident(wv.shape), tspec, tspec, tspec],
        out_specs=[ospec(_B_KV_LORA), ospec(_B_ROPE), q_spec,
                   pl.BlockSpec((None, 2, tm, wk.shape[1]), lambda i, t: (i, 0, t, 0)), v_spec],
        out_shape=[oshape(_B_KV_LORA, _F32), oshape(_B_ROPE, _F32), q_shape,
                   jax.ShapeDtypeStruct((b, 2, s, wk.shape[1]), _BF), v_shape],
        compiler_params=_params(("parallel", "parallel")),
        name="mla_proj",
    )(x, g.reshape(1, d), wd, q_norm_g.reshape(1, -1), wq, kv_norm_g.reshape(1, -1), wk, wv, *tabs)


def _delayed(carry_ref, cols, cur, shifts):
    nseg = carry_ref.shape[0]
    seg = cur.shape[0] // nseg
    outs = [[] for _ in shifts]
    for sg in range(nseg):
        part = cur[sg * seg:(sg + 1) * seg]
        prev = carry_ref[sg, :, cols]
        for o, k in zip(outs, shifts):
            o.append(_shift_rows(prev, part, k))
        carry_ref[sg, :, cols] = part[seg - _SUBLANES:]
    return [o[0] if nseg == 1 else jnp.concatenate(o, axis=0) for o in outs]


def _conv_mix_body(x_ref, g_ref, win_ref, cw_ref, wout_ref, hist_ref, o_ref, hout_ref,
                   carry_sc, z_sc):
    tm, d = x_ref.shape

    @pl.when(pl.program_id(1) == 0)
    def _():
        carry_sc[...] = hist_ref[...]

    x = x_ref[...]
    h = _rms(x, g_ref[...], _NORM_EPS).astype(_BF)
    cw = _MXU_COLS
    for c in range(0, d, cw):
        gate_b = jnp.dot(h, win_ref[:, c:c + cw], preferred_element_type=_F32)
        gate_c = jnp.dot(h, win_ref[:, d + c:d + c + cw], preferred_element_type=_F32)
        val = jnp.dot(h, win_ref[:, 2 * d + c:2 * d + c + cw], preferred_element_type=_F32)
        u = gate_c * val
        u2, u1 = _delayed(carry_sc, slice(c, c + cw), u, (2, 1))
        y = u2 * cw_ref[0:1, c:c + cw] + u1 * cw_ref[1:2, c:c + cw] + u * cw_ref[2:3, c:c + cw]
        z_sc[:, c:c + cw] = (gate_b * y).astype(_BF)
    hout_ref[...] = carry_sc[...]
    o_ref[...] = x + jnp.dot(z_sc[...], wout_ref[...], preferred_element_type=_F32)


def _conv_mix(x, g, w_in, conv_w, w_out, hist, tm):
    b, s, d = x.shape
    nseg = hist.shape[1]
    assert nseg == 1 or tm == s
    xspec = pl.BlockSpec((None, tm, d), lambda i, t: (i, t, 0))
    hspec = pl.BlockSpec((None, nseg, _SUBLANES, d), lambda i, t: (i, 0, 0, 0))
    return pl.pallas_call(
        _conv_mix_body,
        grid=(b, s // tm),
        in_specs=[xspec, _resident((1, d)), _resident(w_in.shape), _resident(conv_w.shape),
                  _resident(w_out.shape), hspec],
        out_specs=[xspec, hspec],
        out_shape=[jax.ShapeDtypeStruct((b, s, d), _F32),
                   jax.ShapeDtypeStruct((b, nseg, _SUBLANES, d), _F32)],
        scratch_shapes=[pltpu.VMEM((nseg, _SUBLANES, d), _F32), pltpu.VMEM((tm, d), _BF)],
        compiler_params=_params(("arbitrary", "arbitrary")),
        name="conv_mixer",
    )(x, g.reshape(1, d), w_in, conv_w, w_out, hist)


_POOL_HALO = 16


def _pool_mix_body(x_ref, g_ref, wg_ref, sc_ref, hist_ref, o_ref, hout_ref, carry_sc, *, hist_valid):
    tm, d = x_ref.shape
    t = pl.program_id(1)

    @pl.when(t == 0)
    def _():
        carry_sc[...] = hist_ref[...]

    x = x_ref[...]
    h = _rms(x, g_ref[...], _NORM_EPS)
    gw = d // len(_D_WINDOWS)
    tpos = t * tm + lax.broadcasted_iota(jnp.int32, (tm, gw), 0)
    outs = []
    for gi, w in enumerate(_D_WINDOWS):
        hg = h[:, gi * gw:(gi + 1) * gw]
        acc = jnp.concatenate([carry_sc[:, gi * gw:(gi + 1) * gw], hg], axis=0)
        k = 1
        while k < w:
            acc = acc + pltpu.roll(acc, k, axis=0)
            k *= 2
        cnt = jnp.minimum(tpos + (hist_valid + 1), w).astype(_F32)
        pooled = acc[_POOL_HALO:] / cnt
        outs.append(jnp.dot((pooled - hg).astype(_BF), wg_ref[gi], preferred_element_type=_F32))
    o_ref[...] = x + jnp.concatenate(outs, axis=1) * sc_ref[...]
    tail = h[tm - _POOL_HALO:, :]
    carry_sc[...] = tail
    hout_ref[...] = tail


def _pool_mix(x, g, w_group, scale, hist16, hist_valid, tm):
    b, s, d = x.shape
    assert all(w & (w - 1) == 0 and w <= _POOL_HALO for w in _D_WINDOWS)
    xspec = pl.BlockSpec((None, tm, d), lambda i, t: (i, t, 0))
    hspec = pl.BlockSpec((None, _POOL_HALO, d), lambda i, t: (i, 0, 0))
    return pl.pallas_call(
        functools.partial(_pool_mix_body, hist_valid=hist_valid),
        grid=(b, s // tm),
        in_specs=[xspec, _resident((1, d)), _resident(w_group.shape), _resident((1, d)), hspec],
        out_specs=[xspec, hspec],
        out_shape=[jax.ShapeDtypeStruct((b, s, d), _F32),
                   jax.ShapeDtypeStruct((b, _POOL_HALO, d), _F32)],
        scratch_shapes=[pltpu.VMEM((_POOL_HALO, d), _F32)],
        compiler_params=_params(("arbitrary", "arbitrary")),
        name="pool_mixer",
    )(x, g.reshape(1, d), w_group, scale.reshape(1, d), hist16)


def _ffn_body(*refs, final, mixed):
    refs = list(refs)
    x_ref, g_ref, wg_ref, wu_ref, cw_ref, cb_ref, wd_ref, hist_ref = refs[:8]
    del refs[:8]
    a_ref, wo_ref = (refs.pop(0), refs.pop(0)) if mixed else (None, None)
    fg_ref = refs.pop(0) if final else None
    o_ref, hout_ref, carry_sc, act_sc = refs
    tm = x_ref.shape[0]
    f = wg_ref.shape[1]

    @pl.when(pl.program_id(1) == 0)
    def _():
        carry_sc[...] = hist_ref[...]

    x = x_ref[...]
    if mixed:
        x = x + jnp.dot(a_ref[...], wo_ref[...], preferred_element_type=_F32)
    h = _rms(x, g_ref[...], _NORM_EPS).astype(_BF)
    cw = _MXU_COLS
    for c in range(0, f, cw):
        gate = jnp.dot(h, wg_ref[:, c:c + cw], preferred_element_type=_F32)
        up = jnp.dot(h, wu_ref[:, c:c + cw], preferred_element_type=_F32)
        g2, g1 = _delayed(carry_sc, slice(c, c + cw), gate, (2, 1))
        y = (g2 * cw_ref[0:1, c:c + cw] + g1 * cw_ref[1:2, c:c + cw]
             + gate * cw_ref[2:3, c:c + cw] + cb_ref[:, c:c + cw])
        act_sc[:, c:c + cw] = (y / (1.0 + jnp.exp(-y)) * up).astype(_BF)
    hout_ref[...] = carry_sc[...]
    out = x + jnp.dot(act_sc[...], wd_ref[...], preferred_element_type=_F32)
    if final:
        out = _rms(out, fg_ref[...], _NORM_EPS)
    o_ref[...] = out


def _ffn(x, g, w_gate, w_up, conv_w, conv_b, w_down, hist, final_g, tm, mixer_out=None, w_o=None):
    b, s, d = x.shape
    f = w_gate.shape[1]
    nseg = hist.shape[1]
    assert f % _MXU_COLS == 0 and (nseg == 1 or tm == s)
    xspec = pl.BlockSpec((None, tm, d), lambda i, t: (i, t, 0))
    hspec = pl.BlockSpec((None, nseg, _SUBLANES, f), lambda i, t: (i, 0, 0, 0))
    in_specs = [xspec, _resident((1, d)), _resident(w_gate.shape), _resident(w_up.shape),
                _resident(conv_w.shape), _resident((1, f)), _resident(w_down.shape), hspec]
    args = [x, g.reshape(1, d), w_gate, w_up, conv_w, conv_b.reshape(1, f), w_down, hist]
    if mixer_out is not None:
        in_specs += [pl.BlockSpec((None, tm, mixer_out.shape[-1]), lambda i, t: (i, t, 0)),
                     _resident(w_o.shape)]
        args += [mixer_out, w_o]
    if final_g is not None:
        in_specs.append(_resident((1, d)))
        args.append(final_g.reshape(1, d))
    return pl.pallas_call(
        functools.partial(_ffn_body, final=final_g is not None, mixed=mixer_out is not None),
        grid=(b, s // tm),
        in_specs=in_specs,
        out_specs=[xspec, hspec],
        out_shape=[jax.ShapeDtypeStruct((b, s, d), _F32),
                   jax.ShapeDtypeStruct((b, nseg, _SUBLANES, f), _F32)],
        scratch_shapes=[pltpu.VMEM((nseg, _SUBLANES, f), _F32), pltpu.VMEM((tm, f), _BF)],
        compiler_params=_params(("arbitrary", "arbitrary")),
        name="conv_ffn",
    )(*args)


def _pad_hist(hist, rows):
    return jnp.pad(hist, ((0, 0), (rows - hist.shape[1], 0), (0, 0)))


def kernel(x_prompt, x_sample, cache_a_k, cache_a_v, cache_b_latent, cache_b_krope, state_c_conv, state_d_pool, state_ffn_conv, norm_mix_g, norm_ffn_g, norm_final_g, a_w_qkv, a_lam, a_subln_g, a_w_o, b_w_down, b_q_norm_g, b_w_uq, b_kv_norm_g, b_w_uk, b_w_uv, b_w_o, c_w_in, c_conv_w, c_w_out, d_w_group, d_scale, ffn_w_gate, ffn_w_up, ffn_conv_w, ffn_conv_b, ffn_w_down):
    depth = norm_mix_g.shape[0]
    n_p, seq, d = x_prompt.shape
    n_s, t_new, _ = x_sample.shape
    past = cache_a_k.shape[2]
    f = ffn_w_gate.shape[-1]
    tm_p, tm_s = min(_ROW_TILE, seq), t_new
    ta = min(_ATT_TILE, seq)
    tabs_p = _rope_tables(jnp.arange(seq, dtype=jnp.int32))
    tabs_s = _rope_tables(past + jnp.arange(t_new, dtype=jnp.int32))
    xp, xs = x_prompt, x_sample
    flat = lambda a: a.reshape(1, n_s * t_new, a.shape[-1])
    outs = {k: [] for k in ("ak_p", "av_p", "bl_p", "br_p", "cc_p", "dp_p", "fc_p",
                            "ak_s", "av_s", "bl_s", "br_s", "cc_s", "dp_s", "fc_s")}
    for i in range(depth):
        m, j = i % _N_MIXERS, i // _N_MIXERS
        g_mix = norm_mix_g[i]
        mix_p, mix_s = {}, {}
        if m == 0:
            lam_init = 0.8 - 0.6 * math.exp(-0.3 * i)
            w_qkv, w_o = a_w_qkv[j].astype(_BF), a_w_o[j].astype(_BF)
            qt, kf, km, vf, vt = _diff_proj(xp, g_mix, w_qkv, tabs_p, ta, ta // _SWEEP_HALVES)
            op = _sweep(qt, km, vt, _A_HEADS, _LANES, 2 * _A_HD, halves=_SWEEP_HALVES,
                        trip_ticks=_TRIP_TICKS_DIFF,
                        lam=a_lam[j], subln_g=a_subln_g[j], lam_init=lam_init)
            mix_p = dict(mixer_out=op, w_o=w_o)
            outs["ak_p"].append(kf.reshape(n_p, seq, _A_HEADS, 2, _A_HD))
            outs["av_p"].append(vf.reshape(n_p, seq, _A_HEADS, 2 * _A_HD))
            qm, kf, vf = _diff_proj(xs, g_mix, w_qkv, tabs_s, tm_s)
            osm = _cached_diff(qm, cache_a_k[j].reshape(n_s, past, d), kf, cache_a_v[j], vf,
                               a_lam[j], a_subln_g[j], lam_init)
            mix_s = dict(mixer_out=flat(osm), w_o=w_o)
            outs["ak_s"].append(kf.reshape(n_s, t_new, _A_HEADS, 2, _A_HD))
            outs["av_s"].append(vf.reshape(n_s, t_new, _A_HEADS, 2 * _A_HD))
        elif m == 1:
            wts = _mla_weights(b_w_down[j], b_w_uq[j], b_w_uk[j], b_w_uv[j])
            w_o = b_w_o[j].astype(_BF)
            hq = 2 * _LANES
            lat, kr, qt, k2, vt = _mla_proj(xp, g_mix, wts, b_q_norm_g[j], b_kv_norm_g[j], tabs_p,
                                            ta, ta, ta // _SWEEP_HALVES)
            op = _sweep(qt[:, None], k2[:, None], vt, _B_HEADS, hq, _B_VD,
                        halves=_SWEEP_HALVES, trip_ticks=_TRIP_TICKS_MLA)
            mix_p = dict(mixer_out=op, w_o=w_o)
            outs["bl_p"].append(lat)
            outs["br_p"].append(kr)
            lat, kr, q, k2, v = _mla_proj(xs, g_mix, wts, b_q_norm_g[j], b_kv_norm_g[j], tabs_s,
                                          tm_s, 0)
            kin = jnp.concatenate([cache_b_latent[j], cache_b_krope[j],
                                   jnp.zeros((n_s, past, _LANES - _B_ROPE), _F32)], axis=-1).astype(_BF)
            k_cache = _mm(kin, wts[2], out_dtype=_BF)
            v_cache = _mm(cache_b_latent[j].astype(_BF), wts[3], out_dtype=_BF)
            k_all = jnp.concatenate([k_cache, k2[:, 0]], axis=1)
            v_all = jnp.concatenate([v_cache, v], axis=1)
            osm = _cached_mla(q, k_all, v_all, _B_HEADS)
            mix_s = dict(mixer_out=flat(osm), w_o=w_o)
            outs["bl_s"].append(lat)
            outs["br_s"].append(kr)
        elif m == 2:
            w_in, w_out = c_w_in[j].astype(_BF), c_w_out[j].astype(_BF)
            xp, hc = _conv_mix(xp, g_mix, w_in, c_conv_w[j], w_out,
                               jnp.zeros((n_p, 1, _SUBLANES, d), _F32), tm_p)
            outs["cc_p"].append(hc[:, 0, -2:])
            xs, hc = _conv_mix(flat(xs), g_mix, w_in, c_conv_w[j], w_out,
                               _pad_hist(state_c_conv[j], _SUBLANES)[None], n_s * t_new)
            xs = xs.reshape(n_s, t_new, d)
            outs["cc_s"].append(hc[0, :, -2:])
        else:
            w_grp = d_w_group[j].astype(_BF)
            xp, hd = _pool_mix(xp, g_mix, w_grp, d_scale[j],
                               jnp.zeros((n_p, _POOL_HALO, d), _F32), 0, tm_p)
            outs["dp_p"].append(hd[:, -_D_HIST:])
            xs, hd = _pool_mix(xs, g_mix, w_grp, d_scale[j],
                               _pad_hist(state_d_pool[j], _POOL_HALO), _D_HIST, tm_s)
            outs["dp_s"].append(hd[:, -_D_HIST:])
        wg, wu, wd = ffn_w_gate[i].astype(_BF), ffn_w_up[i].astype(_BF), ffn_w_down[i].astype(_BF)
        final_g = norm_final_g if i == depth - 1 else None
        xp, hf = _ffn(xp, norm_ffn_g[i], wg, wu, ffn_conv_w[i], ffn_conv_b[i], wd,
                      jnp.zeros((n_p, 1, _SUBLANES, f), _F32), final_g, tm_p, **mix_p)
        outs["fc_p"].append(hf[:, 0, -2:])
        xs, hf = _ffn(flat(xs), norm_ffn_g[i], wg, wu, ffn_conv_w[i], ffn_conv_b[i], wd,
                      _pad_hist(state_ffn_conv[i], _SUBLANES)[None], final_g, n_s * t_new, **mix_s)
        xs = xs.reshape(n_s, t_new, d)
        outs["fc_s"].append(hf[0, :, -2:])
    st = lambda k: jnp.stack(outs[k])
    return (xp, xs, st("ak_p"), st("av_p"), st("bl_p"), st("br_p"), st("cc_p"), st("dp_p"), st("fc_p"),
            st("ak_s"), st("av_s"), st("bl_s"), st("br_s"), st("cc_s"), st("dp_s"), st("fc_s"))
```

```python
import functools
import math

import jax
import jax.numpy as jnp
from jax import lax
from jax.experimental import pallas as pl
from jax.experimental.pallas import tpu as pltpu

_BF = jnp.bfloat16
_F32 = jnp.float32

_CHUNK = 64
_ROPE_THETA = 10000.0
_NORM_EPS = 1e-6
_NEG_INF = -1e30
_A_HEADS = 8
_A_HD = 64
_A_SUBLN_EPS = 1e-5
_B_HEADS = 8
_B_NOPE = 128
_B_ROPE = 64
_B_VD = 128
_B_Q_LORA = 384
_B_KV_LORA = 256
_D_WINDOWS = (2, 4, 8, 16)
_D_HIST = 15
_N_MIXERS = 4
_LOG2E = math.log2(math.e)

_LANES = 128
_SUBLANES = 8
_MXU_COLS = 256
_BF_ROWS = 16
_SWEEP_SLOTS = 4
_TRIP_TICKS_DIFF = 8
_TRIP_TICKS_MLA = 16
_SWEEP_HALVES = 1
_VMEM_LIMIT = 56 * 1024 * 1024

_ROW_TILE = 1024
_ATT_TILE = 512


def _params(sem):
    return pltpu.CompilerParams(dimension_semantics=sem, vmem_limit_bytes=_VMEM_LIMIT)


def _resident(shape):
    nd = len(shape)
    return pl.BlockSpec(shape, lambda *_: (0,) * nd, pipeline_mode=pl.Buffered(1))


def _rms(x, g, eps):
    ms = jnp.mean(x * x, axis=-1, keepdims=True)
    return x * lax.rsqrt(ms + eps) * g


def _rope_lanes(y, cos, sin_lo, sin_hi):
    return (y * cos + pltpu.roll(y, _LANES - 32, axis=1) * sin_lo
            + pltpu.roll(y, 32, axis=1) * sin_hi)


def _rope_tables(pos):
    d = _A_HD
    inv = jnp.power(_ROPE_THETA, -jnp.arange(0, d, 2, dtype=_F32) / d)
    ang = pos.astype(_F32)[:, None] * inv[None, :]
    cos, sin = jnp.cos(ang), jnp.sin(ang)
    zero = jnp.zeros_like(sin)
    cos_t = jnp.tile(cos, (1, 4))
    sin_lo = jnp.tile(jnp.concatenate([-sin, zero], axis=1), (1, 2))
    sin_hi = jnp.tile(jnp.concatenate([zero, sin], axis=1), (1, 2))
    return cos_t, sin_lo, sin_hi


def _shift_rows(carry, cur, k):
    ext = jnp.concatenate([carry, cur], axis=0)
    return pltpu.roll(ext, k, axis=0)[carry.shape[0]:]


def _mask_lanes(rows, row0, tile, base, for_keys):
    r = row0 + lax.broadcasted_iota(jnp.int32, (rows, _LANES), 0)
    chunk = (r % tile) // _CHUNK
    c = lax.broadcasted_iota(jnp.int32, (rows, _LANES), 1) - base
    n = tile // _CHUNK
    if for_keys:
        return jnp.where(c == chunk, 1.0, 0.0)
    return jnp.where(c > chunk, jnp.where(c < n, _NEG_INF, 0.0), 0.0)


def _diff_proj_body(x_ref, g_ref, w_ref, cos_ref, slo_ref, shi_ref, *out_refs, qscale, qt_cols):
    tm, d = x_ref.shape
    if qt_cols:
        qm_ref, kf_ref, km_ref, vf_ref, vt_ref = out_refs
        row0 = pl.program_id(1) * tm
        q_ext = _mask_lanes(tm, row0, tm, _A_HD, False)
        k_ext = _mask_lanes(tm, row0, tm, _A_HD, True).astype(_BF)
        ones = jnp.ones((_BF_ROWS, tm), _BF)
    else:
        qm_ref, kf_ref, vf_ref = out_refs
    h = _rms(x_ref[...], g_ref[...], _NORM_EPS).astype(_BF)
    cos, slo, shi = cos_ref[...], slo_ref[...], shi_ref[...]
    first_half = lax.broadcasted_iota(jnp.int32, (tm, _LANES), 1) < _A_HD
    zero = jnp.zeros((tm, _LANES), _F32)
    cw = _MXU_COLS
    for c in range(0, d, cw):
        yq = jnp.dot(h, w_ref[:, c:c + cw], preferred_element_type=_F32)
        yk = jnp.dot(h, w_ref[:, d + c:d + c + cw], preferred_element_type=_F32)
        yv = jnp.dot(h, w_ref[:, 2 * d + c:2 * d + c + cw], preferred_element_type=_F32)
        vf_ref[:, c:c + cw] = yv
        for s in range(0, cw, _LANES):
            cols = slice(c + s, c + s + _LANES)
            head = (c + s) // _LANES
            q = _rope_lanes(yq[:, s:s + _LANES], cos, slo, shi) * qscale
            k = _rope_lanes(yk[:, s:s + _LANES], cos, slo, shi)
            kf_ref[:, cols] = k
            if not qt_cols:
                qm_ref[0, :, cols] = jnp.where(first_half, q, zero).astype(_BF)
                qm_ref[1, :, cols] = jnp.where(first_half, zero, q).astype(_BF)
                continue
            for mp in range(2):
                qh = q if mp == 0 else pltpu.roll(q, _A_HD, axis=1)
                kh = k if mp == 0 else pltpu.roll(k, _A_HD, axis=1)
                qt = jnp.where(first_half, qh, q_ext).T.astype(_BF)
                for g in range(tm // qt_cols):
                    qm_ref[mp, head, g] = qt[:, g * qt_cols:(g + 1) * qt_cols]
                k0 = jnp.where(first_half, kh, zero).astype(_BF)
                km_ref[mp, 0, :, cols] = k0
                km_ref[mp, 1, :, cols] = k0 + k_ext
            vt_ref[head, 0, :_LANES, :] = yv[:, s:s + _LANES].T.astype(_BF)
            vt_ref[head, 0, _LANES:, :] = ones


def _diff_proj(x, g, w_qkv, tabs, tm, qt_cols=0):
    b, s, d = x.shape
    heads = d // _LANES
    row = lambda dt: jax.ShapeDtypeStruct((b, s, d), dt)
    xspec = pl.BlockSpec((None, tm, d), lambda i, t: (i, t, 0))
    tspec = pl.BlockSpec((tm, _LANES), lambda i, t: (t, 0))
    if qt_cols:
        assert tm % qt_cols == 0
        dve = _LANES + _BF_ROWS
        out_specs = [pl.BlockSpec((None, 2, heads, tm // qt_cols, _LANES, qt_cols),
                                  lambda i, t: (i, 0, 0, t, 0, 0)),
                     xspec, pl.BlockSpec((None, 2, 2, tm, d), lambda i, t: (i, 0, 0, t, 0)), xspec,
                     pl.BlockSpec((None, heads, 1, dve, tm), lambda i, t: (i, 0, t, 0, 0))]
        out_shape = [jax.ShapeDtypeStruct((b, 2, heads, s // qt_cols, _LANES, qt_cols), _BF),
                     row(_F32), jax.ShapeDtypeStruct((b, 2, 2, s, d), _BF), row(_F32),
                     jax.ShapeDtypeStruct((b, heads, s // tm, dve, tm), _BF)]
    else:
        out_specs = [pl.BlockSpec((None, 2, tm, d), lambda i, t: (i, 0, t, 0)), xspec, xspec]
        out_shape = [jax.ShapeDtypeStruct((b, 2, s, d), _BF), row(_F32), row(_F32)]
    return pl.pallas_call(
        functools.partial(_diff_proj_body, qscale=_LOG2E / math.sqrt(_A_HD), qt_cols=qt_cols),
        grid=(b, s // tm),
        in_specs=[xspec, _resident((1, d)), _resident(w_qkv.shape), tspec, tspec, tspec],
        out_specs=out_specs,
        out_shape=out_shape,
        compiler_params=_params(("parallel", "parallel")),
        name="diff_proj",
    )(x, g.reshape(1, d), w_qkv, *tabs)


def _cached_diff_body(q_ref, kc_ref, kn_ref, vc_ref, vn_ref, lam_ref, sg_ref, o_ref, *, lam_init):
    nt = (((1,), (1,)), ((), ()))
    lp = lam_ref[...]
    lam = (jnp.exp(jnp.sum(lp[0:1] * lp[1:2], axis=-1, keepdims=True))
           - jnp.exp(jnp.sum(lp[2:3] * lp[3:4], axis=-1, keepdims=True)) + lam_init)
    for hd in range(o_ref.shape[-1] // _LANES):
        cols = slice(hd * _LANES, (hd + 1) * _LANES)
        kc, kn = kc_ref[:, cols].astype(_BF), kn_ref[:, cols].astype(_BF)
        vc, vn = vc_ref[:, hd, :].astype(_BF), vn_ref[:, cols].astype(_BF)
        outs = []
        for c in range(2):
            q = q_ref[c, :, cols]
            sc = lax.dot_general(q, kc, nt, preferred_element_type=_F32)
            sn = lax.dot_general(q, kn, nt, preferred_element_type=_F32)
            m = jnp.maximum(jnp.max(sc, axis=-1, keepdims=True), jnp.max(sn, axis=-1, keepdims=True))
            pc, pn = jnp.exp2(sc - m), jnp.exp2(sn - m)
            l = jnp.sum(pc, axis=-1, keepdims=True) + jnp.sum(pn, axis=-1, keepdims=True)
            acc = (jnp.dot(pc.astype(_BF), vc, preferred_element_type=_F32)
                   + jnp.dot(pn.astype(_BF), vn, preferred_element_type=_F32))
            outs.append(acc / l)
        o = _rms(outs[0] - lam * outs[1], sg_ref[...], _A_SUBLN_EPS) * (1.0 - lam_init)
        o_ref[:, cols] = o.astype(o_ref.dtype)


def _cached_diff(qm, k_cache, k_new, v_cache, v_new, lam, subln_g, lam_init):
    b, _, t, d = qm.shape
    past, heads = v_cache.shape[1:3]
    blk = lambda rows: pl.BlockSpec((None, rows, d), lambda i: (i, 0, 0))
    vblk = pl.BlockSpec((None, past, heads, _LANES), lambda i: (i, 0, 0, 0))
    return pl.pallas_call(
        functools.partial(_cached_diff_body, lam_init=lam_init),
        grid=(b,),
        in_specs=[pl.BlockSpec((None, 2, t, d), lambda i: (i, 0, 0, 0)),
                  blk(past), blk(t), vblk, blk(t), _resident(lam.shape), _resident((1, _LANES))],
        out_specs=blk(t),
        out_shape=jax.ShapeDtypeStruct((b, t, d), _BF),
        compiler_params=_params(("parallel",)),
        name="cached_diff",
    )(qm, k_cache, k_new, v_cache, v_new, lam, subln_g.reshape(1, _LANES))


def _cached_mla_body(q_ref, k_ref, v_ref, o_ref, *, heads):
    dq, dv = q_ref.shape[-1] // heads, v_ref.shape[-1] // heads
    for hd in range(heads):
        s = lax.dot_general(q_ref[:, hd * dq:(hd + 1) * dq], k_ref[:, hd * dq:(hd + 1) * dq],
                            (((1,), (1,)), ((), ())), preferred_element_type=_F32)
        p = jnp.exp2(s - jnp.max(s, axis=-1, keepdims=True))
        acc = jnp.dot(p.astype(_BF), v_ref[:, hd * dv:(hd + 1) * dv], preferred_element_type=_F32)
        o_ref[:, hd * dv:(hd + 1) * dv] = (acc / jnp.sum(p, axis=-1, keepdims=True)).astype(o_ref.dtype)


def _cached_mla(q, k, v, heads):
    b, t, _ = q.shape
    whole = lambda a: pl.BlockSpec((None,) + a.shape[1:], lambda i: (i, 0, 0))
    return pl.pallas_call(
        functools.partial(_cached_mla_body, heads=heads),
        grid=(b,),
        in_specs=[whole(q), whole(k), whole(v)],
        out_specs=pl.BlockSpec((None, t, v.shape[-1]), lambda i: (i, 0, 0)),
        out_shape=jax.ShapeDtypeStruct((b, t, v.shape[-1]), _BF),
        compiler_params=_params(("parallel",)),
        name="cached_mla",
    )(q, k, v)


def _sweep_body(*refs, nmaps, halves, tile, nb, lam_init):
    if nmaps == 2:
        q_ref, k_ref, vt_ref, lam_ref, sg_ref, o_ref, s_buf, mb_buf, m_sc, acc_sc = refs
    else:
        q_ref, k_ref, vt_ref, o_ref, s_buf, mb_buf, m_sc, acc_sc = refs
    hw = tile // halves
    nch = nmaps * halves
    dv = o_ref.shape[-1]
    ticks = acc_sc.shape[0]
    npairs = nb * (nb + 1) // 2
    acc_sc[...] = jnp.zeros(acc_sc.shape, _F32)

    def nxt(pair):
        qi, t = pair
        last = t == qi
        return jnp.where(last, qi + 1, qi), jnp.where(last, 0, t + 1)

    def scores(pair, slot):
        qi, t = pair
        variant = (t == qi).astype(jnp.int32)
        for mp in range(nmaps):
            k = k_ref[mp, variant, pl.ds(pl.multiple_of(t * tile, tile), tile), :]
            for hf in range(halves):
                c = mp * halves + hf
                s = jnp.dot(k, q_ref[mp, qi * halves + hf], preferred_element_type=_F32)
                s_buf[slot, c] = s
                mb_buf[slot, c] = jnp.max(s, axis=0, keepdims=True)

    def values(pair, pos):
        _, t = pair
        first = t == 0
        slot = pos % _SWEEP_SLOTS
        vt = vt_ref[t]
        for c in range(nch):
            m_prev = jnp.where(first, _NEG_INF, m_sc[c])
            m_new = jnp.maximum(m_prev, mb_buf[slot, c])
            alpha = jnp.exp2(m_prev - m_new)
            m_sc[c] = m_new
            p = jnp.exp2(s_buf[slot, c] - m_new).astype(_BF)
            acc_sc[pos, c] = (alpha * acc_sc[(pos - 1) % ticks, c]
                              + jnp.dot(vt, p, preferred_element_type=_F32))

    def finish(pair, pos):
        qi, t = pair

        @pl.when(t == qi)
        def _():
            if nmaps == 2:
                lp = lam_ref[...]
                lam = (jnp.exp(jnp.sum(lp[0:1] * lp[1:2], axis=-1, keepdims=True))
                       - jnp.exp(jnp.sum(lp[2:3] * lp[3:4], axis=-1, keepdims=True)) + lam_init)
            normed = lambda c: acc_sc[pos, c, :dv, :] / acc_sc[pos, c, dv:dv + 1, :]
            for hf in range(halves):
                if nmaps == 2:
                    ot = normed(hf) - lam * normed(halves + hf)
                    o = _rms(ot.T, sg_ref[...], _A_SUBLN_EPS) * (1.0 - lam_init)
                else:
                    o = normed(hf).T
                row = pl.multiple_of(qi * tile + hf * hw, hw)
                o_ref[pl.ds(row, hw), :] = o.astype(o_ref.dtype)

    zero = jnp.int32(0)
    pair0 = (zero, zero)
    scores(pair0, 0)
    if npairs == 1:
        values(pair0, 0)
        finish(pair0, 0)
        return
    pair1 = nxt(pair0)
    scores(pair1, 1)

    def tick(carry, r):
        pc, pb, pa = carry
        scores(pa, (2 + r) % _SWEEP_SLOTS)
        values(pc, r)
        return pb, pa, nxt(pa)

    def run_ticks(n, carry):
        done = []
        for r in range(n):
            done.append(carry[0])
            carry = tick(carry, r)
        for r, pair in enumerate(done):
            finish(pair, r)
        return carry

    nticks = npairs - 2
    carry = lax.fori_loop(0, nticks // ticks, lambda _, c: run_ticks(ticks, c),
                          (pair0, pair1, nxt(pair1)))
    carry = run_ticks(nticks % ticks, carry)
    before_last, last, _ = carry
    for pair, pos in ((before_last, (npairs - 2) % ticks), (last, (npairs - 1) % ticks)):
        values(pair, pos)
        finish(pair, pos)


def _sweep(q, k, vt, heads, dq, dv, *, halves, trip_ticks, lam=None, subln_g=None, lam_init=0.0):
    b, nmaps = q.shape[:2]
    _, _, nb, dve, tile = vt.shape
    s = nb * tile
    assert tile % _CHUNK == 0 and tile % halves == 0 and trip_ticks % _SWEEP_SLOTS == 0
    assert q.shape[2:] == (heads, s * halves // tile, dq, tile // halves) and dve == dv + _BF_ROWS
    nch = nmaps * halves
    hw = tile // halves
    once = pl.Buffered(1)
    in_specs = [pl.BlockSpec((None, nmaps, None, nb * halves, dq, hw),
                             lambda i, h: (i, 0, h, 0, 0, 0), pipeline_mode=once),
                pl.BlockSpec((None, nmaps, 2, s, dq), lambda i, h: (i, 0, 0, 0, h),
                             pipeline_mode=once),
                pl.BlockSpec((None, None, nb, dve, tile), lambda i, h: (i, h, 0, 0, 0),
                             pipeline_mode=once)]
    args = [q, k, vt]
    if nmaps == 2:
        in_specs += [_resident(lam.shape), _resident((1, dv))]
        args += [lam, subln_g.reshape(1, dv)]
    scratch = [pltpu.VMEM((_SWEEP_SLOTS, nch, tile, hw), _F32),
               pltpu.VMEM((_SWEEP_SLOTS, nch, 1, hw), _F32),
               pltpu.VMEM((nch, 1, hw), _F32), pltpu.VMEM((trip_ticks, nch, dve, hw), _F32)]
    return pl.pallas_call(
        functools.partial(_sweep_body, nmaps=nmaps, halves=halves, tile=tile, nb=nb,
                          lam_init=lam_init),
        grid=(b, heads),
        in_specs=in_specs,
        out_specs=pl.BlockSpec((None, s, dv), lambda i, h: (i, 0, h)),
        out_shape=jax.ShapeDtypeStruct((b, s, heads * dv), _BF),
        scratch_shapes=scratch,
        compiler_params=_params(("parallel", "parallel")),
        name="sweep_diff" if nmaps == 2 else "sweep_mla",
    )(*args)


def _mm_body(a_ref, w_ref, o_ref):
    o_ref[...] = jnp.dot(a_ref[...], w_ref[...], preferred_element_type=_F32).astype(o_ref.dtype)


def _mm(a, w, out_dtype):
    b, s, kdim = a.shape
    n = w.shape[1]
    tm = min(_ROW_TILE, s)
    return pl.pallas_call(
        _mm_body,
        grid=(b, s // tm),
        in_specs=[pl.BlockSpec((None, tm, kdim), lambda i, t: (i, t, 0)), _resident(w.shape)],
        out_specs=pl.BlockSpec((None, tm, n), lambda i, t: (i, t, 0)),
        out_shape=jax.ShapeDtypeStruct((b, s, n), out_dtype),
        compiler_params=_params(("parallel", "parallel")),
        name="matmul",
    )(a, w)


def _mla_proj_body(x_ref, g_ref, wd_ref, qg_ref, wq_ref, kvg_ref, wk_ref, wv_ref,
                   cos_ref, slo_ref, shi_ref, lat_ref, kr_ref, q_ref, k_ref, v_ref, *, qscale, mask_tile,
                   qt_cols):
    tm = x_ref.shape[0]
    h = _rms(x_ref[...], g_ref[...], _NORM_EPS).astype(_BF)
    cos, slo, shi = cos_ref[...], slo_ref[...], shi_ref[...]
    if mask_tile:
        row0 = pl.program_id(1) * tm
        q_ext = _mask_lanes(tm, row0, mask_tile, _B_ROPE, False)
        k_ext = _mask_lanes(tm, row0, mask_tile, _B_ROPE, True)
    else:
        q_ext = k_ext = jnp.zeros((tm, _LANES), _F32)
    dn = jnp.dot(h, wd_ref[...], preferred_element_type=_F32)
    e0, e1 = _B_Q_LORA, _B_Q_LORA + _B_KV_LORA
    cq = _rms(dn[:, :e0], qg_ref[...], _NORM_EPS).astype(_BF)
    lat = _rms(dn[:, e0:e1], kvg_ref[...], _NORM_EPS)
    kr = _rope_lanes(dn[:, e1:e1 + _LANES], cos, slo, shi)
    lat_ref[...] = lat
    kr_ref[...] = kr[:, :_B_ROPE]
    hw = 2 * _LANES
    for hd in range(_B_HEADS):
        yq = jnp.dot(cq, wq_ref[:, hd * hw:(hd + 1) * hw], preferred_element_type=_F32)
        q_nope = yq[:, :_LANES] * qscale
        q_rope = _rope_lanes(yq[:, _LANES:], cos, slo, shi) * qscale + q_ext
        if qt_cols:
            qt = jnp.concatenate([q_nope.T, q_rope.T], axis=0).astype(_BF)
            for g in range(tm // qt_cols):
                q_ref[hd, g] = qt[:, g * qt_cols:(g + 1) * qt_cols]
        else:
            q_ref[:, hd * hw:hd * hw + _LANES] = q_nope.astype(_BF)
            q_ref[:, hd * hw + _LANES:(hd + 1) * hw] = q_rope.astype(_BF)
    kin = jnp.concatenate([lat, kr], axis=1).astype(_BF)
    for hd in range(_B_HEADS):
        yk = jnp.dot(kin, wk_ref[:, hd * hw:(hd + 1) * hw], preferred_element_type=_F32)
        for var in range(2):
            k_ref[var, :, hd * hw:hd * hw + _LANES] = yk[:, :_LANES].astype(_BF)
        k_ref[0, :, hd * hw + _LANES:(hd + 1) * hw] = yk[:, _LANES:].astype(_BF)
        k_ref[1, :, hd * hw + _LANES:(hd + 1) * hw] = (yk[:, _LANES:] + k_ext).astype(_BF)
    latb = lat.astype(_BF)
    for c in range(0, wv_ref.shape[-1], _MXU_COLS):
        yv = jnp.dot(latb, wv_ref[:, c:c + _MXU_COLS], preferred_element_type=_F32)
        if qt_cols:
            for s in range(0, _MXU_COLS, _B_VD):
                v_ref[(c + s) // _B_VD, 0, :_B_VD, :] = yv[:, s:s + _B_VD].T.astype(_BF)
                v_ref[(c + s) // _B_VD, 0, _B_VD:, :] = jnp.ones((_BF_ROWS, tm), _BF)
        else:
            v_ref[:, c:c + _MXU_COLS] = yv.astype(_BF)


def _mla_weights(w_down, w_uq, w_uk, w_uv):
    d = w_down.shape[0]
    hw = 2 * _LANES
    wd = jnp.concatenate([w_down, jnp.zeros((d, _LANES - _B_ROPE), w_down.dtype)], axis=1)
    wq = w_uq.reshape(_B_Q_LORA, _B_HEADS, _B_NOPE + _B_ROPE)
    wq = jnp.pad(wq, ((0, 0), (0, 0), (0, hw - _B_NOPE - _B_ROPE))).reshape(_B_Q_LORA, _B_HEADS * hw)
    eye = jnp.eye(_B_ROPE, dtype=w_uk.dtype)
    wk_top = jnp.pad(w_uk, ((0, 0), (0, 0), (0, hw - _B_NOPE)))
    wk_mid = jnp.broadcast_to(jnp.pad(eye, ((0, 0), (_B_NOPE, hw - _B_NOPE - _B_ROPE)))[:, None, :],
                              (_B_ROPE, _B_HEADS, hw))
    wk_bot = jnp.zeros((_LANES - _B_ROPE, _B_HEADS, hw), w_uk.dtype)
    wk = jnp.concatenate([wk_top, wk_mid, wk_bot], axis=0).reshape(-1, _B_HEADS * hw)
    wv = w_uv.reshape(_B_KV_LORA, _B_HEADS * _B_VD)
    return wd.astype(_BF), wq.astype(_BF), wk.astype(_BF), wv.astype(_BF)


def _mla_proj(x, g, wts, q_norm_g, kv_norm_g, tabs, tm, mask_tile, qt_cols=0):
    b, s, d = x.shape
    wd, wq, wk, wv = wts
    xspec = pl.BlockSpec((None, tm, d), lambda i, t: (i, t, 0))
    tspec = pl.BlockSpec((tm, _LANES), lambda i, t: (t, 0))
    ospec = lambda n: pl.BlockSpec((None, tm, n), lambda i, t: (i, t, 0))
    oshape = lambda n, dt: jax.ShapeDtypeStruct((b, s, n), dt)
    if qt_cols:
        assert tm % qt_cols == 0
        hq = wq.shape[1] // _B_HEADS
        q_spec = pl.BlockSpec((None, _B_HEADS, tm // qt_cols, hq, qt_cols),
                              lambda i, t: (i, 0, t, 0, 0))
        q_shape = jax.ShapeDtypeStruct((b, _B_HEADS, s // qt_cols, hq, qt_cols), _BF)
        dve = _B_VD + _BF_ROWS
        v_spec = pl.BlockSpec((None, _B_HEADS, 1, dve, tm), lambda i, t: (i, 0, t, 0, 0))
        v_shape = jax.ShapeDtypeStruct((b, _B_HEADS, s // tm, dve, tm), _BF)
    else:
        q_spec, q_shape = ospec(wq.shape[1]), oshape(wq.shape[1], _BF)
        v_spec, v_shape = ospec(wv.shape[1]), oshape(wv.shape[1], _BF)
    return pl.pallas_call(
        functools.partial(_mla_proj_body, qscale=_LOG2E / math.sqrt(_B_NOPE + _B_ROPE),
                          mask_tile=mask_tile, qt_cols=qt_cols),
        grid=(b, s // tm),
        in_specs=[xspec, _resident((1, d)), _resident(wd.shape), _resident((1, _B_Q_LORA)),
                  _resident(wq.shape), _resident((1, _B_KV_LORA)), _resident(wk.shape),
                  _resident(wv.shape), tspec, tspec, tspec],
        out_specs=[ospec(_B_KV_LORA), ospec(_B_ROPE), q_spec,
                   pl.BlockSpec((None, 2, tm, wk.shape[1]), lambda i, t: (i, 0, t, 0)), v_spec],
        out_shape=[oshape(_B_KV_LORA, _F32), oshape(_B_ROPE, _F32), q_shape,
                   jax.ShapeDtypeStruct((b, 2, s, wk.shape[1]), _BF), v_shape],
        compiler_params=_params(("parallel", "parallel")),
        name="mla_proj",
    )(x, g.reshape(1, d), wd, q_norm_g.reshape(1, -1), wq, kv_norm_g.reshape(1, -1), wk, wv, *tabs)


def _delayed(carry_ref, cols, cur, shifts):
    nseg = carry_ref.shape[0]
    seg = cur.shape[0] // nseg
    outs = [[] for _ in shifts]
    for sg in range(nseg):
        part = cur[sg * seg:(sg + 1) * seg]
        prev = carry_ref[sg, :, cols]
        for o, k in zip(outs, shifts):
            o.append(_shift_rows(prev, part, k))
        carry_ref[sg, :, cols] = part[seg - _SUBLANES:]
    return [o[0] if nseg == 1 else jnp.concatenate(o, axis=0) for o in outs]


def _conv_mix_body(x_ref, g_ref, win_ref, cw_ref, wout_ref, hist_ref, o_ref, hout_ref,
                   carry_sc, z_sc):
    tm, d = x_ref.shape

    @pl.when(pl.program_id(1) == 0)
    def _():
        carry_sc[...] = hist_ref[...]

    x = x_ref[...]
    h = _rms(x, g_ref[...], _NORM_EPS).astype(_BF)
    cw = _MXU_COLS
    for c in range(0, d, cw):
        gate_b = jnp.dot(h, win_ref[:, c:c + cw], preferred_element_type=_F32)
        gate_c = jnp.dot(h, win_ref[:, d + c:d + c + cw], preferred_element_type=_F32)
        val = jnp.dot(h, win_ref[:, 2 * d + c:2 * d + c + cw], preferred_element_type=_F32)
        u = gate_c * val
        u2, u1 = _delayed(carry_sc, slice(c, c + cw), u, (2, 1))
        y = u2 * cw_ref[0:1, c:c + cw] + u1 * cw_ref[1:2, c:c + cw] + u * cw_ref[2:3, c:c + cw]
        z_sc[:, c:c + cw] = (gate_b * y).astype(_BF)
    hout_ref[...] = carry_sc[...]
    o_ref[...] = x + jnp.dot(z_sc[...], wout_ref[...], preferred_element_type=_F32)


def _conv_mix(x, g, w_in, conv_w, w_out, hist, tm):
    b, s, d = x.shape
    nseg = hist.shape[1]
    assert nseg == 1 or tm == s
    xspec = pl.BlockSpec((None, tm, d), lambda i, t: (i, t, 0))
    hspec = pl.BlockSpec((None, nseg, _SUBLANES, d), lambda i, t: (i, 0, 0, 0))
    return pl.pallas_call(
        _conv_mix_body,
        grid=(b, s // tm),
        in_specs=[xspec, _resident((1, d)), _resident(w_in.shape), _resident(conv_w.shape),
                  _resident(w_out.shape), hspec],
        out_specs=[xspec, hspec],
        out_shape=[jax.ShapeDtypeStruct((b, s, d), _F32),
                   jax.ShapeDtypeStruct((b, nseg, _SUBLANES, d), _F32)],
        scratch_shapes=[pltpu.VMEM((nseg, _SUBLANES, d), _F32), pltpu.VMEM((tm, d), _BF)],
        compiler_params=_params(("arbitrary", "arbitrary")),
        name="conv_mixer",
    )(x, g.reshape(1, d), w_in, conv_w, w_out, hist)


_POOL_HALO = 16


def _pool_step(x, g_ref, wg_ref, sc_ref, hist_ref, hout_ref, carry_sc, hist_valid):
    tm, d = x.shape
    t = pl.program_id(1)

    @pl.when(t == 0)
    def _():
        carry_sc[...] = hist_ref[...]

    h = _rms(x, g_ref[...], _NORM_EPS)
    gw = d // len(_D_WINDOWS)
    tpos = t * tm + lax.broadcasted_iota(jnp.int32, (tm, gw), 0)
    outs = []
    for gi, w in enumerate(_D_WINDOWS):
        hg = h[:, gi * gw:(gi + 1) * gw]
        acc = jnp.concatenate([carry_sc[:, gi * gw:(gi + 1) * gw], hg], axis=0)
        k = 1
        while k < w:
            acc = acc + pltpu.roll(acc, k, axis=0)
            k *= 2
        cnt = jnp.minimum(tpos + (hist_valid + 1), w).astype(_F32)
        pooled = acc[_POOL_HALO:] / cnt
        outs.append(jnp.dot((pooled - hg).astype(_BF), wg_ref[gi], preferred_element_type=_F32))
    tail = h[tm - _POOL_HALO:, :]
    carry_sc[...] = tail
    hout_ref[...] = tail
    return x + jnp.concatenate(outs, axis=1) * sc_ref[...]


def _pool_mix_body(x_ref, g_ref, wg_ref, sc_ref, hist_ref, o_ref, hout_ref, carry_sc, *, hist_valid):
    o_ref[...] = _pool_step(x_ref[...], g_ref, wg_ref, sc_ref, hist_ref, hout_ref, carry_sc,
                            hist_valid)


def _pool_mix(x, g, w_group, scale, hist16, hist_valid, tm):
    b, s, d = x.shape
    assert all(w & (w - 1) == 0 and w <= _POOL_HALO for w in _D_WINDOWS)
    xspec = pl.BlockSpec((None, tm, d), lambda i, t: (i, t, 0))
    hspec = pl.BlockSpec((None, _POOL_HALO, d), lambda i, t: (i, 0, 0))
    return pl.pallas_call(
        functools.partial(_pool_mix_body, hist_valid=hist_valid),
        grid=(b, s // tm),
        in_specs=[xspec, _resident((1, d)), _resident(w_group.shape), _resident((1, d)), hspec],
        out_specs=[xspec, hspec],
        out_shape=[jax.ShapeDtypeStruct((b, s, d), _F32),
                   jax.ShapeDtypeStruct((b, _POOL_HALO, d), _F32)],
        scratch_shapes=[pltpu.VMEM((_POOL_HALO, d), _F32)],
        compiler_params=_params(("arbitrary", "arbitrary")),
        name="pool_mixer",
    )(x, g.reshape(1, d), w_group, scale.reshape(1, d), hist16)


def _ffn_body(*refs, final, mixed, pooled):
    refs = list(refs)
    x_ref, g_ref, wg_ref, wu_ref, cw_ref, cb_ref, wd_ref, hist_ref = refs[:8]
    del refs[:8]
    a_ref, wo_ref = (refs.pop(0), refs.pop(0)) if mixed else (None, None)
    pool_in = [refs.pop(0) for _ in range(4)] if pooled else None
    fg_ref = refs.pop(0) if final else None
    o_ref, hout_ref = refs.pop(0), refs.pop(0)
    pool_hout = refs.pop(0) if pooled else None
    carry_sc, act_sc = refs.pop(0), refs.pop(0)
    tm = x_ref.shape[0]
    f = wg_ref.shape[1]

    @pl.when(pl.program_id(1) == 0)
    def _():
        carry_sc[...] = hist_ref[...]

    x = x_ref[...]
    if mixed:
        x = x + jnp.dot(a_ref[...], wo_ref[...], preferred_element_type=_F32)
    if pooled:
        x = _pool_step(x, *pool_in, pool_hout, refs.pop(0), 0)
    h = _rms(x, g_ref[...], _NORM_EPS).astype(_BF)
    cw = _MXU_COLS
    for c in range(0, f, cw):
        gate = jnp.dot(h, wg_ref[:, c:c + cw], preferred_element_type=_F32)
        up = jnp.dot(h, wu_ref[:, c:c + cw], preferred_element_type=_F32)
        g2, g1 = _delayed(carry_sc, slice(c, c + cw), gate, (2, 1))
        y = (g2 * cw_ref[0:1, c:c + cw] + g1 * cw_ref[1:2, c:c + cw]
             + gate * cw_ref[2:3, c:c + cw] + cb_ref[:, c:c + cw])
        act_sc[:, c:c + cw] = (y / (1.0 + jnp.exp(-y)) * up).astype(_BF)
    hout_ref[...] = carry_sc[...]
    out = x + jnp.dot(act_sc[...], wd_ref[...], preferred_element_type=_F32)
    if final:
        out = _rms(out, fg_ref[...], _NORM_EPS)
    o_ref[...] = out


def _ffn(x, g, w_gate, w_up, conv_w, conv_b, w_down, hist, final_g, tm, mixer_out=None, w_o=None,
         pool=None):
    b, s, d = x.shape
    f = w_gate.shape[1]
    nseg = hist.shape[1]
    assert f % _MXU_COLS == 0 and (nseg == 1 or tm == s) and (pool is None or nseg == 1)
    xspec = pl.BlockSpec((None, tm, d), lambda i, t: (i, t, 0))
    hspec = pl.BlockSpec((None, nseg, _SUBLANES, f), lambda i, t: (i, 0, 0, 0))
    in_specs = [xspec, _resident((1, d)), _resident(w_gate.shape), _resident(w_up.shape),
                _resident(conv_w.shape), _resident((1, f)), _resident(w_down.shape), hspec]
    args = [x, g.reshape(1, d), w_gate, w_up, conv_w, conv_b.reshape(1, f), w_down, hist]
    if mixer_out is not None:
        in_specs += [pl.BlockSpec((None, tm, mixer_out.shape[-1]), lambda i, t: (i, t, 0)),
                     _resident(w_o.shape)]
        args += [mixer_out, w_o]
    out_specs = [xspec, hspec]
    out_shape = [jax.ShapeDtypeStruct((b, s, d), _F32),
                 jax.ShapeDtypeStruct((b, nseg, _SUBLANES, f), _F32)]
    scratch = [pltpu.VMEM((nseg, _SUBLANES, f), _F32), pltpu.VMEM((tm, f), _BF)]
    if pool is not None:
        pg, pw, psc, phist = pool
        pspec = pl.BlockSpec((None, _POOL_HALO, d), lambda i, t: (i, 0, 0))
        in_specs += [_resident((1, d)), _resident(pw.shape), _resident((1, d)), pspec]
        args += [pg.reshape(1, d), pw, psc.reshape(1, d), phist]
        out_specs.append(pspec)
        out_shape.append(jax.ShapeDtypeStruct((b, _POOL_HALO, d), _F32))
        scratch.append(pltpu.VMEM((_POOL_HALO, d), _F32))
    if final_g is not None:
        in_specs.append(_resident((1, d)))
        args.append(final_g.reshape(1, d))
    return pl.pallas_call(
        functools.partial(_ffn_body, final=final_g is not None, mixed=mixer_out is not None,
                          pooled=pool is not None),
        grid=(b, s // tm),
        in_specs=in_specs,
        out_specs=out_specs,
        out_shape=out_shape,
        scratch_shapes=scratch,
        compiler_params=_params(("arbitrary", "arbitrary")),
        name="conv_ffn",
    )(*args)


def _pad_hist(hist, rows):
    return jnp.pad(hist, ((0, 0), (rows - hist.shape[1], 0), (0, 0)))


def kernel(x_prompt, x_sample, cache_a_k, cache_a_v, cache_b_latent, cache_b_krope, state_c_conv, state_d_pool, state_ffn_conv, norm_mix_g, norm_ffn_g, norm_final_g, a_w_qkv, a_lam, a_subln_g, a_w_o, b_w_down, b_q_norm_g, b_w_uq, b_kv_norm_g, b_w_uk, b_w_uv, b_w_o, c_w_in, c_conv_w, c_w_out, d_w_group, d_scale, ffn_w_gate, ffn_w_up, ffn_conv_w, ffn_conv_b, ffn_w_down):
    depth = norm_mix_g.shape[0]
    n_p, seq, d = x_prompt.shape
    n_s, t_new, _ = x_sample.shape
    past = cache_a_k.shape[2]
    f = ffn_w_gate.shape[-1]
    tm_p, tm_s = min(_ROW_TILE, seq), t_new
    ta = min(_ATT_TILE, seq)
    tabs_p = _rope_tables(jnp.arange(seq, dtype=jnp.int32))
    tabs_s = _rope_tables(past + jnp.arange(t_new, dtype=jnp.int32))
    xp, xs = x_prompt, x_sample
    flat = lambda a: a.reshape(1, n_s * t_new, a.shape[-1])
    outs = {k: [] for k in ("ak_p", "av_p", "bl_p", "br_p", "cc_p", "dp_p", "fc_p",
                            "ak_s", "av_s", "bl_s", "br_s", "cc_s", "dp_s", "fc_s")}
    for i in range(depth):
        m, j = i % _N_MIXERS, i // _N_MIXERS
        g_mix = norm_mix_g[i]
        mix_p, mix_s = {}, {}
        if m == 0:
            lam_init = 0.8 - 0.6 * math.exp(-0.3 * i)
            w_qkv, w_o = a_w_qkv[j].astype(_BF), a_w_o[j].astype(_BF)
            qt, kf, km, vf, vt = _diff_proj(xp, g_mix, w_qkv, tabs_p, ta, ta // _SWEEP_HALVES)
            op = _sweep(qt, km, vt, _A_HEADS, _LANES, 2 * _A_HD, halves=_SWEEP_HALVES,
                        trip_ticks=_TRIP_TICKS_DIFF,
                        lam=a_lam[j], subln_g=a_subln_g[j], lam_init=lam_init)
            mix_p = dict(mixer_out=op, w_o=w_o)
            outs["ak_p"].append(kf.reshape(n_p, seq, _A_HEADS, 2, _A_HD))
            outs["av_p"].append(vf.reshape(n_p, seq, _A_HEADS, 2 * _A_HD))
            qm, kf, vf = _diff_proj(xs, g_mix, w_qkv, tabs_s, tm_s)
            osm = _cached_diff(qm, cache_a_k[j].reshape(n_s, past, d), kf, cache_a_v[j], vf,
                               a_lam[j], a_subln_g[j], lam_init)
            mix_s = dict(mixer_out=flat(osm), w_o=w_o)
            outs["ak_s"].append(kf.reshape(n_s, t_new, _A_HEADS, 2, _A_HD))
            outs["av_s"].append(vf.reshape(n_s, t_new, _A_HEADS, 2 * _A_HD))
        elif m == 1:
            wts = _mla_weights(b_w_down[j], b_w_uq[j], b_w_uk[j], b_w_uv[j])
            w_o = b_w_o[j].astype(_BF)
            hq = 2 * _LANES
            lat, kr, qt, k2, vt = _mla_proj(xp, g_mix, wts, b_q_norm_g[j], b_kv_norm_g[j], tabs_p,
                                            ta, ta, ta // _SWEEP_HALVES)
            op = _sweep(qt[:, None], k2[:, None], vt, _B_HEADS, hq, _B_VD,
                        halves=_SWEEP_HALVES, trip_ticks=_TRIP_TICKS_MLA)
            mix_p = dict(mixer_out=op, w_o=w_o)
            outs["bl_p"].append(lat)
            outs["br_p"].append(kr)
            lat, kr, q, k2, v = _mla_proj(xs, g_mix, wts, b_q_norm_g[j], b_kv_norm_g[j], tabs_s,
                                          tm_s, 0)
            kin = jnp.concatenate([cache_b_latent[j], cache_b_krope[j],
                                   jnp.zeros((n_s, past, _LANES - _B_ROPE), _F32)], axis=-1).astype(_BF)
            k_cache = _mm(kin, wts[2], out_dtype=_BF)
            v_cache = _mm(cache_b_latent[j].astype(_BF), wts[3], out_dtype=_BF)
            k_all = jnp.concatenate([k_cache, k2[:, 0]], axis=1)
            v_all = jnp.concatenate([v_cache, v], axis=1)
            osm = _cached_mla(q, k_all, v_all, _B_HEADS)
            mix_s = dict(mixer_out=flat(osm), w_o=w_o)
            outs["bl_s"].append(lat)
            outs["br_s"].append(kr)
        elif m == 2:
            w_in, w_out = c_w_in[j].astype(_BF), c_w_out[j].astype(_BF)
            xp, hc = _conv_mix(xp, g_mix, w_in, c_conv_w[j], w_out,
                               jnp.zeros((n_p, 1, _SUBLANES, d), _F32), tm_p)
            outs["cc_p"].append(hc[:, 0, -2:])
            xs, hc = _conv_mix(flat(xs), g_mix, w_in, c_conv_w[j], w_out,
                               _pad_hist(state_c_conv[j], _SUBLANES)[None], n_s * t_new)
            xs = xs.reshape(n_s, t_new, d)
            outs["cc_s"].append(hc[0, :, -2:])
        else:
            w_grp = d_w_group[j].astype(_BF)
            mix_p = dict(pool=(g_mix, w_grp, d_scale[j], jnp.zeros((n_p, _POOL_HALO, d), _F32)))
            xs, hd = _pool_mix(xs, g_mix, w_grp, d_scale[j],
                               _pad_hist(state_d_pool[j], _POOL_HALO), _D_HIST, tm_s)
            outs["dp_s"].append(hd[:, -_D_HIST:])
        wg, wu, wd = ffn_w_gate[i].astype(_BF), ffn_w_up[i].astype(_BF), ffn_w_down[i].astype(_BF)
        final_g = norm_final_g if i == depth - 1 else None
        xp, hf, *hd_p = _ffn(xp, norm_ffn_g[i], wg, wu, ffn_conv_w[i], ffn_conv_b[i], wd,
                             jnp.zeros((n_p, 1, _SUBLANES, f), _F32), final_g, tm_p, **mix_p)
        outs["fc_p"].append(hf[:, 0, -2:])
        if hd_p:
            outs["dp_p"].append(hd_p[0][:, -_D_HIST:])
        xs, hf = _ffn(flat(xs), norm_ffn_g[i], wg, wu, ffn_conv_w[i], ffn_conv_b[i], wd,
                      _pad_hist(state_ffn_conv[i], _SUBLANES)[None], final_g, n_s * t_new, **mix_s)
        xs = xs.reshape(n_s, t_new, d)
        outs["fc_s"].append(hf[0, :, -2:])
    st = lambda k: jnp.stack(outs[k])
    return (xp, xs, st("ak_p"), st("av_p"), st("bl_p"), st("br_p"), st("cc_p"), st("dp_p"), st("fc_p"),
            st("ak_s"), st("av_s"), st("bl_s"), st("br_s"), st("cc_s"), st("dp_s"), st("fc_s"))
```
